```python
import jax, jax.numpy as jnp
from jax import lax
import numpy as np

D_MODEL = 1024
BATCH = 8
SEQ = 16384
DEPTH = 4

CHUNK = 64
HEAD_DIM = 64
N_HEADS_SB = 4
N_HEADS_CH = 8
CONV_CH = 256
CONV_WIDTH = 3
N_PREV_CHUNKS = 8
BAND = (N_PREV_CHUNKS + 1) * CHUNK
REL_CLIP = 128
Q_BLOCK = 128
D_SB = N_HEADS_SB * HEAD_DIM
D_CH = N_HEADS_CH * HEAD_DIM
D_MIX = D_SB + D_CH + CONV_CH
D_IN = 3 * D_SB + 3 * D_CH + 3 * CONV_CH
N_OUT_GROUPS = D_MIX // HEAD_DIM
D_FF = ((8 * D_MODEL // 3 + 255) // 256) * 256
EPS = 1e-6

kernel_name = "hybrid_stickbreak_chunkattn_shortconv_block"


def rmsnorm(x, w):
    xf = x.astype(jnp.float32)
    y = xf * lax.rsqrt(jnp.mean(xf * xf, axis=-1, keepdims=True) + EPS)
    return (y * w.astype(jnp.float32)).astype(x.dtype)


def group_rmsnorm(y, w):
    b, s, _ = y.shape
    yg = y.astype(jnp.float32).reshape(b, s, N_OUT_GROUPS, HEAD_DIM)
    yg = yg * lax.rsqrt(jnp.mean(yg * yg, axis=-1, keepdims=True) + EPS)
    return (yg.reshape(b, s, D_MIX) * w.astype(jnp.float32)).astype(y.dtype)


def to_heads(t):
    b, s, _ = t.shape
    return t.reshape(b, s, -1, HEAD_DIM).transpose(0, 2, 1, 3)


def from_heads(t):
    b, h, s, d = t.shape
    return t.transpose(0, 2, 1, 3).reshape(b, s, h * d)


def stick_breaking_attention(q, k, v):
    b, h, s, d = q.shape
    n = s // Q_BLOCK
    qb = (q.astype(jnp.float32) * (d ** -0.5)).reshape(b, h, n, Q_BLOCK, d)
    kb = k.astype(jnp.float32).reshape(b, h, n, Q_BLOCK, d)
    vb = v.astype(jnp.float32).reshape(b, h, n, Q_BLOCK, d)
    idx = jnp.arange(Q_BLOCK)
    after_mat = (idx[:, None] > idx[None, :]).astype(jnp.float32)
    diag_mask = idx[None, :] < idx[:, None]
    out = jnp.zeros((b, h, n, Q_BLOCK, d), jnp.float32)
    carry = jnp.zeros((b, h, n, Q_BLOCK), jnp.float32)
    for o in range(n):
        z = jnp.einsum('bhnqd,bhnkd->bhnqk', qb[:, :, o:], kb[:, :, :n - o])
        log_rest = jax.nn.log_sigmoid(-z)
        if o == 0:
            log_rest = jnp.where(diag_mask, log_rest, 0.0)
        after = jnp.einsum('bhnqj,js->bhnqs', log_rest, after_mat) + carry[:, :, o:, :, None]
        w = jnp.exp(jax.nn.log_sigmoid(z) + after)
        if o == 0:
            w = jnp.where(diag_mask, w, 0.0)
        out = out.at[:, :, o:].add(jnp.einsum('bhnqk,bhnkd->bhnqd', w, vb[:, :, :n - o]))
        carry = carry.at[:, :, o:].add(jnp.sum(log_rest, axis=-1))
    return out.reshape(b, h, s, d).astype(v.dtype)


def chunked_relpos_attention(q, k, v, rel_bias):
    b, h, s, d = q.shape
    nc = s // CHUNK
    pad = N_PREV_CHUNKS * CHUNK
    qc = (q.astype(jnp.float32) * (d ** -0.5)).reshape(b, h, nc, CHUNK, d)
    kpad = jnp.pad(k, ((0, 0), (0, 0), (pad, 0), (0, 0))).reshape(b, h, nc + N_PREV_CHUNKS, CHUNK, d)
    vpad = jnp.pad(v, ((0, 0), (0, 0), (pad, 0), (0, 0))).reshape(b, h, nc + N_PREV_CHUNKS, CHUNK, d)
    band_idx = jnp.arange(nc)[:, None] + jnp.arange(N_PREV_CHUNKS + 1)[None, :]
    kb = kpad[:, :, band_idx].reshape(b, h, nc, BAND, d).astype(jnp.float32)
    vb = vpad[:, :, band_idx].reshape(b, h, nc, BAND, d).astype(jnp.float32)
    scores = jnp.einsum('bhcqd,bhckd->bhcqk', qc, kb)
    p = jnp.arange(CHUNK)[:, None]
    m = jnp.arange(BAND)[None, :]
    rel = jnp.clip(N_PREV_CHUNKS * CHUNK + p - m, -REL_CLIP, REL_CLIP) + REL_CLIP
    bias = rel_bias.astype(jnp.float32)[:, rel]
    valid = jnp.repeat(band_idx >= N_PREV_CHUNKS, CHUNK, axis=1)
    scores = jnp.where(valid[None, None, :, None, :], scores + bias[None, :, None], -jnp.inf)
    probs = jax.nn.softmax(scores, axis=-1)
    out = jnp.einsum('bhcqk,bhckd->bhcqd', probs, vb)
    return out.reshape(b, h, s, d).astype(v.dtype)


def short_conv_mixer(gate_b, gate_c, xc, conv_w):
    hc = gate_c * xc
    y = lax.conv_general_dilated(
        hc, conv_w[:, None, :].astype(hc.dtype), window_strides=(1,),
        padding=[(CONV_WIDTH - 1, 0)], dimension_numbers=('NWC', 'WIO', 'NWC'),
        feature_group_count=CONV_CH)
    return gate_b * y


def swiglu(x, w_gate, w_up, w_down):
    return (jax.nn.silu(x @ w_gate) * (x @ w_up)) @ w_down


def _fwd_setup_inputs(seed: int = 0) -> dict:
    key = jax.random.key(seed)
    ks = jax.random.split(key, 14)
    f32 = jnp.float32

    def gain(k, n):
        return 1.0 + 0.02 * jax.random.normal(k, (DEPTH, n), f32)

    return {
        "x": jax.random.normal(ks[0], (BATCH, SEQ, D_MODEL), f32),
        "attn_norm_w": gain(ks[1], D_MODEL),
        "w_in": jax.random.normal(ks[2], (DEPTH, D_MODEL, D_IN), f32) * D_MODEL ** -0.5,
        "q_norm_w": gain(ks[3], HEAD_DIM),
        "k_norm_w": gain(ks[4], HEAD_DIM),
        "rel_bias": 0.1 * jax.random.normal(ks[5], (DEPTH, N_HEADS_CH, 2 * REL_CLIP + 1), f32),
        "conv_w": jax.random.normal(ks[6], (DEPTH, CONV_WIDTH, CONV_CH), f32) * CONV_WIDTH ** -0.5,
        "out_norm_w": gain(ks[7], D_MIX),
        "w_out": jax.random.normal(ks[8], (DEPTH, D_MIX, D_MODEL), f32) * D_MIX ** -0.5,
        "ffn_norm_w": gain(ks[9], D_MODEL),
        "w_gate": jax.random.normal(ks[10], (DEPTH, D_MODEL, D_FF), f32) * D_MODEL ** -0.5,
        "w_up": jax.random.normal(ks[11], (DEPTH, D_MODEL, D_FF), f32) * D_MODEL ** -0.5,
        "w_down": jax.random.normal(ks[12], (DEPTH, D_FF, D_MODEL), f32) * D_FF ** -0.5,
    }


def _fwd_reference(x, attn_norm_w, w_in, q_norm_w, k_norm_w, rel_bias, conv_w, out_norm_w,
              w_out, ffn_norm_w, w_gate, w_up, w_down):
    widths = [D_SB] * 3 + [D_CH] * 3 + [CONV_CH] * 3
    split_points = [int(v) for v in np.cumsum(widths)[:-1]]
    for l in range(DEPTH):
        h = rmsnorm(x, attn_norm_w[l])
        proj = h @ w_in[l]
        (q_a, k_a, v_a, q_b, k_b, v_b, g_b, g_c, x_c) = jnp.split(proj, split_points, axis=-1)
        y_sb = from_heads(stick_breaking_attention(to_heads(q_a), to_heads(k_a), to_heads(v_a)))
        qh = rmsnorm(to_heads(q_b), q_norm_w[l])
        kh = rmsnorm(to_heads(k_b), k_norm_w[l])
        y_ch = from_heads(chunked_relpos_attention(qh, kh, to_heads(v_b), rel_bias[l]))
        y_cv = short_conv_mixer(g_b, g_c, x_c, conv_w[l])
        y = group_rmsnorm(jnp.concatenate([y_sb, y_ch, y_cv], axis=-1), out_norm_w[l])
        x = x + y @ w_out[l]
        x = x + swiglu(rmsnorm(x, ffn_norm_w[l]), w_gate[l], w_up[l], w_down[l])
    return x


import jax as _jax
import jax.numpy as _jnp

TWIN_FORMAT = 'train_step'
FWD_PARAMS = ['x', 'attn_norm_w', 'w_in', 'q_norm_w', 'k_norm_w', 'rel_bias', 'conv_w', 'out_norm_w', 'w_out', 'ffn_norm_w', 'w_gate', 'w_up', 'w_down']
TWIN_WEIGHTS = ['attn_norm_w', 'w_in', 'q_norm_w', 'k_norm_w', 'rel_bias', 'conv_w', 'out_norm_w', 'w_out', 'ffn_norm_w', 'w_gate', 'w_up', 'w_down']
TWIN_DIFF_INPUT = 'x'
TWIN_INPUTS = ['x', 'attn_norm_w', 'w_in', 'q_norm_w', 'k_norm_w', 'rel_bias', 'conv_w', 'out_norm_w', 'w_out', 'ffn_norm_w', 'w_gate', 'w_up', 'w_down', 'loss_target', 'm_attn_norm_w', 'm_w_in', 'm_q_norm_w', 'm_k_norm_w', 'm_rel_bias', 'm_conv_w', 'm_out_norm_w', 'm_w_out', 'm_ffn_norm_w', 'm_w_gate', 'm_w_up', 'm_w_down', 'v_attn_norm_w', 'v_w_in', 'v_q_norm_w', 'v_k_norm_w', 'v_rel_bias', 'v_conv_w', 'v_out_norm_w', 'v_w_out', 'v_ffn_norm_w', 'v_w_gate', 'v_w_up', 'v_w_down']
TWIN_OUTPUTS = ['loss', 'grad_x', 'grad_attn_norm_w', 'grad_w_in', 'grad_q_norm_w', 'grad_k_norm_w', 'grad_rel_bias', 'grad_conv_w', 'grad_out_norm_w', 'grad_w_out', 'grad_ffn_norm_w', 'grad_w_gate', 'grad_w_up', 'grad_w_down', 'delta_attn_norm_w', 'delta_w_in', 'delta_q_norm_w', 'delta_k_norm_w', 'delta_rel_bias', 'delta_conv_w', 'delta_out_norm_w', 'delta_w_out', 'delta_ffn_norm_w', 'delta_w_gate', 'delta_w_up', 'delta_w_down', 'new_m_attn_norm_w', 'new_m_w_in', 'new_m_q_norm_w', 'new_m_k_norm_w', 'new_m_rel_bias', 'new_m_conv_w', 'new_m_out_norm_w', 'new_m_w_out', 'new_m_ffn_norm_w', 'new_m_w_gate', 'new_m_w_up', 'new_m_w_down', 'new_v_attn_norm_w', 'new_v_w_in', 'new_v_q_norm_w', 'new_v_k_norm_w', 'new_v_rel_bias', 'new_v_conv_w', 'new_v_out_norm_w', 'new_v_w_out', 'new_v_ffn_norm_w', 'new_v_w_gate', 'new_v_w_up', 'new_v_w_down']
TWIN_LEAF_KINDS = {'loss': 'loss', 'grad_x': 'grad_x', 'grad_attn_norm_w': 'grad_w', 'grad_w_in': 'grad_w', 'grad_q_norm_w': 'grad_w', 'grad_k_norm_w': 'grad_w', 'grad_rel_bias': 'grad_w', 'grad_conv_w': 'grad_w', 'grad_out_norm_w': 'grad_w', 'grad_w_out': 'grad_w', 'grad_ffn_norm_w': 'grad_w', 'grad_w_gate': 'grad_w', 'grad_w_up': 'grad_w', 'grad_w_down': 'grad_w', 'delta_attn_norm_w': 'delta_w', 'delta_w_in': 'delta_w', 'delta_q_norm_w': 'delta_w', 'delta_k_norm_w': 'delta_w', 'delta_rel_bias': 'delta_w', 'delta_conv_w': 'delta_w', 'delta_out_norm_w': 'delta_w', 'delta_w_out': 'delta_w', 'delta_ffn_norm_w': 'delta_w', 'delta_w_gate': 'delta_w', 'delta_w_up': 'delta_w', 'delta_w_down': 'delta_w', 'new_m_attn_norm_w': 'new_m', 'new_m_w_in': 'new_m', 'new_m_q_norm_w': 'new_m', 'new_m_k_norm_w': 'new_m', 'new_m_rel_bias': 'new_m', 'new_m_conv_w': 'new_m', 'new_m_out_norm_w': 'new_m', 'new_m_w_out': 'new_m', 'new_m_ffn_norm_w': 'new_m', 'new_m_w_gate': 'new_m', 'new_m_w_up': 'new_m', 'new_m_w_down': 'new_m', 'new_v_attn_norm_w': 'new_v', 'new_v_w_in': 'new_v', 'new_v_q_norm_w': 'new_v', 'new_v_k_norm_w': 'new_v', 'new_v_rel_bias': 'new_v', 'new_v_conv_w': 'new_v', 'new_v_out_norm_w': 'new_v', 'new_v_w_out': 'new_v', 'new_v_ffn_norm_w': 'new_v', 'new_v_w_gate': 'new_v', 'new_v_w_up': 'new_v', 'new_v_w_down': 'new_v'}


def _forward(args):
    return _fwd_reference(*[args[k] for k in FWD_PARAMS])


def _output_shape():
    def fwd():
        inp = _fwd_setup_inputs(0)
        return _fwd_reference(*[inp[k] for k in FWD_PARAMS])
    out = _jax.eval_shape(fwd)
    return out.shape, out.dtype

N_MICROBATCH = 1
ADAM_LR = 0.001
ADAM_B1 = 0.9
ADAM_B2 = 0.999
ADAM_EPS = 1e-08
ADAM_WD = 0.01
ADAM_STEP = 10
PER_EXAMPLE_BATCH_AXIS = {'x': 0, 'loss_target': 0}
SHARED_INPUTS = []
_WEIGHT_DTYPES = {'attn_norm_w': _jnp.float32, 'w_in': _jnp.float32, 'q_norm_w': _jnp.float32, 'k_norm_w': _jnp.float32, 'rel_bias': _jnp.float32, 'conv_w': _jnp.float32, 'out_norm_w': _jnp.float32, 'w_out': _jnp.float32, 'ffn_norm_w': _jnp.float32, 'w_gate': _jnp.float32, 'w_up': _jnp.float32, 'w_down': _jnp.float32}
MOMENT_SCALE = {'attn_norm_w': 4.351623e+01, 'w_in': 2.554659e+01, 'q_norm_w': 1.101916e+01, 'k_norm_w': 1.118078e+01, 'rel_bias': 4.828640e+00, 'conv_w': 3.420656e+00, 'out_norm_w': 1.448049e+02, 'w_out': 4.572037e+01, 'ffn_norm_w': 9.774858e+01, 'w_gate': 3.241051e+00, 'w_up': 4.545515e+00, 'w_down': 7.320077e+00}


def _to_microbatches(a, axis):
    t = _jnp.moveaxis(a, axis, 0)
    t = t.reshape((N_MICROBATCH, t.shape[0] // N_MICROBATCH) + t.shape[1:])
    return _jnp.moveaxis(t, 1, axis + 1)


def setup_inputs(seed: int = 0) -> dict:
    inp = _fwd_setup_inputs(seed)
    key = _jax.random.fold_in(_jax.random.key(seed), 7919)
    shape, _ = _output_shape()
    out = dict(inp)
    out["loss_target"] = _jax.random.normal(_jax.random.fold_in(key, 0), shape, _jnp.float32)
    for i, name in enumerate(TWIN_WEIGHTS):
        w = inp[name].astype(_jnp.float32)
        if MOMENT_SCALE is None:
            s = _jnp.sqrt(_jnp.mean(_jnp.square(w)) + 1e-30)
        else:
            s = MOMENT_SCALE[name]
        km, kv = _jax.random.split(_jax.random.fold_in(key, i + 1))
        out[name] = w
        out["m_" + name] = s * _jax.random.normal(km, w.shape, _jnp.float32)
        out["v_" + name] = (s * s) * _jax.random.uniform(kv, w.shape, _jnp.float32, 0.5, 1.5)
    if N_MICROBATCH > 1:
        for name, axis in PER_EXAMPLE_BATCH_AXIS.items():
            out[name] = _to_microbatches(out[name], axis)
    return {'x': out['x'], 'attn_norm_w': out['attn_norm_w'], 'w_in': out['w_in'], 'q_norm_w': out['q_norm_w'], 'k_norm_w': out['k_norm_w'], 'rel_bias': out['rel_bias'], 'conv_w': out['conv_w'], 'out_norm_w': out['out_norm_w'], 'w_out': out['w_out'], 'ffn_norm_w': out['ffn_norm_w'], 'w_gate': out['w_gate'], 'w_up': out['w_up'], 'w_down': out['w_down'], 'loss_target': out['loss_target'], 'm_attn_norm_w': out['m_attn_norm_w'], 'm_w_in': out['m_w_in'], 'm_q_norm_w': out['m_q_norm_w'], 'm_k_norm_w': out['m_k_norm_w'], 'm_rel_bias': out['m_rel_bias'], 'm_conv_w': out['m_conv_w'], 'm_out_norm_w': out['m_out_norm_w'], 'm_w_out': out['m_w_out'], 'm_ffn_norm_w': out['m_ffn_norm_w'], 'm_w_gate': out['m_w_gate'], 'm_w_up': out['m_w_up'], 'm_w_down': out['m_w_down'], 'v_attn_norm_w': out['v_attn_norm_w'], 'v_w_in': out['v_w_in'], 'v_q_norm_w': out['v_q_norm_w'], 'v_k_norm_w': out['v_k_norm_w'], 'v_rel_bias': out['v_rel_bias'], 'v_conv_w': out['v_conv_w'], 'v_out_norm_w': out['v_out_norm_w'], 'v_w_out': out['v_w_out'], 'v_ffn_norm_w': out['v_ffn_norm_w'], 'v_w_gate': out['v_w_gate'], 'v_w_up': out['v_w_up'], 'v_w_down': out['v_w_down']}


def _loss(weights, diff, rest, loss_target):
    with _jax.named_scope("forward"):
        args = {**rest, TWIN_DIFF_INPUT: diff, **{k: w.astype(_WEIGHT_DTYPES[k]) for k, w in weights.items()}}
        y = _forward(args)
    with _jax.named_scope("loss_head"):
        err = _jnp.square(y.astype(_jnp.float32) - loss_target)
        return 0.5 * _jnp.sum(_jnp.mean(err, axis=-1)) if err.ndim else 0.5 * err


def _adamw(w, g, m, v):
    m = ADAM_B1 * m + (1.0 - ADAM_B1) * g
    v = ADAM_B2 * v + (1.0 - ADAM_B2) * _jnp.square(g)
    m_hat = m / (1.0 - ADAM_B1 ** ADAM_STEP)
    v_hat = v / (1.0 - ADAM_B2 ** ADAM_STEP)
    delta = -ADAM_LR * (m_hat / (_jnp.sqrt(v_hat) + ADAM_EPS) + ADAM_WD * w)
    return delta, m, v


def reference(x, attn_norm_w, w_in, q_norm_w, k_norm_w, rel_bias, conv_w, out_norm_w, w_out, ffn_norm_w, w_gate, w_up, w_down, loss_target, m_attn_norm_w, m_w_in, m_q_norm_w, m_k_norm_w, m_rel_bias, m_conv_w, m_out_norm_w, m_w_out, m_ffn_norm_w, m_w_gate, m_w_up, m_w_down, v_attn_norm_w, v_w_in, v_q_norm_w, v_k_norm_w, v_rel_bias, v_conv_w, v_out_norm_w, v_w_out, v_ffn_norm_w, v_w_gate, v_w_up, v_w_down):
    given = dict(x=x, attn_norm_w=attn_norm_w, w_in=w_in, q_norm_w=q_norm_w, k_norm_w=k_norm_w, rel_bias=rel_bias, conv_w=conv_w, out_norm_w=out_norm_w, w_out=w_out, ffn_norm_w=ffn_norm_w, w_gate=w_gate, w_up=w_up, w_down=w_down, loss_target=loss_target, m_attn_norm_w=m_attn_norm_w, m_w_in=m_w_in, m_q_norm_w=m_q_norm_w, m_k_norm_w=m_k_norm_w, m_rel_bias=m_rel_bias, m_conv_w=m_conv_w, m_out_norm_w=m_out_norm_w, m_w_out=m_w_out, m_ffn_norm_w=m_ffn_norm_w, m_w_gate=m_w_gate, m_w_up=m_w_up, m_w_down=m_w_down, v_attn_norm_w=v_attn_norm_w, v_w_in=v_w_in, v_q_norm_w=v_q_norm_w, v_k_norm_w=v_k_norm_w, v_rel_bias=v_rel_bias, v_conv_w=v_conv_w, v_out_norm_w=v_out_norm_w, v_w_out=v_w_out, v_ffn_norm_w=v_ffn_norm_w, v_w_gate=v_w_gate, v_w_up=v_w_up, v_w_down=v_w_down)
    weights = {n: given[n] for n in TWIN_WEIGHTS}
    shared = {n: given[n] for n in SHARED_INPUTS}
    per_example = {n: given[n] for n in ['x']}
    grad_fn = _jax.value_and_grad(_loss, argnums=(0, 1))

    def one_microbatch(ex, loss_target):
        ex = dict(ex)
        diff = ex.pop(TWIN_DIFF_INPUT)
        return grad_fn(weights, diff, {**shared, **ex}, loss_target)

    if N_MICROBATCH == 1:
        loss, (grad_w, grad_x) = one_microbatch(per_example, given["loss_target"])
    else:
        def body(carry, xs):
            loss_sum, grad_sum = carry
            l_k, (gw_k, gx_k) = one_microbatch(xs[0], xs[1])
            with _jax.named_scope("update"):
                return (loss_sum + l_k, _jax.tree.map(_jnp.add, grad_sum, gw_k)), gx_k

        init = (_jnp.zeros((), _jnp.float32), _jax.tree.map(_jnp.zeros_like, weights))
        (loss, grad_w), grad_x = _jax.lax.scan(body, init, (per_example, given["loss_target"]))
    with _jax.named_scope("update"):
        delta_w, new_m, new_v = {}, {}, {}
        for n in TWIN_WEIGHTS:
            delta_w[n], new_m[n], new_v[n] = _adamw(weights[n], grad_w[n], given["m_" + n], given["v_" + n])
    return (loss, grad_x, *[grad_w[n] for n in TWIN_WEIGHTS], *[delta_w[n] for n in TWIN_WEIGHTS],
            *[new_m[n] for n in TWIN_WEIGHTS], *[new_v[n] for n in TWIN_WEIGHTS])
```

```python
import functools

import numpy as np
import jax
import jax.numpy as jnp
from jax import lax
from jax.experimental import pallas as pl
from jax.experimental.pallas import tpu as pltpu

F32 = jnp.float32
_MXU = jnp.bfloat16

D = 1024
DEPTH = 4
HD = 64
D_IN = 3072
N_SH = 4
IN_SH = D_IN // N_SH
FF_SH = 704
OUT_SH = 256
N_HEADS_CH = 8
REL = 257
QA, KA, VA, QB, KB, VB, GB, GC, XC = 0, 256, 512, 768, 1280, 1792, 2304, 2560, 2816
EPS = 1e-6
NEG = -1e30
SB_BLK = 128
CH_TQ = 256
EXP_ZERO = -104.0

ADAM_LR, ADAM_B1, ADAM_B2, ADAM_EPS, ADAM_WD, ADAM_STEP = 0.001, 0.9, 0.999, 1e-08, 0.01, 10

NT = (((1,), (1,)), ((), ()))
TN = (((0,), (0,)), ((), ()))
MESH = pl.DeviceIdType.MESH
ANY = pl.BlockSpec(memory_space=pl.ANY)
MB = 1 << 20


def _cp(sem=None, vmem_mb=None):
    return pltpu.CompilerParams(dimension_semantics=sem, vmem_limit_bytes=None if vmem_mb is None else vmem_mb * MB)


def _mx(a):
    return a.astype(_MXU)


def _dot(a, b):
    return jnp.dot(_mx(a), _mx(b), preferred_element_type=F32)


def _dg(a, b, dims):
    return lax.dot_general(_mx(a), _mx(b), dims, preferred_element_type=F32)


def _dot2(a, m):
    hi = _mx(a)
    r = jnp.dot(hi, m, preferred_element_type=F32)
    if _MXU != F32:
        r = r + jnp.dot(_mx(a - hi.astype(F32)), m, preferred_element_type=F32)
    return r


def _sigmoid(u):
    return 1.0 / (1.0 + jnp.exp(-u))


def _row_norm(x, w):
    r = lax.rsqrt(jnp.mean(x * x, axis=-1, keepdims=True) + EPS)
    return x * r * w


def _row_norm_bwd(x, w, dy):
    r = lax.rsqrt(jnp.mean(x * x, axis=-1, keepdims=True) + EPS)
    xh = x * r
    dw = jnp.sum(dy * xh, axis=0, keepdims=True)
    dxh = dy * w
    return r * (dxh - xh * jnp.mean(dxh * xh, axis=-1, keepdims=True)), dw


def _half_rs(x, lo):
    sq = x * x
    s0 = jnp.sum(jnp.where(lo, sq, 0.0), axis=-1, keepdims=True)
    s1 = jnp.sum(jnp.where(lo, 0.0, sq), axis=-1, keepdims=True)
    return jnp.where(lo, lax.rsqrt(s0 * (1.0 / HD) + EPS), lax.rsqrt(s1 * (1.0 / HD) + EPS))


def _half_norm(x, w, lo):
    return x * _half_rs(x, lo) * w


def _half_norm_bwd(x, w, dy, lo):
    r = _half_rs(x, lo)
    xh = x * r
    dw = jnp.sum(dy * xh, axis=0, keepdims=True)
    dxh = dy * w
    pr = dxh * xh
    m0 = jnp.sum(jnp.where(lo, pr, 0.0), axis=-1, keepdims=True)
    m1 = jnp.sum(jnp.where(lo, 0.0, pr), axis=-1, keepdims=True)
    mean = jnp.where(lo, m0, m1) * (1.0 / HD)
    return r * (dxh - xh * mean), dw


def _lo_mask(t):
    return lax.broadcasted_iota(jnp.int32, (t, 128), 1) < HD


def _allgather_layer(parts):
    n = len(parts)

    def body(*refs):
        ins, outs = refs[:n], refs[n:2 * n]
        lsem, ssem, rsem = refs[2 * n:]
        x, y, c = lax.axis_index("x"), lax.axis_index("y"), lax.axis_index("c")
        me = 2 * x + y
        peers = [(1 - x, y, c), (x, 1 - y, c), (1 - x, 1 - y, c)]
        local = [pltpu.make_async_copy(ins[f], outs[f].at[me], lsem.at[f]) for f in range(n)]
        for cp in local:
            cp.start()
        remote = [pltpu.make_async_remote_copy(src_ref=ins[f], dst_ref=outs[f].at[me], send_sem=ssem.at[3 * f + k],
                                               recv_sem=rsem.at[3 * f + k], device_id=peers[k], device_id_type=MESH)
                  for f in range(n) for k in range(3)]
        for cp in remote:
            cp.start()
        for cp in remote:
            cp.wait()
        for cp in local:
            cp.wait()

    return pl.pallas_call(
        body, name="allgather_layer",
        out_shape=[jax.ShapeDtypeStruct((N_SH,) + p.shape, p.dtype) for p in parts],
        in_specs=[ANY] * n, out_specs=[ANY] * n,
        scratch_shapes=[pltpu.SemaphoreType.DMA((n,)), pltpu.SemaphoreType.DMA((3 * n,)), pltpu.SemaphoreType.DMA((3 * n,))],
    )(*parts)


def _attn_in(x, nw, w_sh):
    S = x.shape[0]
    tm = min(512, S)

    def body(x_ref, nw_ref, w_ref, h_ref, pf_ref, pb_ref):
        @pl.when(pl.program_id(1) == 0)
        def _():
            h_ref[...] = _row_norm(x_ref[...], nw_ref[...]).astype(h_ref.dtype)

        p = jnp.dot(h_ref[...], w_ref[0], preferred_element_type=F32)
        pf_ref[...] = p
        pb_ref[...] = p.astype(pb_ref.dtype)

    return pl.pallas_call(
        body, name="attn_in", grid=(S // tm, N_SH),
        in_specs=[pl.BlockSpec((tm, D), lambda i, j: (i, 0)), pl.BlockSpec((1, D), lambda i, j: (0, 0)),
                  pl.BlockSpec((1, D, IN_SH), lambda i, j: (j, 0, 0))],
        out_specs=[pl.BlockSpec((tm, D), lambda i, j: (i, 0)), pl.BlockSpec((tm, IN_SH), lambda i, j: (i, j)),
                   pl.BlockSpec((tm, IN_SH), lambda i, j: (i, j))],
        out_shape=[jax.ShapeDtypeStruct((S, D), _MXU), jax.ShapeDtypeStruct((S, D_IN), F32), jax.ShapeDtypeStruct((S, D_IN), _MXU)],
        compiler_params=_cp(("parallel", "arbitrary"), 40),
    )(x, nw, w_sh)


def _sb_consts():
    row = lax.broadcasted_iota(jnp.int32, (SB_BLK, SB_BLK), 0)
    col = lax.broadcasted_iota(jnp.int32, (SB_BLK, SB_BLK), 1)
    return row, col


def _sb_block(qs_h, kj, keep, carry_h, amat):
    z = _dg(qs_h, kj, NT)
    lr = -(jnp.maximum(z, 0.0) + jnp.log1p(jnp.exp(-jnp.abs(z))))
    lr = jnp.where(keep, lr, 0.0)
    after = _dot2(lr, amat) + carry_h
    w = jnp.where(keep, jnp.exp(z + lr + after), 0.0)
    return z, lr, w


def _sb_fwd(pb):
    S = pb.shape[0]
    nq = S // SB_BLK

    def body(q_ref, k_ref, v_ref, o_ref):
        i = pl.program_id(1)
        row, col = _sb_consts()
        lo = col < HD
        amat = (row > col).astype(_MXU)
        strict = col < row
        q = q_ref[...]
        qs = [jnp.where(lo, q, 0) * 0.125, jnp.where(lo, 0, q) * 0.125]

        def cond(st):
            return jnp.logical_and(st[0] >= 0, st[1] > EXP_ZERO)

        def step(st):
            j, _, acc, c0, c1 = st
            start = pl.multiple_of(j * SB_BLK, SB_BLK)
            kj, vj = k_ref[pl.ds(start, SB_BLK), :], v_ref[pl.ds(start, SB_BLK), :]
            keep = jnp.logical_or(j != i, strict)
            _, lr0, w0 = _sb_block(qs[0], kj, keep, c0, amat)
            _, lr1, w1 = _sb_block(qs[1], kj, keep, c1, amat)
            pv = jnp.where(lo, _dot(w0, vj), _dot(w1, vj))
            c0 = c0 + jnp.sum(lr0, axis=-1, keepdims=True)
            c1 = c1 + jnp.sum(lr1, axis=-1, keepdims=True)
            return j - 1, jnp.max(jnp.maximum(c0, c1)), acc + pv, c0, c1

        zc = jnp.zeros((SB_BLK, 1), F32)
        st = lax.while_loop(cond, step, (i, jnp.float32(0.0), jnp.zeros((SB_BLK, SB_BLK), F32), zc, zc))
        o_ref[...] = st[2]

    return pl.pallas_call(
        body, name="sb_fwd", grid=(2, nq),
        in_specs=[pl.BlockSpec((SB_BLK, 128), lambda p, i: (i, QA // 128 + p)),
                  pl.BlockSpec((S, 128), lambda p, i: (0, KA // 128 + p)),
                  pl.BlockSpec((S, 128), lambda p, i: (0, VA // 128 + p))],
        out_specs=pl.BlockSpec((SB_BLK, 128), lambda p, i: (i, p)),
        out_shape=jax.ShapeDtypeStruct((S, 256), F32),
        compiler_params=_cp(("parallel", "arbitrary"), 40),
    )(pb, pb, pb)


def _sb_bwd(pb, dycat):
    S = pb.shape[0]
    nq = S // SB_BLK

    def body(q_ref, k_ref, v_ref, do_ref, dq_ref, dk_hbm, dv_hbm, dk_acc, dv_acc, sem):
        p, i = pl.program_id(0), pl.program_id(1)
        row, col = _sb_consts()
        lo = col < HD
        amat = (row > col).astype(_MXU)
        minc = (row >= col).astype(_MXU)
        strict = col < row

        @pl.when(i == 0)
        def _():
            dk_acc[...] = jnp.zeros_like(dk_acc)
            dv_acc[...] = jnp.zeros_like(dv_acc)

        q = q_ref[...]
        qs = [jnp.where(lo, q, 0) * 0.125, jnp.where(lo, 0, q) * 0.125]
        do = do_ref[...]
        dos = [_mx(jnp.where(lo, do, 0.0)), _mx(jnp.where(lo, 0.0, do))]

        def cond(st):
            return jnp.logical_and(st[0] >= 0, st[1] > EXP_ZERO)

        def blocks(j):
            start = pl.multiple_of(j * SB_BLK, SB_BLK)
            return start, k_ref[pl.ds(start, SB_BLK), :], v_ref[pl.ds(start, SB_BLK), :], jnp.logical_or(j != i, strict)

        def sum_step(st):
            j, _, c0, c1, t0, t1 = st
            _, kj, vj, keep = blocks(j)
            out = []
            for h, (cc, tt) in enumerate(((c0, t0), (c1, t1))):
                _, lr, w = _sb_block(qs[h], kj, keep, cc, amat)
                g = _dg(dos[h], vj, NT) * w
                out.append((cc + jnp.sum(lr, axis=-1, keepdims=True), tt + jnp.sum(g, axis=-1, keepdims=True)))
            return j - 1, jnp.max(jnp.maximum(out[0][0], out[1][0])), out[0][0], out[1][0], out[0][1], out[1][1]

        zc = jnp.zeros((SB_BLK, 1), F32)
        tot = lax.while_loop(cond, sum_step, (i, jnp.float32(0.0), zc, zc, zc, zc))[4:]

        def grad_step(st):
            j, _, dq, c0, c1, g0, g1 = st
            start, kj, vj, keep = blocks(j)
            dqs, new = [], []
            dk_blk = jnp.zeros((SB_BLK, 128), F32)
            dv_blk = jnp.zeros((SB_BLK, 128), F32)
            for h, (cc, gg) in enumerate(((c0, g0), (c1, g1))):
                z, lr, w = _sb_block(qs[h], kj, keep, cc, amat)
                beta = jnp.exp(z + lr)
                g = _dg(dos[h], vj, NT) * w
                farther = tot[h] - gg - _dot2(g, minc)
                dz = _mx(jnp.where(keep, g * (1.0 - beta) - farther * beta, 0.0))
                dqs.append(_dot(dz, kj))
                dk_blk = dk_blk + _dg(dz, qs[h], TN)
                dv_blk = dv_blk + _dg(w, dos[h], TN)
                new.append((cc + jnp.sum(lr, axis=-1, keepdims=True), gg + jnp.sum(g, axis=-1, keepdims=True)))
            dk_acc[pl.ds(start, SB_BLK), :] += dk_blk
            dv_acc[pl.ds(start, SB_BLK), :] += dv_blk
            dq = dq + 0.125 * jnp.where(lo, dqs[0], dqs[1])
            return (j - 1, jnp.max(jnp.maximum(new[0][0], new[1][0])), dq, new[0][0], new[1][0], new[0][1], new[1][1])

        st = lax.while_loop(cond, grad_step, (i, jnp.float32(0.0), jnp.zeros((SB_BLK, 128), F32), zc, zc, zc, zc))
        dq_ref[...] = st[2]

        @pl.when(i == nq - 1)
        def _():
            cols = pl.ds(pl.multiple_of(p * 128, 128), 128)
            ck = pltpu.make_async_copy(dk_acc, dk_hbm.at[:, cols], sem.at[0])
            cv = pltpu.make_async_copy(dv_acc, dv_hbm.at[:, cols], sem.at[1])
            ck.start()
            cv.start()
            ck.wait()
            cv.wait()

    return pl.pallas_call(
        body, name="sb_bwd", grid=(2, nq),
        in_specs=[pl.BlockSpec((SB_BLK, 128), lambda p, i: (i, QA // 128 + p)),
                  pl.BlockSpec((S, 128), lambda p, i: (0, KA // 128 + p)),
                  pl.BlockSpec((S, 128), lambda p, i: (0, VA // 128 + p)),
                  pl.BlockSpec((SB_BLK, 128), lambda p, i: (i, p))],
        out_specs=[pl.BlockSpec((SB_BLK, 128), lambda p, i: (i, p)), ANY, ANY],
        out_shape=[jax.ShapeDtypeStruct((S, 256), F32)] * 3,
        scratch_shapes=[pltpu.VMEM((S, 128), F32), pltpu.VMEM((S, 128), F32), pltpu.SemaphoreType.DMA((2,))],
        compiler_params=_cp(("arbitrary", "arbitrary"), 48),
    )(pb, pb, pb, dycat)


def _bias_tiles(rel_bias_l):
    r = np.arange(CH_TQ)[:, None]
    c = np.arange(CH_TQ)[None, :]
    idx, valid = [], []
    for b in range(3):
        dist = 512 - 256 * b + r - c
        idx.append(np.clip(dist, -128, 128) + 128)
        kc = 4 * b + c // 64
        valid.append((kc >= r // 64) & (kc <= r // 64 + 8))
    idx, valid = np.stack(idx), np.stack(valid)
    return jnp.where(valid[None], rel_bias_l[:, idx], NEG), idx


def _ch_scores(i, q_ref, k_refs, b_ref, qw_ref, kw_ref):
    lo = _lo_mask(CH_TQ)
    qn = _half_norm(q_ref[...], qw_ref[...], lo)
    kn = [_mx(_half_norm(kr[...], kw_ref[...], lo)) for kr in k_refs]
    qh = [_mx(jnp.where(lo, qn, 0.0) * 0.125), _mx(jnp.where(lo, 0.0, qn) * 0.125)]
    probs = []
    for h in range(2):
        s = [_dg(qh[h], kn[b], NT) + b_ref[h, b] for b in range(3)]
        s[0] = jnp.where(i >= 2, s[0], NEG)
        s[1] = jnp.where(i >= 1, s[1], NEG)
        m = jnp.maximum(jnp.maximum(jnp.max(s[0], axis=-1, keepdims=True), jnp.max(s[1], axis=-1, keepdims=True)),
                        jnp.max(s[2], axis=-1, keepdims=True))
        e = [jnp.exp(sb - m) for sb in s]
        l = jnp.sum(e[0], axis=-1, keepdims=True) + jnp.sum(e[1], axis=-1, keepdims=True) + jnp.sum(e[2], axis=-1, keepdims=True)
        probs.append((e, l))
    return lo, qn, kn, qh, probs


def _ch_specs(S):
    nt = S // CH_TQ

    def kspec(b, base):
        return pl.BlockSpec((CH_TQ, 128), lambda p, i: (jnp.maximum(i - 2 + b, 0), base // 128 + p))

    specs = [pl.BlockSpec((CH_TQ, 128), lambda p, i: (i, QB // 128 + p))]
    specs += [kspec(b, KB) for b in range(3)] + [kspec(b, VB) for b in range(3)]
    specs += [pl.BlockSpec((2, 3, CH_TQ, CH_TQ), lambda p, i: (p, 0, 0, 0)),
              pl.BlockSpec((1, 128), lambda p, i: (0, 0)), pl.BlockSpec((1, 128), lambda p, i: (0, 0))]
    return nt, specs


def _ch_fwd(pf, pb, bias, qw2, kw2):
    S = pf.shape[0]
    nt, specs = _ch_specs(S)

    def body(q_ref, k0, k1, k2, v0, v1, v2, b_ref, qw_ref, kw_ref, o_ref):
        i = pl.program_id(1)
        lo, _, _, _, probs = _ch_scores(i, q_ref, (k0, k1, k2), b_ref, qw_ref, kw_ref)
        vs = [v0[...], v1[...], v2[...]]
        outs = []
        for h in range(2):
            e, l = probs[h]
            pv = _dot(e[0], vs[0]) + _dot(e[1], vs[1]) + _dot(e[2], vs[2])
            outs.append(pv / l)
        o_ref[...] = jnp.where(lo, outs[0], outs[1])

    return pl.pallas_call(
        body, name="ch_fwd", grid=(4, nt), in_specs=specs,
        out_specs=pl.BlockSpec((CH_TQ, 128), lambda p, i: (i, p)),
        out_shape=jax.ShapeDtypeStruct((S, 512), F32),
        compiler_params=_cp(("parallel", "arbitrary"), 40),
    )(pf, pf, pf, pf, pb, pb, pb, bias, qw2, kw2)


def _ch_bwd(pf, pb, bias, qw2, kw2, dycat):
    S = pf.shape[0]
    nt, specs = _ch_specs(S)
    specs = specs + [pl.BlockSpec((CH_TQ, 128), lambda p, i: (i, 256 // 128 + p))]

    def body(q_ref, k0, k1, k2, v0, v1, v2, b_ref, qw_ref, kw_ref, do_ref, dq_ref, db_ref, dk_hbm, dv_hbm, dk_acc, dv_acc, sem):
        p, i = pl.program_id(0), pl.program_id(1)

        @pl.when(i == 0)
        def _():
            dk_acc[...] = jnp.zeros_like(dk_acc)
            dv_acc[...] = jnp.zeros_like(dv_acc)
            db_ref[...] = jnp.zeros_like(db_ref)

        lo, _, kn, qh, probs = _ch_scores(i, q_ref, (k0, k1, k2), b_ref, qw_ref, kw_ref)
        vs = [v0[...], v1[...], v2[...]]
        do = do_ref[...]
        dos = [_mx(jnp.where(lo, do, 0.0)), _mx(jnp.where(lo, 0.0, do))]
        dqn = []
        dk_b = [jnp.zeros((CH_TQ, 128), F32) for _ in range(3)]
        dv_b = [jnp.zeros((CH_TQ, 128), F32) for _ in range(3)]
        for h in range(2):
            e, l = probs[h]
            inv = 1.0 / l
            pr = [eb * inv for eb in e]
            dp = [_dg(dos[h], vs[b], NT) for b in range(3)]
            delta = sum(jnp.sum(pr[b] * dp[b], axis=-1, keepdims=True) for b in range(3))
            ds = [pr[b] * (dp[b] - delta) for b in range(3)]
            acc = jnp.zeros((CH_TQ, 128), F32)
            for b in range(3):
                db_ref[h, b] += ds[b]
                dsm = _mx(ds[b])
                acc = acc + _dot(dsm, kn[b])
                dk_b[b] = dk_b[b] + _dg(dsm, qh[h], TN)
                dv_b[b] = dv_b[b] + _dg(pr[b], dos[h], TN)
            dqn.append(acc)
        dq_ref[...] = 0.125 * jnp.where(lo, dqn[0], dqn[1])

        for b in range(3):
            @pl.when(i - 2 + b >= 0)
            def _(b=b):
                rows = pl.ds(pl.multiple_of((i - 2 + b) * CH_TQ, CH_TQ), CH_TQ)
                dk_acc[rows, :] += dk_b[b]
                dv_acc[rows, :] += dv_b[b]

        @pl.when(i == nt - 1)
        def _():
            cols = pl.ds(pl.multiple_of(p * 128, 128), 128)
            ck = pltpu.make_async_copy(dk_acc, dk_hbm.at[:, cols], sem.at[0])
            cv = pltpu.make_async_copy(dv_acc, dv_hbm.at[:, cols], sem.at[1])
            ck.start()
            cv.start()
            ck.wait()
            cv.wait()

    return pl.pallas_call(
        body, name="ch_bwd", grid=(4, nt), in_specs=specs,
        out_specs=[pl.BlockSpec((CH_TQ, 128), lambda p, i: (i, p)),
                   pl.BlockSpec((2, 3, CH_TQ, CH_TQ), lambda p, i: (p, 0, 0, 0)), ANY, ANY],
        out_shape=[jax.ShapeDtypeStruct((S, 512), F32), jax.ShapeDtypeStruct((N_HEADS_CH, 3, CH_TQ, CH_TQ), F32),
                   jax.ShapeDtypeStruct((S, 512), F32), jax.ShapeDtypeStruct((S, 512), F32)],
        scratch_shapes=[pltpu.VMEM((S, 128), F32), pltpu.VMEM((S, 128), F32), pltpu.SemaphoreType.DMA((2,))],
        compiler_params=_cp(("arbitrary", "arbitrary"), 48),
    )(pf, pf, pf, pf, pb, pb, pb, bias, qw2, kw2, dycat)


def _relb_fold(dbias, idx):
    def body(d_ref, i_ref, o_ref):
        lane = lax.broadcasted_iota(jnp.int32, (1, 384), 1)
        d = d_ref[0]
        ix = i_ref[...]

        def step(r, acc):
            return jnp.where(lane == r, jnp.sum(jnp.where(ix == r, d, 0.0)), acc)

        o_ref[0] = lax.fori_loop(0, REL, step, jnp.zeros((1, 384), F32))

    return pl.pallas_call(
        body, name="relb_fold", grid=(N_HEADS_CH,),
        in_specs=[pl.BlockSpec((1, 3, CH_TQ, CH_TQ), lambda h: (h, 0, 0, 0)), pl.BlockSpec((3, CH_TQ, CH_TQ), lambda h: (0, 0, 0))],
        out_specs=pl.BlockSpec((1, 1, 384), lambda h: (h, 0, 0)),
        out_shape=jax.ShapeDtypeStruct((N_HEADS_CH, 1, 384), F32),
        compiler_params=_cp(("parallel",), 32),
    )(dbias, idx)


def _conv_fwd(gc, xc, hc_prev, cw_ref):
    hc = gc * xc
    ext = jnp.concatenate([hc_prev, hc], axis=0)
    r1 = pltpu.roll(ext, 1, 0)[8:]
    r2 = pltpu.roll(ext, 2, 0)[8:]
    return cw_ref[0:1, :] * r2 + cw_ref[1:2, :] * r1 + cw_ref[2:3, :] * hc, hc, r1, r2


def _merge_out(x, ysb, ych, pf, cw, onw, w_out):
    S = x.shape[0]
    tm = min(512, S)
    t8 = tm // 8

    def body(x_ref, ysb_ref, ych_ref, gb_ref, gc_ref, xc_ref, gch_ref, xch_ref, cw_ref, onw_ref, w_ref, xo_ref, yn_ref, ycv_ref):
        i = pl.program_id(0)
        lo = _lo_mask(tm)
        hc_prev = jnp.where(i > 0, gch_ref[...] * xch_ref[...], 0.0)
        conv, _, _, _ = _conv_fwd(gc_ref[...], xc_ref[...], hc_prev, cw_ref)
        ycv = gb_ref[...] * conv
        ycv_ref[...] = ycv
        chunks = [ysb_ref[:, 0:128], ysb_ref[:, 128:256]] + [ych_ref[:, 128 * k:128 * (k + 1)] for k in range(4)]
        chunks += [ycv[:, 0:128], ycv[:, 128:256]]
        for k, ch in enumerate(chunks):
            yn_ref[:, 128 * k:128 * (k + 1)] = _half_norm(ch, onw_ref[:, 128 * k:128 * (k + 1)], lo).astype(yn_ref.dtype)
        xo_ref[...] = x_ref[...] + jnp.dot(yn_ref[...], w_ref[...], preferred_element_type=F32)

    row = lambda i: (i, 0)
    halo = lambda cb: pl.BlockSpec((8, 256), lambda i: (jnp.maximum(i * t8 - 1, 0), cb))
    return pl.pallas_call(
        body, name="merge_out", grid=(S // tm,),
        in_specs=[pl.BlockSpec((tm, D), row), pl.BlockSpec((tm, 256), row), pl.BlockSpec((tm, 512), row),
                  pl.BlockSpec((tm, 256), lambda i: (i, GB // 256)), pl.BlockSpec((tm, 256), lambda i: (i, GC // 256)),
                  pl.BlockSpec((tm, 256), lambda i: (i, XC // 256)), halo(GC // 256), halo(XC // 256),
                  pl.BlockSpec((3, 256), lambda i: (0, 0)), pl.BlockSpec((1, D), lambda i: (0, 0)), pl.BlockSpec((D, D), lambda i: (0, 0))],
        out_specs=[pl.BlockSpec((tm, D), row), pl.BlockSpec((tm, D), row), pl.BlockSpec((tm, 256), row)],
        out_shape=[jax.ShapeDtypeStruct((S, D), F32), jax.ShapeDtypeStruct((S, D), _MXU), jax.ShapeDtypeStruct((S, 256), F32)],
        compiler_params=_cp(("parallel",), 40),
    )(x, ysb, ych, pf, pf, pf, pf, pf, cw, onw, w_out)


def _ffn_fwd(x, nw, wg, wu, wd):
    S = x.shape[0]
    tm = min(512, S)

    def body(x_ref, nw_ref, wg_ref, wu_ref, wd_ref, xo_ref, hn_ref, u_ref, t_ref, a_ref, acc):
        j = pl.program_id(1)

        @pl.when(j == 0)
        def _():
            hn_ref[...] = _row_norm(x_ref[...], nw_ref[...]).astype(hn_ref.dtype)
            acc[...] = jnp.zeros_like(acc)

        hn = hn_ref[...]
        u = jnp.dot(hn, wg_ref[0], preferred_element_type=F32)
        t = jnp.dot(hn, wu_ref[0], preferred_element_type=F32)
        a = (u * _sigmoid(u) * t).astype(a_ref.dtype)
        u_ref[0], t_ref[0], a_ref[0] = u, t, a
        acc[...] += jnp.dot(a, wd_ref[0], preferred_element_type=F32)

        @pl.when(j == N_SH - 1)
        def _():
            xo_ref[...] = x_ref[...] + acc[...]

    row = lambda i, j: (i, 0)
    sh = lambda i, j: (j, i, 0)
    return pl.pallas_call(
        body, name="ffn_fwd", grid=(S // tm, N_SH),
        in_specs=[pl.BlockSpec((tm, D), row), pl.BlockSpec((1, D), lambda i, j: (0, 0)),
                  pl.BlockSpec((1, D, FF_SH), lambda i, j: (j, 0, 0)), pl.BlockSpec((1, D, FF_SH), lambda i, j: (j, 0, 0)),
                  pl.BlockSpec((1, FF_SH, D), lambda i, j: (j, 0, 0))],
        out_specs=[pl.BlockSpec((tm, D), row), pl.BlockSpec((tm, D), row), pl.BlockSpec((1, tm, FF_SH), sh),
                   pl.BlockSpec((1, tm, FF_SH), sh), pl.BlockSpec((1, tm, FF_SH), sh)],
        out_shape=[jax.ShapeDtypeStruct((S, D), F32), jax.ShapeDtypeStruct((S, D), _MXU),
                   jax.ShapeDtypeStruct((N_SH, S, FF_SH), F32), jax.ShapeDtypeStruct((N_SH, S, FF_SH), F32),
                   jax.ShapeDtypeStruct((N_SH, S, FF_SH), _MXU)],
        scratch_shapes=[pltpu.VMEM((tm, D), F32)],
        compiler_params=_cp(("parallel", "arbitrary"), 48),
    )(x, nw, wg, wu, wd)


def _loss_head(x, target):
    S = x.shape[0]
    tm = min(512, S)

    def body(x_ref, t_ref, dx_ref, l_ref):
        @pl.when(pl.program_id(0) == 0)
        def _():
            l_ref[...] = jnp.zeros_like(l_ref)

        e = x_ref[...] - t_ref[...]
        dx_ref[...] = e * (1.0 / D)
        l_ref[...] += 0.5 * jnp.sum(jnp.sum(e * e, axis=-1, keepdims=True) * (1.0 / D))

    return pl.pallas_call(
        body, name="loss_head", grid=(S // tm,),
        in_specs=[pl.BlockSpec((tm, D), lambda i: (i, 0))] * 2,
        out_specs=[pl.BlockSpec((tm, D), lambda i: (i, 0)), pl.BlockSpec((8, 128), lambda i: (0, 0))],
        out_shape=[jax.ShapeDtypeStruct((S, D), F32), jax.ShapeDtypeStruct((8, 128), F32)],
        compiler_params=_cp(("arbitrary",), 32),
    )(x, target)


def _ffn_bwd(dx, xmid, nw, wg, wu, wd, u, t):
    S = dx.shape[0]
    tm = min(512, S)

    def body(dx_ref, x_ref, nw_ref, wg_ref, wu_ref, wd_ref, u_ref, t_ref, du_ref, dt_ref, dxm_ref, dnw_ref, acc):
        i, j = pl.program_id(0), pl.program_id(1)

        @pl.when(j == 0)
        def _():
            acc[...] = jnp.zeros_like(acc)

        @pl.when(jnp.logical_and(i == 0, j == 0))
        def _():
            dnw_ref[...] = jnp.zeros_like(dnw_ref)

        da = _dg(dx_ref[...], wd_ref[0], NT)
        u, t = u_ref[0], t_ref[0]
        sg = _sigmoid(u)
        du = _mx(da * t * (sg * (1.0 + u * (1.0 - sg))))
        dt = _mx(da * (u * sg))
        du_ref[0], dt_ref[0] = du, dt
        acc[...] += _dg(du, wg_ref[0], NT) + _dg(dt, wu_ref[0], NT)

        @pl.when(j == N_SH - 1)
        def _():
            dxn, dw = _row_norm_bwd(x_ref[...], nw_ref[...], acc[...])
            dxm_ref[...] = dx_ref[...] + dxn
            dnw_ref[0:1, :] += dw

    row = lambda i, j: (i, 0)
    sh = lambda i, j: (j, i, 0)
    wsp = lambda r, c: pl.BlockSpec((1, r, c), lambda i, j: (j, 0, 0))
    return pl.pallas_call(
        body, name="ffn_bwd", grid=(S // tm, N_SH),
        in_specs=[pl.BlockSpec((tm, D), row), pl.BlockSpec((tm, D), row), pl.BlockSpec((1, D), lambda i, j: (0, 0)),
                  wsp(D, FF_SH), wsp(D, FF_SH), wsp(FF_SH, D), pl.BlockSpec((1, tm, FF_SH), sh), pl.BlockSpec((1, tm, FF_SH), sh)],
        out_specs=[pl.BlockSpec((1, tm, FF_SH), sh), pl.BlockSpec((1, tm, FF_SH), sh), pl.BlockSpec((tm, D), row),
                   pl.BlockSpec((8, D), lambda i, j: (0, 0))],
        out_shape=[jax.ShapeDtypeStruct((N_SH, S, FF_SH), _MXU), jax.ShapeDtypeStruct((N_SH, S, FF_SH), _MXU),
                   jax.ShapeDtypeStruct((S, D), F32), jax.ShapeDtypeStruct((8, D), F32)],
        scratch_shapes=[pltpu.VMEM((tm, D), F32)],
        compiler_params=_cp(("arbitrary", "arbitrary"), 48),
    )(dx, xmid, nw, wg, wu, wd, u, t)


def _wgrad(a, b, a_spec, b_spec, out_block, name):
    S = a.shape[-2]
    tk = min(1024, S)

    def body(a_ref, b_ref, o_ref):
        @pl.when(pl.program_id(1) == 0)
        def _():
            o_ref[...] = jnp.zeros_like(o_ref)

        av = a_ref[0] if len(a_ref.shape) == 3 else a_ref[...]
        bv = b_ref[0] if len(b_ref.shape) == 3 else b_ref[...]
        o_ref[0] += _dg(av, bv, TN)

    return pl.pallas_call(
        body, name=name, grid=(N_SH, S // tk),
        in_specs=[a_spec(tk), b_spec(tk)],
        out_specs=pl.BlockSpec((1,) + out_block, lambda r, k: (r, 0, 0)),
        out_shape=jax.ShapeDtypeStruct((N_SH,) + out_block, F32),
        compiler_params=_cp(("parallel", "arbitrary"), 48),
    )(a, b)


def _tok(width):
    return lambda tk: pl.BlockSpec((tk, width), lambda r, k: (k, 0))


def _tok_cols(width):
    return lambda tk: pl.BlockSpec((tk, width), lambda r, k: (k, r))


def _tok_sh(width):
    return lambda tk: pl.BlockSpec((1, tk, width), lambda r, k: (r, k, 0))


def _out_bwd(dxm, w_out, ysb, ych, ycv, onw):
    S = dxm.shape[0]
    tm = min(512, S)

    def body(dx_ref, w_ref, ysb_ref, ych_ref, ycv_ref, onw_ref, dy_ref, dw_ref):
        @pl.when(pl.program_id(0) == 0)
        def _():
            dw_ref[...] = jnp.zeros_like(dw_ref)

        lo = _lo_mask(tm)
        dyn = _dg(dx_ref[...], w_ref[...], NT)
        chunks = [ysb_ref[:, 0:128], ysb_ref[:, 128:256]] + [ych_ref[:, 128 * k:128 * (k + 1)] for k in range(4)]
        chunks += [ycv_ref[:, 0:128], ycv_ref[:, 128:256]]
        for k, ch in enumerate(chunks):
            sl = slice(128 * k, 128 * (k + 1))
            d, dw = _half_norm_bwd(ch, onw_ref[:, sl], dyn[:, sl], lo)
            dy_ref[:, sl] = d
            dw_ref[0:1, sl] += dw

    row = lambda i: (i, 0)
    return pl.pallas_call(
        body, name="out_bwd", grid=(S // tm,),
        in_specs=[pl.BlockSpec((tm, D), row), pl.BlockSpec((D, D), lambda i: (0, 0)), pl.BlockSpec((tm, 256), row),
                  pl.BlockSpec((tm, 512), row), pl.BlockSpec((tm, 256), row), pl.BlockSpec((1, D), lambda i: (0, 0))],
        out_specs=[pl.BlockSpec((tm, D), row), pl.BlockSpec((8, D), lambda i: (0, 0))],
        out_shape=[jax.ShapeDtypeStruct((S, D), F32), jax.ShapeDtypeStruct((8, D), F32)],
        compiler_params=_cp(("arbitrary",), 40),
    )(dxm, w_out, ysb, ych, ycv, onw)


def _in_bwd(dxm, x, anw, w_sh, pf, dqa, dka, dva, dqh, dkh, dvb, dycat, cw, qw2, kw2):
    S = x.shape[0]
    tm = min(256, S)
    t8 = tm // 8
    last8 = S // 8 - 1

    def body(dxm_ref, x_ref, anw_ref, w_ref, q0_ref, q1_ref, k0_ref, k1_ref, gb_ref, gc_ref, xc_ref, gch_ref, xch_ref, gbn_ref,
             dqa_ref, dka_ref, dva_ref, dqh_ref, dkh_ref, dvb_ref, dyc_ref, dycn_ref, cw_ref, qw_ref, kw_ref,
             dp_ref, dx_ref, danw_ref, dqw_ref, dkw_ref, dcw_ref):
        i = pl.program_id(0)
        n = pl.num_programs(0)

        @pl.when(i == 0)
        def _():
            for r in (danw_ref, dqw_ref, dkw_ref, dcw_ref):
                r[...] = jnp.zeros_like(r)

        lo = _lo_mask(tm)
        dp_ref[:, QA:QA + 256] = dqa_ref[...].astype(dp_ref.dtype)
        dp_ref[:, KA:KA + 256] = dka_ref[...].astype(dp_ref.dtype)
        dp_ref[:, VA:VA + 256] = dva_ref[...].astype(dp_ref.dtype)
        dp_ref[:, VB:VB + 512] = dvb_ref[...].astype(dp_ref.dtype)
        for base, raws, d_ref, nw_ref, dw_ref in ((QB, (q0_ref, q1_ref), dqh_ref, qw_ref, dqw_ref), (KB, (k0_ref, k1_ref), dkh_ref, kw_ref, dkw_ref)):
            for k in range(4):
                raw = raws[k // 2][:, 128 * (k % 2):128 * (k % 2 + 1)]
                d, dw = _half_norm_bwd(raw, nw_ref[...], d_ref[:, 128 * k:128 * (k + 1)], lo)
                dp_ref[:, base + 128 * k:base + 128 * (k + 1)] = d.astype(dp_ref.dtype)
                dw_ref[0:1, :] += dw
        gb, gc, xc = gb_ref[...], gc_ref[...], xc_ref[...]
        hc_prev = jnp.where(i > 0, gch_ref[...] * xch_ref[...], 0.0)
        conv, hc, r1, r2 = _conv_fwd(gc, xc, hc_prev, cw_ref)
        dyc = dyc_ref[...]
        dconv = dyc * gb
        dconv_next = jnp.where(i < n - 1, dycn_ref[...] * gbn_ref[...], 0.0)
        ext = jnp.concatenate([dconv, dconv_next], axis=0)
        l1 = pltpu.roll(ext, tm + 7, 0)[:tm]
        l2 = pltpu.roll(ext, tm + 6, 0)[:tm]
        dhc = cw_ref[2:3, :] * dconv + cw_ref[1:2, :] * l1 + cw_ref[0:1, :] * l2
        dp_ref[:, GB:GB + 256] = (dyc * conv).astype(dp_ref.dtype)
        dp_ref[:, GC:GC + 256] = (dhc * xc).astype(dp_ref.dtype)
        dp_ref[:, XC:XC + 256] = (dhc * gc).astype(dp_ref.dtype)
        dcw_ref[0:1, :] += jnp.sum(dconv * r2, axis=0, keepdims=True)
        dcw_ref[1:2, :] += jnp.sum(dconv * r1, axis=0, keepdims=True)
        dcw_ref[2:3, :] += jnp.sum(dconv * hc, axis=0, keepdims=True)
        dh = jnp.zeros((tm, D), F32)
        for r in range(N_SH):
            dh = dh + lax.dot_general(dp_ref[:, IN_SH * r:IN_SH * (r + 1)], w_ref[r], NT, preferred_element_type=F32)
        dxn, dw = _row_norm_bwd(x_ref[...], anw_ref[...], dh)
        dx_ref[...] = dxm_ref[...] + dxn
        danw_ref[0:1, :] += dw

    row = lambda i: (i, 0)
    colb = lambda w, cb: pl.BlockSpec((tm, w), lambda i: (i, cb))
    prev = lambda cb: pl.BlockSpec((8, 256), lambda i: (jnp.maximum(i * t8 - 1, 0), cb))
    nxt = lambda cb: pl.BlockSpec((8, 256), lambda i: (jnp.minimum((i + 1) * t8, last8), cb))
    const = lambda shp: pl.BlockSpec(shp, lambda i: (0,) * len(shp))
    return pl.pallas_call(
        body, name="in_bwd", grid=(S // tm,),
        in_specs=[pl.BlockSpec((tm, D), row), pl.BlockSpec((tm, D), row), const((1, D)), const((N_SH, D, IN_SH)),
                  colb(256, QB // 256), colb(256, QB // 256 + 1), colb(256, KB // 256), colb(256, KB // 256 + 1),
                  colb(256, GB // 256), colb(256, GC // 256), colb(256, XC // 256), prev(GC // 256), prev(XC // 256), nxt(GB // 256),
                  pl.BlockSpec((tm, 256), row), pl.BlockSpec((tm, 256), row), pl.BlockSpec((tm, 256), row),
                  pl.BlockSpec((tm, 512), row), pl.BlockSpec((tm, 512), row), pl.BlockSpec((tm, 512), row),
                  colb(256, 768 // 256), nxt(768 // 256), const((3, 256)), const((1, 128)), const((1, 128))],
        out_specs=[pl.BlockSpec((tm, D_IN), row), pl.BlockSpec((tm, D), row), const((8, D)), const((8, 128)), const((8, 128)), const((8, 256))],
        out_shape=[jax.ShapeDtypeStruct((S, D_IN), _MXU), jax.ShapeDtypeStruct((S, D), F32), jax.ShapeDtypeStruct((8, D), F32),
                   jax.ShapeDtypeStruct((8, 128), F32), jax.ShapeDtypeStruct((8, 128), F32), jax.ShapeDtypeStruct((8, 256), F32)],
        compiler_params=_cp(("arbitrary",), 56),
    )(dxm, x, anw, w_sh, pf, pf, pf, pf, pf, pf, pf, pf, pf, pf, dqa, dka, dva, dqh, dkh, dvb, dycat, dycat, cw, qw2, kw2)


def _rows_tile(h):
    return h if h <= 512 else 256


def _add_halves(g, r1, c_arr):
    L, _, R, C = g.shape
    h = R // 2
    tr = _rows_tile(h)
    nb = h // tr

    def body(c_ref, g_ref, r_ref, o_ref):
        o_ref[...] = g_ref[...] + r_ref[...]

    blk = (1, 1, tr, C)
    return pl.pallas_call(
        body, name="add_halves",
        grid_spec=pltpu.PrefetchScalarGridSpec(
            num_scalar_prefetch=1, grid=(L, N_SH, nb),
            in_specs=[pl.BlockSpec(blk, lambda l, s, i, c: (l, s, c[0] * nb + i, 0)), pl.BlockSpec(blk, lambda l, s, i, c: (l, s, i, 0))],
            out_specs=pl.BlockSpec(blk, lambda l, s, i, c: (l, s, i, 0))),
        out_shape=jax.ShapeDtypeStruct((L, N_SH, h, C), F32),
        compiler_params=_cp(("parallel", "parallel", "parallel"), 32),
    )(c_arr, g, r1)


def _add_shards(s1, r2, me_arr):
    L, _, h, C = s1.shape
    tr = _rows_tile(h)

    def body(me_ref, s_ref, r_ref, o_ref):
        o_ref[0] = ((s_ref[0, 0] + r_ref[0, 0]) + r_ref[1, 0]) + r_ref[2, 0]

    return pl.pallas_call(
        body, name="add_shards",
        grid_spec=pltpu.PrefetchScalarGridSpec(
            num_scalar_prefetch=1, grid=(L, h // tr),
            in_specs=[pl.BlockSpec((1, 1, tr, C), lambda l, i, me: (l, me[0], i, 0)), pl.BlockSpec((3, 1, tr, C), lambda l, i, me: (0, l, i, 0))],
            out_specs=pl.BlockSpec((1, tr, C), lambda l, i, me: (l, i, 0))),
        out_shape=jax.ShapeDtypeStruct((L, h, C), F32),
        compiler_params=_cp(("parallel", "parallel"), 32),
    )(me_arr, s1, r2)


def _adam_math(w, g, m, v):
    m = ADAM_B1 * m + (1.0 - ADAM_B1) * g
    v = ADAM_B2 * v + (1.0 - ADAM_B2) * (g * g)
    m_hat = m / (1.0 - ADAM_B1 ** ADAM_STEP)
    v_hat = v / (1.0 - ADAM_B2 ** ADAM_STEP)
    return -ADAM_LR * (m_hat / (jnp.sqrt(v_hat) + ADAM_EPS) + ADAM_WD * w), m, v


def _adam_big(w, m, v, mine, theirs, c_arr):
    L, R, C = w.shape
    h = R // 2
    tr = _rows_tile(h)
    nb = h // tr

    def body(c_ref, w_ref, m_ref, v_ref, a_ref, b_ref, g_ref, d_ref, mo_ref, vo_ref):
        g = jnp.where(pl.program_id(1) == c_ref[0], a_ref[...], b_ref[...])
        g_ref[...] = g
        d_ref[...], mo_ref[...], vo_ref[...] = _adam_math(w_ref[...], g, m_ref[...], v_ref[...])

    full = pl.BlockSpec((1, tr, C), lambda l, s, i, c: (l, s * nb + i, 0))
    half = pl.BlockSpec((1, tr, C), lambda l, s, i, c: (l, i, 0))
    return pl.pallas_call(
        body, name="adam_big",
        grid_spec=pltpu.PrefetchScalarGridSpec(num_scalar_prefetch=1, grid=(L, 2, nb), in_specs=[full, full, full, half, half],
                                               out_specs=[full] * 4),
        out_shape=[jax.ShapeDtypeStruct((L, R, C), F32)] * 4,
        compiler_params=_cp(("parallel", "parallel", "parallel"), 32),
    )(c_arr, w, m, v, mine, theirs)


def _adam_small(w, g, m, v):
    def body(w_ref, g_ref, m_ref, v_ref, d_ref, mo_ref, vo_ref):
        d_ref[...], mo_ref[...], vo_ref[...] = _adam_math(w_ref[...], g_ref[...], m_ref[...], v_ref[...])

    return pl.pallas_call(body, name="adam_small", out_shape=[jax.ShapeDtypeStruct(w.shape, F32)] * 3)(w, g, m, v)


def _allreduce_small(buf):
    R = buf.shape[0]

    def body(x_ref, o_ref, gat, ssem, rsem):
        x, y, c = lax.axis_index("x"), lax.axis_index("y"), lax.axis_index("c")
        me = 4 * x + 2 * y + c
        gat[me] = x_ref[...]
        cps = []
        for k in range(1, 8):
            peer = (x ^ (k >> 2), y ^ ((k >> 1) & 1), c ^ (k & 1))
            cps.append(pltpu.make_async_remote_copy(src_ref=x_ref, dst_ref=gat.at[me], send_sem=ssem.at[k - 1], recv_sem=rsem.at[k - 1],
                                                    device_id=peer, device_id_type=MESH))
        for cp in cps:
            cp.start()
        for cp in cps:
            cp.wait()
        acc = gat[0]
        for s in range(1, 8):
            acc = acc + gat[s]
        o_ref[...] = acc

    return pl.pallas_call(
        body, name="allreduce_small", out_shape=jax.ShapeDtypeStruct((R, 128), F32),
        in_specs=[pl.BlockSpec(memory_space=pltpu.VMEM)], out_specs=pl.BlockSpec(memory_space=pltpu.VMEM),
        scratch_shapes=[pltpu.VMEM((8, R, 128), F32), pltpu.SemaphoreType.DMA((7,)), pltpu.SemaphoreType.DMA((7,))],
    )(buf)


def _reduce_scatter(grads):
    x, y, c = lax.axis_index("x"), lax.axis_index("y"), lax.axis_index("c")
    c_arr = jnp.reshape(c, (1,)).astype(jnp.int32)
    me_arr = jnp.reshape(2 * x + y, (1,)).astype(jnp.int32)
    halves = [g.shape[2] // 2 for g in grads]

    n = len(grads)

    def body1(*refs):
        ins, outs = refs[:n], refs[n:2 * n]
        ssem, rsem = refs[2 * n:]
        x, y, c = lax.axis_index("x"), lax.axis_index("y"), lax.axis_index("c")
        cps = []
        for f in range(n):
            h = halves[f]
            src = ins[f].at[:, :, pl.ds(pl.multiple_of((1 - c) * h, 8), h), :]
            cps.append(pltpu.make_async_remote_copy(src_ref=src, dst_ref=outs[f], send_sem=ssem.at[f], recv_sem=rsem.at[f],
                                                    device_id=(x, y, 1 - c), device_id_type=MESH))
        for cp in cps:
            cp.start()
        for cp in cps:
            cp.wait()

    r1 = pl.pallas_call(
        body1, name="rs_sibling_halves",
        out_shape=[jax.ShapeDtypeStruct(g.shape[:2] + (h, g.shape[3]), F32) for g, h in zip(grads, halves)],
        in_specs=[ANY] * n, out_specs=[ANY] * n,
        scratch_shapes=[pltpu.SemaphoreType.DMA((n,)), pltpu.SemaphoreType.DMA((n,))],
    )(*grads)
    s1 = [_add_halves(g, r, c_arr) for g, r in zip(grads, r1)]

    def body2(*refs):
        ins, outs = refs[:n], refs[n:2 * n]
        ssem, rsem = refs[2 * n:]
        x, y, c = lax.axis_index("x"), lax.axis_index("y"), lax.axis_index("c")
        peers = [(1 - x, y), (x, 1 - y), (1 - x, 1 - y)]
        cps = []
        for f in range(n):
            for k, (px, py) in enumerate(peers):
                cps.append(pltpu.make_async_remote_copy(src_ref=ins[f].at[:, 2 * px + py], dst_ref=outs[f].at[k], send_sem=ssem.at[3 * f + k],
                                                        recv_sem=rsem.at[3 * f + k], device_id=(px, py, c), device_id_type=MESH))
        for cp in cps:
            cp.start()
        for cp in cps:
            cp.wait()

    r2 = pl.pallas_call(
        body2, name="rs_chip_shards",
        out_shape=[jax.ShapeDtypeStruct((3, s.shape[0]) + s.shape[2:], F32) for s in s1],
        in_specs=[ANY] * n, out_specs=[ANY] * n,
        scratch_shapes=[pltpu.SemaphoreType.DMA((3 * n,)), pltpu.SemaphoreType.DMA((3 * n,))],
    )(*s1)
    mine = [_add_shards(s, r, me_arr) for s, r in zip(s1, r2)]

    def body3(*refs):
        ins, outs = refs[:n], refs[n:2 * n]
        ssem, rsem = refs[2 * n:]
        x, y, c = lax.axis_index("x"), lax.axis_index("y"), lax.axis_index("c")
        cps = [pltpu.make_async_remote_copy(src_ref=ins[f], dst_ref=outs[f], send_sem=ssem.at[f], recv_sem=rsem.at[f],
                                            device_id=(x, y, 1 - c), device_id_type=MESH) for f in range(n)]
        for cp in cps:
            cp.start()
        for cp in cps:
            cp.wait()

    theirs = pl.pallas_call(
        body3, name="rs_sibling_reduced",
        out_shape=[jax.ShapeDtypeStruct(m.shape, F32) for m in mine],
        in_specs=[ANY] * n, out_specs=[ANY] * n,
        scratch_shapes=[pltpu.SemaphoreType.DMA((n,)), pltpu.SemaphoreType.DMA((n,))],
    )(*mine)
    return list(zip(mine, theirs)), c_arr


def _pad_rows(flat, rows):
    return jnp.pad(flat, (0, rows * 128 - flat.shape[0])).reshape(rows, 128)


def _local_step(x, target, small, big):
    S = x.shape[0]
    saved = []
    for l in range(DEPTH):
        W = big[l]
        anw, onw, fnw = small["attn_norm_w"][l][None], small["out_norm_w"][l][None], small["ffn_norm_w"][l][None]
        qw2 = jnp.tile(small["q_norm_w"][l], 2)[None]
        kw2 = jnp.tile(small["k_norm_w"][l], 2)[None]
        cw = small["conv_w"][l]
        bias, idx = _bias_tiles(small["rel_bias"][l])
        h, pf, pb = _attn_in(x, anw, W["w_in"])
        ysb = _sb_fwd(pb)
        ych = _ch_fwd(pf, pb, bias, qw2, kw2)
        w_out = W["w_out"].reshape(D, D)
        xmid, yn, ycv = _merge_out(x, ysb, ych, pf, cw, onw, w_out)
        xo, hn, u, t, a = _ffn_fwd(xmid, fnw, W["w_gate"], W["w_up"], W["w_down"])
        saved.append(dict(x=x, h=h, pf=pf, pb=pb, ysb=ysb, ych=ych, ycv=ycv, yn=yn, xmid=xmid, hn=hn, u=u, t=t, a=a,
                          anw=anw, onw=onw, fnw=fnw, qw2=qw2, kw2=kw2, cw=cw, bias=bias, idx=idx, w_out=w_out))
        x = xo
    dx, lpart = _loss_head(x, target)
    loss = lpart[0, 0]
    grads = [None] * DEPTH
    for l in reversed(range(DEPTH)):
        W, sv = big[l], saved[l]
        du, dt, dxm, dfnw = _ffn_bwd(dx, sv["xmid"], sv["fnw"], W["w_gate"], W["w_up"], W["w_down"], sv["u"], sv["t"])
        g_gate = _wgrad(sv["hn"], du, _tok(D), _tok_sh(FF_SH), (D, FF_SH), "wgrad_gate")
        g_up = _wgrad(sv["hn"], dt, _tok(D), _tok_sh(FF_SH), (D, FF_SH), "wgrad_up")
        g_down = _wgrad(sv["a"], dx, _tok_sh(FF_SH), _tok(D), (FF_SH, D), "wgrad_down")
        dycat, donw = _out_bwd(dxm, sv["w_out"], sv["ysb"], sv["ych"], sv["ycv"], sv["onw"])
        g_out = _wgrad(sv["yn"], dxm, _tok_cols(OUT_SH), _tok(D), (OUT_SH, D), "wgrad_out")
        dqa, dka, dva = _sb_bwd(sv["pb"], dycat)
        dqh, dbias, dkh, dvb = _ch_bwd(sv["pf"], sv["pb"], sv["bias"], sv["qw2"], sv["kw2"], dycat)
        drelb = _relb_fold(dbias, jnp.asarray(sv["idx"], jnp.int32))[:, 0, :REL]
        dproj, dx, danw, dqw, dkw, dcw = _in_bwd(dxm, sv["x"], sv["anw"], W["w_in"], sv["pf"], dqa, dka, dva, dqh, dkh, dvb, dycat,
                                                 sv["cw"], sv["qw2"], sv["kw2"])
        g_in = _wgrad(sv["h"], dproj, _tok(D), _tok_cols(IN_SH), (D, IN_SH), "wgrad_in")
        grads[l] = dict(w_in=g_in, w_out=g_out, w_gate=g_gate, w_up=g_up, w_down=g_down,
                        attn_norm_w=danw[0], out_norm_w=donw[0], ffn_norm_w=dfnw[0],
                        q_norm_w=dqw[0, :HD] + dqw[0, HD:], k_norm_w=dkw[0, :HD] + dkw[0, HD:], rel_bias=drelb, conv_w=dcw[:3])
    return loss, dx, grads


_BIG = ("w_in", "w_out", "w_gate", "w_up", "w_down")
_SMALL = ("attn_norm_w", "q_norm_w", "k_norm_w", "rel_bias", "out_norm_w", "ffn_norm_w")
_ORDER = ("attn_norm_w", "w_in", "q_norm_w", "k_norm_w", "rel_bias", "conv_w", "out_norm_w", "w_out", "ffn_norm_w", "w_gate", "w_up", "w_down")


def kernel(x, attn_norm_w, w_in, q_norm_w, k_norm_w, rel_bias, conv_w, out_norm_w, w_out, ffn_norm_w, w_gate, w_up, w_down, loss_target, m_attn_norm_w, m_w_in, m_q_norm_w, m_k_norm_w, m_rel_bias, m_conv_w, m_out_norm_w, m_w_out, m_ffn_norm_w, m_w_gate, m_w_up, m_w_down, v_attn_norm_w, v_w_in, v_q_norm_w, v_k_norm_w, v_rel_bias, v_conv_w, v_out_norm_w, v_w_out, v_ffn_norm_w, v_w_gate, v_w_up, v_w_down):
    w = dict(attn_norm_w=attn_norm_w, w_in=w_in, q_norm_w=q_norm_w, k_norm_w=k_norm_w, rel_bias=rel_bias, conv_w=conv_w,
             out_norm_w=out_norm_w, w_out=w_out, ffn_norm_w=ffn_norm_w, w_gate=w_gate, w_up=w_up, w_down=w_down)
    m = dict(attn_norm_w=m_attn_norm_w, w_in=m_w_in, q_norm_w=m_q_norm_w, k_norm_w=m_k_norm_w, rel_bias=m_rel_bias, conv_w=m_conv_w,
             out_norm_w=m_out_norm_w, w_out=m_w_out, ffn_norm_w=m_ffn_norm_w, w_gate=m_w_gate, w_up=m_w_up, w_down=m_w_down)
    v = dict(attn_norm_w=v_attn_norm_w, w_in=v_w_in, q_norm_w=v_q_norm_w, k_norm_w=v_k_norm_w, rel_bias=v_rel_bias, conv_w=v_conv_w,
             out_norm_w=v_out_norm_w, w_out=v_w_out, ffn_norm_w=v_ffn_norm_w, w_gate=v_w_gate, w_up=v_w_up, w_down=v_w_down)
    xi, yi = lax.axis_index("x"), lax.axis_index("y")
    me = 2 * xi + yi

    big, conv_full = [], []
    for l in range(DEPTH):
        cpad = jnp.zeros((8, 128), F32).at[:3, :HD].set(conv_w[l])
        parts = [w[k][l].astype(_MXU) for k in _BIG] + [cpad]
        got = _allgather_layer(parts)
        big.append(dict(zip(_BIG, got[:5])))
        conv_full.append(jnp.concatenate([got[5][r, :3, :HD] for r in range(N_SH)], axis=-1))
    small = {k: w[k] for k in _SMALL}
    small["conv_w"] = jnp.stack(conv_full)

    loss, grad_x, grads = _local_step(x[0], loss_target[0], small, big)
    loss = lax.psum(loss, ("x", "y", "c"))

    stacked = [jnp.stack([grads[l][k] for l in range(DEPTH)]) for k in _BIG]
    reduced, c_arr = _reduce_scatter(stacked)
    out = {}
    for k, (mine, theirs) in zip(_BIG, reduced):
        out[k] = _adam_big(w[k], m[k], v[k], mine, theirs, c_arr)

    names = _SMALL + ("conv_w",)
    gflat = jnp.concatenate([jnp.stack([grads[l][k] for l in range(DEPTH)]).reshape(-1) for k in names])
    rows = -(-gflat.shape[0] // 1024) * 8
    gsum = _allreduce_small(_pad_rows(gflat, rows)).reshape(-1)
    gs, off = {}, 0
    for k in names:
        shp = (DEPTH, 3, 4 * HD) if k == "conv_w" else w[k].shape
        size = int(np.prod(shp))
        gs[k] = gsum[off:off + size].reshape(shp)
        off += size
    gs["conv_w"] = lax.dynamic_slice_in_dim(gs["conv_w"], me * HD, HD, axis=2)
    flat = lambda d: jnp.concatenate([d[k].reshape(-1) for k in names])
    rows = -(-flat(w).shape[0] // 1024) * 8
    dsm, msm, vsm = _adam_small(_pad_rows(flat(w), rows), _pad_rows(flat(gs), rows), _pad_rows(flat(m), rows), _pad_rows(flat(v), rows))
    off = 0
    for k in names:
        size = int(np.prod(w[k].shape))
        out[k] = (gs[k],) + tuple(a.reshape(-1)[off:off + size].reshape(w[k].shape) for a in (dsm, msm, vsm))
        off += size

    return (loss, grad_x[None]) + tuple(out[k][j] for j in range(4) for k in _ORDER)
```

```python
import functools

import numpy as np
import jax
import jax.numpy as jnp
from jax import lax
from jax.experimental import pallas as pl
from jax.experimental.pallas import tpu as pltpu

F32 = jnp.float32
_MXU = jnp.bfloat16

D = 1024
DEPTH = 4
HD = 64
D_IN = 3072
N_SH = 4
IN_SH = D_IN // N_SH
FF_SH = 704
OUT_SH = 256
N_HEADS_CH = 8
REL = 257
QA, KA, VA, QB, KB, VB, GB, GC, XC = 0, 256, 512, 768, 1280, 1792, 2304, 2560, 2816
EPS = 1e-6
NEG = -1e30
SB_BLK = 128
CH_TQ = 256
EXP_ZERO = -104.0

ADAM_LR, ADAM_B1, ADAM_B2, ADAM_EPS, ADAM_WD, ADAM_STEP = 0.001, 0.9, 0.999, 1e-08, 0.01, 10

NT = (((1,), (1,)), ((), ()))
TN = (((0,), (0,)), ((), ()))
MESH = pl.DeviceIdType.MESH
ANY = pl.BlockSpec(memory_space=pl.ANY)
MB = 1 << 20


def _cp(sem=None, vmem_mb=None):
    return pltpu.CompilerParams(dimension_semantics=sem, vmem_limit_bytes=None if vmem_mb is None else vmem_mb * MB)


def _mx(a):
    return a.astype(_MXU)


def _dot(a, b):
    return jnp.dot(_mx(a), _mx(b), preferred_element_type=F32)


def _dg(a, b, dims):
    return lax.dot_general(_mx(a), _mx(b), dims, preferred_element_type=F32)


def _dot2(a, m):
    hi = _mx(a)
    r = jnp.dot(hi, m, preferred_element_type=F32)
    if _MXU != F32:
        r = r + jnp.dot(_mx(a - hi.astype(F32)), m, preferred_element_type=F32)
    return r


def _sigmoid(u):
    return 1.0 / (1.0 + jnp.exp(-u))


def _row_norm(x, w):
    r = lax.rsqrt(jnp.mean(x * x, axis=-1, keepdims=True) + EPS)
    return x * r * w


def _row_norm_bwd(x, w, dy):
    r = lax.rsqrt(jnp.mean(x * x, axis=-1, keepdims=True) + EPS)
    xh = x * r
    dw = jnp.sum(dy * xh, axis=0, keepdims=True)
    dxh = dy * w
    return r * (dxh - xh * jnp.mean(dxh * xh, axis=-1, keepdims=True)), dw


def _half_rs(x, lo):
    sq = x * x
    s0 = jnp.sum(jnp.where(lo, sq, 0.0), axis=-1, keepdims=True)
    s1 = jnp.sum(jnp.where(lo, 0.0, sq), axis=-1, keepdims=True)
    return jnp.where(lo, lax.rsqrt(s0 * (1.0 / HD) + EPS), lax.rsqrt(s1 * (1.0 / HD) + EPS))


def _half_norm(x, w, lo):
    return x * _half_rs(x, lo) * w


def _half_norm_bwd(x, w, dy, lo):
    r = _half_rs(x, lo)
    xh = x * r
    dw = jnp.sum(dy * xh, axis=0, keepdims=True)
    dxh = dy * w
    pr = dxh * xh
    m0 = jnp.sum(jnp.where(lo, pr, 0.0), axis=-1, keepdims=True)
    m1 = jnp.sum(jnp.where(lo, 0.0, pr), axis=-1, keepdims=True)
    mean = jnp.where(lo, m0, m1) * (1.0 / HD)
    return r * (dxh - xh * mean), dw


def _lo_mask(t):
    return lax.broadcasted_iota(jnp.int32, (t, 128), 1) < HD


def _allgather_layer(parts, split):
    n = len(parts)

    def body(*refs):
        ins, outs = refs[:n], refs[n:2 * n]
        lsem, ssem, rsem, fsem, gsem = refs[2 * n:]
        x, y, c = lax.axis_index("x"), lax.axis_index("y"), lax.axis_index("c")
        me = 2 * x + y
        chips = [(1 - x, y), (x, 1 - y), (1 - x, 1 - y)]

        def half(f, which):
            h = parts[f].shape[0] // 2
            return pl.ds(pl.multiple_of(which * h, 16), h)

        local = [pltpu.make_async_copy(ins[f], outs[f].at[me], lsem.at[f]) for f in range(n)]
        for cp in local:
            cp.start()
        sends = []
        for f in range(n):
            for k, (px, py) in enumerate(chips):
                src, dst = (ins[f].at[half(f, c)], outs[f].at[me, half(f, c)]) if split[f] else (ins[f], outs[f].at[me])
                sends.append(pltpu.make_async_remote_copy(src_ref=src, dst_ref=dst, send_sem=ssem.at[3 * f + k], recv_sem=rsem.at[3 * f + k],
                                                          device_id=(px, py, c), device_id_type=MESH))
        for cp in sends:
            cp.start()
        passed = []
        for f in range(n):
            for k, (px, py) in enumerate(chips):
                if split[f]:
                    got = outs[f].at[2 * px + py, half(f, c)]
                    sends[3 * f + k].wait_recv()
                    cp = pltpu.make_async_remote_copy(src_ref=got, dst_ref=got, send_sem=fsem.at[3 * f + k], recv_sem=gsem.at[3 * f + k],
                                                      device_id=(x, y, 1 - c), device_id_type=MESH)
                    cp.start()
                    passed.append(cp)
                else:
                    sends[3 * f + k].wait_recv()
        for cp in passed:
            cp.wait_recv()
        for cp in sends + passed:
            cp.wait_send()
        for cp in local:
            cp.wait()

    return pl.pallas_call(
        body, name="allgather_layer",
        out_shape=[jax.ShapeDtypeStruct((N_SH,) + p.shape, p.dtype) for p in parts],
        in_specs=[ANY] * n, out_specs=[ANY] * n,
        scratch_shapes=[pltpu.SemaphoreType.DMA((n,))] + [pltpu.SemaphoreType.DMA((3 * n,))] * 4,
    )(*parts)


def _attn_in(x, nw, w_sh):
    S = x.shape[0]
    tm = min(512, S)

    def body(x_ref, nw_ref, w_ref, h_ref, pf_ref, pb_ref):
        @pl.when(pl.program_id(1) == 0)
        def _():
            h_ref[...] = _row_norm(x_ref[...], nw_ref[...]).astype(h_ref.dtype)

        p = jnp.dot(h_ref[...], w_ref[0], preferred_element_type=F32)
        pf_ref[...] = p
        pb_ref[...] = p.astype(pb_ref.dtype)

    return pl.pallas_call(
        body, name="attn_in", grid=(S // tm, N_SH),
        in_specs=[pl.BlockSpec((tm, D), lambda i, j: (i, 0)), pl.BlockSpec((1, D), lambda i, j: (0, 0)),
                  pl.BlockSpec((1, D, IN_SH), lambda i, j: (j, 0, 0))],
        out_specs=[pl.BlockSpec((tm, D), lambda i, j: (i, 0)), pl.BlockSpec((tm, IN_SH), lambda i, j: (i, j)),
                   pl.BlockSpec((tm, IN_SH), lambda i, j: (i, j))],
        out_shape=[jax.ShapeDtypeStruct((S, D), _MXU), jax.ShapeDtypeStruct((S, D_IN), F32), jax.ShapeDtypeStruct((S, D_IN), _MXU)],
        compiler_params=_cp(("parallel", "arbitrary"), 40),
    )(x, nw, w_sh)


SB_NB = 3
SB_TK = SB_NB * SB_BLK


class _SbConsts:
    def __init__(self):
        r = lax.broadcasted_iota(jnp.int32, (SB_TK, 256), 0)
        c = lax.broadcasted_iota(jnp.int32, (SB_TK, 256), 1)
        self.kofs = (r - (r & 127)) + (c & 127)
        self.qofs = r & 127
        r2 = lax.broadcasted_iota(jnp.int32, (256, 256), 0)
        c2 = lax.broadcasted_iota(jnp.int32, (256, 256), 1)
        same = (r2 >= 128) == (c2 >= 128)
        self.nearer = jnp.logical_and(same, (r2 & 127) > (c2 & 127)).astype(_MXU)
        self.at_or_nearer = jnp.logical_and(same, (r2 & 127) >= (c2 & 127)).astype(_MXU)
        self.lo = lax.broadcasted_iota(jnp.int32, (SB_BLK, 128), 1) < HD
        self.head0 = lax.broadcasted_iota(jnp.int32, (SB_BLK, 256), 1) < 128


def _two_heads(x, lo):
    return jnp.concatenate([jnp.where(lo, x, 0), jnp.where(lo, 0, x)], axis=0)


def _per_block(vals, head0):
    return jnp.concatenate([jnp.where(head0, v0, v1) for v0, v1 in vals], axis=0)


def _head_sums(x):
    return jnp.sum(x[:, :128], axis=-1, keepdims=True), jnp.sum(x[:, 128:], axis=-1, keepdims=True)


def _sb_tile(q2, k_ref, i, jt, c0, c1, K):
    s0 = pl.multiple_of(jnp.maximum(jt - (SB_NB - 1), 0) * SB_BLK, SB_BLK)
    kt = k_ref[pl.ds(s0, SB_TK), :]
    k2 = [_two_heads(kt[b * SB_BLK:(b + 1) * SB_BLK], K.lo) for b in range(SB_NB)]
    z = jnp.concatenate([_dg(q2, k2[b], NT) for b in range(SB_NB)], axis=0)
    kpos = K.kofs + s0
    keep = jnp.logical_and(kpos < K.qofs + i * SB_BLK, kpos < (jt + 1) * SB_BLK)
    lr = jnp.where(keep, -(jnp.maximum(z, 0.0) + jnp.log1p(jnp.exp(-jnp.abs(z)))), 0.0)
    carries = [None] * SB_NB
    for b in reversed(range(SB_NB)):
        carries[b] = (c0, c1)
        s = _head_sums(lr[b * SB_BLK:(b + 1) * SB_BLK])
        c0, c1 = c0 + s[0], c1 + s[1]
    after = _dot2(lr, K.nearer) + _per_block(carries, K.head0)
    w = jnp.where(keep, jnp.exp(z + lr + after), 0.0)
    return s0, k2, z, lr, w, keep, c0, c1


def _sb_fwd(pb):
    S = pb.shape[0]
    nq = S // SB_BLK

    def body(q_ref, k_ref, v_ref, o_ref):
        i = pl.program_id(1)
        K = _SbConsts()
        q2 = q_ref[...] * 0.125

        def cond(st):
            return jnp.logical_and(st[0] >= 0, st[1] > EXP_ZERO)

        def step(st):
            jt, _, acc, c0, c1 = st
            s0, _, _, _, w, _, c0, c1 = _sb_tile(q2, k_ref, i, jt, c0, c1, K)
            vt = v_ref[pl.ds(s0, SB_TK), :]
            for b in range(SB_NB):
                rows = slice(b * SB_BLK, (b + 1) * SB_BLK)
                acc = acc + _dot(w[rows], _two_heads(vt[rows], K.lo))
            return jt - SB_NB, jnp.max(jnp.maximum(c0, c1)), acc, c0, c1

        zc = jnp.zeros((SB_BLK, 1), F32)
        st = lax.while_loop(cond, step, (i, jnp.float32(0.0), jnp.zeros((SB_BLK, 128), F32), zc, zc))
        o_ref[...] = st[2]

    return pl.pallas_call(
        body, name="sb_fwd", grid=(2, nq),
        in_specs=[pl.BlockSpec((SB_BLK, 128), lambda p, i: (i, QA // 128 + p)),
                  pl.BlockSpec((S, 128), lambda p, i: (0, KA // 128 + p)),
                  pl.BlockSpec((S, 128), lambda p, i: (0, VA // 128 + p))],
        out_specs=pl.BlockSpec((SB_BLK, 128), lambda p, i: (i, p)),
        out_shape=jax.ShapeDtypeStruct((S, 256), F32),
        compiler_params=_cp(("parallel", "arbitrary"), 40),
    )(pb, pb, pb)


def _sb_bwd(pb, dycat):
    S = pb.shape[0]
    nq = S // SB_BLK

    def body(q_ref, k_ref, v_ref, do_ref, dq_ref, dk_hbm, dv_hbm, dk_acc, dv_acc, sem):
        p, i = pl.program_id(0), pl.program_id(1)
        K = _SbConsts()

        @pl.when(i == 0)
        def _():
            dk_acc[...] = jnp.zeros_like(dk_acc)
            dv_acc[...] = jnp.zeros_like(dv_acc)

        q2 = q_ref[...] * 0.125
        do = _mx(do_ref[...])

        def cond(st):
            return jnp.logical_and(st[0] >= 0, st[1] > EXP_ZERO)

        def tile(jt, c0, c1):
            s0, k2, z, lr, w, keep, c0, c1 = _sb_tile(q2, k_ref, i, jt, c0, c1, K)
            vt = v_ref[pl.ds(s0, SB_TK), :]
            g = jnp.concatenate([_dg(do, _two_heads(vt[b * SB_BLK:(b + 1) * SB_BLK], K.lo), NT) for b in range(SB_NB)], axis=0) * w
            gs = [_head_sums(g[b * SB_BLK:(b + 1) * SB_BLK]) for b in range(SB_NB)]
            return dict(s0=s0, k2=k2, z=z, lr=lr, w=w, keep=keep, g=g, gs=gs, c0=c0, c1=c1, mx=jnp.max(jnp.maximum(c0, c1)),
                        g0=gs[0][0] + gs[1][0] + gs[2][0], g1=gs[0][1] + gs[1][1] + gs[2][1])

        def grads(T, tot, n0, n1):
            nearer = [None] * SB_NB
            for b in reversed(range(SB_NB)):
                nearer[b] = (n0, n1)
                n0, n1 = n0 + T["gs"][b][0], n1 + T["gs"][b][1]
            farther = _per_block([tot] * SB_NB, K.head0) - _per_block(nearer, K.head0) - _dot2(T["g"], K.at_or_nearer)
            beta = jnp.exp(T["z"] + T["lr"])
            dz = _mx(jnp.where(T["keep"], T["g"] * (1.0 - beta) - farther * beta, 0.0))
            wm = _mx(T["w"])
            dq = jnp.zeros((SB_BLK, 128), F32)
            dks, dvs = [], []
            for b in range(SB_NB):
                rows = slice(b * SB_BLK, (b + 1) * SB_BLK)
                dq = dq + _dot(dz[rows], T["k2"][b])
                dk2 = _dg(dz[rows], q2, TN)
                dv2 = _dg(wm[rows], do, TN)
                dks.append(jnp.where(K.lo, dk2[:SB_BLK], dk2[SB_BLK:]))
                dvs.append(jnp.where(K.lo, dv2[:SB_BLK], dv2[SB_BLK:]))
            dk_acc[pl.ds(T["s0"], SB_TK), :] += jnp.concatenate(dks, axis=0)
            dv_acc[pl.ds(T["s0"], SB_TK), :] += jnp.concatenate(dvs, axis=0)
            return 0.125 * dq

        zc = jnp.zeros((SB_BLK, 1), F32)
        first = tile(i, zc, zc)

        def sum_step(st):
            jt, _, c0, c1, t0, t1 = st
            T = tile(jt, c0, c1)
            return jt - SB_NB, T["mx"], T["c0"], T["c1"], t0 + T["g0"], t1 + T["g1"]

        tot = lax.while_loop(cond, sum_step, (i - SB_NB, first["mx"], first["c0"], first["c1"], first["g0"], first["g1"]))[4:]

        def grad_step(st):
            jt, _, dq, c0, c1, n0, n1 = st
            T = tile(jt, c0, c1)
            return jt - SB_NB, T["mx"], dq + grads(T, tot, n0, n1), T["c0"], T["c1"], n0 + T["g0"], n1 + T["g1"]

        st = lax.while_loop(cond, grad_step, (i - SB_NB, first["mx"], grads(first, tot, zc, zc), first["c0"], first["c1"],
                                              first["g0"], first["g1"]))
        dq_ref[...] = st[2]

        @pl.when(i == nq - 1)
        def _():
            cols = pl.ds(pl.multiple_of(p * 128, 128), 128)
            ck = pltpu.make_async_copy(dk_acc, dk_hbm.at[:, cols], sem.at[0])
            cv = pltpu.make_async_copy(dv_acc, dv_hbm.at[:, cols], sem.at[1])
            ck.start()
            cv.start()
            ck.wait()
            cv.wait()

    return pl.pallas_call(
        body, name="sb_bwd", grid=(2, nq),
        in_specs=[pl.BlockSpec((SB_BLK, 128), lambda p, i: (i, QA // 128 + p)),
                  pl.BlockSpec((S, 128), lambda p, i: (0, KA // 128 + p)),
                  pl.BlockSpec((S, 128), lambda p, i: (0, VA // 128 + p)),
                  pl.BlockSpec((SB_BLK, 128), lambda p, i: (i, p))],
        out_specs=[pl.BlockSpec((SB_BLK, 128), lambda p, i: (i, p)), ANY, ANY],
        out_shape=[jax.ShapeDtypeStruct((S, 256), F32)] * 3,
        scratch_shapes=[pltpu.VMEM((S, 128), F32), pltpu.VMEM((S, 128), F32), pltpu.SemaphoreType.DMA((2,))],
        compiler_params=_cp(("arbitrary", "arbitrary"), 48),
    )(pb, pb, pb, dycat)


def _bias_tiles(rel_bias_l):
    r = np.arange(CH_TQ)[:, None]
    c = np.arange(CH_TQ)[None, :]
    valid = np.stack([(4 * b + c // 64 >= r // 64) & (4 * b + c // 64 <= r // 64 + 8) for b in range(3)])
    rev = rel_bias_l[:, _DIAG_IDX[:, ::-1]]
    rows = jnp.broadcast_to(jnp.pad(rev, ((0, 0), (0, 0), (0, 1)))[:, :, None, :], (N_HEADS_CH, 3, CH_TQ, 2 * CH_TQ))
    skew = rows.reshape(N_HEADS_CH, 3, -1)[:, :, :CH_TQ * (2 * CH_TQ - 1)].reshape(N_HEADS_CH, 3, CH_TQ, 2 * CH_TQ - 1)
    return jnp.where(valid[None], skew[:, :, :, CH_TQ - 1:], NEG)


_DIAG_IDX = np.stack([np.clip(512 - 256 * b + np.arange(2 * CH_TQ - 1) - (CH_TQ - 1), -128, 128) + 128 for b in range(3)])


def _bias_grad(dbias):
    d = jnp.pad(dbias, ((0, 0), (0, 0), (0, 0), (CH_TQ - 1, 0))).reshape(N_HEADS_CH, 3, -1)
    d = jnp.pad(d, ((0, 0), (0, 0), (0, CH_TQ))).reshape(N_HEADS_CH, 3, CH_TQ, 2 * CH_TQ)
    diag = _col_sum(d)[:, :, 0, :2 * CH_TQ - 1][:, :, ::-1]
    onehot = (_DIAG_IDX[:, :, None] == np.arange(REL)[None, None, :]).astype(np.float32)
    return jnp.einsum("hbk,bkr->hr", diag, jnp.asarray(onehot), precision=lax.Precision.HIGHEST)


def _ch_scores(i, q_ref, k_refs, b_ref, qw_ref, kw_ref):
    lo = _lo_mask(CH_TQ)
    qn = _half_norm(q_ref[...], qw_ref[...], lo)
    kn = [_mx(_half_norm(kr[...], kw_ref[...], lo)) for kr in k_refs]
    qh = [_mx(jnp.where(lo, qn, 0.0) * 0.125), _mx(jnp.where(lo, 0.0, qn) * 0.125)]
    probs = []
    for h in range(2):
        s = [_dg(qh[h], kn[b], NT) + b_ref[h, b] for b in range(3)]
        s[0] = jnp.where(i >= 2, s[0], NEG)
        s[1] = jnp.where(i >= 1, s[1], NEG)
        m = jnp.maximum(jnp.maximum(jnp.max(s[0], axis=-1, keepdims=True), jnp.max(s[1], axis=-1, keepdims=True)),
                        jnp.max(s[2], axis=-1, keepdims=True))
        e = [jnp.exp(sb - m) for sb in s]
        l = jnp.sum(e[0], axis=-1, keepdims=True) + jnp.sum(e[1], axis=-1, keepdims=True) + jnp.sum(e[2], axis=-1, keepdims=True)
        probs.append((e, l))
    return lo, qn, kn, qh, probs


def _ch_specs(S):
    nt = S // CH_TQ

    def kspec(b, base):
        return pl.BlockSpec((CH_TQ, 128), lambda p, i: (jnp.maximum(i - 2 + b, 0), base // 128 + p))

    specs = [pl.BlockSpec((CH_TQ, 128), lambda p, i: (i, QB // 128 + p))]
    specs += [kspec(b, KB) for b in range(3)] + [kspec(b, VB) for b in range(3)]
    specs += [pl.BlockSpec((2, 3, CH_TQ, CH_TQ), lambda p, i: (p, 0, 0, 0)),
              pl.BlockSpec((1, 128), lambda p, i: (0, 0)), pl.BlockSpec((1, 128), lambda p, i: (0, 0))]
    return nt, specs


def _ch_fwd(pf, pb, bias, qw2, kw2):
    S = pf.shape[0]
    nt, specs = _ch_specs(S)

    def body(q_ref, k0, k1, k2, v0, v1, v2, b_ref, qw_ref, kw_ref, o_ref):
        i = pl.program_id(1)
        lo, _, _, _, probs = _ch_scores(i, q_ref, (k0, k1, k2), b_ref, qw_ref, kw_ref)
        vs = [v0[...], v1[...], v2[...]]
        outs = []
        for h in range(2):
            e, l = probs[h]
            pv = _dot(e[0], vs[0]) + _dot(e[1], vs[1]) + _dot(e[2], vs[2])
            outs.append(pv / l)
        o_ref[...] = jnp.where(lo, outs[0], outs[1])

    return pl.pallas_call(
        body, name="ch_fwd", grid=(4, nt), in_specs=specs,
        out_specs=pl.BlockSpec((CH_TQ, 128), lambda p, i: (i, p)),
        out_shape=jax.ShapeDtypeStruct((S, 512), F32),
        compiler_params=_cp(("parallel", "arbitrary"), 40),
    )(pf, pf, pf, pf, pb, pb, pb, bias, qw2, kw2)


def _ch_bwd(pf, pb, bias, qw2, kw2, dycat):
    S = pf.shape[0]
    nt, specs = _ch_specs(S)
    specs = specs + [pl.BlockSpec((CH_TQ, 128), lambda p, i: (i, 256 // 128 + p))]

    def body(q_ref, k0, k1, k2, v0, v1, v2, b_ref, qw_ref, kw_ref, do_ref, dq_ref, db_ref, dk_hbm, dv_hbm, dk_acc, dv_acc, sem):
        p, i = pl.program_id(0), pl.program_id(1)

        @pl.when(i == 0)
        def _():
            dk_acc[...] = jnp.zeros_like(dk_acc)
            dv_acc[...] = jnp.zeros_like(dv_acc)
            db_ref[...] = jnp.zeros_like(db_ref)

        lo, _, kn, qh, probs = _ch_scores(i, q_ref, (k0, k1, k2), b_ref, qw_ref, kw_ref)
        vs = [v0[...], v1[...], v2[...]]
        do = do_ref[...]
        dos = [_mx(jnp.where(lo, do, 0.0)), _mx(jnp.where(lo, 0.0, do))]
        dqn = []
        dk_b = [jnp.zeros((CH_TQ, 128), F32) for _ in range(3)]
        dv_b = [jnp.zeros((CH_TQ, 128), F32) for _ in range(3)]
        for h in range(2):
            e, l = probs[h]
            inv = 1.0 / l
            pr = [eb * inv for eb in e]
            dp = [_dg(dos[h], vs[b], NT) for b in range(3)]
            delta = sum(jnp.sum(pr[b] * dp[b], axis=-1, keepdims=True) for b in range(3))
            ds = [pr[b] * (dp[b] - delta) for b in range(3)]
            acc = jnp.zeros((CH_TQ, 128), F32)
            for b in range(3):
                db_ref[h, b] += ds[b]
                dsm = _mx(ds[b])
                acc = acc + _dot(dsm, kn[b])
                dk_b[b] = dk_b[b] + _dg(dsm, qh[h], TN)
                dv_b[b] = dv_b[b] + _dg(pr[b], dos[h], TN)
            dqn.append(acc)
        dq_ref[...] = 0.125 * jnp.where(lo, dqn[0], dqn[1])

        for b in range(3):
            @pl.when(i - 2 + b >= 0)
            def _(b=b):
                rows = pl.ds(pl.multiple_of((i - 2 + b) * CH_TQ, CH_TQ), CH_TQ)
                dk_acc[rows, :] += dk_b[b]
                dv_acc[rows, :] += dv_b[b]

        @pl.when(i == nt - 1)
        def _():
            cols = pl.ds(pl.multiple_of(p * 128, 128), 128)
            ck = pltpu.make_async_copy(dk_acc, dk_hbm.at[:, cols], sem.at[0])
            cv = pltpu.make_async_copy(dv_acc, dv_hbm.at[:, cols], sem.at[1])
            ck.start()
            cv.start()
            ck.wait()
            cv.wait()

    return pl.pallas_call(
        body, name="ch_bwd", grid=(4, nt), in_specs=specs,
        out_specs=[pl.BlockSpec((CH_TQ, 128), lambda p, i: (i, p)),
                   pl.BlockSpec((2, 3, CH_TQ, CH_TQ), lambda p, i: (p, 0, 0, 0)), ANY, ANY],
        out_shape=[jax.ShapeDtypeStruct((S, 512), F32), jax.ShapeDtypeStruct((N_HEADS_CH, 3, CH_TQ, CH_TQ), F32),
                   jax.ShapeDtypeStruct((S, 512), F32), jax.ShapeDtypeStruct((S, 512), F32)],
        scratch_shapes=[pltpu.VMEM((S, 128), F32), pltpu.VMEM((S, 128), F32), pltpu.SemaphoreType.DMA((2,))],
        compiler_params=_cp(("arbitrary", "arbitrary"), 48),
    )(pf, pf, pf, pf, pb, pb, pb, bias, qw2, kw2, dycat)


def _col_sum(d):
    _, _, R, C = d.shape

    def body(d_ref, o_ref):
        for b in range(3):
            o_ref[0, b] = jnp.sum(d_ref[0, b], axis=0, keepdims=True)

    return pl.pallas_call(
        body, name="col_sum", grid=(N_HEADS_CH,),
        in_specs=[pl.BlockSpec((1, 3, R, C), lambda h: (h, 0, 0, 0))],
        out_specs=pl.BlockSpec((1, 3, 1, C), lambda h: (h, 0, 0, 0)),
        out_shape=jax.ShapeDtypeStruct((N_HEADS_CH, 3, 1, C), F32),
        compiler_params=_cp(("parallel",), 32),
    )(d)


def _conv_fwd(gc, xc, hc_prev, cw_ref):
    hc = gc * xc
    ext = jnp.concatenate([hc_prev, hc], axis=0)
    r1 = pltpu.roll(ext, 1, 0)[8:]
    r2 = pltpu.roll(ext, 2, 0)[8:]
    return cw_ref[0:1, :] * r2 + cw_ref[1:2, :] * r1 + cw_ref[2:3, :] * hc, hc, r1, r2


def _merge_out(x, ysb, ych, pf, cw, onw, w_out):
    S = x.shape[0]
    tm = min(512, S)
    t8 = tm // 8

    def body(x_ref, ysb_ref, ych_ref, gb_ref, gc_ref, xc_ref, gch_ref, xch_ref, cw_ref, onw_ref, w_ref, xo_ref, yn_ref, ycv_ref):
        i = pl.program_id(0)
        lo = _lo_mask(tm)
        hc_prev = jnp.where(i > 0, gch_ref[...] * xch_ref[...], 0.0)
        conv, _, _, _ = _conv_fwd(gc_ref[...], xc_ref[...], hc_prev, cw_ref)
        ycv = gb_ref[...] * conv
        ycv_ref[...] = ycv
        chunks = [ysb_ref[:, 0:128], ysb_ref[:, 128:256]] + [ych_ref[:, 128 * k:128 * (k + 1)] for k in range(4)]
        chunks += [ycv[:, 0:128], ycv[:, 128:256]]
        for k, ch in enumerate(chunks):
            yn_ref[:, 128 * k:128 * (k + 1)] = _half_norm(ch, onw_ref[:, 128 * k:128 * (k + 1)], lo).astype(yn_ref.dtype)
        xo_ref[...] = x_ref[...] + jnp.dot(yn_ref[...], w_ref[...], preferred_element_type=F32)

    row = lambda i: (i, 0)
    halo = lambda cb: pl.BlockSpec((8, 256), lambda i: (jnp.maximum(i * t8 - 1, 0), cb))
    return pl.pallas_call(
        body, name="merge_out", grid=(S // tm,),
        in_specs=[pl.BlockSpec((tm, D), row), pl.BlockSpec((tm, 256), row), pl.BlockSpec((tm, 512), row),
                  pl.BlockSpec((tm, 256), lambda i: (i, GB // 256)), pl.BlockSpec((tm, 256), lambda i: (i, GC // 256)),
                  pl.BlockSpec((tm, 256), lambda i: (i, XC // 256)), halo(GC // 256), halo(XC // 256),
                  pl.BlockSpec((3, 256), lambda i: (0, 0)), pl.BlockSpec((1, D), lambda i: (0, 0)), pl.BlockSpec((D, D), lambda i: (0, 0))],
        out_specs=[pl.BlockSpec((tm, D), row), pl.BlockSpec((tm, D), row), pl.BlockSpec((tm, 256), row)],
        out_shape=[jax.ShapeDtypeStruct((S, D), F32), jax.ShapeDtypeStruct((S, D), _MXU), jax.ShapeDtypeStruct((S, 256), F32)],
        compiler_params=_cp(("parallel",), 40),
    )(x, ysb, ych, pf, pf, pf, pf, pf, cw, onw, w_out)


def _ffn_fwd(x, nw, wg, wu, wd):
    S = x.shape[0]
    tm = min(512, S)

    def body(x_ref, nw_ref, wg_ref, wu_ref, wd_ref, xo_ref, hn_ref, u_ref, t_ref, a_ref, acc):
        j = pl.program_id(1)

        @pl.when(j == 0)
        def _():
            hn_ref[...] = _row_norm(x_ref[...], nw_ref[...]).astype(hn_ref.dtype)
            acc[...] = jnp.zeros_like(acc)

        hn = hn_ref[...]
        u = jnp.dot(hn, wg_ref[0], preferred_element_type=F32)
        t = jnp.dot(hn, wu_ref[0], preferred_element_type=F32)
        a = (u * _sigmoid(u) * t).astype(a_ref.dtype)
        u_ref[0], t_ref[0], a_ref[0] = u, t, a
        acc[...] += jnp.dot(a, wd_ref[0], preferred_element_type=F32)

        @pl.when(j == N_SH - 1)
        def _():
            xo_ref[...] = x_ref[...] + acc[...]

    row = lambda i, j: (i, 0)
    sh = lambda i, j: (j, i, 0)
    return pl.pallas_call(
        body, name="ffn_fwd", grid=(S // tm, N_SH),
        in_specs=[pl.BlockSpec((tm, D), row), pl.BlockSpec((1, D), lambda i, j: (0, 0)),
                  pl.BlockSpec((1, D, FF_SH), lambda i, j: (j, 0, 0)), pl.BlockSpec((1, D, FF_SH), lambda i, j: (j, 0, 0)),
                  pl.BlockSpec((1, FF_SH, D), lambda i, j: (j, 0, 0))],
        out_specs=[pl.BlockSpec((tm, D), row), pl.BlockSpec((tm, D), row), pl.BlockSpec((1, tm, FF_SH), sh),
                   pl.BlockSpec((1, tm, FF_SH), sh), pl.BlockSpec((1, tm, FF_SH), sh)],
        out_shape=[jax.ShapeDtypeStruct((S, D), F32), jax.ShapeDtypeStruct((S, D), _MXU),
                   jax.ShapeDtypeStruct((N_SH, S, FF_SH), F32), jax.ShapeDtypeStruct((N_SH, S, FF_SH), F32),
                   jax.ShapeDtypeStruct((N_SH, S, FF_SH), _MXU)],
        scratch_shapes=[pltpu.VMEM((tm, D), F32)],
        compiler_params=_cp(("parallel", "arbitrary"), 48),
    )(x, nw, wg, wu, wd)


def _loss_head(x, target):
    S = x.shape[0]
    tm = min(512, S)

    def body(x_ref, t_ref, dx_ref, l_ref):
        @pl.when(pl.program_id(0) == 0)
        def _():
            l_ref[...] = jnp.zeros_like(l_ref)

        e = x_ref[...] - t_ref[...]
        dx_ref[...] = e * (1.0 / D)
        l_ref[...] += 0.5 * jnp.sum(jnp.sum(e * e, axis=-1, keepdims=True) * (1.0 / D))

    return pl.pallas_call(
        body, name="loss_head", grid=(S // tm,),
        in_specs=[pl.BlockSpec((tm, D), lambda i: (i, 0))] * 2,
        out_specs=[pl.BlockSpec((tm, D), lambda i: (i, 0)), pl.BlockSpec((8, 128), lambda i: (0, 0))],
        out_shape=[jax.ShapeDtypeStruct((S, D), F32), jax.ShapeDtypeStruct((8, 128), F32)],
        compiler_params=_cp(("arbitrary",), 32),
    )(x, target)


def _ffn_bwd(dx, xmid, nw, wg, wu, wd, u, t):
    S = dx.shape[0]
    tm = min(512, S)

    def body(dx_ref, x_ref, nw_ref, wg_ref, wu_ref, wd_ref, u_ref, t_ref, du_ref, dt_ref, dxm_ref, dnw_ref, acc):
        i, j = pl.program_id(0), pl.program_id(1)

        @pl.when(j == 0)
        def _():
            acc[...] = jnp.zeros_like(acc)

        @pl.when(jnp.logical_and(i == 0, j == 0))
        def _():
            dnw_ref[...] = jnp.zeros_like(dnw_ref)

        da = _dg(dx_ref[...], wd_ref[0], NT)
        u, t = u_ref[0], t_ref[0]
        sg = _sigmoid(u)
        du = _mx(da * t * (sg * (1.0 + u * (1.0 - sg))))
        dt = _mx(da * (u * sg))
        du_ref[0], dt_ref[0] = du, dt
        acc[...] += _dg(du, wg_ref[0], NT) + _dg(dt, wu_ref[0], NT)

        @pl.when(j == N_SH - 1)
        def _():
            dxn, dw = _row_norm_bwd(x_ref[...], nw_ref[...], acc[...])
            dxm_ref[...] = dx_ref[...] + dxn
            dnw_ref[0:1, :] += dw

    row = lambda i, j: (i, 0)
    sh = lambda i, j: (j, i, 0)
    wsp = lambda r, c: pl.BlockSpec((1, r, c), lambda i, j: (j, 0, 0))
    return pl.pallas_call(
        body, name="ffn_bwd", grid=(S // tm, N_SH),
        in_specs=[pl.BlockSpec((tm, D), row), pl.BlockSpec((tm, D), row), pl.BlockSpec((1, D), lambda i, j: (0, 0)),
                  wsp(D, FF_SH), wsp(D, FF_SH), wsp(FF_SH, D), pl.BlockSpec((1, tm, FF_SH), sh), pl.BlockSpec((1, tm, FF_SH), sh)],
        out_specs=[pl.BlockSpec((1, tm, FF_SH), sh), pl.BlockSpec((1, tm, FF_SH), sh), pl.BlockSpec((tm, D), row),
                   pl.BlockSpec((8, D), lambda i, j: (0, 0))],
        out_shape=[jax.ShapeDtypeStruct((N_SH, S, FF_SH), _MXU), jax.ShapeDtypeStruct((N_SH, S, FF_SH), _MXU),
                   jax.ShapeDtypeStruct((S, D), F32), jax.ShapeDtypeStruct((8, D), F32)],
        scratch_shapes=[pltpu.VMEM((tm, D), F32)],
        compiler_params=_cp(("arbitrary", "arbitrary"), 48),
    )(dx, xmid, nw, wg, wu, wd, u, t)


def _wgrad(a, b, a_spec, b_spec, out_block, name):
    S = a.shape[-2]
    tk = min(1024, S)

    def body(a_ref, b_ref, o_ref):
        @pl.when(pl.program_id(1) == 0)
        def _():
            o_ref[...] = jnp.zeros_like(o_ref)

        av = a_ref[0] if len(a_ref.shape) == 3 else a_ref[...]
        bv = b_ref[0] if len(b_ref.shape) == 3 else b_ref[...]
        o_ref[0] += _dg(av, bv, TN)

    return pl.pallas_call(
        body, name=name, grid=(N_SH, S // tk),
        in_specs=[a_spec(tk), b_spec(tk)],
        out_specs=pl.BlockSpec((1,) + out_block, lambda r, k: (r, 0, 0)),
        out_shape=jax.ShapeDtypeStruct((N_SH,) + out_block, F32),
        compiler_params=_cp(("parallel", "arbitrary"), 48),
    )(a, b)


def _tok(width):
    return lambda tk: pl.BlockSpec((tk, width), lambda r, k: (k, 0))


def _tok_cols(width):
    return lambda tk: pl.BlockSpec((tk, width), lambda r, k: (k, r))


def _tok_sh(width):
    return lambda tk: pl.BlockSpec((1, tk, width), lambda r, k: (r, k, 0))


def _out_bwd(dxm, w_out, ysb, ych, ycv, onw):
    S = dxm.shape[0]
    tm = min(512, S)

    def body(dx_ref, w_ref, ysb_ref, ych_ref, ycv_ref, onw_ref, dy_ref, dw_ref):
        @pl.when(pl.program_id(0) == 0)
        def _():
            dw_ref[...] = jnp.zeros_like(dw_ref)

        lo = _lo_mask(tm)
        dyn = _dg(dx_ref[...], w_ref[...], NT)
        chunks = [ysb_ref[:, 0:128], ysb_ref[:, 128:256]] + [ych_ref[:, 128 * k:128 * (k + 1)] for k in range(4)]
        chunks += [ycv_ref[:, 0:128], ycv_ref[:, 128:256]]
        for k, ch in enumerate(chunks):
            sl = slice(128 * k, 128 * (k + 1))
            d, dw = _half_norm_bwd(ch, onw_ref[:, sl], dyn[:, sl], lo)
            dy_ref[:, sl] = d
            dw_ref[0:1, sl] += dw

    row = lambda i: (i, 0)
    return pl.pallas_call(
        body, name="out_bwd", grid=(S // tm,),
        in_specs=[pl.BlockSpec((tm, D), row), pl.BlockSpec((D, D), lambda i: (0, 0)), pl.BlockSpec((tm, 256), row),
                  pl.BlockSpec((tm, 512), row), pl.BlockSpec((tm, 256), row), pl.BlockSpec((1, D), lambda i: (0, 0))],
        out_specs=[pl.BlockSpec((tm, D), row), pl.BlockSpec((8, D), lambda i: (0, 0))],
        out_shape=[jax.ShapeDtypeStruct((S, D), F32), jax.ShapeDtypeStruct((8, D), F32)],
        compiler_params=_cp(("arbitrary",), 40),
    )(dxm, w_out, ysb, ych, ycv, onw)


def _in_bwd(dxm, x, anw, w_sh, pf, dqa, dka, dva, dqh, dkh, dvb, dycat, cw, qw2, kw2):
    S = x.shape[0]
    tm = min(256, S)
    t8 = tm // 8
    last8 = S // 8 - 1

    def body(dxm_ref, x_ref, anw_ref, w_ref, q0_ref, q1_ref, k0_ref, k1_ref, gb_ref, gc_ref, xc_ref, gch_ref, xch_ref, gbn_ref,
             dqa_ref, dka_ref, dva_ref, dqh_ref, dkh_ref, dvb_ref, dyc_ref, dycn_ref, cw_ref, qw_ref, kw_ref,
             dp_ref, dx_ref, danw_ref, dqw_ref, dkw_ref, dcw_ref):
        i = pl.program_id(0)
        n = pl.num_programs(0)

        @pl.when(i == 0)
        def _():
            for r in (danw_ref, dqw_ref, dkw_ref, dcw_ref):
                r[...] = jnp.zeros_like(r)

        lo = _lo_mask(tm)
        dp_ref[:, QA:QA + 256] = dqa_ref[...].astype(dp_ref.dtype)
        dp_ref[:, KA:KA + 256] = dka_ref[...].astype(dp_ref.dtype)
        dp_ref[:, VA:VA + 256] = dva_ref[...].astype(dp_ref.dtype)
        dp_ref[:, VB:VB + 512] = dvb_ref[...].astype(dp_ref.dtype)
        for base, raws, d_ref, nw_ref, dw_ref in ((QB, (q0_ref, q1_ref), dqh_ref, qw_ref, dqw_ref), (KB, (k0_ref, k1_ref), dkh_ref, kw_ref, dkw_ref)):
            for k in range(4):
                raw = raws[k // 2][:, 128 * (k % 2):128 * (k % 2 + 1)]
                d, dw = _half_norm_bwd(raw, nw_ref[...], d_ref[:, 128 * k:128 * (k + 1)], lo)
                dp_ref[:, base + 128 * k:base + 128 * (k + 1)] = d.astype(dp_ref.dtype)
                dw_ref[0:1, :] += dw
        gb, gc, xc = gb_ref[...], gc_ref[...], xc_ref[...]
        hc_prev = jnp.where(i > 0, gch_ref[...] * xch_ref[...], 0.0)
        conv, hc, r1, r2 = _conv_fwd(gc, xc, hc_prev, cw_ref)
        dyc = dyc_ref[...]
        dconv = dyc * gb
        dconv_next = jnp.where(i < n - 1, dycn_ref[...] * gbn_ref[...], 0.0)
        ext = jnp.concatenate([dconv, dconv_next], axis=0)
        l1 = pltpu.roll(ext, tm + 7, 0)[:tm]
        l2 = pltpu.roll(ext, tm + 6, 0)[:tm]
        dhc = cw_ref[2:3, :] * dconv + cw_ref[1:2, :] * l1 + cw_ref[0:1, :] * l2
        dp_ref[:, GB:GB + 256] = (dyc * conv).astype(dp_ref.dtype)
        dp_ref[:, GC:GC + 256] = (dhc * xc).astype(dp_ref.dtype)
        dp_ref[:, XC:XC + 256] = (dhc * gc).astype(dp_ref.dtype)
        dcw_ref[0:1, :] += jnp.sum(dconv * r2, axis=0, keepdims=True)
        dcw_ref[1:2, :] += jnp.sum(dconv * r1, axis=0, keepdims=True)
        dcw_ref[2:3, :] += jnp.sum(dconv * hc, axis=0, keepdims=True)
        dh = jnp.zeros((tm, D), F32)
        for r in range(N_SH):
            dh = dh + lax.dot_general(dp_ref[:, IN_SH * r:IN_SH * (r + 1)], w_ref[r], NT, preferred_element_type=F32)
        dxn, dw = _row_norm_bwd(x_ref[...], anw_ref[...], dh)
        dx_ref[...] = dxm_ref[...] + dxn
        danw_ref[0:1, :] += dw

    row = lambda i: (i, 0)
    colb = lambda w, cb: pl.BlockSpec((tm, w), lambda i: (i, cb))
    prev = lambda cb: pl.BlockSpec((8, 256), lambda i: (jnp.maximum(i * t8 - 1, 0), cb))
    nxt = lambda cb: pl.BlockSpec((8, 256), lambda i: (jnp.minimum((i + 1) * t8, last8), cb))
    const = lambda shp: pl.BlockSpec(shp, lambda i: (0,) * len(shp))
    return pl.pallas_call(
        body, name="in_bwd", grid=(S // tm,),
        in_specs=[pl.BlockSpec((tm, D), row), pl.BlockSpec((tm, D), row), const((1, D)), const((N_SH, D, IN_SH)),
                  colb(256, QB // 256), colb(256, QB // 256 + 1), colb(256, KB // 256), colb(256, KB // 256 + 1),
                  colb(256, GB // 256), colb(256, GC // 256), colb(256, XC // 256), prev(GC // 256), prev(XC // 256), nxt(GB // 256),
                  pl.BlockSpec((tm, 256), row), pl.BlockSpec((tm, 256), row), pl.BlockSpec((tm, 256), row),
                  pl.BlockSpec((tm, 512), row), pl.BlockSpec((tm, 512), row), pl.BlockSpec((tm, 512), row),
                  colb(256, 768 // 256), nxt(768 // 256), const((3, 256)), const((1, 128)), const((1, 128))],
        out_specs=[pl.BlockSpec((tm, D_IN), row), pl.BlockSpec((tm, D), row), const((8, D)), const((8, 128)), const((8, 128)), const((8, 256))],
        out_shape=[jax.ShapeDtypeStruct((S, D_IN), _MXU), jax.ShapeDtypeStruct((S, D), F32), jax.ShapeDtypeStruct((8, D), F32),
                   jax.ShapeDtypeStruct((8, 128), F32), jax.ShapeDtypeStruct((8, 128), F32), jax.ShapeDtypeStruct((8, 256), F32)],
        compiler_params=_cp(("arbitrary",), 56),
    )(dxm, x, anw, w_sh, pf, pf, pf, pf, pf, pf, pf, pf, pf, pf, dqa, dka, dva, dqh, dkh, dvb, dycat, dycat, cw, qw2, kw2)


def _rows_tile(h):
    return h if h <= 512 else 256


def _add_halves(g, r1, c_arr):
    L, _, R, C = g.shape
    h = R // 2
    tr = _rows_tile(h)
    nb = h // tr

    def body(c_ref, g_ref, r_ref, o_ref, ob_ref):
        s = g_ref[...] + r_ref[...]
        o_ref[...] = s
        ob_ref[...] = s.astype(ob_ref.dtype)

    blk = (1, 1, tr, C)
    return pl.pallas_call(
        body, name="add_halves",
        grid_spec=pltpu.PrefetchScalarGridSpec(
            num_scalar_prefetch=1, grid=(L, N_SH, nb),
            in_specs=[pl.BlockSpec(blk, lambda l, s, i, c: (l, s, c[0] * nb + i, 0)), pl.BlockSpec(blk, lambda l, s, i, c: (l, s, i, 0))],
            out_specs=[pl.BlockSpec(blk, lambda l, s, i, c: (l, s, i, 0))] * 2),
        out_shape=[jax.ShapeDtypeStruct((L, N_SH, h, C), F32), jax.ShapeDtypeStruct((L, N_SH, h, C), jnp.bfloat16)],
        compiler_params=_cp(("parallel", "parallel", "parallel"), 32),
    )(c_arr, g, r1)


def _add_shards(s1, r2, me_arr):
    L, _, h, C = s1.shape
    tr = _rows_tile(h)

    def body(me_ref, s_ref, r_ref, o_ref):
        o_ref[0] = ((s_ref[0, 0] + r_ref[0, 0].astype(F32)) + r_ref[1, 0].astype(F32)) + r_ref[2, 0].astype(F32)

    return pl.pallas_call(
        body, name="add_shards",
        grid_spec=pltpu.PrefetchScalarGridSpec(
            num_scalar_prefetch=1, grid=(L, h // tr),
            in_specs=[pl.BlockSpec((1, 1, tr, C), lambda l, i, me: (l, me[0], i, 0)), pl.BlockSpec((3, 1, tr, C), lambda l, i, me: (0, l, i, 0))],
            out_specs=pl.BlockSpec((1, tr, C), lambda l, i, me: (l, i, 0))),
        out_shape=jax.ShapeDtypeStruct((L, h, C), F32),
        compiler_params=_cp(("parallel", "parallel"), 32),
    )(me_arr, s1, r2)


def _adam_math(w, g, m, v):
    m = ADAM_B1 * m + (1.0 - ADAM_B1) * g
    v = ADAM_B2 * v + (1.0 - ADAM_B2) * (g * g)
    m_hat = m / (1.0 - ADAM_B1 ** ADAM_STEP)
    v_hat = v / (1.0 - ADAM_B2 ** ADAM_STEP)
    return -ADAM_LR * (m_hat / (jnp.sqrt(v_hat) + ADAM_EPS) + ADAM_WD * w), m, v


def _adam_big(w, m, v, mine, theirs, c_arr):
    L, R, C = w.shape
    h = R // 2
    tr = _rows_tile(h)
    nb = h // tr

    def body(c_ref, w_ref, m_ref, v_ref, a_ref, b_ref, g_ref, d_ref, mo_ref, vo_ref):
        g = jnp.where(pl.program_id(1) == c_ref[0], a_ref[...], b_ref[...])
        g_ref[...] = g
        d_ref[...], mo_ref[...], vo_ref[...] = _adam_math(w_ref[...], g, m_ref[...], v_ref[...])

    full = pl.BlockSpec((1, tr, C), lambda l, s, i, c: (l, s * nb + i, 0))
    half = pl.BlockSpec((1, tr, C), lambda l, s, i, c: (l, i, 0))
    return pl.pallas_call(
        body, name="adam_big",
        grid_spec=pltpu.PrefetchScalarGridSpec(num_scalar_prefetch=1, grid=(L, 2, nb), in_specs=[full, full, full, half, half],
                                               out_specs=[full] * 4),
        out_shape=[jax.ShapeDtypeStruct((L, R, C), F32)] * 4,
        compiler_params=_cp(("parallel", "parallel", "parallel"), 32),
    )(c_arr, w, m, v, mine, theirs)


def _adam_small(w, g, m, v):
    def body(w_ref, g_ref, m_ref, v_ref, d_ref, mo_ref, vo_ref):
        d_ref[...], mo_ref[...], vo_ref[...] = _adam_math(w_ref[...], g_ref[...], m_ref[...], v_ref[...])

    return pl.pallas_call(body, name="adam_small", out_shape=[jax.ShapeDtypeStruct(w.shape, F32)] * 3)(w, g, m, v)


def _allreduce_small(buf):
    R = buf.shape[0]

    def body(x_ref, o_ref, gat, ssem, rsem):
        x, y, c = lax.axis_index("x"), lax.axis_index("y"), lax.axis_index("c")
        me = 4 * x + 2 * y + c
        gat[me] = x_ref[...]
        cps = []
        for k in range(1, 8):
            peer = (x ^ (k >> 2), y ^ ((k >> 1) & 1), c ^ (k & 1))
            cps.append(pltpu.make_async_remote_copy(src_ref=x_ref, dst_ref=gat.at[me], send_sem=ssem.at[k - 1], recv_sem=rsem.at[k - 1],
                                                    device_id=peer, device_id_type=MESH))
        for cp in cps:
            cp.start()
        for cp in cps:
            cp.wait()
        acc = gat[0]
        for s in range(1, 8):
            acc = acc + gat[s]
        o_ref[...] = acc

    return pl.pallas_call(
        body, name="allreduce_small", out_shape=jax.ShapeDtypeStruct((R, 128), F32),
        in_specs=[pl.BlockSpec(memory_space=pltpu.VMEM)], out_specs=pl.BlockSpec(memory_space=pltpu.VMEM),
        scratch_shapes=[pltpu.VMEM((8, R, 128), F32), pltpu.SemaphoreType.DMA((7,)), pltpu.SemaphoreType.DMA((7,))],
    )(buf)


def _reduce_scatter(grads):
    x, y, c = lax.axis_index("x"), lax.axis_index("y"), lax.axis_index("c")
    c_arr = jnp.reshape(c, (1,)).astype(jnp.int32)
    me_arr = jnp.reshape(2 * x + y, (1,)).astype(jnp.int32)
    halves = [g.shape[2] // 2 for g in grads]

    n = len(grads)

    def body1(*refs):
        ins, outs = refs[:n], refs[n:2 * n]
        ssem, rsem = refs[2 * n:]
        x, y, c = lax.axis_index("x"), lax.axis_index("y"), lax.axis_index("c")
        cps = []
        for f in range(n):
            h = halves[f]
            src = ins[f].at[:, :, pl.ds(pl.multiple_of((1 - c) * h, 8), h), :]
            cps.append(pltpu.make_async_remote_copy(src_ref=src, dst_ref=outs[f], send_sem=ssem.at[f], recv_sem=rsem.at[f],
                                                    device_id=(x, y, 1 - c), device_id_type=MESH))
        for cp in cps:
            cp.start()
        for cp in cps:
            cp.wait()

    r1 = pl.pallas_call(
        body1, name="rs_sibling_halves",
        out_shape=[jax.ShapeDtypeStruct(g.shape[:2] + (h, g.shape[3]), F32) for g, h in zip(grads, halves)],
        in_specs=[ANY] * n, out_specs=[ANY] * n,
        scratch_shapes=[pltpu.SemaphoreType.DMA((n,)), pltpu.SemaphoreType.DMA((n,))],
    )(*grads)
    s1, s1_wire = zip(*[_add_halves(g, r, c_arr) for g, r in zip(grads, r1)])

    def body2(*refs):
        ins, outs = refs[:n], refs[n:2 * n]
        ssem, rsem = refs[2 * n:]
        x, y, c = lax.axis_index("x"), lax.axis_index("y"), lax.axis_index("c")
        peers = [(1 - x, y), (x, 1 - y), (1 - x, 1 - y)]
        cps = []
        for f in range(n):
            for k, (px, py) in enumerate(peers):
                cps.append(pltpu.make_async_remote_copy(src_ref=ins[f].at[:, 2 * px + py], dst_ref=outs[f].at[k], send_sem=ssem.at[3 * f + k],
                                                        recv_sem=rsem.at[3 * f + k], device_id=(px, py, c), device_id_type=MESH))
        for cp in cps:
            cp.start()
        for cp in cps:
            cp.wait()

    r2 = pl.pallas_call(
        body2, name="rs_chip_shards",
        out_shape=[jax.ShapeDtypeStruct((3, s.shape[0]) + s.shape[2:], s.dtype) for s in s1_wire],
        in_specs=[ANY] * n, out_specs=[ANY] * n,
        scratch_shapes=[pltpu.SemaphoreType.DMA((3 * n,)), pltpu.SemaphoreType.DMA((3 * n,))],
    )(*s1_wire)
    mine = [_add_shards(s, r, me_arr) for s, r in zip(s1, r2)]

    def body3(*refs):
        ins, outs = refs[:n], refs[n:2 * n]
        ssem, rsem = refs[2 * n:]
        x, y, c = lax.axis_index("x"), lax.axis_index("y"), lax.axis_index("c")
        cps = [pltpu.make_async_remote_copy(src_ref=ins[f], dst_ref=outs[f], send_sem=ssem.at[f], recv_sem=rsem.at[f],
                                            device_id=(x, y, 1 - c), device_id_type=MESH) for f in range(n)]
        for cp in cps:
            cp.start()
        for cp in cps:
            cp.wait()

    theirs = pl.pallas_call(
        body3, name="rs_sibling_reduced",
        out_shape=[jax.ShapeDtypeStruct(m.shape, F32) for m in mine],
        in_specs=[ANY] * n, out_specs=[ANY] * n,
        scratch_shapes=[pltpu.SemaphoreType.DMA((n,)), pltpu.SemaphoreType.DMA((n,))],
    )(*mine)
    return list(zip(mine, theirs)), c_arr


def _pad_rows(flat, rows):
    return jnp.pad(flat, (0, rows * 128 - flat.shape[0])).reshape(rows, 128)


def _local_step(x, target, small, big):
    S = x.shape[0]
    saved = []
    for l in range(DEPTH):
        W = big[l]
        anw, onw, fnw = small["attn_norm_w"][l][None], small["out_norm_w"][l][None], small["ffn_norm_w"][l][None]
        qw2 = jnp.tile(small["q_norm_w"][l], 2)[None]
        kw2 = jnp.tile(small["k_norm_w"][l], 2)[None]
        cw = small["conv_w"][l]
        bias = _bias_tiles(small["rel_bias"][l])
        h, pf, pb = _attn_in(x, anw, W["w_in"])
        ysb = _sb_fwd(pb)
        ych = _ch_fwd(pf, pb, bias, qw2, kw2)
        w_out = W["w_out"].reshape(D, D)
        xmid, yn, ycv = _merge_out(x, ysb, ych, pf, cw, onw, w_out)
        xo, hn, u, t, a = _ffn_fwd(xmid, fnw, W["w_gate"], W["w_up"], W["w_down"])
        saved.append(dict(x=x, h=h, pf=pf, pb=pb, ysb=ysb, ych=ych, ycv=ycv, yn=yn, xmid=xmid, hn=hn, u=u, t=t, a=a,
                          anw=anw, onw=onw, fnw=fnw, qw2=qw2, kw2=kw2, cw=cw, bias=bias, w_out=w_out))
        x = xo
    dx, lpart = _loss_head(x, target)
    loss = lpart[0, 0]
    grads = [None] * DEPTH
    for l in reversed(range(DEPTH)):
        W, sv = big[l], saved[l]
        du, dt, dxm, dfnw = _ffn_bwd(dx, sv["xmid"], sv["fnw"], W["w_gate"], W["w_up"], W["w_down"], sv["u"], sv["t"])
        g_gate = _wgrad(sv["hn"], du, _tok(D), _tok_sh(FF_SH), (D, FF_SH), "wgrad_gate")
        g_up = _wgrad(sv["hn"], dt, _tok(D), _tok_sh(FF_SH), (D, FF_SH), "wgrad_up")
        g_down = _wgrad(sv["a"], dx, _tok_sh(FF_SH), _tok(D), (FF_SH, D), "wgrad_down")
        dycat, donw = _out_bwd(dxm, sv["w_out"], sv["ysb"], sv["ych"], sv["ycv"], sv["onw"])
        g_out = _wgrad(sv["yn"], dxm, _tok_cols(OUT_SH), _tok(D), (OUT_SH, D), "wgrad_out")
        dqa, dka, dva = _sb_bwd(sv["pb"], dycat)
        dqh, dbias, dkh, dvb = _ch_bwd(sv["pf"], sv["pb"], sv["bias"], sv["qw2"], sv["kw2"], dycat)
        drelb = _bias_grad(dbias)
        dproj, dx, danw, dqw, dkw, dcw = _in_bwd(dxm, sv["x"], sv["anw"], W["w_in"], sv["pf"], dqa, dka, dva, dqh, dkh, dvb, dycat,
                                                 sv["cw"], sv["qw2"], sv["kw2"])
        g_in = _wgrad(sv["h"], dproj, _tok(D), _tok_cols(IN_SH), (D, IN_SH), "wgrad_in")
        grads[l] = dict(w_in=g_in, w_out=g_out, w_gate=g_gate, w_up=g_up, w_down=g_down,
                        attn_norm_w=danw[0], out_norm_w=donw[0], ffn_norm_w=dfnw[0],
                        q_norm_w=dqw[0, :HD] + dqw[0, HD:], k_norm_w=dkw[0, :HD] + dkw[0, HD:], rel_bias=drelb, conv_w=dcw[:3])
    return loss, dx, grads


_BIG = ("w_in", "w_out", "w_gate", "w_up", "w_down")
_SMALL = ("attn_norm_w", "q_norm_w", "k_norm_w", "rel_bias", "out_norm_w", "ffn_norm_w")
_ORDER = ("attn_norm_w", "w_in", "q_norm_w", "k_norm_w", "rel_bias", "conv_w", "out_norm_w", "w_out", "ffn_norm_w", "w_gate", "w_up", "w_down")


def kernel(x, attn_norm_w, w_in, q_norm_w, k_norm_w, rel_bias, conv_w, out_norm_w, w_out, ffn_norm_w, w_gate, w_up, w_down, loss_target, m_attn_norm_w, m_w_in, m_q_norm_w, m_k_norm_w, m_rel_bias, m_conv_w, m_out_norm_w, m_w_out, m_ffn_norm_w, m_w_gate, m_w_up, m_w_down, v_attn_norm_w, v_w_in, v_q_norm_w, v_k_norm_w, v_rel_bias, v_conv_w, v_out_norm_w, v_w_out, v_ffn_norm_w, v_w_gate, v_w_up, v_w_down):
    w = dict(attn_norm_w=attn_norm_w, w_in=w_in, q_norm_w=q_norm_w, k_norm_w=k_norm_w, rel_bias=rel_bias, conv_w=conv_w,
             out_norm_w=out_norm_w, w_out=w_out, ffn_norm_w=ffn_norm_w, w_gate=w_gate, w_up=w_up, w_down=w_down)
    m = dict(attn_norm_w=m_attn_norm_w, w_in=m_w_in, q_norm_w=m_q_norm_w, k_norm_w=m_k_norm_w, rel_bias=m_rel_bias, conv_w=m_conv_w,
             out_norm_w=m_out_norm_w, w_out=m_w_out, ffn_norm_w=m_ffn_norm_w, w_gate=m_w_gate, w_up=m_w_up, w_down=m_w_down)
    v = dict(attn_norm_w=v_attn_norm_w, w_in=v_w_in, q_norm_w=v_q_norm_w, k_norm_w=v_k_norm_w, rel_bias=v_rel_bias, conv_w=v_conv_w,
             out_norm_w=v_out_norm_w, w_out=v_w_out, ffn_norm_w=v_ffn_norm_w, w_gate=v_w_gate, w_up=v_w_up, w_down=v_w_down)
    xi, yi = lax.axis_index("x"), lax.axis_index("y")
    me = 2 * xi + yi

    big, conv_full = [], []
    for l in range(DEPTH):
        cpad = jnp.zeros((8, 128), F32).at[:3, :HD].set(conv_w[l])
        parts = [w[k][l].astype(_MXU) for k in _BIG] + [cpad]
        got = _allgather_layer(parts, [True] * len(_BIG) + [False])
        big.append(dict(zip(_BIG, got[:5])))
        conv_full.append(jnp.concatenate([got[5][r, :3, :HD] for r in range(N_SH)], axis=-1))
    small = {k: w[k] for k in _SMALL}
    small["conv_w"] = jnp.stack(conv_full)

    loss, grad_x, grads = _local_step(x[0], loss_target[0], small, big)
    loss = lax.psum(loss, ("x", "y", "c"))

    stacked = [jnp.stack([grads[l][k] for l in range(DEPTH)]) for k in _BIG]
    reduced, c_arr = _reduce_scatter(stacked)
    out = {}
    for k, (mine, theirs) in zip(_BIG, reduced):
        out[k] = _adam_big(w[k], m[k], v[k], mine, theirs, c_arr)

    names = _SMALL + ("conv_w",)
    gflat = jnp.concatenate([jnp.stack([grads[l][k] for l in range(DEPTH)]).reshape(-1) for k in names])
    rows = -(-gflat.shape[0] // 1024) * 8
    gsum = _allreduce_small(_pad_rows(gflat, rows)).reshape(-1)
    gs, off = {}, 0
    for k in names:
        shp = (DEPTH, 3, 4 * HD) if k == "conv_w" else w[k].shape
        size = int(np.prod(shp))
        gs[k] = gsum[off:off + size].reshape(shp)
        off += size
    gs["conv_w"] = lax.dynamic_slice_in_dim(gs["conv_w"], me * HD, HD, axis=2)
    flat = lambda d: jnp.concatenate([d[k].reshape(-1) for k in names])
    rows = -(-flat(w).shape[0] // 1024) * 8
    dsm, msm, vsm = _adam_small(_pad_rows(flat(w), rows), _pad_rows(flat(gs), rows), _pad_rows(flat(m), rows), _pad_rows(flat(v), rows))
    off = 0
    for k in names:
        size = int(np.prod(w[k].shape))
        out[k] = (gs[k],) + tuple(a.reshape(-1)[off:off + size].reshape(w[k].shape) for a in (dsm, msm, vsm))
        off += size

    return (loss, grad_x[None]) + tuple(out[k][j] for j in range(4) for k in _ORDER)
```

```python
import functools

import numpy as np
import jax
import jax.numpy as jnp
from jax import lax
from jax.experimental import pallas as pl
from jax.experimental.pallas import tpu as pltpu

F32 = jnp.float32
_MXU = jnp.bfloat16

D = 1024
DEPTH = 4
HD = 64
D_IN = 3072
N_SH = 4
IN_SH = D_IN // N_SH
FF_SH = 704
OUT_SH = 256
N_HEADS_CH = 8
D_CH = N_HEADS_CH * HD
REL = 257
QA, KA, VA, QB, KB, VB, GB, GC, XC = 0, 256, 512, 768, 1280, 1792, 2304, 2560, 2816
EPS = 1e-6
NEG = -1e30
SB_BLK = 128
CH_TQ = 256
EXP_ZERO = -104.0

ADAM_LR, ADAM_B1, ADAM_B2, ADAM_EPS, ADAM_WD, ADAM_STEP = 0.001, 0.9, 0.999, 1e-08, 0.01, 10

NT = (((1,), (1,)), ((), ()))
TN = (((0,), (0,)), ((), ()))
MESH = pl.DeviceIdType.MESH
ANY = pl.BlockSpec(memory_space=pl.ANY)
MB = 1 << 20


def _cp(sem=None, vmem_mb=None):
    return pltpu.CompilerParams(dimension_semantics=sem, vmem_limit_bytes=None if vmem_mb is None else vmem_mb * MB)


def _mx(a):
    return a.astype(_MXU)


def _dot(a, b):
    return jnp.dot(_mx(a), _mx(b), preferred_element_type=F32)


def _dg(a, b, dims):
    return lax.dot_general(_mx(a), _mx(b), dims, preferred_element_type=F32)


def _dot2(a, m):
    hi = _mx(a)
    r = jnp.dot(hi, m, preferred_element_type=F32)
    if _MXU != F32:
        r = r + jnp.dot(_mx(a - hi.astype(F32)), m, preferred_element_type=F32)
    return r


def _sigmoid(u):
    return 1.0 / (1.0 + jnp.exp(-u))


def _row_norm(x, w):
    r = lax.rsqrt(jnp.mean(x * x, axis=-1, keepdims=True) + EPS)
    return x * r * w


def _row_norm_bwd(x, w, dy):
    r = lax.rsqrt(jnp.mean(x * x, axis=-1, keepdims=True) + EPS)
    xh = x * r
    dw = jnp.sum(dy * xh, axis=0, keepdims=True)
    dxh = dy * w
    return r * (dxh - xh * jnp.mean(dxh * xh, axis=-1, keepdims=True)), dw


def _half_rs(x, lo):
    sq = x * x
    s0 = jnp.sum(jnp.where(lo, sq, 0.0), axis=-1, keepdims=True)
    s1 = jnp.sum(jnp.where(lo, 0.0, sq), axis=-1, keepdims=True)
    return jnp.where(lo, lax.rsqrt(s0 * (1.0 / HD) + EPS), lax.rsqrt(s1 * (1.0 / HD) + EPS))


def _half_norm(x, w, lo):
    return x * _half_rs(x, lo) * w


def _half_norm_bwd(x, w, dy, lo):
    r = _half_rs(x, lo)
    xh = x * r
    dw = jnp.sum(dy * xh, axis=0, keepdims=True)
    dxh = dy * w
    pr = dxh * xh
    m0 = jnp.sum(jnp.where(lo, pr, 0.0), axis=-1, keepdims=True)
    m1 = jnp.sum(jnp.where(lo, 0.0, pr), axis=-1, keepdims=True)
    mean = jnp.where(lo, m0, m1) * (1.0 / HD)
    return r * (dxh - xh * mean), dw


def _lo_mask(t):
    return lax.broadcasted_iota(jnp.int32, (t, 128), 1) < HD


class _Gather:
    def __init__(self, ins, outs, sems, split):
        self.n = n = len(ins)
        lsem, ssem, rsem, fsem, gsem = sems
        x, y, c = lax.axis_index("x"), lax.axis_index("y"), lax.axis_index("c")
        me = 2 * x + y
        chips = [(1 - x, y), (x, 1 - y), (1 - x, 1 - y)]

        def half(f):
            h = ins[f].shape[0] // 2
            return pl.ds(pl.multiple_of(c * h, 16), h)

        self.local = [pltpu.make_async_copy(ins[f], outs[f].at[me], lsem.at[f]) for f in range(n)]
        self.sends, self.passed = [], []
        for f in range(n):
            for k, (px, py) in enumerate(chips):
                src, dst = (ins[f].at[half(f)], outs[f].at[me, half(f)]) if split[f] else (ins[f], outs[f].at[me])
                self.sends.append(pltpu.make_async_remote_copy(src_ref=src, dst_ref=dst, send_sem=ssem.at[3 * f + k], recv_sem=rsem.at[3 * f + k],
                                                               device_id=(px, py, c), device_id_type=MESH))
                if split[f]:
                    got = outs[f].at[2 * px + py, half(f)]
                    self.passed.append(pltpu.make_async_remote_copy(src_ref=got, dst_ref=got, send_sem=fsem.at[3 * f + k], recv_sem=gsem.at[3 * f + k],
                                                                    device_id=(x, y, 1 - c), device_id_type=MESH))
                else:
                    self.passed.append(None)

    def start(self):
        for cp in self.local + self.sends:
            cp.start()

    def forward(self):
        for cp, fw in zip(self.sends, self.passed):
            cp.wait_recv()
            if fw is not None:
                fw.start()

    def finish(self):
        passed = [fw for fw in self.passed if fw is not None]
        for fw in passed:
            fw.wait_recv()
        for cp in self.sends + passed:
            cp.wait_send()
        for cp in self.local:
            cp.wait()

    @staticmethod
    def sems(n):
        return [pltpu.SemaphoreType.DMA((n,))] + [pltpu.SemaphoreType.DMA((3 * n,))] * 4

    @staticmethod
    def out_shape(parts):
        return [jax.ShapeDtypeStruct((N_SH,) + p.shape, p.dtype) for p in parts]


def _allgather_layer(parts, split):
    n = len(parts)

    def body(*refs):
        g = _Gather(refs[:n], refs[n:2 * n], refs[2 * n:], split)
        g.start()
        g.forward()
        g.finish()

    return pl.pallas_call(
        body, name="allgather_layer", out_shape=_Gather.out_shape(parts),
        in_specs=[ANY] * n, out_specs=[ANY] * n, scratch_shapes=_Gather.sems(n),
    )(*parts)


def _attn_in(x, nw, w_sh):
    S = x.shape[0]
    tm = min(512, S)
    assert (QB, VB, GB) == (IN_SH, 2 * IN_SH + 256, 3 * IN_SH)

    def body(x_ref, nw_ref, w_ref, h_ref, pa_ref, pqk_ref, pv_ref, pc_ref):
        h = _row_norm(x_ref[...], nw_ref[...]).astype(h_ref.dtype)
        h_ref[...] = h
        pa_ref[...] = jnp.dot(h, w_ref[0], preferred_element_type=F32).astype(pa_ref.dtype)
        pqk_ref[:, :IN_SH] = jnp.dot(h, w_ref[1], preferred_element_type=F32)
        p2 = jnp.dot(h, w_ref[2], preferred_element_type=F32)
        pqk_ref[:, IN_SH:] = p2[:, :256]
        pv_ref[...] = p2[:, 256:].astype(pv_ref.dtype)
        pc_ref[...] = jnp.dot(h, w_ref[3], preferred_element_type=F32)

    row = lambda i: (i, 0)
    return pl.pallas_call(
        body, name="attn_in", grid=(S // tm,),
        in_specs=[pl.BlockSpec((tm, D), row), pl.BlockSpec((1, D), lambda i: (0, 0)), pl.BlockSpec((N_SH, D, IN_SH), lambda i: (0, 0, 0))],
        out_specs=[pl.BlockSpec((tm, D), row), pl.BlockSpec((tm, IN_SH), row), pl.BlockSpec((tm, 2 * D_CH), row),
                   pl.BlockSpec((tm, D_CH), row), pl.BlockSpec((tm, IN_SH), row)],
        out_shape=[jax.ShapeDtypeStruct((S, D), _MXU), jax.ShapeDtypeStruct((S, IN_SH), _MXU), jax.ShapeDtypeStruct((S, 2 * D_CH), F32),
                   jax.ShapeDtypeStruct((S, D_CH), _MXU), jax.ShapeDtypeStruct((S, IN_SH), F32)],
        compiler_params=_cp(("parallel",), 48),
    )(x, nw, w_sh)


SB_NB = 3
SB_TK = SB_NB * SB_BLK


class _SbConsts:
    def __init__(self):
        r = lax.broadcasted_iota(jnp.int32, (SB_TK, 256), 0)
        c = lax.broadcasted_iota(jnp.int32, (SB_TK, 256), 1)
        self.kofs = (r - (r & 127)) + (c & 127)
        self.qofs = r & 127
        r2 = lax.broadcasted_iota(jnp.int32, (256, 256), 0)
        c2 = lax.broadcasted_iota(jnp.int32, (256, 256), 1)
        same = (r2 >= 128) == (c2 >= 128)
        self.nearer = jnp.logical_and(same, (r2 & 127) > (c2 & 127)).astype(_MXU)
        self.at_or_nearer = jnp.logical_and(same, (r2 & 127) >= (c2 & 127)).astype(_MXU)
        self.lo = lax.broadcasted_iota(jnp.int32, (SB_BLK, 128), 1) < HD
        self.head0 = lax.broadcasted_iota(jnp.int32, (SB_BLK, 256), 1) < 128


def _two_heads(x, lo):
    return jnp.concatenate([jnp.where(lo, x, 0), jnp.where(lo, 0, x)], axis=0)


def _per_block(vals, head0):
    return jnp.concatenate([jnp.where(head0, v0, v1) for v0, v1 in vals], axis=0)


def _head_sums(x):
    return jnp.sum(x[:, :128], axis=-1, keepdims=True), jnp.sum(x[:, 128:], axis=-1, keepdims=True)


def _sb_tile(q2, k_ref, i, jt, c0, c1, K):
    s0 = pl.multiple_of(jnp.maximum(jt - (SB_NB - 1), 0) * SB_BLK, SB_BLK)
    kt = k_ref[pl.ds(s0, SB_TK), :]
    k2 = [_two_heads(kt[b * SB_BLK:(b + 1) * SB_BLK], K.lo) for b in range(SB_NB)]
    z = jnp.concatenate([_dg(q2, k2[b], NT) for b in range(SB_NB)], axis=0)
    kpos = K.kofs + s0
    keep = jnp.logical_and(kpos < K.qofs + i * SB_BLK, kpos < (jt + 1) * SB_BLK)
    lr = jnp.where(keep, -(jnp.maximum(z, 0.0) + jnp.log1p(jnp.exp(-jnp.abs(z)))), 0.0)
    carries = [None] * SB_NB
    for b in reversed(range(SB_NB)):
        carries[b] = (c0, c1)
        s = _head_sums(lr[b * SB_BLK:(b + 1) * SB_BLK])
        c0, c1 = c0 + s[0], c1 + s[1]
    after = _dot2(lr, K.nearer) + _per_block(carries, K.head0)
    w = jnp.where(keep, jnp.exp(z + lr + after), 0.0)
    return s0, k2, z, lr, w, keep, c0, c1


def _sb_fwd(pa):
    S = pa.shape[0]
    nq = S // SB_BLK

    def body(q_ref, k_ref, v_ref, o_ref):
        i = pl.program_id(1)
        K = _SbConsts()
        q2 = q_ref[...] * 0.125

        def cond(st):
            return jnp.logical_and(st[0] >= 0, st[1] > EXP_ZERO)

        def step(st):
            jt, _, acc, c0, c1 = st
            s0, _, _, _, w, _, c0, c1 = _sb_tile(q2, k_ref, i, jt, c0, c1, K)
            vt = v_ref[pl.ds(s0, SB_TK), :]
            for b in range(SB_NB):
                rows = slice(b * SB_BLK, (b + 1) * SB_BLK)
                acc = acc + _dot(w[rows], _two_heads(vt[rows], K.lo))
            return jt - SB_NB, jnp.max(jnp.maximum(c0, c1)), acc, c0, c1

        zc = jnp.zeros((SB_BLK, 1), F32)
        st = lax.while_loop(cond, step, (i, jnp.float32(0.0), jnp.zeros((SB_BLK, 128), F32), zc, zc))
        o_ref[...] = st[2]

    return pl.pallas_call(
        body, name="sb_fwd", grid=(2, nq),
        in_specs=[pl.BlockSpec((SB_BLK, 128), lambda p, i: (i, QA // 128 + p)),
                  pl.BlockSpec((S, 128), lambda p, i: (0, KA // 128 + p)),
                  pl.BlockSpec((S, 128), lambda p, i: (0, VA // 128 + p))],
        out_specs=pl.BlockSpec((SB_BLK, 128), lambda p, i: (i, p)),
        out_shape=jax.ShapeDtypeStruct((S, 256), F32),
        compiler_params=_cp(("parallel", "arbitrary"), 40),
    )(pa, pa, pa)


def _sb_bwd(pa, dycat):
    S = pa.shape[0]
    nq = S // SB_BLK

    def body(q_ref, k_ref, v_ref, do_ref, dq_ref, dk_hbm, dv_hbm, dk_acc, dv_acc, sem):
        p, i = pl.program_id(0), pl.program_id(1)
        K = _SbConsts()

        @pl.when(i == 0)
        def _():
            dk_acc[...] = jnp.zeros_like(dk_acc)
            dv_acc[...] = jnp.zeros_like(dv_acc)

        q2 = q_ref[...] * 0.125
        do = _mx(do_ref[...])

        def cond(st):
            return jnp.logical_and(st[0] >= 0, st[1] > EXP_ZERO)

        def tile(jt, c0, c1):
            s0, k2, z, lr, w, keep, c0, c1 = _sb_tile(q2, k_ref, i, jt, c0, c1, K)
            vt = v_ref[pl.ds(s0, SB_TK), :]
            g = jnp.concatenate([_dg(do, _two_heads(vt[b * SB_BLK:(b + 1) * SB_BLK], K.lo), NT) for b in range(SB_NB)], axis=0) * w
            gs = [_head_sums(g[b * SB_BLK:(b + 1) * SB_BLK]) for b in range(SB_NB)]
            return dict(s0=s0, k2=k2, z=z, lr=lr, w=w, keep=keep, g=g, gs=gs, c0=c0, c1=c1, mx=jnp.max(jnp.maximum(c0, c1)),
                        g0=gs[0][0] + gs[1][0] + gs[2][0], g1=gs[0][1] + gs[1][1] + gs[2][1])

        def grads(T, tot, n0, n1):
            nearer = [None] * SB_NB
            for b in reversed(range(SB_NB)):
                nearer[b] = (n0, n1)
                n0, n1 = n0 + T["gs"][b][0], n1 + T["gs"][b][1]
            farther = _per_block([tot] * SB_NB, K.head0) - _per_block(nearer, K.head0) - _dot2(T["g"], K.at_or_nearer)
            beta = jnp.exp(T["z"] + T["lr"])
            dz = _mx(jnp.where(T["keep"], T["g"] * (1.0 - beta) - farther * beta, 0.0))
            wm = _mx(T["w"])
            dq = jnp.zeros((SB_BLK, 128), F32)
            dks, dvs = [], []
            for b in range(SB_NB):
                rows = slice(b * SB_BLK, (b + 1) * SB_BLK)
                dq = dq + _dot(dz[rows], T["k2"][b])
                dk2 = _dg(dz[rows], q2, TN)
                dv2 = _dg(wm[rows], do, TN)
                dks.append(jnp.where(K.lo, dk2[:SB_BLK], dk2[SB_BLK:]))
                dvs.append(jnp.where(K.lo, dv2[:SB_BLK], dv2[SB_BLK:]))
            dk_acc[pl.ds(T["s0"], SB_TK), :] += jnp.concatenate(dks, axis=0)
            dv_acc[pl.ds(T["s0"], SB_TK), :] += jnp.concatenate(dvs, axis=0)
            return 0.125 * dq

        zc = jnp.zeros((SB_BLK, 1), F32)
        first = tile(i, zc, zc)

        def sum_step(st):
            jt, _, c0, c1, t0, t1 = st
            T = tile(jt, c0, c1)
            return jt - SB_NB, T["mx"], T["c0"], T["c1"], t0 + T["g0"], t1 + T["g1"]

        tot = lax.while_loop(cond, sum_step, (i - SB_NB, first["mx"], first["c0"], first["c1"], first["g0"], first["g1"]))[4:]

        def grad_step(st):
            jt, _, dq, c0, c1, n0, n1 = st
            T = tile(jt, c0, c1)
            return jt - SB_NB, T["mx"], dq + grads(T, tot, n0, n1), T["c0"], T["c1"], n0 + T["g0"], n1 + T["g1"]

        st = lax.while_loop(cond, grad_step, (i - SB_NB, first["mx"], grads(first, tot, zc, zc), first["c0"], first["c1"],
                                              first["g0"], first["g1"]))
        dq_ref[...] = st[2]

        @pl.when(i == nq - 1)
        def _():
            cols = pl.ds(pl.multiple_of(p * 128, 128), 128)
            ck = pltpu.make_async_copy(dk_acc, dk_hbm.at[:, cols], sem.at[0])
            cv = pltpu.make_async_copy(dv_acc, dv_hbm.at[:, cols], sem.at[1])
            ck.start()
            cv.start()
            ck.wait()
            cv.wait()

    return pl.pallas_call(
        body, name="sb_bwd", grid=(2, nq),
        in_specs=[pl.BlockSpec((SB_BLK, 128), lambda p, i: (i, QA // 128 + p)),
                  pl.BlockSpec((S, 128), lambda p, i: (0, KA // 128 + p)),
                  pl.BlockSpec((S, 128), lambda p, i: (0, VA // 128 + p)),
                  pl.BlockSpec((SB_BLK, 128), lambda p, i: (i, p))],
        out_specs=[pl.BlockSpec((SB_BLK, 128), lambda p, i: (i, p)), ANY, ANY],
        out_shape=[jax.ShapeDtypeStruct((S, 256), F32)] * 3,
        scratch_shapes=[pltpu.VMEM((S, 128), F32), pltpu.VMEM((S, 128), F32), pltpu.SemaphoreType.DMA((2,))],
        compiler_params=_cp(("arbitrary", "arbitrary"), 48),
    )(pa, pa, pa, dycat)


def _bias_tiles(rel_bias_l):
    r = np.arange(CH_TQ)[:, None]
    c = np.arange(CH_TQ)[None, :]
    valid = np.stack([(4 * b + c // 64 >= r // 64) & (4 * b + c // 64 <= r // 64 + 8) for b in range(3)])
    rev = rel_bias_l[:, _DIAG_IDX[:, ::-1]]
    rows = jnp.broadcast_to(jnp.pad(rev, ((0, 0), (0, 0), (0, 1)))[:, :, None, :], (N_HEADS_CH, 3, CH_TQ, 2 * CH_TQ))
    skew = rows.reshape(N_HEADS_CH, 3, -1)[:, :, :CH_TQ * (2 * CH_TQ - 1)].reshape(N_HEADS_CH, 3, CH_TQ, 2 * CH_TQ - 1)
    return jnp.where(valid[None], skew[:, :, :, CH_TQ - 1:], NEG)


_DIAG_IDX = np.stack([np.clip(512 - 256 * b + np.arange(2 * CH_TQ - 1) - (CH_TQ - 1), -128, 128) + 128 for b in range(3)])


def _bias_grad(dbias):
    d = jnp.pad(dbias, ((0, 0), (0, 0), (0, 0), (CH_TQ - 1, 0))).reshape(N_HEADS_CH, 3, -1)
    d = jnp.pad(d, ((0, 0), (0, 0), (0, CH_TQ))).reshape(N_HEADS_CH, 3, CH_TQ, 2 * CH_TQ)
    diag = _col_sum(d)[:, :, 0, :2 * CH_TQ - 1][:, :, ::-1]
    onehot = (_DIAG_IDX[:, :, None] == np.arange(REL)[None, None, :]).astype(np.float32)
    return jnp.einsum("hbk,bkr->hr", diag, jnp.asarray(onehot), precision=lax.Precision.HIGHEST)


def _ch_scores(i, q_ref, k_refs, b_ref, qw_ref, kw_ref):
    lo = _lo_mask(CH_TQ)
    qn = _half_norm(q_ref[...], qw_ref[...], lo)
    kn = [_mx(_half_norm(kr[...], kw_ref[...], lo)) for kr in k_refs]
    qh = [_mx(jnp.where(lo, qn, 0.0) * 0.125), _mx(jnp.where(lo, 0.0, qn) * 0.125)]
    probs = []
    for h in range(2):
        s = [_dg(qh[h], kn[b], NT) + b_ref[h, b] for b in range(3)]
        s[0] = jnp.where(i >= 2, s[0], NEG)
        s[1] = jnp.where(i >= 1, s[1], NEG)
        m = jnp.maximum(jnp.maximum(jnp.max(s[0], axis=-1, keepdims=True), jnp.max(s[1], axis=-1, keepdims=True)),
                        jnp.max(s[2], axis=-1, keepdims=True))
        e = [jnp.exp(sb - m) for sb in s]
        l = jnp.sum(e[0], axis=-1, keepdims=True) + jnp.sum(e[1], axis=-1, keepdims=True) + jnp.sum(e[2], axis=-1, keepdims=True)
        probs.append((e, l))
    return lo, qn, kn, qh, probs


def _ch_specs(S):
    nt = S // CH_TQ

    def kspec(b, base):
        return pl.BlockSpec((CH_TQ, 128), lambda p, i: (jnp.maximum(i - 2 + b, 0), base // 128 + p))

    specs = [pl.BlockSpec((CH_TQ, 128), lambda p, i: (i, p))]
    specs += [kspec(b, D_CH) for b in range(3)] + [kspec(b, 0) for b in range(3)]
    specs += [pl.BlockSpec((2, 3, CH_TQ, CH_TQ), lambda p, i: (p, 0, 0, 0)),
              pl.BlockSpec((1, 128), lambda p, i: (0, 0)), pl.BlockSpec((1, 128), lambda p, i: (0, 0))]
    return nt, specs


def _ch_fwd(pqk, pv, bias, qw2, kw2):
    S = pqk.shape[0]
    nt, specs = _ch_specs(S)

    def body(q_ref, k0, k1, k2, v0, v1, v2, b_ref, qw_ref, kw_ref, o_ref):
        i = pl.program_id(1)
        lo, _, _, _, probs = _ch_scores(i, q_ref, (k0, k1, k2), b_ref, qw_ref, kw_ref)
        vs = [v0[...], v1[...], v2[...]]
        outs = []
        for h in range(2):
            e, l = probs[h]
            pv = _dot(e[0], vs[0]) + _dot(e[1], vs[1]) + _dot(e[2], vs[2])
            outs.append(pv / l)
        o_ref[...] = jnp.where(lo, outs[0], outs[1])

    return pl.pallas_call(
        body, name="ch_fwd", grid=(4, nt), in_specs=specs,
        out_specs=pl.BlockSpec((CH_TQ, 128), lambda p, i: (i, p)),
        out_shape=jax.ShapeDtypeStruct((S, 512), F32),
        compiler_params=_cp(("parallel", "arbitrary"), 40),
    )(pqk, pqk, pqk, pqk, pv, pv, pv, bias, qw2, kw2)


def _ch_bwd(pqk, pv, bias, qw2, kw2, dycat):
    S = pqk.shape[0]
    nt, specs = _ch_specs(S)
    specs = specs + [pl.BlockSpec((CH_TQ, 128), lambda p, i: (i, 256 // 128 + p))]

    def body(q_ref, k0, k1, k2, v0, v1, v2, b_ref, qw_ref, kw_ref, do_ref, dq_ref, db_ref, dk_hbm, dv_hbm, dk_acc, dv_acc, sem):
        p, i = pl.program_id(0), pl.program_id(1)

        @pl.when(i == 0)
        def _():
            dk_acc[...] = jnp.zeros_like(dk_acc)
            dv_acc[...] = jnp.zeros_like(dv_acc)
            db_ref[...] = jnp.zeros_like(db_ref)

        lo, _, kn, qh, probs = _ch_scores(i, q_ref, (k0, k1, k2), b_ref, qw_ref, kw_ref)
        vs = [v0[...], v1[...], v2[...]]
        do = do_ref[...]
        dos = [_mx(jnp.where(lo, do, 0.0)), _mx(jnp.where(lo, 0.0, do))]
        dqn = []
        dk_b = [jnp.zeros((CH_TQ, 128), F32) for _ in range(3)]
        dv_b = [jnp.zeros((CH_TQ, 128), F32) for _ in range(3)]
        for h in range(2):
            e, l = probs[h]
            inv = 1.0 / l
            pr = [eb * inv for eb in e]
            dp = [_dg(dos[h], vs[b], NT) for b in range(3)]
            delta = sum(jnp.sum(pr[b] * dp[b], axis=-1, keepdims=True) for b in range(3))
            ds = [pr[b] * (dp[b] - delta) for b in range(3)]
            acc = jnp.zeros((CH_TQ, 128), F32)
            for b in range(3):
                db_ref[h, b] += ds[b]
                dsm = _mx(ds[b])
                acc = acc + _dot(dsm, kn[b])
                dk_b[b] = dk_b[b] + _dg(dsm, qh[h], TN)
                dv_b[b] = dv_b[b] + _dg(pr[b], dos[h], TN)
            dqn.append(acc)
        dq_ref[...] = 0.125 * jnp.where(lo, dqn[0], dqn[1])

        for b in range(3):
            @pl.when(i - 2 + b >= 0)
            def _(b=b):
                rows = pl.ds(pl.multiple_of((i - 2 + b) * CH_TQ, CH_TQ), CH_TQ)
                dk_acc[rows, :] += dk_b[b]
                dv_acc[rows, :] += dv_b[b]

        @pl.when(i == nt - 1)
        def _():
            cols = pl.ds(pl.multiple_of(p * 128, 128), 128)
            ck = pltpu.make_async_copy(dk_acc, dk_hbm.at[:, cols], sem.at[0])
            cv = pltpu.make_async_copy(dv_acc, dv_hbm.at[:, cols], sem.at[1])
            ck.start()
            cv.start()
            ck.wait()
            cv.wait()

    return pl.pallas_call(
        body, name="ch_bwd", grid=(4, nt), in_specs=specs,
        out_specs=[pl.BlockSpec((CH_TQ, 128), lambda p, i: (i, p)),
                   pl.BlockSpec((2, 3, CH_TQ, CH_TQ), lambda p, i: (p, 0, 0, 0)), ANY, ANY],
        out_shape=[jax.ShapeDtypeStruct((S, 512), F32), jax.ShapeDtypeStruct((N_HEADS_CH, 3, CH_TQ, CH_TQ), F32),
                   jax.ShapeDtypeStruct((S, 512), F32), jax.ShapeDtypeStruct((S, 512), F32)],
        scratch_shapes=[pltpu.VMEM((S, 128), F32), pltpu.VMEM((S, 128), F32), pltpu.SemaphoreType.DMA((2,))],
        compiler_params=_cp(("arbitrary", "arbitrary"), 48),
    )(pqk, pqk, pqk, pqk, pv, pv, pv, bias, qw2, kw2, dycat)


def _col_sum(d):
    _, _, R, C = d.shape

    def body(d_ref, o_ref):
        for b in range(3):
            o_ref[0, b] = jnp.sum(d_ref[0, b], axis=0, keepdims=True)

    return pl.pallas_call(
        body, name="col_sum", grid=(N_HEADS_CH,),
        in_specs=[pl.BlockSpec((1, 3, R, C), lambda h: (h, 0, 0, 0))],
        out_specs=pl.BlockSpec((1, 3, 1, C), lambda h: (h, 0, 0, 0)),
        out_shape=jax.ShapeDtypeStruct((N_HEADS_CH, 3, 1, C), F32),
        compiler_params=_cp(("parallel",), 32),
    )(d)


def _conv_fwd(gc, xc, hc_prev, cw_ref):
    hc = gc * xc
    ext = jnp.concatenate([hc_prev, hc], axis=0)
    r1 = pltpu.roll(ext, 1, 0)[8:]
    r2 = pltpu.roll(ext, 2, 0)[8:]
    return cw_ref[0:1, :] * r2 + cw_ref[1:2, :] * r1 + cw_ref[2:3, :] * hc, hc, r1, r2


def _merge_out(x, ysb, ych, pc, cw, onw, w_out):
    S = x.shape[0]
    tm = min(512, S)
    t8 = tm // 8

    def body(x_ref, ysb_ref, ych_ref, gb_ref, gc_ref, xc_ref, gch_ref, xch_ref, cw_ref, onw_ref, w_ref, xo_ref, yn_ref, ycv_ref):
        i = pl.program_id(0)
        lo = _lo_mask(tm)
        hc_prev = jnp.where(i > 0, gch_ref[...] * xch_ref[...], 0.0)
        conv, _, _, _ = _conv_fwd(gc_ref[...], xc_ref[...], hc_prev, cw_ref)
        ycv = gb_ref[...] * conv
        ycv_ref[...] = ycv
        chunks = [ysb_ref[:, 0:128], ysb_ref[:, 128:256]] + [ych_ref[:, 128 * k:128 * (k + 1)] for k in range(4)]
        chunks += [ycv[:, 0:128], ycv[:, 128:256]]
        for k, ch in enumerate(chunks):
            yn_ref[:, 128 * k:128 * (k + 1)] = _half_norm(ch, onw_ref[:, 128 * k:128 * (k + 1)], lo).astype(yn_ref.dtype)
        xo_ref[...] = x_ref[...] + jnp.dot(yn_ref[...], w_ref[...], preferred_element_type=F32)

    row = lambda i: (i, 0)
    halo = lambda cb: pl.BlockSpec((8, 256), lambda i: (jnp.maximum(i * t8 - 1, 0), cb))
    return pl.pallas_call(
        body, name="merge_out", grid=(S // tm,),
        in_specs=[pl.BlockSpec((tm, D), row), pl.BlockSpec((tm, 256), row), pl.BlockSpec((tm, 512), row),
                  pl.BlockSpec((tm, 256), lambda i: (i, 0)), pl.BlockSpec((tm, 256), lambda i: (i, 1)),
                  pl.BlockSpec((tm, 256), lambda i: (i, 2)), halo(1), halo(2),
                  pl.BlockSpec((3, 256), lambda i: (0, 0)), pl.BlockSpec((1, D), lambda i: (0, 0)), pl.BlockSpec((D, D), lambda i: (0, 0))],
        out_specs=[pl.BlockSpec((tm, D), row), pl.BlockSpec((tm, D), row), pl.BlockSpec((tm, 256), row)],
        out_shape=[jax.ShapeDtypeStruct((S, D), F32), jax.ShapeDtypeStruct((S, D), _MXU), jax.ShapeDtypeStruct((S, 256), F32)],
        compiler_params=_cp(("parallel",), 40),
    )(x, ysb, ych, pc, pc, pc, pc, pc, cw, onw, w_out)


def _ffn_fwd(x, nw, wg, wu, wd, nxt=(), split=()):
    S = x.shape[0]
    tm = min(1024, S)
    ni = S // tm
    n = len(nxt)

    def body(*refs):
        x_ref, nw_ref, wg_ref, wu_ref, wd_ref = refs[:5]
        xo_ref, hn_ref, u_ref, t_ref, a_ref = refs[5 + n:10 + n]
        acc = refs[10 + 2 * n]
        i, j = pl.program_id(0), pl.program_id(1)
        if n:
            gather = _Gather(refs[5:5 + n], refs[10 + n:10 + 2 * n], refs[11 + 2 * n:], split)

            @pl.when(jnp.logical_and(i == 0, j == 0))
            def _():
                gather.start()

            @pl.when(jnp.logical_and(i == max(ni - 2, 0), j == 0))
            def _():
                gather.forward()

        @pl.when(j == 0)
        def _():
            hn_ref[...] = _row_norm(x_ref[...], nw_ref[...]).astype(hn_ref.dtype)
            acc[...] = jnp.zeros_like(acc)

        hn = hn_ref[...]
        u = jnp.dot(hn, wg_ref[0], preferred_element_type=F32)
        t = jnp.dot(hn, wu_ref[0], preferred_element_type=F32)
        a = (u * _sigmoid(u) * t).astype(a_ref.dtype)
        u_ref[0], t_ref[0], a_ref[0] = u.astype(u_ref.dtype), t.astype(t_ref.dtype), a
        acc[...] += jnp.dot(a, wd_ref[0], preferred_element_type=F32)

        @pl.when(j == N_SH - 1)
        def _():
            xo_ref[...] = x_ref[...] + acc[...]

        if n:
            @pl.when(jnp.logical_and(i == ni - 1, j == N_SH - 1))
            def _():
                gather.finish()

    row = lambda i, j: (i, 0)
    sh = lambda i, j: (j, i, 0)
    return pl.pallas_call(
        body, name="ffn_fwd_gather" if n else "ffn_fwd", grid=(ni, N_SH),
        in_specs=[pl.BlockSpec((tm, D), row), pl.BlockSpec((1, D), lambda i, j: (0, 0)),
                  pl.BlockSpec((1, D, FF_SH), lambda i, j: (j, 0, 0)), pl.BlockSpec((1, D, FF_SH), lambda i, j: (j, 0, 0)),
                  pl.BlockSpec((1, FF_SH, D), lambda i, j: (j, 0, 0))] + [ANY] * n,
        out_specs=[pl.BlockSpec((tm, D), row), pl.BlockSpec((tm, D), row), pl.BlockSpec((1, tm, FF_SH), sh),
                   pl.BlockSpec((1, tm, FF_SH), sh), pl.BlockSpec((1, tm, FF_SH), sh)] + [ANY] * n,
        out_shape=[jax.ShapeDtypeStruct((S, D), F32), jax.ShapeDtypeStruct((S, D), _MXU),
                   jax.ShapeDtypeStruct((N_SH, S, FF_SH), _MXU), jax.ShapeDtypeStruct((N_SH, S, FF_SH), _MXU),
                   jax.ShapeDtypeStruct((N_SH, S, FF_SH), _MXU)] + _Gather.out_shape(nxt),
        scratch_shapes=[pltpu.VMEM((tm, D), F32)] + (_Gather.sems(n) if n else []),
        compiler_params=_cp(("arbitrary", "arbitrary"), 60),
    )(x, nw, wg, wu, wd, *nxt)


def _loss_head(x, target):
    S = x.shape[0]
    tm = min(512, S)

    def body(x_ref, t_ref, dx_ref, l_ref):
        @pl.when(pl.program_id(0) == 0)
        def _():
            l_ref[...] = jnp.zeros_like(l_ref)

        e = x_ref[...] - t_ref[...]
        dx_ref[...] = e * (1.0 / D)
        l_ref[...] += 0.5 * jnp.sum(jnp.sum(e * e, axis=-1, keepdims=True) * (1.0 / D))

    return pl.pallas_call(
        body, name="loss_head", grid=(S // tm,),
        in_specs=[pl.BlockSpec((tm, D), lambda i: (i, 0))] * 2,
        out_specs=[pl.BlockSpec((tm, D), lambda i: (i, 0)), pl.BlockSpec((8, 128), lambda i: (0, 0))],
        out_shape=[jax.ShapeDtypeStruct((S, D), F32), jax.ShapeDtypeStruct((8, 128), F32)],
        compiler_params=_cp(("arbitrary",), 32),
    )(x, target)


def _ffn_bwd(dx, xmid, nw, wg, wu, wd, u, t):
    S = dx.shape[0]
    tm = min(1024, S)

    def body(dx_ref, x_ref, nw_ref, wg_ref, wu_ref, wd_ref, u_ref, t_ref, du_ref, dt_ref, dxm_ref, dnw_ref, acc):
        i, j = pl.program_id(0), pl.program_id(1)

        @pl.when(j == 0)
        def _():
            acc[...] = jnp.zeros_like(acc)

        @pl.when(jnp.logical_and(i == 0, j == 0))
        def _():
            dnw_ref[...] = jnp.zeros_like(dnw_ref)

        da = _dg(dx_ref[...], wd_ref[0], NT)
        u, t = u_ref[0].astype(F32), t_ref[0].astype(F32)
        sg = _sigmoid(u)
        du = _mx(da * t * (sg * (1.0 + u * (1.0 - sg))))
        dt = _mx(da * (u * sg))
        du_ref[0], dt_ref[0] = du, dt
        acc[...] += _dg(du, wg_ref[0], NT) + _dg(dt, wu_ref[0], NT)

        @pl.when(j == N_SH - 1)
        def _():
            dxn, dw = _row_norm_bwd(x_ref[...], nw_ref[...], acc[...])
            dxm_ref[...] = dx_ref[...] + dxn
            dnw_ref[0:1, :] += dw

    row = lambda i, j: (i, 0)
    sh = lambda i, j: (j, i, 0)
    wsp = lambda r, c: pl.BlockSpec((1, r, c), lambda i, j: (j, 0, 0))
    return pl.pallas_call(
        body, name="ffn_bwd", grid=(S // tm, N_SH),
        in_specs=[pl.BlockSpec((tm, D), row), pl.BlockSpec((tm, D), row), pl.BlockSpec((1, D), lambda i, j: (0, 0)),
                  wsp(D, FF_SH), wsp(D, FF_SH), wsp(FF_SH, D), pl.BlockSpec((1, tm, FF_SH), sh), pl.BlockSpec((1, tm, FF_SH), sh)],
        out_specs=[pl.BlockSpec((1, tm, FF_SH), sh), pl.BlockSpec((1, tm, FF_SH), sh), pl.BlockSpec((tm, D), row),
                   pl.BlockSpec((8, D), lambda i, j: (0, 0))],
        out_shape=[jax.ShapeDtypeStruct((N_SH, S, FF_SH), _MXU), jax.ShapeDtypeStruct((N_SH, S, FF_SH), _MXU),
                   jax.ShapeDtypeStruct((S, D), F32), jax.ShapeDtypeStruct((8, D), F32)],
        scratch_shapes=[pltpu.VMEM((tm, D), F32)],
        compiler_params=_cp(("arbitrary", "arbitrary"), 60),
    )(dx, xmid, nw, wg, wu, wd, u, t)


def _wgrad(a, b, a_spec, b_spec, out_block, name):
    S = a.shape[-2]
    tk = min(1024, S)

    def body(a_ref, b_ref, o_ref):
        @pl.when(pl.program_id(0) == 0)
        def _():
            o_ref[...] = jnp.zeros_like(o_ref)

        for r in range(N_SH):
            o_ref[r] += _dg(a_spec[1](a_ref, r), b_spec[1](b_ref, r), TN)

    return pl.pallas_call(
        body, name=name, grid=(S // tk,),
        in_specs=[a_spec[0](tk), b_spec[0](tk)],
        out_specs=pl.BlockSpec((N_SH,) + out_block, lambda k: (0, 0, 0)),
        out_shape=jax.ShapeDtypeStruct((N_SH,) + out_block, F32),
        compiler_params=_cp(("arbitrary",), 56),
    )(a, b)


def _tok(width):
    return (lambda tk: pl.BlockSpec((tk, width), lambda k: (k, 0))), (lambda ref, r: ref[...])


def _tok_cols(width):
    return (lambda tk: pl.BlockSpec((tk, N_SH * width), lambda k: (k, 0))), (lambda ref, r: ref[:, r * width:(r + 1) * width])


def _tok_sh(width):
    return (lambda tk: pl.BlockSpec((N_SH, tk, width), lambda k: (0, k, 0))), (lambda ref, r: ref[r])


def _out_bwd(dxm, w_out, ysb, ych, ycv, onw):
    S = dxm.shape[0]
    tm = min(512, S)

    def body(dx_ref, w_ref, ysb_ref, ych_ref, ycv_ref, onw_ref, dy_ref, dw_ref):
        @pl.when(pl.program_id(0) == 0)
        def _():
            dw_ref[...] = jnp.zeros_like(dw_ref)

        lo = _lo_mask(tm)
        dyn = _dg(dx_ref[...], w_ref[...], NT)
        chunks = [ysb_ref[:, 0:128], ysb_ref[:, 128:256]] + [ych_ref[:, 128 * k:128 * (k + 1)] for k in range(4)]
        chunks += [ycv_ref[:, 0:128], ycv_ref[:, 128:256]]
        for k, ch in enumerate(chunks):
            sl = slice(128 * k, 128 * (k + 1))
            d, dw = _half_norm_bwd(ch, onw_ref[:, sl], dyn[:, sl], lo)
            dy_ref[:, sl] = d
            dw_ref[0:1, sl] += dw

    row = lambda i: (i, 0)
    return pl.pallas_call(
        body, name="out_bwd", grid=(S // tm,),
        in_specs=[pl.BlockSpec((tm, D), row), pl.BlockSpec((D, D), lambda i: (0, 0)), pl.BlockSpec((tm, 256), row),
                  pl.BlockSpec((tm, 512), row), pl.BlockSpec((tm, 256), row), pl.BlockSpec((1, D), lambda i: (0, 0))],
        out_specs=[pl.BlockSpec((tm, D), row), pl.BlockSpec((8, D), lambda i: (0, 0))],
        out_shape=[jax.ShapeDtypeStruct((S, D), F32), jax.ShapeDtypeStruct((8, D), F32)],
        compiler_params=_cp(("arbitrary",), 40),
    )(dxm, w_out, ysb, ych, ycv, onw)


def _in_bwd(dxm, x, anw, w_sh, pqk, pc, dqa, dka, dva, dqh, dkh, dvb, dycat, cw, qw2, kw2):
    S = x.shape[0]
    tm = min(256, S)
    t8 = tm // 8
    last8 = S // 8 - 1

    def body(dxm_ref, x_ref, anw_ref, w_ref, q0_ref, q1_ref, k0_ref, k1_ref, gb_ref, gc_ref, xc_ref, gch_ref, xch_ref, gbn_ref,
             dqa_ref, dka_ref, dva_ref, dqh_ref, dkh_ref, dvb_ref, dyc_ref, dycn_ref, cw_ref, qw_ref, kw_ref,
             dp_ref, dx_ref, danw_ref, dqw_ref, dkw_ref, dcw_ref):
        i = pl.program_id(0)
        n = pl.num_programs(0)

        @pl.when(i == 0)
        def _():
            for r in (danw_ref, dqw_ref, dkw_ref, dcw_ref):
                r[...] = jnp.zeros_like(r)

        lo = _lo_mask(tm)
        dp_ref[:, QA:QA + 256] = dqa_ref[...].astype(dp_ref.dtype)
        dp_ref[:, KA:KA + 256] = dka_ref[...].astype(dp_ref.dtype)
        dp_ref[:, VA:VA + 256] = dva_ref[...].astype(dp_ref.dtype)
        dp_ref[:, VB:VB + 512] = dvb_ref[...].astype(dp_ref.dtype)
        for base, raws, d_ref, nw_ref, dw_ref in ((QB, (q0_ref, q1_ref), dqh_ref, qw_ref, dqw_ref), (KB, (k0_ref, k1_ref), dkh_ref, kw_ref, dkw_ref)):
            for k in range(4):
                raw = raws[k // 2][:, 128 * (k % 2):128 * (k % 2 + 1)]
                d, dw = _half_norm_bwd(raw, nw_ref[...], d_ref[:, 128 * k:128 * (k + 1)], lo)
                dp_ref[:, base + 128 * k:base + 128 * (k + 1)] = d.astype(dp_ref.dtype)
                dw_ref[0:1, :] += dw
        gb, gc, xc = gb_ref[...], gc_ref[...], xc_ref[...]
        hc_prev = jnp.where(i > 0, gch_ref[...] * xch_ref[...], 0.0)
        conv, hc, r1, r2 = _conv_fwd(gc, xc, hc_prev, cw_ref)
        dyc = dyc_ref[...]
        dconv = dyc * gb
        dconv_next = jnp.where(i < n - 1, dycn_ref[...] * gbn_ref[...], 0.0)
        ext = jnp.concatenate([dconv, dconv_next], axis=0)
        l1 = pltpu.roll(ext, tm + 7, 0)[:tm]
        l2 = pltpu.roll(ext, tm + 6, 0)[:tm]
        dhc = cw_ref[2:3, :] * dconv + cw_ref[1:2, :] * l1 + cw_ref[0:1, :] * l2
        dp_ref[:, GB:GB + 256] = (dyc * conv).astype(dp_ref.dtype)
        dp_ref[:, GC:GC + 256] = (dhc * xc).astype(dp_ref.dtype)
        dp_ref[:, XC:XC + 256] = (dhc * gc).astype(dp_ref.dtype)
        dcw_ref[0:1, :] += jnp.sum(dconv * r2, axis=0, keepdims=True)
        dcw_ref[1:2, :] += jnp.sum(dconv * r1, axis=0, keepdims=True)
        dcw_ref[2:3, :] += jnp.sum(dconv * hc, axis=0, keepdims=True)
        dh = jnp.zeros((tm, D), F32)
        for r in range(N_SH):
            dh = dh + lax.dot_general(dp_ref[:, IN_SH * r:IN_SH * (r + 1)], w_ref[r], NT, preferred_element_type=F32)
        dxn, dw = _row_norm_bwd(x_ref[...], anw_ref[...], dh)
        dx_ref[...] = dxm_ref[...] + dxn
        danw_ref[0:1, :] += dw

    row = lambda i: (i, 0)
    colb = lambda w, cb: pl.BlockSpec((tm, w), lambda i: (i, cb))
    prev = lambda cb: pl.BlockSpec((8, 256), lambda i: (jnp.maximum(i * t8 - 1, 0), cb))
    nxt = lambda cb: pl.BlockSpec((8, 256), lambda i: (jnp.minimum((i + 1) * t8, last8), cb))
    const = lambda shp: pl.BlockSpec(shp, lambda i: (0,) * len(shp))
    return pl.pallas_call(
        body, name="in_bwd", grid=(S // tm,),
        in_specs=[pl.BlockSpec((tm, D), row), pl.BlockSpec((tm, D), row), const((1, D)), const((N_SH, D, IN_SH)),
                  colb(256, 0), colb(256, 1), colb(256, 2), colb(256, 3),
                  colb(256, 0), colb(256, 1), colb(256, 2), prev(1), prev(2), nxt(0),
                  pl.BlockSpec((tm, 256), row), pl.BlockSpec((tm, 256), row), pl.BlockSpec((tm, 256), row),
                  pl.BlockSpec((tm, 512), row), pl.BlockSpec((tm, 512), row), pl.BlockSpec((tm, 512), row),
                  colb(256, 768 // 256), nxt(768 // 256), const((3, 256)), const((1, 128)), const((1, 128))],
        out_specs=[pl.BlockSpec((tm, D_IN), row), pl.BlockSpec((tm, D), row), const((8, D)), const((8, 128)), const((8, 128)), const((8, 256))],
        out_shape=[jax.ShapeDtypeStruct((S, D_IN), _MXU), jax.ShapeDtypeStruct((S, D), F32), jax.ShapeDtypeStruct((8, D), F32),
                   jax.ShapeDtypeStruct((8, 128), F32), jax.ShapeDtypeStruct((8, 128), F32), jax.ShapeDtypeStruct((8, 256), F32)],
        compiler_params=_cp(("arbitrary",), 56),
    )(dxm, x, anw, w_sh, pqk, pqk, pqk, pqk, pc, pc, pc, pc, pc, pc, dqa, dka, dva, dqh, dkh, dvb, dycat, dycat, cw, qw2, kw2)


def _rows_tile(h):
    return h if h <= 512 else 256


def _add_halves(g, r1, c_arr):
    L, _, R, C = g.shape
    h = R // 2
    tr = _rows_tile(h)
    nb = h // tr

    def body(c_ref, g_ref, r_ref, o_ref, ob_ref):
        s = g_ref[...] + r_ref[...]
        o_ref[...] = s
        ob_ref[...] = s.astype(ob_ref.dtype)

    blk = (1, 1, tr, C)
    return pl.pallas_call(
        body, name="add_halves",
        grid_spec=pltpu.PrefetchScalarGridSpec(
            num_scalar_prefetch=1, grid=(L, N_SH, nb),
            in_specs=[pl.BlockSpec(blk, lambda l, s, i, c: (l, s, c[0] * nb + i, 0)), pl.BlockSpec(blk, lambda l, s, i, c: (l, s, i, 0))],
            out_specs=[pl.BlockSpec(blk, lambda l, s, i, c: (l, s, i, 0))] * 2),
        out_shape=[jax.ShapeDtypeStruct((L, N_SH, h, C), F32), jax.ShapeDtypeStruct((L, N_SH, h, C), jnp.bfloat16)],
        compiler_params=_cp(("parallel", "parallel", "parallel"), 32),
    )(c_arr, g, r1)


def _add_shards(s1, r2, me_arr):
    L, _, h, C = s1.shape
    tr = _rows_tile(h)

    def body(me_ref, s_ref, r_ref, o_ref):
        o_ref[0] = ((s_ref[0, 0] + r_ref[0, 0].astype(F32)) + r_ref[1, 0].astype(F32)) + r_ref[2, 0].astype(F32)

    return pl.pallas_call(
        body, name="add_shards",
        grid_spec=pltpu.PrefetchScalarGridSpec(
            num_scalar_prefetch=1, grid=(L, h // tr),
            in_specs=[pl.BlockSpec((1, 1, tr, C), lambda l, i, me: (l, me[0], i, 0)), pl.BlockSpec((3, 1, tr, C), lambda l, i, me: (0, l, i, 0))],
            out_specs=pl.BlockSpec((1, tr, C), lambda l, i, me: (l, i, 0))),
        out_shape=jax.ShapeDtypeStruct((L, h, C), F32),
        compiler_params=_cp(("parallel", "parallel"), 32),
    )(me_arr, s1, r2)


def _adam_math(w, g, m, v):
    m = ADAM_B1 * m + (1.0 - ADAM_B1) * g
    v = ADAM_B2 * v + (1.0 - ADAM_B2) * (g * g)
    m_hat = m / (1.0 - ADAM_B1 ** ADAM_STEP)
    v_hat = v / (1.0 - ADAM_B2 ** ADAM_STEP)
    return -ADAM_LR * (m_hat / (jnp.sqrt(v_hat) + ADAM_EPS) + ADAM_WD * w), m, v


def _adam_big(w, m, v, mine, theirs, c_arr):
    L, R, C = w.shape
    h = R // 2
    tr = _rows_tile(h)
    nb = h // tr

    def body(c_ref, w_ref, m_ref, v_ref, a_ref, b_ref, g_ref, d_ref, mo_ref, vo_ref):
        g = jnp.where(pl.program_id(1) == c_ref[0], a_ref[...], b_ref[...])
        g_ref[...] = g
        d_ref[...], mo_ref[...], vo_ref[...] = _adam_math(w_ref[...], g, m_ref[...], v_ref[...])

    full = pl.BlockSpec((1, tr, C), lambda l, s, i, c: (l, s * nb + i, 0))
    half = pl.BlockSpec((1, tr, C), lambda l, s, i, c: (l, i, 0))
    return pl.pallas_call(
        body, name="adam_big",
        grid_spec=pltpu.PrefetchScalarGridSpec(num_scalar_prefetch=1, grid=(L, 2, nb), in_specs=[full, full, full, half, half],
                                               out_specs=[full] * 4),
        out_shape=[jax.ShapeDtypeStruct((L, R, C), F32)] * 4,
        compiler_params=_cp(("parallel", "parallel", "parallel"), 32),
    )(c_arr, w, m, v, mine, theirs)


def _adam_small(w, g, m, v):
    def body(w_ref, g_ref, m_ref, v_ref, d_ref, mo_ref, vo_ref):
        d_ref[...], mo_ref[...], vo_ref[...] = _adam_math(w_ref[...], g_ref[...], m_ref[...], v_ref[...])

    return pl.pallas_call(body, name="adam_small", out_shape=[jax.ShapeDtypeStruct(w.shape, F32)] * 3)(w, g, m, v)


def _allreduce_small(buf):
    R = buf.shape[0]

    def body(x_ref, o_ref, gat, ssem, rsem):
        x, y, c = lax.axis_index("x"), lax.axis_index("y"), lax.axis_index("c")
        me = 4 * x + 2 * y + c
        gat[me] = x_ref[...]
        cps = []
        for k in range(1, 8):
            peer = (x ^ (k >> 2), y ^ ((k >> 1) & 1), c ^ (k & 1))
            cps.append(pltpu.make_async_remote_copy(src_ref=x_ref, dst_ref=gat.at[me], send_sem=ssem.at[k - 1], recv_sem=rsem.at[k - 1],
                                                    device_id=peer, device_id_type=MESH))
        for cp in cps:
            cp.start()
        for cp in cps:
            cp.wait()
        acc = gat[0]
        for s in range(1, 8):
            acc = acc + gat[s]
        o_ref[...] = acc

    return pl.pallas_call(
        body, name="allreduce_small", out_shape=jax.ShapeDtypeStruct((R, 128), F32),
        in_specs=[pl.BlockSpec(memory_space=pltpu.VMEM)], out_specs=pl.BlockSpec(memory_space=pltpu.VMEM),
        scratch_shapes=[pltpu.VMEM((8, R, 128), F32), pltpu.SemaphoreType.DMA((7,)), pltpu.SemaphoreType.DMA((7,))],
    )(buf)


def _reduce_scatter(grads):
    x, y, c = lax.axis_index("x"), lax.axis_index("y"), lax.axis_index("c")
    c_arr = jnp.reshape(c, (1,)).astype(jnp.int32)
    me_arr = jnp.reshape(2 * x + y, (1,)).astype(jnp.int32)
    halves = [g.shape[2] // 2 for g in grads]

    n = len(grads)

    def body1(*refs):
        ins, outs = refs[:n], refs[n:2 * n]
        ssem, rsem = refs[2 * n:]
        x, y, c = lax.axis_index("x"), lax.axis_index("y"), lax.axis_index("c")
        cps = []
        for f in range(n):
            h = halves[f]
            src = ins[f].at[:, :, pl.ds(pl.multiple_of((1 - c) * h, 8), h), :]
            cps.append(pltpu.make_async_remote_copy(src_ref=src, dst_ref=outs[f], send_sem=ssem.at[f], recv_sem=rsem.at[f],
                                                    device_id=(x, y, 1 - c), device_id_type=MESH))
        for cp in cps:
            cp.start()
        for cp in cps:
            cp.wait()

    r1 = pl.pallas_call(
        body1, name="rs_sibling_halves",
        out_shape=[jax.ShapeDtypeStruct(g.shape[:2] + (h, g.shape[3]), F32) for g, h in zip(grads, halves)],
        in_specs=[ANY] * n, out_specs=[ANY] * n,
        scratch_shapes=[pltpu.SemaphoreType.DMA((n,)), pltpu.SemaphoreType.DMA((n,))],
    )(*grads)
    s1, s1_wire = zip(*[_add_halves(g, r, c_arr) for g, r in zip(grads, r1)])

    def body2(*refs):
        ins, outs = refs[:n], refs[n:2 * n]
        ssem, rsem = refs[2 * n:]
        x, y, c = lax.axis_index("x"), lax.axis_index("y"), lax.axis_index("c")
        peers = [(1 - x, y), (x, 1 - y), (1 - x, 1 - y)]
        cps = []
        for f in range(n):
            for k, (px, py) in enumerate(peers):
                cps.append(pltpu.make_async_remote_copy(src_ref=ins[f].at[:, 2 * px + py], dst_ref=outs[f].at[k], send_sem=ssem.at[3 * f + k],
                                                        recv_sem=rsem.at[3 * f + k], device_id=(px, py, c), device_id_type=MESH))
        for cp in cps:
            cp.start()
        for cp in cps:
            cp.wait()

    r2 = pl.pallas_call(
        body2, name="rs_chip_shards",
        out_shape=[jax.ShapeDtypeStruct((3, s.shape[0]) + s.shape[2:], s.dtype) for s in s1_wire],
        in_specs=[ANY] * n, out_specs=[ANY] * n,
        scratch_shapes=[pltpu.SemaphoreType.DMA((3 * n,)), pltpu.SemaphoreType.DMA((3 * n,))],
    )(*s1_wire)
    mine = [_add_shards(s, r, me_arr) for s, r in zip(s1, r2)]

    def body3(*refs):
        ins, outs = refs[:n], refs[n:2 * n]
        ssem, rsem = refs[2 * n:]
        x, y, c = lax.axis_index("x"), lax.axis_index("y"), lax.axis_index("c")
        cps = [pltpu.make_async_remote_copy(src_ref=ins[f], dst_ref=outs[f], send_sem=ssem.at[f], recv_sem=rsem.at[f],
                                            device_id=(x, y, 1 - c), device_id_type=MESH) for f in range(n)]
        for cp in cps:
            cp.start()
        for cp in cps:
            cp.wait()

    theirs = pl.pallas_call(
        body3, name="rs_sibling_reduced",
        out_shape=[jax.ShapeDtypeStruct(m.shape, F32) for m in mine],
        in_specs=[ANY] * n, out_specs=[ANY] * n,
        scratch_shapes=[pltpu.SemaphoreType.DMA((n,)), pltpu.SemaphoreType.DMA((n,))],
    )(*mine)
    return list(zip(mine, theirs)), c_arr


def _pad_rows(flat, rows):
    return jnp.pad(flat, (0, rows * 128 - flat.shape[0])).reshape(rows, 128)


class _ShardedWeights:
    split = [True] * 5 + [False]

    def __init__(self, w):
        self.parts = [[w[k][l].astype(_MXU) for k in _BIG] + [jnp.zeros((8, 128), F32).at[:3, :HD].set(w["conv_w"][l])]
                      for l in range(DEPTH)]

    def first(self):
        return self.unpack(_allgather_layer(self.parts[0], self.split))

    def following(self, l):
        return (self.parts[l + 1], self.split) if l + 1 < DEPTH else ((), ())

    @staticmethod
    def unpack(got):
        W = dict(zip(_BIG, got[:5]))
        W["conv_w"] = jnp.concatenate([got[5][r, :3, :HD] for r in range(N_SH)], axis=-1)
        return W


def _local_step(x, target, small, weights):
    S = x.shape[0]
    saved, big = [], []
    W = weights.first()
    for l in range(DEPTH):
        big.append(W)
        anw, onw, fnw = small["attn_norm_w"][l][None], small["out_norm_w"][l][None], small["ffn_norm_w"][l][None]
        qw2 = jnp.tile(small["q_norm_w"][l], 2)[None]
        kw2 = jnp.tile(small["k_norm_w"][l], 2)[None]
        cw = W["conv_w"]
        bias = _bias_tiles(small["rel_bias"][l])
        h, pa, pqk, pv, pc = _attn_in(x, anw, W["w_in"])
        ysb = _sb_fwd(pa)
        ych = _ch_fwd(pqk, pv, bias, qw2, kw2)
        w_out = W["w_out"].reshape(D, D)
        xmid, yn, ycv = _merge_out(x, ysb, ych, pc, cw, onw, w_out)
        outs = _ffn_fwd(xmid, fnw, W["w_gate"], W["w_up"], W["w_down"], *weights.following(l))
        xo, hn, u, t, a = outs[:5]
        saved.append(dict(x=x, h=h, pa=pa, pqk=pqk, pv=pv, pc=pc, ysb=ysb, ych=ych, ycv=ycv, yn=yn, xmid=xmid, hn=hn, u=u, t=t, a=a,
                          anw=anw, onw=onw, fnw=fnw, qw2=qw2, kw2=kw2, cw=cw, bias=bias, w_out=w_out))
        x = xo
        if l + 1 < DEPTH:
            W = weights.unpack(outs[5:])
    dx, lpart = _loss_head(x, target)
    loss = lpart[0, 0]
    grads = [None] * DEPTH
    for l in reversed(range(DEPTH)):
        W, sv = big[l], saved[l]
        du, dt, dxm, dfnw = _ffn_bwd(dx, sv["xmid"], sv["fnw"], W["w_gate"], W["w_up"], W["w_down"], sv["u"], sv["t"])
        g_gate = _wgrad(sv["hn"], du, _tok(D), _tok_sh(FF_SH), (D, FF_SH), "wgrad_gate")
        g_up = _wgrad(sv["hn"], dt, _tok(D), _tok_sh(FF_SH), (D, FF_SH), "wgrad_up")
        g_down = _wgrad(sv["a"], dx, _tok_sh(FF_SH), _tok(D), (FF_SH, D), "wgrad_down")
        dycat, donw = _out_bwd(dxm, sv["w_out"], sv["ysb"], sv["ych"], sv["ycv"], sv["onw"])
        g_out = _wgrad(sv["yn"], dxm, _tok_cols(OUT_SH), _tok(D), (OUT_SH, D), "wgrad_out")
        dqa, dka, dva = _sb_bwd(sv["pa"], dycat)
        dqh, dbias, dkh, dvb = _ch_bwd(sv["pqk"], sv["pv"], sv["bias"], sv["qw2"], sv["kw2"], dycat)
        drelb = _bias_grad(dbias)
        dproj, dx, danw, dqw, dkw, dcw = _in_bwd(dxm, sv["x"], sv["anw"], W["w_in"], sv["pqk"], sv["pc"], dqa, dka, dva, dqh, dkh, dvb, dycat,
                                                 sv["cw"], sv["qw2"], sv["kw2"])
        g_in = _wgrad(sv["h"], dproj, _tok(D), _tok_cols(IN_SH), (D, IN_SH), "wgrad_in")
        grads[l] = dict(w_in=g_in, w_out=g_out, w_gate=g_gate, w_up=g_up, w_down=g_down,
                        attn_norm_w=danw[0], out_norm_w=donw[0], ffn_norm_w=dfnw[0],
                        q_norm_w=dqw[0, :HD] + dqw[0, HD:], k_norm_w=dkw[0, :HD] + dkw[0, HD:], rel_bias=drelb, conv_w=dcw[:3])
    return loss, dx, grads


_BIG = ("w_in", "w_out", "w_gate", "w_up", "w_down")
_SMALL = ("attn_norm_w", "q_norm_w", "k_norm_w", "rel_bias", "out_norm_w", "ffn_norm_w")
_ORDER = ("attn_norm_w", "w_in", "q_norm_w", "k_norm_w", "rel_bias", "conv_w", "out_norm_w", "w_out", "ffn_norm_w", "w_gate", "w_up", "w_down")


def kernel(x, attn_norm_w, w_in, q_norm_w, k_norm_w, rel_bias, conv_w, out_norm_w, w_out, ffn_norm_w, w_gate, w_up, w_down, loss_target, m_attn_norm_w, m_w_in, m_q_norm_w, m_k_norm_w, m_rel_bias, m_conv_w, m_out_norm_w, m_w_out, m_ffn_norm_w, m_w_gate, m_w_up, m_w_down, v_attn_norm_w, v_w_in, v_q_norm_w, v_k_norm_w, v_rel_bias, v_conv_w, v_out_norm_w, v_w_out, v_ffn_norm_w, v_w_gate, v_w_up, v_w_down):
    w = dict(attn_norm_w=attn_norm_w, w_in=w_in, q_norm_w=q_norm_w, k_norm_w=k_norm_w, rel_bias=rel_bias, conv_w=conv_w,
             out_norm_w=out_norm_w, w_out=w_out, ffn_norm_w=ffn_norm_w, w_gate=w_gate, w_up=w_up, w_down=w_down)
    m = dict(attn_norm_w=m_attn_norm_w, w_in=m_w_in, q_norm_w=m_q_norm_w, k_norm_w=m_k_norm_w, rel_bias=m_rel_bias, conv_w=m_conv_w,
             out_norm_w=m_out_norm_w, w_out=m_w_out, ffn_norm_w=m_ffn_norm_w, w_gate=m_w_gate, w_up=m_w_up, w_down=m_w_down)
    v = dict(attn_norm_w=v_attn_norm_w, w_in=v_w_in, q_norm_w=v_q_norm_w, k_norm_w=v_k_norm_w, rel_bias=v_rel_bias, conv_w=v_conv_w,
             out_norm_w=v_out_norm_w, w_out=v_w_out, ffn_norm_w=v_ffn_norm_w, w_gate=v_w_gate, w_up=v_w_up, w_down=v_w_down)
    xi, yi = lax.axis_index("x"), lax.axis_index("y")
    me = 2 * xi + yi

    small = {k: w[k] for k in _SMALL}
    loss, grad_x, grads = _local_step(x[0], loss_target[0], small, _ShardedWeights(w))
    loss = lax.psum(loss, ("x", "y", "c"))

    stacked = [jnp.stack([grads[l][k] for l in range(DEPTH)]) for k in _BIG]
    reduced, c_arr = _reduce_scatter(stacked)
    out = {}
    for k, (mine, theirs) in zip(_BIG, reduced):
        out[k] = _adam_big(w[k], m[k], v[k], mine, theirs, c_arr)

    names = _SMALL + ("conv_w",)
    gflat = jnp.concatenate([jnp.stack([grads[l][k] for l in range(DEPTH)]).reshape(-1) for k in names])
    rows = -(-gflat.shape[0] // 1024) * 8
    gsum = _allreduce_small(_pad_rows(gflat, rows)).reshape(-1)
    gs, off = {}, 0
    for k in names:
        shp = (DEPTH, 3, 4 * HD) if k == "conv_w" else w[k].shape
        size = int(np.prod(shp))
        gs[k] = gsum[off:off + size].reshape(shp)
        off += size
    gs["conv_w"] = lax.dynamic_slice_in_dim(gs["conv_w"], me * HD, HD, axis=2)
    flat = lambda d: jnp.concatenate([d[k].reshape(-1) for k in names])
    rows = -(-flat(w).shape[0] // 1024) * 8
    dsm, msm, vsm = _adam_small(_pad_rows(flat(w), rows), _pad_rows(flat(gs), rows), _pad_rows(flat(m), rows), _pad_rows(flat(v), rows))
    off = 0
    for k in names:
        size = int(np.prod(w[k].shape))
        out[k] = (gs[k],) + tuple(a.reshape(-1)[off:off + size].reshape(w[k].shape) for a in (dsm, msm, vsm))
        off += size

    return (loss, grad_x[None]) + tuple(out[k][j] for j in range(4) for k in _ORDER)
```

```python
import functools

import numpy as np
import jax
import jax.numpy as jnp
from jax import lax
from jax.experimental import pallas as pl
from jax.experimental.pallas import tpu as pltpu

F32 = jnp.float32
_MXU = jnp.bfloat16

D = 1024
DEPTH = 4
HD = 64
D_IN = 3072
N_SH = 4
IN_SH = D_IN // N_SH
FF_SH = 704
OUT_SH = 256
N_HEADS_CH = 8
D_CH = N_HEADS_CH * HD
REL = 257
QA, KA, VA, QB, KB, VB, GB, GC, XC = 0, 256, 512, 768, 1280, 1792, 2304, 2560, 2816
EPS = 1e-6
NEG = -1e30
SB_BLK = 128
CH_TQ = 256
EXP_ZERO = -104.0

ADAM_LR, ADAM_B1, ADAM_B2, ADAM_EPS, ADAM_WD, ADAM_STEP = 0.001, 0.9, 0.999, 1e-08, 0.01, 10

NT = (((1,), (1,)), ((), ()))
TN = (((0,), (0,)), ((), ()))
MESH = pl.DeviceIdType.MESH
ANY = pl.BlockSpec(memory_space=pl.ANY)
MB = 1 << 20


def _cp(sem=None, vmem_mb=None):
    return pltpu.CompilerParams(dimension_semantics=sem, vmem_limit_bytes=None if vmem_mb is None else vmem_mb * MB)


def _mx(a):
    return a.astype(_MXU)


def _dot(a, b):
    return jnp.dot(_mx(a), _mx(b), preferred_element_type=F32)


def _dg(a, b, dims):
    return lax.dot_general(_mx(a), _mx(b), dims, preferred_element_type=F32)


def _sigmoid(u):
    return 1.0 / (1.0 + jnp.exp(-u))


def _row_norm(x, w):
    r = lax.rsqrt(jnp.mean(x * x, axis=-1, keepdims=True) + EPS)
    return x * r * w


def _row_norm_bwd(x, w, dy):
    r = lax.rsqrt(jnp.mean(x * x, axis=-1, keepdims=True) + EPS)
    xh = x * r
    dw = jnp.sum(dy * xh, axis=0, keepdims=True)
    dxh = dy * w
    return r * (dxh - xh * jnp.mean(dxh * xh, axis=-1, keepdims=True)), dw


def _half_rs(x, lo):
    sq = x * x
    s0 = jnp.sum(jnp.where(lo, sq, 0.0), axis=-1, keepdims=True)
    s1 = jnp.sum(jnp.where(lo, 0.0, sq), axis=-1, keepdims=True)
    return jnp.where(lo, lax.rsqrt(s0 * (1.0 / HD) + EPS), lax.rsqrt(s1 * (1.0 / HD) + EPS))


def _half_norm(x, w, lo):
    return x * _half_rs(x, lo) * w


def _half_norm_bwd(x, w, dy, lo):
    r = _half_rs(x, lo)
    xh = x * r
    dw = jnp.sum(dy * xh, axis=0, keepdims=True)
    dxh = dy * w
    pr = dxh * xh
    m0 = jnp.sum(jnp.where(lo, pr, 0.0), axis=-1, keepdims=True)
    m1 = jnp.sum(jnp.where(lo, 0.0, pr), axis=-1, keepdims=True)
    mean = jnp.where(lo, m0, m1) * (1.0 / HD)
    return r * (dxh - xh * mean), dw


def _lo_mask(t):
    return lax.broadcasted_iota(jnp.int32, (t, 128), 1) < HD


class _Gather:
    def __init__(self, ins, outs, sems, split):
        self.n = n = len(ins)
        lsem, ssem, rsem, fsem, gsem = sems
        x, y, c = lax.axis_index("x"), lax.axis_index("y"), lax.axis_index("c")
        me = 2 * x + y
        chips = [(1 - x, y), (x, 1 - y), (1 - x, 1 - y)]

        def half(f):
            h = ins[f].shape[0] // 2
            return pl.ds(pl.multiple_of(c * h, 16), h)

        self.local = [pltpu.make_async_copy(ins[f], outs[f].at[me], lsem.at[f]) for f in range(n)]
        self.sends, self.passed = [], []
        for f in range(n):
            for k, (px, py) in enumerate(chips):
                src, dst = (ins[f].at[half(f)], outs[f].at[me, half(f)]) if split[f] else (ins[f], outs[f].at[me])
                self.sends.append(pltpu.make_async_remote_copy(src_ref=src, dst_ref=dst, send_sem=ssem.at[3 * f + k], recv_sem=rsem.at[3 * f + k],
                                                               device_id=(px, py, c), device_id_type=MESH))
                if split[f]:
                    got = outs[f].at[2 * px + py, half(f)]
                    self.passed.append(pltpu.make_async_remote_copy(src_ref=got, dst_ref=got, send_sem=fsem.at[3 * f + k], recv_sem=gsem.at[3 * f + k],
                                                                    device_id=(x, y, 1 - c), device_id_type=MESH))
                else:
                    self.passed.append(None)

    def start(self):
        for cp in self.local + self.sends:
            cp.start()

    def forward(self):
        for cp, fw in zip(self.sends, self.passed):
            cp.wait_recv()
            if fw is not None:
                fw.start()

    def finish(self):
        passed = [fw for fw in self.passed if fw is not None]
        for fw in passed:
            fw.wait_recv()
        for cp in self.sends + passed:
            cp.wait_send()
        for cp in self.local:
            cp.wait()

    @staticmethod
    def sems(n):
        return [pltpu.SemaphoreType.DMA((n,))] + [pltpu.SemaphoreType.DMA((3 * n,))] * 4

    @staticmethod
    def out_shape(parts):
        return [jax.ShapeDtypeStruct((N_SH,) + p.shape, p.dtype) for p in parts]


def _allgather_layer(parts, split):
    n = len(parts)

    def body(*refs):
        g = _Gather(refs[:n], refs[n:2 * n], refs[2 * n:], split)
        g.start()
        g.forward()
        g.finish()

    return pl.pallas_call(
        body, name="allgather_layer", out_shape=_Gather.out_shape(parts),
        in_specs=[ANY] * n, out_specs=[ANY] * n, scratch_shapes=_Gather.sems(n),
    )(*parts)


def _attn_in(x, nw, w_sh, qw2, kw2):
    S = x.shape[0]
    tm = min(512, S)
    assert (QB, VB, GB) == (IN_SH, 2 * IN_SH + 256, 3 * IN_SH)

    def body(x_ref, nw_ref, w_ref, qw_ref, kw_ref, h_ref, pa_ref, pqk_ref, pn_ref, pv_ref, pc_ref):
        h = _row_norm(x_ref[...], nw_ref[...]).astype(h_ref.dtype)
        h_ref[...] = h
        pa_ref[...] = jnp.dot(h, w_ref[0], preferred_element_type=F32).astype(pa_ref.dtype)
        pqk_ref[:, :IN_SH] = jnp.dot(h, w_ref[1], preferred_element_type=F32)
        p2 = jnp.dot(h, w_ref[2], preferred_element_type=F32)
        pqk_ref[:, IN_SH:] = p2[:, :256]
        pv_ref[...] = p2[:, 256:].astype(pv_ref.dtype)
        pc_ref[...] = jnp.dot(h, w_ref[3], preferred_element_type=F32)
        lo = _lo_mask(tm)
        for k in range(2 * D_CH // 128):
            cols = slice(128 * k, 128 * (k + 1))
            is_q = k < D_CH // 128
            n = _half_norm(pqk_ref[:, cols], (qw_ref if is_q else kw_ref)[...], lo)
            pn_ref[:, cols] = (n * 0.125 if is_q else n).astype(pn_ref.dtype)

    row = lambda i: (i, 0)
    const = lambda shp: pl.BlockSpec(shp, lambda i: (0,) * len(shp))
    return pl.pallas_call(
        body, name="attn_in", grid=(S // tm,),
        in_specs=[pl.BlockSpec((tm, D), row), const((1, D)), const((N_SH, D, IN_SH)), const((1, 128)), const((1, 128))],
        out_specs=[pl.BlockSpec((tm, D), row), pl.BlockSpec((tm, IN_SH), row), pl.BlockSpec((tm, 2 * D_CH), row),
                   pl.BlockSpec((tm, 2 * D_CH), row), pl.BlockSpec((tm, D_CH), row), pl.BlockSpec((tm, IN_SH), row)],
        out_shape=[jax.ShapeDtypeStruct((S, D), _MXU), jax.ShapeDtypeStruct((S, IN_SH), _MXU), jax.ShapeDtypeStruct((S, 2 * D_CH), F32),
                   jax.ShapeDtypeStruct((S, 2 * D_CH), _MXU), jax.ShapeDtypeStruct((S, D_CH), _MXU), jax.ShapeDtypeStruct((S, IN_SH), F32)],
        compiler_params=_cp(("parallel",), 48),
    )(x, nw, w_sh, qw2, kw2)


SB_NB = 3
SB_TK = SB_NB * SB_BLK
SB_QB = 2


def _sb_any_open(states):
    return functools.reduce(jnp.logical_or, [jnp.logical_and(st[0] >= 0, st[1] > EXP_ZERO) for st in states])


def _sb_const_arrays():
    r, c = np.meshgrid(np.arange(SB_TK), np.arange(256), indexing="ij")
    kofs = (r - (r & 127)) + (c & 127)
    r2, c2 = np.meshgrid(np.arange(256), np.arange(256), indexing="ij")
    same = (r2 >= 128) == (c2 >= 128)
    nearer = same & ((r2 & 127) > (c2 & 127))
    at_or_nearer = same & ((r2 & 127) >= (c2 & 127))
    return (jnp.asarray(kofs, jnp.int32), jnp.asarray(kofs - (r & 127), jnp.int32), jnp.asarray(nearer, _MXU), jnp.asarray(at_or_nearer, _MXU))


_SB_CONST_SPECS = [pl.BlockSpec((SB_TK, 256), lambda p, i: (0, 0))] * 2 + [pl.BlockSpec((256, 256), lambda p, i: (0, 0))] * 2


class _SbConsts:
    def __init__(self, kofs_ref, rel_ref, nearer_ref, at_or_nearer_ref):
        self.kofs, self.rel, self.nearer, self.at_or_nearer = kofs_ref, rel_ref, nearer_ref, at_or_nearer_ref
        self.lo = lax.broadcasted_iota(jnp.int32, (SB_BLK, 128), 1) < HD
        self.head0 = lax.broadcasted_iota(jnp.int32, (SB_BLK, 256), 1) < 128


def _two_heads(x, lo):
    return jnp.concatenate([jnp.where(lo, x, 0), jnp.where(lo, 0, x)], axis=0)


def _per_block(vals, head0):
    return jnp.concatenate([jnp.where(head0, v0, v1) for v0, v1 in vals], axis=0)


def _head_sums(x):
    return jnp.sum(x[:, :128], axis=-1, keepdims=True), jnp.sum(x[:, 128:], axis=-1, keepdims=True)


def _sb_tile(q2, k_ref, i, jt, c0, c1, K):
    s0 = pl.multiple_of(jnp.maximum(jt - (SB_NB - 1), 0) * SB_BLK, SB_BLK)
    kt = k_ref[pl.ds(s0, SB_TK), :]
    k2 = [_two_heads(kt[b * SB_BLK:(b + 1) * SB_BLK], K.lo) for b in range(SB_NB)]
    keep = jnp.logical_and(K.rel[...] < i * SB_BLK - s0, K.kofs[...] < (jt + 1) * SB_BLK - s0)
    z = jnp.where(keep, jnp.concatenate([_dg(q2, k2[b], NT) for b in range(SB_NB)], axis=0), NEG)
    lr = -(jnp.maximum(z, 0.0) + jnp.log(1.0 + jnp.exp(-jnp.abs(z))))
    carries = [None] * SB_NB
    for b in reversed(range(SB_NB)):
        carries[b] = (c0, c1)
        s = _head_sums(lr[b * SB_BLK:(b + 1) * SB_BLK])
        c0, c1 = c0 + s[0], c1 + s[1]
    after = _dot(lr, K.nearer[...]) + _per_block(carries, K.head0)
    w = jnp.exp(z + lr + after)
    return s0, k2, z, lr, w, c0, c1


def _sb_fwd(pa):
    S = pa.shape[0]
    nq = S // SB_BLK
    assert nq % SB_QB == 0

    def body(q_ref, k_ref, v_ref, kofs_ref, rel_ref, nearer_ref, aon_ref, o_ref):
        K = _SbConsts(kofs_ref, rel_ref, nearer_ref, aon_ref)
        blocks = [pl.program_id(1) * SB_QB + s for s in range(SB_QB)]
        q2 = [q_ref[s * SB_BLK:(s + 1) * SB_BLK, :] * 0.125 for s in range(SB_QB)]

        def step(st):
            new = []
            for s, (jt, _, acc, c0, c1) in enumerate(st):
                s0, _, _, _, w, c0, c1 = _sb_tile(q2[s], k_ref, blocks[s], jt, c0, c1, K)
                vt = v_ref[pl.ds(s0, SB_TK), :]
                for b in range(SB_NB):
                    rows = slice(b * SB_BLK, (b + 1) * SB_BLK)
                    acc = acc + _dot(w[rows], _two_heads(vt[rows], K.lo))
                new.append((jt - SB_NB, jnp.max(jnp.maximum(c0, c1)), acc, c0, c1))
            return tuple(new)

        zc = jnp.zeros((SB_BLK, 1), F32)
        st = lax.while_loop(_sb_any_open, step, tuple((i, jnp.float32(0.0), jnp.zeros((SB_BLK, 128), F32), zc, zc) for i in blocks))
        for s in range(SB_QB):
            o_ref[s * SB_BLK:(s + 1) * SB_BLK, :] = st[s][2]

    tq = SB_QB * SB_BLK
    return pl.pallas_call(
        body, name="sb_fwd", grid=(2, nq // SB_QB),
        in_specs=[pl.BlockSpec((tq, 128), lambda p, i: (i, QA // 128 + p)),
                  pl.BlockSpec((S, 128), lambda p, i: (0, KA // 128 + p)),
                  pl.BlockSpec((S, 128), lambda p, i: (0, VA // 128 + p))] + _SB_CONST_SPECS,
        out_specs=pl.BlockSpec((tq, 128), lambda p, i: (i, p)),
        out_shape=jax.ShapeDtypeStruct((S, 256), F32),
        compiler_params=_cp(("parallel", "arbitrary"), 40),
    )(pa, pa, pa, *_sb_const_arrays())


def _sb_bwd(pa, dycat):
    S = pa.shape[0]
    nq = S // SB_BLK
    assert nq % SB_QB == 0
    n_steps = nq // SB_QB

    def body(q_ref, k_ref, v_ref, do_ref, kofs_ref, rel_ref, nearer_ref, aon_ref, dq_ref, dk_hbm, dv_hbm, dk_acc, dv_acc, sem):
        p, step_i = pl.program_id(0), pl.program_id(1)
        K = _SbConsts(kofs_ref, rel_ref, nearer_ref, aon_ref)

        @pl.when(step_i == 0)
        def _():
            dk_acc[...] = jnp.zeros_like(dk_acc)
            dv_acc[...] = jnp.zeros_like(dv_acc)

        blocks = [step_i * SB_QB + s for s in range(SB_QB)]
        q2 = [q_ref[s * SB_BLK:(s + 1) * SB_BLK, :] * 0.125 for s in range(SB_QB)]
        do = [_mx(do_ref[s * SB_BLK:(s + 1) * SB_BLK, :]) for s in range(SB_QB)]

        def tile(s, jt, c0, c1):
            s0, k2, z, lr, w, c0, c1 = _sb_tile(q2[s], k_ref, blocks[s], jt, c0, c1, K)
            vt = v_ref[pl.ds(s0, SB_TK), :]
            g = jnp.concatenate([_dg(do[s], _two_heads(vt[b * SB_BLK:(b + 1) * SB_BLK], K.lo), NT) for b in range(SB_NB)], axis=0) * w
            gs = [_head_sums(g[b * SB_BLK:(b + 1) * SB_BLK]) for b in range(SB_NB)]
            return dict(s0=s0, k2=k2, z=z, lr=lr, w=w, g=g, gs=gs, c0=c0, c1=c1, mx=jnp.max(jnp.maximum(c0, c1)),
                        g0=sum(x[0] for x in gs), g1=sum(x[1] for x in gs))

        def grads(s, T, tot, n0, n1):
            nearer = [None] * SB_NB
            for b in reversed(range(SB_NB)):
                nearer[b] = (n0, n1)
                n0, n1 = n0 + T["gs"][b][0], n1 + T["gs"][b][1]
            farther = _per_block([tot] * SB_NB, K.head0) - _per_block(nearer, K.head0) - _dot(T["g"], K.at_or_nearer[...])
            beta = jnp.exp(T["z"] + T["lr"])
            dz = _mx(T["g"] * (1.0 - beta) - farther * beta)
            wm = _mx(T["w"])
            dq = jnp.zeros((SB_BLK, 128), F32)
            dks, dvs = [], []
            for b in range(SB_NB):
                rows = slice(b * SB_BLK, (b + 1) * SB_BLK)
                dq = dq + _dot(dz[rows], T["k2"][b])
                dk2 = _dg(dz[rows], q2[s], TN)
                dv2 = _dg(wm[rows], do[s], TN)
                dks.append(jnp.where(K.lo, dk2[:SB_BLK], dk2[SB_BLK:]))
                dvs.append(jnp.where(K.lo, dv2[:SB_BLK], dv2[SB_BLK:]))
            dk_acc[pl.ds(T["s0"], SB_TK), :] += jnp.concatenate(dks, axis=0)
            dv_acc[pl.ds(T["s0"], SB_TK), :] += jnp.concatenate(dvs, axis=0)
            return 0.125 * dq

        zc = jnp.zeros((SB_BLK, 1), F32)
        first = [tile(s, blocks[s], zc, zc) for s in range(SB_QB)]

        def sum_step(st):
            new = []
            for s, (jt, _, c0, c1, t0, t1) in enumerate(st):
                T = tile(s, jt, c0, c1)
                new.append((jt - SB_NB, T["mx"], T["c0"], T["c1"], t0 + T["g0"], t1 + T["g1"]))
            return tuple(new)

        st = lax.while_loop(_sb_any_open, sum_step,
                            tuple((blocks[s] - SB_NB, T["mx"], T["c0"], T["c1"], T["g0"], T["g1"]) for s, T in enumerate(first)))
        tot = [x[4:] for x in st]

        def grad_step(st):
            new = []
            for s, (jt, _, dq, c0, c1, n0, n1) in enumerate(st):
                T = tile(s, jt, c0, c1)
                new.append((jt - SB_NB, T["mx"], dq + grads(s, T, tot[s], n0, n1), T["c0"], T["c1"], n0 + T["g0"], n1 + T["g1"]))
            return tuple(new)

        st = lax.while_loop(_sb_any_open, grad_step,
                            tuple((blocks[s] - SB_NB, T["mx"], grads(s, T, tot[s], zc, zc), T["c0"], T["c1"], T["g0"], T["g1"])
                                  for s, T in enumerate(first)))
        for s in range(SB_QB):
            dq_ref[s * SB_BLK:(s + 1) * SB_BLK, :] = st[s][2]

        @pl.when(step_i == n_steps - 1)
        def _():
            cols = pl.ds(pl.multiple_of(p * 128, 128), 128)
            ck = pltpu.make_async_copy(dk_acc, dk_hbm.at[:, cols], sem.at[0])
            cv = pltpu.make_async_copy(dv_acc, dv_hbm.at[:, cols], sem.at[1])
            ck.start()
            cv.start()
            ck.wait()
            cv.wait()

    tq = SB_QB * SB_BLK
    return pl.pallas_call(
        body, name="sb_bwd", grid=(2, n_steps),
        in_specs=[pl.BlockSpec((tq, 128), lambda p, i: (i, QA // 128 + p)),
                  pl.BlockSpec((S, 128), lambda p, i: (0, KA // 128 + p)),
                  pl.BlockSpec((S, 128), lambda p, i: (0, VA // 128 + p)),
                  pl.BlockSpec((tq, 128), lambda p, i: (i, p))] + _SB_CONST_SPECS,
        out_specs=[pl.BlockSpec((tq, 128), lambda p, i: (i, p)), ANY, ANY],
        out_shape=[jax.ShapeDtypeStruct((S, 256), F32)] * 3,
        scratch_shapes=[pltpu.VMEM((S, 128), F32), pltpu.VMEM((S, 128), F32), pltpu.SemaphoreType.DMA((2,))],
        compiler_params=_cp(("arbitrary", "arbitrary"), 48),
    )(pa, pa, pa, dycat, *_sb_const_arrays())


def _bias_tiles(rel_bias_l):
    r = np.arange(CH_TQ)[:, None]
    c = np.arange(CH_TQ)[None, :]
    valid = np.stack([(4 * b + c // 64 >= r // 64) & (4 * b + c // 64 <= r // 64 + 8) for b in range(3)])
    rev = rel_bias_l[:, _DIAG_IDX[:, ::-1]]
    rows = jnp.broadcast_to(jnp.pad(rev, ((0, 0), (0, 0), (0, 1)))[:, :, None, :], (N_HEADS_CH, 3, CH_TQ, 2 * CH_TQ))
    skew = rows.reshape(N_HEADS_CH, 3, -1)[:, :, :CH_TQ * (2 * CH_TQ - 1)].reshape(N_HEADS_CH, 3, CH_TQ, 2 * CH_TQ - 1)
    return jnp.where(valid[None], skew[:, :, :, CH_TQ - 1:], NEG)


_DIAG_IDX = np.stack([np.clip(512 - 256 * b + np.arange(2 * CH_TQ - 1) - (CH_TQ - 1), -128, 128) + 128 for b in range(3)])


def _bias_grad(dbias):
    d = jnp.pad(dbias, ((0, 0), (0, 0), (0, 0), (CH_TQ - 1, 0))).reshape(N_HEADS_CH, 3, -1)
    d = jnp.pad(d, ((0, 0), (0, 0), (0, CH_TQ))).reshape(N_HEADS_CH, 3, CH_TQ, 2 * CH_TQ)
    diag = _col_sum(d)[:, :, 0, :2 * CH_TQ - 1][:, :, ::-1]
    onehot = (_DIAG_IDX[:, :, None] == np.arange(REL)[None, None, :]).astype(np.float32)
    return jnp.einsum("hbk,bkr->hr", diag, jnp.asarray(onehot), precision=lax.Precision.HIGHEST)


def _ch_scores(i, q_ref, k_refs, b_ref):
    lo = _lo_mask(CH_TQ)
    q = q_ref[...]
    qh = [jnp.where(lo, q, 0), jnp.where(lo, 0, q)]
    kn = [kr[...] for kr in k_refs]
    probs = []
    for h in range(2):
        s = [_dg(qh[h], kn[b], NT) + b_ref[h, b] for b in range(3)]
        s[0] = jnp.where(i >= 2, s[0], NEG)
        s[1] = jnp.where(i >= 1, s[1], NEG)
        m = jnp.max(jnp.maximum(jnp.maximum(s[0], s[1]), s[2]), axis=-1, keepdims=True)
        e = [jnp.exp(sb - m) for sb in s]
        l = jnp.sum((e[0] + e[1]) + e[2], axis=-1, keepdims=True)
        probs.append((e, l))
    return lo, kn, qh, probs


def _ch_specs(S):
    nt = S // CH_TQ

    def kspec(b, base):
        return pl.BlockSpec((CH_TQ, 128), lambda p, i: (jnp.maximum(i - 2 + b, 0), base // 128 + p))

    specs = [pl.BlockSpec((CH_TQ, 128), lambda p, i: (i, p))]
    specs += [kspec(b, D_CH) for b in range(3)] + [kspec(b, 0) for b in range(3)]
    specs += [pl.BlockSpec((2, 3, CH_TQ, CH_TQ), lambda p, i: (p, 0, 0, 0))]
    return nt, specs


def _ch_fwd(pn, pv, bias):
    S = pn.shape[0]
    nt, specs = _ch_specs(S)

    def body(q_ref, k0, k1, k2, v0, v1, v2, b_ref, o_ref):
        i = pl.program_id(1)
        lo, _, _, probs = _ch_scores(i, q_ref, (k0, k1, k2), b_ref)
        vs = [v0[...], v1[...], v2[...]]
        outs = []
        for h in range(2):
            e, l = probs[h]
            pv = _dot(e[0], vs[0]) + _dot(e[1], vs[1]) + _dot(e[2], vs[2])
            outs.append(pv / l)
        o_ref[...] = jnp.where(lo, outs[0], outs[1])

    return pl.pallas_call(
        body, name="ch_fwd", grid=(4, nt), in_specs=specs,
        out_specs=pl.BlockSpec((CH_TQ, 128), lambda p, i: (i, p)),
        out_shape=jax.ShapeDtypeStruct((S, 512), F32),
        compiler_params=_cp(("parallel", "arbitrary"), 40),
    )(pn, pn, pn, pn, pv, pv, pv, bias)


def _ch_bwd(pn, pv, bias, dycat):
    S = pn.shape[0]
    nt, specs = _ch_specs(S)
    specs = specs + [pl.BlockSpec((CH_TQ, 128), lambda p, i: (i, 256 // 128 + p))]

    def body(q_ref, k0, k1, k2, v0, v1, v2, b_ref, do_ref, dq_ref, db_ref, dk_hbm, dv_hbm, dk_acc, dv_acc, sem):
        p, i = pl.program_id(0), pl.program_id(1)

        @pl.when(i == 0)
        def _():
            dk_acc[...] = jnp.zeros_like(dk_acc)
            dv_acc[...] = jnp.zeros_like(dv_acc)
            db_ref[...] = jnp.zeros_like(db_ref)

        lo, kn, qh, probs = _ch_scores(i, q_ref, (k0, k1, k2), b_ref)
        vs = [v0[...], v1[...], v2[...]]
        do = do_ref[...]
        dos = [_mx(jnp.where(lo, do, 0.0)), _mx(jnp.where(lo, 0.0, do))]
        dqn = []
        dk_b = [jnp.zeros((CH_TQ, 128), F32) for _ in range(3)]
        dv_b = [jnp.zeros((CH_TQ, 128), F32) for _ in range(3)]
        for h in range(2):
            e, l = probs[h]
            inv = 1.0 / l
            pr = [eb * inv for eb in e]
            dp = [_dg(dos[h], vs[b], NT) for b in range(3)]
            delta = jnp.sum((pr[0] * dp[0] + pr[1] * dp[1]) + pr[2] * dp[2], axis=-1, keepdims=True)
            ds = [pr[b] * (dp[b] - delta) for b in range(3)]
            acc = jnp.zeros((CH_TQ, 128), F32)
            for b in range(3):
                db_ref[h, b] += ds[b]
                dsm = _mx(ds[b])
                acc = acc + _dot(dsm, kn[b])
                dk_b[b] = dk_b[b] + _dg(dsm, qh[h], TN)
                dv_b[b] = dv_b[b] + _dg(pr[b], dos[h], TN)
            dqn.append(acc)
        dq_ref[...] = 0.125 * jnp.where(lo, dqn[0], dqn[1])

        for b in range(3):
            @pl.when(i - 2 + b >= 0)
            def _(b=b):
                rows = pl.ds(pl.multiple_of((i - 2 + b) * CH_TQ, CH_TQ), CH_TQ)
                dk_acc[rows, :] += dk_b[b]
                dv_acc[rows, :] += dv_b[b]

        @pl.when(i == nt - 1)
        def _():
            cols = pl.ds(pl.multiple_of(p * 128, 128), 128)
            ck = pltpu.make_async_copy(dk_acc, dk_hbm.at[:, cols], sem.at[0])
            cv = pltpu.make_async_copy(dv_acc, dv_hbm.at[:, cols], sem.at[1])
            ck.start()
            cv.start()
            ck.wait()
            cv.wait()

    return pl.pallas_call(
        body, name="ch_bwd", grid=(4, nt), in_specs=specs,
        out_specs=[pl.BlockSpec((CH_TQ, 128), lambda p, i: (i, p)),
                   pl.BlockSpec((2, 3, CH_TQ, CH_TQ), lambda p, i: (p, 0, 0, 0)), ANY, ANY],
        out_shape=[jax.ShapeDtypeStruct((S, 512), F32), jax.ShapeDtypeStruct((N_HEADS_CH, 3, CH_TQ, CH_TQ), F32),
                   jax.ShapeDtypeStruct((S, 512), F32), jax.ShapeDtypeStruct((S, 512), F32)],
        scratch_shapes=[pltpu.VMEM((S, 128), F32), pltpu.VMEM((S, 128), F32), pltpu.SemaphoreType.DMA((2,))],
        compiler_params=_cp(("arbitrary", "arbitrary"), 48),
    )(pn, pn, pn, pn, pv, pv, pv, bias, dycat)


def _col_sum(d):
    _, _, R, C = d.shape

    def body(d_ref, o_ref):
        for b in range(3):
            o_ref[0, b] = jnp.sum(d_ref[0, b], axis=0, keepdims=True)

    return pl.pallas_call(
        body, name="col_sum", grid=(N_HEADS_CH,),
        in_specs=[pl.BlockSpec((1, 3, R, C), lambda h: (h, 0, 0, 0))],
        out_specs=pl.BlockSpec((1, 3, 1, C), lambda h: (h, 0, 0, 0)),
        out_shape=jax.ShapeDtypeStruct((N_HEADS_CH, 3, 1, C), F32),
        compiler_params=_cp(("parallel",), 32),
    )(d)


def _conv_fwd(gc, xc, hc_prev, cw_ref):
    hc = gc * xc
    ext = jnp.concatenate([hc_prev, hc], axis=0)
    r1 = pltpu.roll(ext, 1, 0)[8:]
    r2 = pltpu.roll(ext, 2, 0)[8:]
    return cw_ref[0:1, :] * r2 + cw_ref[1:2, :] * r1 + cw_ref[2:3, :] * hc, hc, r1, r2


def _merge_out(x, ysb, ych, pc, cw, onw, w_out):
    S = x.shape[0]
    tm = min(512, S)
    t8 = tm // 8

    def body(x_ref, ysb_ref, ych_ref, gb_ref, gc_ref, xc_ref, gch_ref, xch_ref, cw_ref, onw_ref, w_ref, xo_ref, yn_ref, ycv_ref):
        i = pl.program_id(0)
        lo = _lo_mask(tm)
        hc_prev = jnp.where(i > 0, gch_ref[...] * xch_ref[...], 0.0)
        conv, _, _, _ = _conv_fwd(gc_ref[...], xc_ref[...], hc_prev, cw_ref)
        ycv = gb_ref[...] * conv
        ycv_ref[...] = ycv
        chunks = [ysb_ref[:, 0:128], ysb_ref[:, 128:256]] + [ych_ref[:, 128 * k:128 * (k + 1)] for k in range(4)]
        chunks += [ycv[:, 0:128], ycv[:, 128:256]]
        for k, ch in enumerate(chunks):
            yn_ref[:, 128 * k:128 * (k + 1)] = _half_norm(ch, onw_ref[:, 128 * k:128 * (k + 1)], lo).astype(yn_ref.dtype)
        xo_ref[...] = x_ref[...] + jnp.dot(yn_ref[...], w_ref[...], preferred_element_type=F32)

    row = lambda i: (i, 0)
    halo = lambda cb: pl.BlockSpec((8, 256), lambda i: (jnp.maximum(i * t8 - 1, 0), cb))
    return pl.pallas_call(
        body, name="merge_out", grid=(S // tm,),
        in_specs=[pl.BlockSpec((tm, D), row), pl.BlockSpec((tm, 256), row), pl.BlockSpec((tm, 512), row),
                  pl.BlockSpec((tm, 256), lambda i: (i, 0)), pl.BlockSpec((tm, 256), lambda i: (i, 1)),
                  pl.BlockSpec((tm, 256), lambda i: (i, 2)), halo(1), halo(2),
                  pl.BlockSpec((3, 256), lambda i: (0, 0)), pl.BlockSpec((1, D), lambda i: (0, 0)), pl.BlockSpec((D, D), lambda i: (0, 0))],
        out_specs=[pl.BlockSpec((tm, D), row), pl.BlockSpec((tm, D), row), pl.BlockSpec((tm, 256), row)],
        out_shape=[jax.ShapeDtypeStruct((S, D), F32), jax.ShapeDtypeStruct((S, D), _MXU), jax.ShapeDtypeStruct((S, 256), F32)],
        compiler_params=_cp(("parallel",), 40),
    )(x, ysb, ych, pc, pc, pc, pc, pc, cw, onw, w_out)


def _ffn_fwd(x, nw, wg, wu, wd, nxt=(), split=()):
    S = x.shape[0]
    tm = min(1024, S)
    ni = S // tm
    n = len(nxt)

    def body(*refs):
        x_ref, nw_ref, wg_ref, wu_ref, wd_ref = refs[:5]
        xo_ref, hn_ref, u_ref, t_ref, a_ref = refs[5 + n:10 + n]
        acc = refs[10 + 2 * n]
        i, j = pl.program_id(0), pl.program_id(1)
        if n:
            gather = _Gather(refs[5:5 + n], refs[10 + n:10 + 2 * n], refs[11 + 2 * n:], split)

            @pl.when(jnp.logical_and(i == 0, j == 0))
            def _():
                gather.start()

            @pl.when(jnp.logical_and(i == max(ni - 2, 0), j == 0))
            def _():
                gather.forward()

        @pl.when(j == 0)
        def _():
            hn_ref[...] = _row_norm(x_ref[...], nw_ref[...]).astype(hn_ref.dtype)
            acc[...] = jnp.zeros_like(acc)

        hn = hn_ref[...]
        u = jnp.dot(hn, wg_ref[0], preferred_element_type=F32)
        t = jnp.dot(hn, wu_ref[0], preferred_element_type=F32)
        a = (u * _sigmoid(u) * t).astype(a_ref.dtype)
        u_ref[0], t_ref[0], a_ref[0] = u.astype(u_ref.dtype), t.astype(t_ref.dtype), a
        acc[...] += jnp.dot(a, wd_ref[0], preferred_element_type=F32)

        @pl.when(j == N_SH - 1)
        def _():
            xo_ref[...] = x_ref[...] + acc[...]

        if n:
            @pl.when(jnp.logical_and(i == ni - 1, j == N_SH - 1))
            def _():
                gather.finish()

    row = lambda i, j: (i, 0)
    sh = lambda i, j: (j, i, 0)
    return pl.pallas_call(
        body, name="ffn_fwd_gather" if n else "ffn_fwd", grid=(ni, N_SH),
        in_specs=[pl.BlockSpec((tm, D), row), pl.BlockSpec((1, D), lambda i, j: (0, 0)),
                  pl.BlockSpec((1, D, FF_SH), lambda i, j: (j, 0, 0)), pl.BlockSpec((1, D, FF_SH), lambda i, j: (j, 0, 0)),
                  pl.BlockSpec((1, FF_SH, D), lambda i, j: (j, 0, 0))] + [ANY] * n,
        out_specs=[pl.BlockSpec((tm, D), row), pl.BlockSpec((tm, D), row), pl.BlockSpec((1, tm, FF_SH), sh),
                   pl.BlockSpec((1, tm, FF_SH), sh), pl.BlockSpec((1, tm, FF_SH), sh)] + [ANY] * n,
        out_shape=[jax.ShapeDtypeStruct((S, D), F32), jax.ShapeDtypeStruct((S, D), _MXU),
                   jax.ShapeDtypeStruct((N_SH, S, FF_SH), _MXU), jax.ShapeDtypeStruct((N_SH, S, FF_SH), _MXU),
                   jax.ShapeDtypeStruct((N_SH, S, FF_SH), _MXU)] + _Gather.out_shape(nxt),
        scratch_shapes=[pltpu.VMEM((tm, D), F32)] + (_Gather.sems(n) if n else []),
        compiler_params=_cp(("arbitrary", "arbitrary"), 60),
    )(x, nw, wg, wu, wd, *nxt)


def _loss_head(x, target):
    S = x.shape[0]
    tm = min(512, S)

    def body(x_ref, t_ref, dx_ref, l_ref):
        @pl.when(pl.program_id(0) == 0)
        def _():
            l_ref[...] = jnp.zeros_like(l_ref)

        e = x_ref[...] - t_ref[...]
        dx_ref[...] = e * (1.0 / D)
        l_ref[...] += 0.5 * jnp.sum(jnp.sum(e * e, axis=-1, keepdims=True) * (1.0 / D))

    return pl.pallas_call(
        body, name="loss_head", grid=(S // tm,),
        in_specs=[pl.BlockSpec((tm, D), lambda i: (i, 0))] * 2,
        out_specs=[pl.BlockSpec((tm, D), lambda i: (i, 0)), pl.BlockSpec((8, 128), lambda i: (0, 0))],
        out_shape=[jax.ShapeDtypeStruct((S, D), F32), jax.ShapeDtypeStruct((8, 128), F32)],
        compiler_params=_cp(("arbitrary",), 32),
    )(x, target)


def _ffn_bwd(dx, xmid, nw, wg, wu, wd, u, t):
    S = dx.shape[0]
    tm = min(1024, S)

    def body(dx_ref, x_ref, nw_ref, wg_ref, wu_ref, wd_ref, u_ref, t_ref, du_ref, dt_ref, dxm_ref, dnw_ref, acc):
        i, j = pl.program_id(0), pl.program_id(1)

        @pl.when(j == 0)
        def _():
            acc[...] = jnp.zeros_like(acc)

        @pl.when(jnp.logical_and(i == 0, j == 0))
        def _():
            dnw_ref[...] = jnp.zeros_like(dnw_ref)

        da = _dg(dx_ref[...], wd_ref[0], NT)
        u, t = u_ref[0].astype(F32), t_ref[0].astype(F32)
        sg = _sigmoid(u)
        du = _mx(da * t * (sg * (1.0 + u * (1.0 - sg))))
        dt = _mx(da * (u * sg))
        du_ref[0], dt_ref[0] = du, dt
        acc[...] += _dg(du, wg_ref[0], NT) + _dg(dt, wu_ref[0], NT)

        @pl.when(j == N_SH - 1)
        def _():
            dxn, dw = _row_norm_bwd(x_ref[...], nw_ref[...], acc[...])
            dxm_ref[...] = dx_ref[...] + dxn
            dnw_ref[0:1, :] += dw

    row = lambda i, j: (i, 0)
    sh = lambda i, j: (j, i, 0)
    wsp = lambda r, c: pl.BlockSpec((1, r, c), lambda i, j: (j, 0, 0))
    return pl.pallas_call(
        body, name="ffn_bwd", grid=(S // tm, N_SH),
        in_specs=[pl.BlockSpec((tm, D), row), pl.BlockSpec((tm, D), row), pl.BlockSpec((1, D), lambda i, j: (0, 0)),
                  wsp(D, FF_SH), wsp(D, FF_SH), wsp(FF_SH, D), pl.BlockSpec((1, tm, FF_SH), sh), pl.BlockSpec((1, tm, FF_SH), sh)],
        out_specs=[pl.BlockSpec((1, tm, FF_SH), sh), pl.BlockSpec((1, tm, FF_SH), sh), pl.BlockSpec((tm, D), row),
                   pl.BlockSpec((8, D), lambda i, j: (0, 0))],
        out_shape=[jax.ShapeDtypeStruct((N_SH, S, FF_SH), _MXU), jax.ShapeDtypeStruct((N_SH, S, FF_SH), _MXU),
                   jax.ShapeDtypeStruct((S, D), F32), jax.ShapeDtypeStruct((8, D), F32)],
        scratch_shapes=[pltpu.VMEM((tm, D), F32)],
        compiler_params=_cp(("arbitrary", "arbitrary"), 60),
    )(dx, xmid, nw, wg, wu, wd, u, t)


def _wgrad(a, b, a_spec, b_spec, out_block, name):
    S = a.shape[-2]
    tk = min(1024, S)

    def body(a_ref, b_ref, o_ref):
        @pl.when(pl.program_id(0) == 0)
        def _():
            o_ref[...] = jnp.zeros_like(o_ref)

        for r in range(N_SH):
            o_ref[r] += _dg(a_spec[1](a_ref, r), b_spec[1](b_ref, r), TN)

    return pl.pallas_call(
        body, name=name, grid=(S // tk,),
        in_specs=[a_spec[0](tk), b_spec[0](tk)],
        out_specs=pl.BlockSpec((N_SH,) + out_block, lambda k: (0, 0, 0)),
        out_shape=jax.ShapeDtypeStruct((N_SH,) + out_block, F32),
        compiler_params=_cp(("arbitrary",), 56),
    )(a, b)


def _tok(width):
    return (lambda tk: pl.BlockSpec((tk, width), lambda k: (k, 0))), (lambda ref, r: ref[...])


def _tok_cols(width):
    return (lambda tk: pl.BlockSpec((tk, N_SH * width), lambda k: (k, 0))), (lambda ref, r: ref[:, r * width:(r + 1) * width])


def _tok_sh(width):
    return (lambda tk: pl.BlockSpec((N_SH, tk, width), lambda k: (0, k, 0))), (lambda ref, r: ref[r])


def _out_bwd(dxm, w_out, ysb, ych, ycv, onw):
    S = dxm.shape[0]
    tm = min(512, S)

    def body(dx_ref, w_ref, ysb_ref, ych_ref, ycv_ref, onw_ref, dy_ref, dw_ref):
        @pl.when(pl.program_id(0) == 0)
        def _():
            dw_ref[...] = jnp.zeros_like(dw_ref)

        lo = _lo_mask(tm)
        dyn = _dg(dx_ref[...], w_ref[...], NT)
        chunks = [ysb_ref[:, 0:128], ysb_ref[:, 128:256]] + [ych_ref[:, 128 * k:128 * (k + 1)] for k in range(4)]
        chunks += [ycv_ref[:, 0:128], ycv_ref[:, 128:256]]
        for k, ch in enumerate(chunks):
            sl = slice(128 * k, 128 * (k + 1))
            d, dw = _half_norm_bwd(ch, onw_ref[:, sl], dyn[:, sl], lo)
            dy_ref[:, sl] = d
            dw_ref[0:1, sl] += dw

    row = lambda i: (i, 0)
    return pl.pallas_call(
        body, name="out_bwd", grid=(S // tm,),
        in_specs=[pl.BlockSpec((tm, D), row), pl.BlockSpec((D, D), lambda i: (0, 0)), pl.BlockSpec((tm, 256), row),
                  pl.BlockSpec((tm, 512), row), pl.BlockSpec((tm, 256), row), pl.BlockSpec((1, D), lambda i: (0, 0))],
        out_specs=[pl.BlockSpec((tm, D), row), pl.BlockSpec((8, D), lambda i: (0, 0))],
        out_shape=[jax.ShapeDtypeStruct((S, D), F32), jax.ShapeDtypeStruct((8, D), F32)],
        compiler_params=_cp(("arbitrary",), 40),
    )(dxm, w_out, ysb, ych, ycv, onw)


def _in_bwd(dxm, x, anw, w_sh, pqk, pc, dqa, dka, dva, dqh, dkh, dvb, dycat, cw, qw2, kw2):
    S = x.shape[0]
    tm = min(256, S)
    t8 = tm // 8
    last8 = S // 8 - 1

    def body(dxm_ref, x_ref, anw_ref, w_ref, q0_ref, q1_ref, k0_ref, k1_ref, gb_ref, gc_ref, xc_ref, gch_ref, xch_ref, gbn_ref,
             dqa_ref, dka_ref, dva_ref, dqh_ref, dkh_ref, dvb_ref, dyc_ref, dycn_ref, cw_ref, qw_ref, kw_ref,
             dp_ref, dx_ref, danw_ref, dqw_ref, dkw_ref, dcw_ref):
        i = pl.program_id(0)
        n = pl.num_programs(0)

        @pl.when(i == 0)
        def _():
            for r in (danw_ref, dqw_ref, dkw_ref, dcw_ref):
                r[...] = jnp.zeros_like(r)

        lo = _lo_mask(tm)
        dp_ref[:, QA:QA + 256] = dqa_ref[...].astype(dp_ref.dtype)
        dp_ref[:, KA:KA + 256] = dka_ref[...].astype(dp_ref.dtype)
        dp_ref[:, VA:VA + 256] = dva_ref[...].astype(dp_ref.dtype)
        dp_ref[:, VB:VB + 512] = dvb_ref[...].astype(dp_ref.dtype)
        for base, raws, d_ref, nw_ref, dw_ref in ((QB, (q0_ref, q1_ref), dqh_ref, qw_ref, dqw_ref), (KB, (k0_ref, k1_ref), dkh_ref, kw_ref, dkw_ref)):
            for k in range(4):
                raw = raws[k // 2][:, 128 * (k % 2):128 * (k % 2 + 1)]
                d, dw = _half_norm_bwd(raw, nw_ref[...], d_ref[:, 128 * k:128 * (k + 1)], lo)
                dp_ref[:, base + 128 * k:base + 128 * (k + 1)] = d.astype(dp_ref.dtype)
                dw_ref[0:1, :] += dw
        gb, gc, xc = gb_ref[...], gc_ref[...], xc_ref[...]
        hc_prev = jnp.where(i > 0, gch_ref[...] * xch_ref[...], 0.0)
        conv, hc, r1, r2 = _conv_fwd(gc, xc, hc_prev, cw_ref)
        dyc = dyc_ref[...]
        dconv = dyc * gb
        dconv_next = jnp.where(i < n - 1, dycn_ref[...] * gbn_ref[...], 0.0)
        ext = jnp.concatenate([dconv, dconv_next], axis=0)
        l1 = pltpu.roll(ext, tm + 7, 0)[:tm]
        l2 = pltpu.roll(ext, tm + 6, 0)[:tm]
        dhc = cw_ref[2:3, :] * dconv + cw_ref[1:2, :] * l1 + cw_ref[0:1, :] * l2
        dp_ref[:, GB:GB + 256] = (dyc * conv).astype(dp_ref.dtype)
        dp_ref[:, GC:GC + 256] = (dhc * xc).astype(dp_ref.dtype)
        dp_ref[:, XC:XC + 256] = (dhc * gc).astype(dp_ref.dtype)
        dcw_ref[0:1, :] += jnp.sum(dconv * r2, axis=0, keepdims=True)
        dcw_ref[1:2, :] += jnp.sum(dconv * r1, axis=0, keepdims=True)
        dcw_ref[2:3, :] += jnp.sum(dconv * hc, axis=0, keepdims=True)
        dh = jnp.zeros((tm, D), F32)
        for r in range(N_SH):
            dh = dh + lax.dot_general(dp_ref[:, IN_SH * r:IN_SH * (r + 1)], w_ref[r], NT, preferred_element_type=F32)
        dxn, dw = _row_norm_bwd(x_ref[...], anw_ref[...], dh)
        dx_ref[...] = dxm_ref[...] + dxn
        danw_ref[0:1, :] += dw

    row = lambda i: (i, 0)
    colb = lambda w, cb: pl.BlockSpec((tm, w), lambda i: (i, cb))
    prev = lambda cb: pl.BlockSpec((8, 256), lambda i: (jnp.maximum(i * t8 - 1, 0), cb))
    nxt = lambda cb: pl.BlockSpec((8, 256), lambda i: (jnp.minimum((i + 1) * t8, last8), cb))
    const = lambda shp: pl.BlockSpec(shp, lambda i: (0,) * len(shp))
    return pl.pallas_call(
        body, name="in_bwd", grid=(S // tm,),
        in_specs=[pl.BlockSpec((tm, D), row), pl.BlockSpec((tm, D), row), const((1, D)), const((N_SH, D, IN_SH)),
                  colb(256, 0), colb(256, 1), colb(256, 2), colb(256, 3),
                  colb(256, 0), colb(256, 1), colb(256, 2), prev(1), prev(2), nxt(0),
                  pl.BlockSpec((tm, 256), row), pl.BlockSpec((tm, 256), row), pl.BlockSpec((tm, 256), row),
                  pl.BlockSpec((tm, 512), row), pl.BlockSpec((tm, 512), row), pl.BlockSpec((tm, 512), row),
                  colb(256, 768 // 256), nxt(768 // 256), const((3, 256)), const((1, 128)), const((1, 128))],
        out_specs=[pl.BlockSpec((tm, D_IN), row), pl.BlockSpec((tm, D), row), const((8, D)), const((8, 128)), const((8, 128)), const((8, 256))],
        out_shape=[jax.ShapeDtypeStruct((S, D_IN), _MXU), jax.ShapeDtypeStruct((S, D), F32), jax.ShapeDtypeStruct((8, D), F32),
                   jax.ShapeDtypeStruct((8, 128), F32), jax.ShapeDtypeStruct((8, 128), F32), jax.ShapeDtypeStruct((8, 256), F32)],
        compiler_params=_cp(("arbitrary",), 56),
    )(dxm, x, anw, w_sh, pqk, pqk, pqk, pqk, pc, pc, pc, pc, pc, pc, dqa, dka, dva, dqh, dkh, dvb, dycat, dycat, cw, qw2, kw2)


def _rows_tile(h):
    return h if h <= 512 else 256


def _add_halves(g, r1, c_arr):
    L, _, R, C = g.shape
    h = R // 2
    tr = _rows_tile(h)
    nb = h // tr

    def body(c_ref, g_ref, r_ref, o_ref, ob_ref):
        s = g_ref[...] + r_ref[...]
        o_ref[...] = s
        ob_ref[...] = s.astype(ob_ref.dtype)

    blk = (1, 1, tr, C)
    return pl.pallas_call(
        body, name="add_halves",
        grid_spec=pltpu.PrefetchScalarGridSpec(
            num_scalar_prefetch=1, grid=(L, N_SH, nb),
            in_specs=[pl.BlockSpec(blk, lambda l, s, i, c: (l, s, c[0] * nb + i, 0)), pl.BlockSpec(blk, lambda l, s, i, c: (l, s, i, 0))],
            out_specs=[pl.BlockSpec(blk, lambda l, s, i, c: (l, s, i, 0))] * 2),
        out_shape=[jax.ShapeDtypeStruct((L, N_SH, h, C), F32), jax.ShapeDtypeStruct((L, N_SH, h, C), jnp.bfloat16)],
        compiler_params=_cp(("parallel", "parallel", "parallel"), 32),
    )(c_arr, g, r1)


def _add_shards(s1, r2, me_arr):
    L, _, h, C = s1.shape
    tr = _rows_tile(h)

    def body(me_ref, s_ref, r_ref, o_ref):
        o_ref[0] = ((s_ref[0, 0] + r_ref[0, 0].astype(F32)) + r_ref[1, 0].astype(F32)) + r_ref[2, 0].astype(F32)

    return pl.pallas_call(
        body, name="add_shards",
        grid_spec=pltpu.PrefetchScalarGridSpec(
            num_scalar_prefetch=1, grid=(L, h // tr),
            in_specs=[pl.BlockSpec((1, 1, tr, C), lambda l, i, me: (l, me[0], i, 0)), pl.BlockSpec((3, 1, tr, C), lambda l, i, me: (0, l, i, 0))],
            out_specs=pl.BlockSpec((1, tr, C), lambda l, i, me: (l, i, 0))),
        out_shape=jax.ShapeDtypeStruct((L, h, C), F32),
        compiler_params=_cp(("parallel", "parallel"), 32),
    )(me_arr, s1, r2)


def _adam_math(w, g, m, v):
    m = ADAM_B1 * m + (1.0 - ADAM_B1) * g
    v = ADAM_B2 * v + (1.0 - ADAM_B2) * (g * g)
    m_hat = m / (1.0 - ADAM_B1 ** ADAM_STEP)
    v_hat = v / (1.0 - ADAM_B2 ** ADAM_STEP)
    return -ADAM_LR * (m_hat / (jnp.sqrt(v_hat) + ADAM_EPS) + ADAM_WD * w), m, v


def _adam_big(w, m, v, mine, theirs, c_arr):
    L, R, C = w.shape
    h = R // 2
    tr = _rows_tile(h)
    nb = h // tr

    def body(c_ref, w_ref, m_ref, v_ref, a_ref, b_ref, g_ref, d_ref, mo_ref, vo_ref):
        g = jnp.where(pl.program_id(1) == c_ref[0], a_ref[...], b_ref[...])
        g_ref[...] = g
        d_ref[...], mo_ref[...], vo_ref[...] = _adam_math(w_ref[...], g, m_ref[...], v_ref[...])

    full = pl.BlockSpec((1, tr, C), lambda l, s, i, c: (l, s * nb + i, 0))
    half = pl.BlockSpec((1, tr, C), lambda l, s, i, c: (l, i, 0))
    return pl.pallas_call(
        body, name="adam_big",
        grid_spec=pltpu.PrefetchScalarGridSpec(num_scalar_prefetch=1, grid=(L, 2, nb), in_specs=[full, full, full, half, half],
                                               out_specs=[full] * 4),
        out_shape=[jax.ShapeDtypeStruct((L, R, C), F32)] * 4,
        compiler_params=_cp(("parallel", "parallel", "parallel"), 32),
    )(c_arr, w, m, v, mine, theirs)


def _adam_small(w, g, m, v):
    def body(w_ref, g_ref, m_ref, v_ref, d_ref, mo_ref, vo_ref):
        d_ref[...], mo_ref[...], vo_ref[...] = _adam_math(w_ref[...], g_ref[...], m_ref[...], v_ref[...])

    return pl.pallas_call(body, name="adam_small", out_shape=[jax.ShapeDtypeStruct(w.shape, F32)] * 3)(w, g, m, v)


def _allreduce_small(buf):
    R = buf.shape[0]

    def body(x_ref, o_ref, gat, ssem, rsem):
        x, y, c = lax.axis_index("x"), lax.axis_index("y"), lax.axis_index("c")
        me = 4 * x + 2 * y + c
        gat[me] = x_ref[...]
        cps = []
        for k in range(1, 8):
            peer = (x ^ (k >> 2), y ^ ((k >> 1) & 1), c ^ (k & 1))
            cps.append(pltpu.make_async_remote_copy(src_ref=x_ref, dst_ref=gat.at[me], send_sem=ssem.at[k - 1], recv_sem=rsem.at[k - 1],
                                                    device_id=peer, device_id_type=MESH))
        for cp in cps:
            cp.start()
        for cp in cps:
            cp.wait()
        acc = gat[0]
        for s in range(1, 8):
            acc = acc + gat[s]
        o_ref[...] = acc

    return pl.pallas_call(
        body, name="allreduce_small", out_shape=jax.ShapeDtypeStruct((R, 128), F32),
        in_specs=[pl.BlockSpec(memory_space=pltpu.VMEM)], out_specs=pl.BlockSpec(memory_space=pltpu.VMEM),
        scratch_shapes=[pltpu.VMEM((8, R, 128), F32), pltpu.SemaphoreType.DMA((7,)), pltpu.SemaphoreType.DMA((7,))],
    )(buf)


def _reduce_scatter(grads):
    x, y, c = lax.axis_index("x"), lax.axis_index("y"), lax.axis_index("c")
    c_arr = jnp.reshape(c, (1,)).astype(jnp.int32)
    me_arr = jnp.reshape(2 * x + y, (1,)).astype(jnp.int32)
    halves = [g.shape[2] // 2 for g in grads]

    n = len(grads)

    def body1(*refs):
        ins, outs = refs[:n], refs[n:2 * n]
        ssem, rsem = refs[2 * n:]
        x, y, c = lax.axis_index("x"), lax.axis_index("y"), lax.axis_index("c")
        cps = []
        for f in range(n):
            h = halves[f]
            src = ins[f].at[:, :, pl.ds(pl.multiple_of((1 - c) * h, 8), h), :]
            cps.append(pltpu.make_async_remote_copy(src_ref=src, dst_ref=outs[f], send_sem=ssem.at[f], recv_sem=rsem.at[f],
                                                    device_id=(x, y, 1 - c), device_id_type=MESH))
        for cp in cps:
            cp.start()
        for cp in cps:
            cp.wait()

    r1 = pl.pallas_call(
        body1, name="rs_sibling_halves",
        out_shape=[jax.ShapeDtypeStruct(g.shape[:2] + (h, g.shape[3]), F32) for g, h in zip(grads, halves)],
        in_specs=[ANY] * n, out_specs=[ANY] * n,
        scratch_shapes=[pltpu.SemaphoreType.DMA((n,)), pltpu.SemaphoreType.DMA((n,))],
    )(*grads)
    s1, s1_wire = zip(*[_add_halves(g, r, c_arr) for g, r in zip(grads, r1)])

    def body2(*refs):
        ins, outs = refs[:n], refs[n:2 * n]
        ssem, rsem = refs[2 * n:]
        x, y, c = lax.axis_index("x"), lax.axis_index("y"), lax.axis_index("c")
        peers = [(1 - x, y), (x, 1 - y), (1 - x, 1 - y)]
        cps = []
        for f in range(n):
            for k, (px, py) in enumerate(peers):
                cps.append(pltpu.make_async_remote_copy(src_ref=ins[f].at[:, 2 * px + py], dst_ref=outs[f].at[k], send_sem=ssem.at[3 * f + k],
                                                        recv_sem=rsem.at[3 * f + k], device_id=(px, py, c), device_id_type=MESH))
        for cp in cps:
            cp.start()
        for cp in cps:
            cp.wait()

    r2 = pl.pallas_call(
        body2, name="rs_chip_shards",
        out_shape=[jax.ShapeDtypeStruct((3, s.shape[0]) + s.shape[2:], s.dtype) for s in s1_wire],
        in_specs=[ANY] * n, out_specs=[ANY] * n,
        scratch_shapes=[pltpu.SemaphoreType.DMA((3 * n,)), pltpu.SemaphoreType.DMA((3 * n,))],
    )(*s1_wire)
    mine = [_add_shards(s, r, me_arr) for s, r in zip(s1, r2)]

    def body3(*refs):
        ins, outs = refs[:n], refs[n:2 * n]
        ssem, rsem = refs[2 * n:]
        x, y, c = lax.axis_index("x"), lax.axis_index("y"), lax.axis_index("c")
        cps = [pltpu.make_async_remote_copy(src_ref=ins[f], dst_ref=outs[f], send_sem=ssem.at[f], recv_sem=rsem.at[f],
                                            device_id=(x, y, 1 - c), device_id_type=MESH) for f in range(n)]
        for cp in cps:
            cp.start()
        for cp in cps:
            cp.wait()

    theirs = pl.pallas_call(
        body3, name="rs_sibling_reduced",
        out_shape=[jax.ShapeDtypeStruct(m.shape, F32) for m in mine],
        in_specs=[ANY] * n, out_specs=[ANY] * n,
        scratch_shapes=[pltpu.SemaphoreType.DMA((n,)), pltpu.SemaphoreType.DMA((n,))],
    )(*mine)
    return list(zip(mine, theirs)), c_arr


def _pad_rows(flat, rows):
    return jnp.pad(flat, (0, rows * 128 - flat.shape[0])).reshape(rows, 128)


class _ShardedWeights:
    split = [True] * 5 + [False]

    def __init__(self, w):
        self.parts = [[w[k][l].astype(_MXU) for k in _BIG] + [jnp.zeros((8, 128), F32).at[:3, :HD].set(w["conv_w"][l])]
                      for l in range(DEPTH)]

    def first(self):
        return self.unpack(_allgather_layer(self.parts[0], self.split))

    def following(self, l):
        return (self.parts[l + 1], self.split) if l + 1 < DEPTH else ((), ())

    @staticmethod
    def unpack(got):
        W = dict(zip(_BIG, got[:5]))
        W["conv_w"] = jnp.concatenate([got[5][r, :3, :HD] for r in range(N_SH)], axis=-1)
        return W


def _local_step(x, target, small, weights):
    S = x.shape[0]
    saved, big = [], []
    W = weights.first()
    for l in range(DEPTH):
        big.append(W)
        anw, onw, fnw = small["attn_norm_w"][l][None], small["out_norm_w"][l][None], small["ffn_norm_w"][l][None]
        qw2 = jnp.tile(small["q_norm_w"][l], 2)[None]
        kw2 = jnp.tile(small["k_norm_w"][l], 2)[None]
        cw = W["conv_w"]
        bias = _bias_tiles(small["rel_bias"][l])
        h, pa, pqk, pn, pv, pc = _attn_in(x, anw, W["w_in"], qw2, kw2)
        ysb = _sb_fwd(pa)
        ych = _ch_fwd(pn, pv, bias)
        w_out = W["w_out"].reshape(D, D)
        xmid, yn, ycv = _merge_out(x, ysb, ych, pc, cw, onw, w_out)
        outs = _ffn_fwd(xmid, fnw, W["w_gate"], W["w_up"], W["w_down"], *weights.following(l))
        xo, hn, u, t, a = outs[:5]
        saved.append(dict(x=x, h=h, pa=pa, pqk=pqk, pn=pn, pv=pv, pc=pc, ysb=ysb, ych=ych, ycv=ycv, yn=yn, xmid=xmid, hn=hn, u=u, t=t, a=a,
                          anw=anw, onw=onw, fnw=fnw, qw2=qw2, kw2=kw2, cw=cw, bias=bias, w_out=w_out))
        x = xo
        if l + 1 < DEPTH:
            W = weights.unpack(outs[5:])
    dx, lpart = _loss_head(x, target)
    loss = lpart[0, 0]
    grads = [None] * DEPTH
    for l in reversed(range(DEPTH)):
        W, sv = big[l], saved[l]
        du, dt, dxm, dfnw = _ffn_bwd(dx, sv["xmid"], sv["fnw"], W["w_gate"], W["w_up"], W["w_down"], sv["u"], sv["t"])
        g_gate = _wgrad(sv["hn"], du, _tok(D), _tok_sh(FF_SH), (D, FF_SH), "wgrad_gate")
        g_up = _wgrad(sv["hn"], dt, _tok(D), _tok_sh(FF_SH), (D, FF_SH), "wgrad_up")
        g_down = _wgrad(sv["a"], dx, _tok_sh(FF_SH), _tok(D), (FF_SH, D), "wgrad_down")
        dycat, donw = _out_bwd(dxm, sv["w_out"], sv["ysb"], sv["ych"], sv["ycv"], sv["onw"])
        g_out = _wgrad(sv["yn"], dxm, _tok_cols(OUT_SH), _tok(D), (OUT_SH, D), "wgrad_out")
        dqa, dka, dva = _sb_bwd(sv["pa"], dycat)
        dqh, dbias, dkh, dvb = _ch_bwd(sv["pn"], sv["pv"], sv["bias"], dycat)
        drelb = _bias_grad(dbias)
        dproj, dx, danw, dqw, dkw, dcw = _in_bwd(dxm, sv["x"], sv["anw"], W["w_in"], sv["pqk"], sv["pc"], dqa, dka, dva, dqh, dkh, dvb, dycat,
                                                 sv["cw"], sv["qw2"], sv["kw2"])
        g_in = _wgrad(sv["h"], dproj, _tok(D), _tok_cols(IN_SH), (D, IN_SH), "wgrad_in")
        grads[l] = dict(w_in=g_in, w_out=g_out, w_gate=g_gate, w_up=g_up, w_down=g_down,
                        attn_norm_w=danw[0], out_norm_w=donw[0], ffn_norm_w=dfnw[0],
                        q_norm_w=dqw[0, :HD] + dqw[0, HD:], k_norm_w=dkw[0, :HD] + dkw[0, HD:], rel_bias=drelb, conv_w=dcw[:3])
    return loss, dx, grads


_BIG = ("w_in", "w_out", "w_gate", "w_up", "w_down")
_SMALL = ("attn_norm_w", "q_norm_w", "k_norm_w", "rel_bias", "out_norm_w", "ffn_norm_w")
_ORDER = ("attn_norm_w", "w_in", "q_norm_w", "k_norm_w", "rel_bias", "conv_w", "out_norm_w", "w_out", "ffn_norm_w", "w_gate", "w_up", "w_down")


def kernel(x, attn_norm_w, w_in, q_norm_w, k_norm_w, rel_bias, conv_w, out_norm_w, w_out, ffn_norm_w, w_gate, w_up, w_down, loss_target, m_attn_norm_w, m_w_in, m_q_norm_w, m_k_norm_w, m_rel_bias, m_conv_w, m_out_norm_w, m_w_out, m_ffn_norm_w, m_w_gate, m_w_up, m_w_down, v_attn_norm_w, v_w_in, v_q_norm_w, v_k_norm_w, v_rel_bias, v_conv_w, v_out_norm_w, v_w_out, v_ffn_norm_w, v_w_gate, v_w_up, v_w_down):
    w = dict(attn_norm_w=attn_norm_w, w_in=w_in, q_norm_w=q_norm_w, k_norm_w=k_norm_w, rel_bias=rel_bias, conv_w=conv_w,
             out_norm_w=out_norm_w, w_out=w_out, ffn_norm_w=ffn_norm_w, w_gate=w_gate, w_up=w_up, w_down=w_down)
    m = dict(attn_norm_w=m_attn_norm_w, w_in=m_w_in, q_norm_w=m_q_norm_w, k_norm_w=m_k_norm_w, rel_bias=m_rel_bias, conv_w=m_conv_w,
             out_norm_w=m_out_norm_w, w_out=m_w_out, ffn_norm_w=m_ffn_norm_w, w_gate=m_w_gate, w_up=m_w_up, w_down=m_w_down)
    v = dict(attn_norm_w=v_attn_norm_w, w_in=v_w_in, q_norm_w=v_q_norm_w, k_norm_w=v_k_norm_w, rel_bias=v_rel_bias, conv_w=v_conv_w,
             out_norm_w=v_out_norm_w, w_out=v_w_out, ffn_norm_w=v_ffn_norm_w, w_gate=v_w_gate, w_up=v_w_up, w_down=v_w_down)
    xi, yi = lax.axis_index("x"), lax.axis_index("y")
    me = 2 * xi + yi

    small = {k: w[k] for k in _SMALL}
    loss, grad_x, grads = _local_step(x[0], loss_target[0], small, _ShardedWeights(w))
    loss = lax.psum(loss, ("x", "y", "c"))

    stacked = [jnp.stack([grads[l][k] for l in range(DEPTH)]) for k in _BIG]
    reduced, c_arr = _reduce_scatter(stacked)
    out = {}
    for k, (mine, theirs) in zip(_BIG, reduced):
        out[k] = _adam_big(w[k], m[k], v[k], mine, theirs, c_arr)

    names = _SMALL + ("conv_w",)
    gflat = jnp.concatenate([jnp.stack([grads[l][k] for l in range(DEPTH)]).reshape(-1) for k in names])
    rows = -(-gflat.shape[0] // 1024) * 8
    gsum = _allreduce_small(_pad_rows(gflat, rows)).reshape(-1)
    gs, off = {}, 0
    for k in names:
        shp = (DEPTH, 3, 4 * HD) if k == "conv_w" else w[k].shape
        size = int(np.prod(shp))
        gs[k] = gsum[off:off + size].reshape(shp)
        off += size
    gs["conv_w"] = lax.dynamic_slice_in_dim(gs["conv_w"], me * HD, HD, axis=2)
    flat = lambda d: jnp.concatenate([d[k].reshape(-1) for k in names])
    rows = -(-flat(w).shape[0] // 1024) * 8
    dsm, msm, vsm = _adam_small(_pad_rows(flat(w), rows), _pad_rows(flat(gs), rows), _pad_rows(flat(m), rows), _pad_rows(flat(v), rows))
    off = 0
    for k in names:
        size = int(np.prod(w[k].shape))
        out[k] = (gs[k],) + tuple(a.reshape(-1)[off:off + size].reshape(w[k].shape) for a in (dsm, msm, vsm))
        off += size

    return (loss, grad_x[None]) + tuple(out[k][j] for j in range(4) for k in _ORDER)
```

```python
import functools

import numpy as np
import jax
import jax.numpy as jnp
from jax import lax
from jax.experimental import pallas as pl
from jax.experimental.pallas import tpu as pltpu

F32 = jnp.float32
_MXU = jnp.bfloat16

D = 1024
DEPTH = 4
HD = 64
D_IN = 3072
N_SH = 4
IN_SH = D_IN // N_SH
FF_SH = 704
OUT_SH = 256
N_HEADS_CH = 8
D_CH = N_HEADS_CH * HD
REL = 257
QA, KA, VA, QB, KB, VB, GB, GC, XC = 0, 256, 512, 768, 1280, 1792, 2304, 2560, 2816
EPS = 1e-6
NEG = -1e30
FFN_SUB = 256
SB_BLK = 128
CH_TQ = 256
CH_SUB = 256
EXP_ZERO = -104.0

ADAM_LR, ADAM_B1, ADAM_B2, ADAM_EPS, ADAM_WD, ADAM_STEP = 0.001, 0.9, 0.999, 1e-08, 0.01, 10

NT = (((1,), (1,)), ((), ()))
TN = (((0,), (0,)), ((), ()))
MESH = pl.DeviceIdType.MESH
ANY = pl.BlockSpec(memory_space=pl.ANY)
MB = 1 << 20


def _cp(sem=None, vmem_mb=None):
    return pltpu.CompilerParams(dimension_semantics=sem, vmem_limit_bytes=None if vmem_mb is None else vmem_mb * MB)


def _mx(a):
    return a.astype(_MXU)


def _dot(a, b):
    return jnp.dot(_mx(a), _mx(b), preferred_element_type=F32)


def _dg(a, b, dims):
    return lax.dot_general(_mx(a), _mx(b), dims, preferred_element_type=F32)


def _sigmoid(u):
    return 1.0 / (1.0 + jnp.exp(-u))


def _row_norm(x, w):
    r = lax.rsqrt(jnp.mean(x * x, axis=-1, keepdims=True) + EPS)
    return x * r * w


def _row_norm_bwd(x, w, dy):
    r = lax.rsqrt(jnp.mean(x * x, axis=-1, keepdims=True) + EPS)
    xh = x * r
    dw = jnp.sum(dy * xh, axis=0, keepdims=True)
    dxh = dy * w
    return r * (dxh - xh * jnp.mean(dxh * xh, axis=-1, keepdims=True)), dw


def _half_rs(x, lo):
    sq = x * x
    s0 = jnp.sum(jnp.where(lo, sq, 0.0), axis=-1, keepdims=True)
    s1 = jnp.sum(jnp.where(lo, 0.0, sq), axis=-1, keepdims=True)
    return jnp.where(lo, lax.rsqrt(s0 * (1.0 / HD) + EPS), lax.rsqrt(s1 * (1.0 / HD) + EPS))


def _half_norm(x, w, lo):
    return x * _half_rs(x, lo) * w


def _half_norm_bwd(x, w, dy, lo):
    r = _half_rs(x, lo)
    xh = x * r
    dw = jnp.sum(dy * xh, axis=0, keepdims=True)
    dxh = dy * w
    pr = dxh * xh
    m0 = jnp.sum(jnp.where(lo, pr, 0.0), axis=-1, keepdims=True)
    m1 = jnp.sum(jnp.where(lo, 0.0, pr), axis=-1, keepdims=True)
    mean = jnp.where(lo, m0, m1) * (1.0 / HD)
    return r * (dxh - xh * mean), dw


def _lo_mask(t):
    return lax.broadcasted_iota(jnp.int32, (t, 128), 1) < HD


class _Gather:
    def __init__(self, ins, outs, sems, split):
        self.n = n = len(ins)
        lsem, ssem, rsem, fsem, gsem = sems
        x, y, c = lax.axis_index("x"), lax.axis_index("y"), lax.axis_index("c")
        me = 2 * x + y
        chips = [(1 - x, y), (x, 1 - y), (1 - x, 1 - y)]

        def half(f):
            h = ins[f].shape[0] // 2
            return pl.ds(pl.multiple_of(c * h, 16), h)

        self.local = [pltpu.make_async_copy(ins[f], outs[f].at[me], lsem.at[f]) for f in range(n)]
        self.sends, self.passed = [], []
        for f in range(n):
            for k, (px, py) in enumerate(chips):
                src, dst = (ins[f].at[half(f)], outs[f].at[me, half(f)]) if split[f] else (ins[f], outs[f].at[me])
                self.sends.append(pltpu.make_async_remote_copy(src_ref=src, dst_ref=dst, send_sem=ssem.at[3 * f + k], recv_sem=rsem.at[3 * f + k],
                                                               device_id=(px, py, c), device_id_type=MESH))
                if split[f]:
                    got = outs[f].at[2 * px + py, half(f)]
                    self.passed.append(pltpu.make_async_remote_copy(src_ref=got, dst_ref=got, send_sem=fsem.at[3 * f + k], recv_sem=gsem.at[3 * f + k],
                                                                    device_id=(x, y, 1 - c), device_id_type=MESH))
                else:
                    self.passed.append(None)

    def start(self):
        for cp in self.local + self.sends:
            cp.start()

    def forward(self):
        for cp, fw in zip(self.sends, self.passed):
            cp.wait_recv()
            if fw is not None:
                fw.start()

    def finish(self):
        passed = [fw for fw in self.passed if fw is not None]
        for fw in passed:
            fw.wait_recv()
        for cp in self.sends + passed:
            cp.wait_send()
        for cp in self.local:
            cp.wait()

    @staticmethod
    def sems(n):
        return [pltpu.SemaphoreType.DMA((n,))] + [pltpu.SemaphoreType.DMA((3 * n,))] * 4

    @staticmethod
    def out_shape(parts):
        return [jax.ShapeDtypeStruct((N_SH,) + p.shape, p.dtype) for p in parts]


def _allgather_layer(parts, split):
    n = len(parts)

    def body(*refs):
        g = _Gather(refs[:n], refs[n:2 * n], refs[2 * n:], split)
        g.start()
        g.forward()
        g.finish()

    return pl.pallas_call(
        body, name="allgather_layer", out_shape=_Gather.out_shape(parts),
        in_specs=[ANY] * n, out_specs=[ANY] * n, scratch_shapes=_Gather.sems(n),
    )(*parts)


def _attn_in(x, nw, w_sh, qw2, kw2):
    S = x.shape[0]
    tm = min(512, S)
    assert (QB, VB, GB) == (IN_SH, 2 * IN_SH + 256, 3 * IN_SH)

    def body(x_ref, nw_ref, w_ref, qw_ref, kw_ref, h_ref, pa_ref, pqk_ref, pn_ref, pv_ref, pc_ref):
        h = _row_norm(x_ref[...], nw_ref[...]).astype(h_ref.dtype)
        h_ref[...] = h
        pa_ref[...] = jnp.dot(h, w_ref[0], preferred_element_type=F32).astype(pa_ref.dtype)
        pqk_ref[:, :IN_SH] = jnp.dot(h, w_ref[1], preferred_element_type=F32)
        p2 = jnp.dot(h, w_ref[2], preferred_element_type=F32)
        pqk_ref[:, IN_SH:] = p2[:, :256]
        pv_ref[...] = p2[:, 256:].astype(pv_ref.dtype)
        pc_ref[...] = jnp.dot(h, w_ref[3], preferred_element_type=F32)
        lo = _lo_mask(tm)
        for k in range(2 * D_CH // 128):
            cols = slice(128 * k, 128 * (k + 1))
            is_q = k < D_CH // 128
            n = _half_norm(pqk_ref[:, cols], (qw_ref if is_q else kw_ref)[...], lo)
            pn_ref[:, cols] = (n * 0.125 if is_q else n).astype(pn_ref.dtype)

    row = lambda i: (i, 0)
    const = lambda shp: pl.BlockSpec(shp, lambda i: (0,) * len(shp))
    return pl.pallas_call(
        body, name="attn_in", grid=(S // tm,),
        in_specs=[pl.BlockSpec((tm, D), row), const((1, D)), const((N_SH, D, IN_SH)), const((1, 128)), const((1, 128))],
        out_specs=[pl.BlockSpec((tm, D), row), pl.BlockSpec((tm, IN_SH), row), pl.BlockSpec((tm, 2 * D_CH), row),
                   pl.BlockSpec((tm, 2 * D_CH), row), pl.BlockSpec((tm, D_CH), row), pl.BlockSpec((tm, IN_SH), row)],
        out_shape=[jax.ShapeDtypeStruct((S, D), _MXU), jax.ShapeDtypeStruct((S, IN_SH), _MXU), jax.ShapeDtypeStruct((S, 2 * D_CH), F32),
                   jax.ShapeDtypeStruct((S, 2 * D_CH), _MXU), jax.ShapeDtypeStruct((S, D_CH), _MXU), jax.ShapeDtypeStruct((S, IN_SH), F32)],
        compiler_params=_cp(("parallel",), 48),
    )(x, nw, w_sh, qw2, kw2)


SB_NB = 3
SB_TK = SB_NB * SB_BLK
SB_QB = 4


def _sb_any_open(states):
    return functools.reduce(jnp.logical_or, [jnp.logical_and(st[0] >= 0, st[1] > EXP_ZERO) for st in states])


def _sb_const_arrays():
    r, c = np.meshgrid(np.arange(SB_TK), np.arange(256), indexing="ij")
    kofs = (r - (r & 127)) + (c & 127)
    r2, c2 = np.meshgrid(np.arange(256), np.arange(256), indexing="ij")
    same = (r2 >= 128) == (c2 >= 128)
    nearer = same & ((r2 & 127) > (c2 & 127))
    at_or_nearer = same & ((r2 & 127) >= (c2 & 127))
    return (jnp.asarray(kofs, jnp.int32), jnp.asarray(kofs - (r & 127), jnp.int32), jnp.asarray(nearer, _MXU), jnp.asarray(at_or_nearer, _MXU))


_SB_CONST_SPECS = [pl.BlockSpec((SB_TK, 256), lambda p, i: (0, 0))] * 2 + [pl.BlockSpec((256, 256), lambda p, i: (0, 0))] * 2


class _SbConsts:
    def __init__(self, kofs_ref, rel_ref, nearer_ref, at_or_nearer_ref):
        self.kofs, self.rel, self.nearer, self.at_or_nearer = kofs_ref, rel_ref, nearer_ref, at_or_nearer_ref
        self.lo = lax.broadcasted_iota(jnp.int32, (SB_BLK, 128), 1) < HD
        self.head0 = lax.broadcasted_iota(jnp.int32, (SB_BLK, 256), 1) < 128


def _two_heads(x, lo):
    return jnp.concatenate([jnp.where(lo, x, 0), jnp.where(lo, 0, x)], axis=0)


def _per_block(vals, head0):
    return jnp.concatenate([jnp.where(head0, v0, v1) for v0, v1 in vals], axis=0)


def _head_sums(x):
    return jnp.sum(x[:, :128], axis=-1, keepdims=True), jnp.sum(x[:, 128:], axis=-1, keepdims=True)


def _sb_tile(q2, k_ref, i, jt, c0, c1, K):
    s0 = pl.multiple_of(jnp.maximum(jt - (SB_NB - 1), 0) * SB_BLK, SB_BLK)
    kt = k_ref[pl.ds(s0, SB_TK), :]
    k2 = [_two_heads(kt[b * SB_BLK:(b + 1) * SB_BLK], K.lo) for b in range(SB_NB)]
    keep = jnp.logical_and(K.rel[...] < i * SB_BLK - s0, K.kofs[...] < (jt + 1) * SB_BLK - s0)
    z = jnp.where(keep, jnp.concatenate([_dg(q2, k2[b], NT) for b in range(SB_NB)], axis=0), NEG)
    lr = -(jnp.maximum(z, 0.0) + jnp.log(1.0 + jnp.exp(-jnp.abs(z))))
    carries = [None] * SB_NB
    for b in reversed(range(SB_NB)):
        carries[b] = (c0, c1)
        s = _head_sums(lr[b * SB_BLK:(b + 1) * SB_BLK])
        c0, c1 = c0 + s[0], c1 + s[1]
    after = _dot(lr, K.nearer[...]) + _per_block(carries, K.head0)
    w = jnp.exp(z + lr + after)
    return s0, k2, z, lr, w, c0, c1


def _sb_fwd(pa):
    S = pa.shape[0]
    nq = S // SB_BLK
    assert nq % SB_QB == 0

    def body(q_ref, k_ref, v_ref, kofs_ref, rel_ref, nearer_ref, aon_ref, o_ref):
        K = _SbConsts(kofs_ref, rel_ref, nearer_ref, aon_ref)
        blocks = [pl.program_id(1) * SB_QB + s for s in range(SB_QB)]
        q2 = [q_ref[s * SB_BLK:(s + 1) * SB_BLK, :] * 0.125 for s in range(SB_QB)]

        def step(st):
            new = []
            for s, (jt, _, acc, c0, c1) in enumerate(st):
                s0, _, _, _, w, c0, c1 = _sb_tile(q2[s], k_ref, blocks[s], jt, c0, c1, K)
                vt = v_ref[pl.ds(s0, SB_TK), :]
                for b in range(SB_NB):
                    rows = slice(b * SB_BLK, (b + 1) * SB_BLK)
                    acc = acc + _dot(w[rows], _two_heads(vt[rows], K.lo))
                new.append((jt - SB_NB, jnp.max(jnp.maximum(c0, c1)), acc, c0, c1))
            return tuple(new)

        zc = jnp.zeros((SB_BLK, 1), F32)
        st = lax.while_loop(_sb_any_open, step, tuple((i, jnp.float32(0.0), jnp.zeros((SB_BLK, 128), F32), zc, zc) for i in blocks))
        for s in range(SB_QB):
            o_ref[s * SB_BLK:(s + 1) * SB_BLK, :] = st[s][2]

    tq = SB_QB * SB_BLK
    return pl.pallas_call(
        body, name="sb_fwd", grid=(2, nq // SB_QB),
        in_specs=[pl.BlockSpec((tq, 128), lambda p, i: (i, QA // 128 + p)),
                  pl.BlockSpec((S, 128), lambda p, i: (0, KA // 128 + p)),
                  pl.BlockSpec((S, 128), lambda p, i: (0, VA // 128 + p))] + _SB_CONST_SPECS,
        out_specs=pl.BlockSpec((tq, 128), lambda p, i: (i, p)),
        out_shape=jax.ShapeDtypeStruct((S, 256), F32),
        compiler_params=_cp(("parallel", "arbitrary"), 40),
    )(pa, pa, pa, *_sb_const_arrays())


def _sb_bwd(pa, dycat):
    S = pa.shape[0]
    nq = S // SB_BLK
    assert nq % SB_QB == 0
    n_steps = nq // SB_QB

    def body(q_ref, k_ref, v_ref, do_ref, kofs_ref, rel_ref, nearer_ref, aon_ref, dq_ref, dk_hbm, dv_hbm, dk_acc, dv_acc, sem):
        p, step_i = pl.program_id(0), pl.program_id(1)
        K = _SbConsts(kofs_ref, rel_ref, nearer_ref, aon_ref)

        @pl.when(step_i == 0)
        def _():
            dk_acc[...] = jnp.zeros_like(dk_acc)
            dv_acc[...] = jnp.zeros_like(dv_acc)

        blocks = [step_i * SB_QB + s for s in range(SB_QB)]
        q2 = [q_ref[s * SB_BLK:(s + 1) * SB_BLK, :] * 0.125 for s in range(SB_QB)]
        do = [_mx(do_ref[s * SB_BLK:(s + 1) * SB_BLK, :]) for s in range(SB_QB)]

        def tile(s, jt, c0, c1):
            s0, k2, z, lr, w, c0, c1 = _sb_tile(q2[s], k_ref, blocks[s], jt, c0, c1, K)
            vt = v_ref[pl.ds(s0, SB_TK), :]
            g = jnp.concatenate([_dg(do[s], _two_heads(vt[b * SB_BLK:(b + 1) * SB_BLK], K.lo), NT) for b in range(SB_NB)], axis=0) * w
            gs = [_head_sums(g[b * SB_BLK:(b + 1) * SB_BLK]) for b in range(SB_NB)]
            return dict(s0=s0, k2=k2, z=z, lr=lr, w=w, g=g, gs=gs, c0=c0, c1=c1, mx=jnp.max(jnp.maximum(c0, c1)),
                        g0=sum(x[0] for x in gs), g1=sum(x[1] for x in gs))

        def grads(s, T, tot, n0, n1):
            nearer = [None] * SB_NB
            for b in reversed(range(SB_NB)):
                nearer[b] = (n0, n1)
                n0, n1 = n0 + T["gs"][b][0], n1 + T["gs"][b][1]
            farther = _per_block([tot] * SB_NB, K.head0) - _per_block(nearer, K.head0) - _dot(T["g"], K.at_or_nearer[...])
            beta = jnp.exp(T["z"] + T["lr"])
            dz = _mx(T["g"] * (1.0 - beta) - farther * beta)
            wm = _mx(T["w"])
            dq = jnp.zeros((SB_BLK, 128), F32)
            dks, dvs = [], []
            for b in range(SB_NB):
                rows = slice(b * SB_BLK, (b + 1) * SB_BLK)
                dq = dq + _dot(dz[rows], T["k2"][b])
                dk2 = _dg(dz[rows], q2[s], TN)
                dv2 = _dg(wm[rows], do[s], TN)
                dks.append(jnp.where(K.lo, dk2[:SB_BLK], dk2[SB_BLK:]))
                dvs.append(jnp.where(K.lo, dv2[:SB_BLK], dv2[SB_BLK:]))
            dk_acc[pl.ds(T["s0"], SB_TK), :] += jnp.concatenate(dks, axis=0)
            dv_acc[pl.ds(T["s0"], SB_TK), :] += jnp.concatenate(dvs, axis=0)
            return 0.125 * dq

        zc = jnp.zeros((SB_BLK, 1), F32)
        first = [tile(s, blocks[s], zc, zc) for s in range(SB_QB)]

        def sum_step(st):
            new = []
            for s, (jt, _, c0, c1, t0, t1) in enumerate(st):
                T = tile(s, jt, c0, c1)
                new.append((jt - SB_NB, T["mx"], T["c0"], T["c1"], t0 + T["g0"], t1 + T["g1"]))
            return tuple(new)

        st = lax.while_loop(_sb_any_open, sum_step,
                            tuple((blocks[s] - SB_NB, T["mx"], T["c0"], T["c1"], T["g0"], T["g1"]) for s, T in enumerate(first)))
        tot = [x[4:] for x in st]

        def grad_step(st):
            new = []
            for s, (jt, _, dq, c0, c1, n0, n1) in enumerate(st):
                T = tile(s, jt, c0, c1)
                new.append((jt - SB_NB, T["mx"], dq + grads(s, T, tot[s], n0, n1), T["c0"], T["c1"], n0 + T["g0"], n1 + T["g1"]))
            return tuple(new)

        st = lax.while_loop(_sb_any_open, grad_step,
                            tuple((blocks[s] - SB_NB, T["mx"], grads(s, T, tot[s], zc, zc), T["c0"], T["c1"], T["g0"], T["g1"])
                                  for s, T in enumerate(first)))
        for s in range(SB_QB):
            dq_ref[s * SB_BLK:(s + 1) * SB_BLK, :] = st[s][2]

        @pl.when(step_i == n_steps - 1)
        def _():
            cols = pl.ds(pl.multiple_of(p * 128, 128), 128)
            ck = pltpu.make_async_copy(dk_acc, dk_hbm.at[:, cols], sem.at[0])
            cv = pltpu.make_async_copy(dv_acc, dv_hbm.at[:, cols], sem.at[1])
            ck.start()
            cv.start()
            ck.wait()
            cv.wait()

    tq = SB_QB * SB_BLK
    return pl.pallas_call(
        body, name="sb_bwd", grid=(2, n_steps),
        in_specs=[pl.BlockSpec((tq, 128), lambda p, i: (i, QA // 128 + p)),
                  pl.BlockSpec((S, 128), lambda p, i: (0, KA // 128 + p)),
                  pl.BlockSpec((S, 128), lambda p, i: (0, VA // 128 + p)),
                  pl.BlockSpec((tq, 128), lambda p, i: (i, p))] + _SB_CONST_SPECS,
        out_specs=[pl.BlockSpec((tq, 128), lambda p, i: (i, p)), ANY, ANY],
        out_shape=[jax.ShapeDtypeStruct((S, 256), F32)] * 3,
        scratch_shapes=[pltpu.VMEM((S, 128), F32), pltpu.VMEM((S, 128), F32), pltpu.SemaphoreType.DMA((2,))],
        compiler_params=_cp(("arbitrary", "arbitrary"), 48),
    )(pa, pa, pa, dycat, *_sb_const_arrays())


def _bias_tiles(rel_bias_l):
    r = np.arange(CH_TQ)[:, None]
    c = np.arange(CH_TQ)[None, :]
    valid = np.stack([(4 * b + c // 64 >= r // 64) & (4 * b + c // 64 <= r // 64 + 8) for b in range(3)])
    rev = rel_bias_l[:, _DIAG_IDX[:, ::-1]]
    rows = jnp.broadcast_to(jnp.pad(rev, ((0, 0), (0, 0), (0, 1)))[:, :, None, :], (N_HEADS_CH, 3, CH_TQ, 2 * CH_TQ))
    skew = rows.reshape(N_HEADS_CH, 3, -1)[:, :, :CH_TQ * (2 * CH_TQ - 1)].reshape(N_HEADS_CH, 3, CH_TQ, 2 * CH_TQ - 1)
    return jnp.where(valid[None], skew[:, :, :, CH_TQ - 1:], NEG)


_DIAG_IDX = np.stack([np.clip(512 - 256 * b + np.arange(2 * CH_TQ - 1) - (CH_TQ - 1), -128, 128) + 128 for b in range(3)])


def _bias_grad(dbias):
    d = jnp.pad(dbias, ((0, 0), (0, 0), (0, 0), (CH_TQ - 1, 0))).reshape(N_HEADS_CH, 3, -1)
    d = jnp.pad(d, ((0, 0), (0, 0), (0, CH_TQ))).reshape(N_HEADS_CH, 3, CH_TQ, 2 * CH_TQ)
    diag = _col_sum(d)[:, :, 0, :2 * CH_TQ - 1][:, :, ::-1]
    onehot = (_DIAG_IDX[:, :, None] == np.arange(REL)[None, None, :]).astype(np.float32)
    return jnp.einsum("hbk,bkr->hr", diag, jnp.asarray(onehot), precision=lax.Precision.HIGHEST)


def _ch_logits(i, q_ref, kn, b_ref, rows):
    lo = _lo_mask(rows.stop - rows.start)
    q = q_ref[rows, :]
    qh = [jnp.where(lo, q, 0), jnp.where(lo, 0, q)]
    scores = []
    for h in range(2):
        s = [_dg(qh[h], kn[b], NT) + b_ref[h, b, rows, :] for b in range(3)]
        s[0] = jnp.where(i >= 2, s[0], NEG)
        s[1] = jnp.where(i >= 1, s[1], NEG)
        scores.append(s)
    return lo, qh, scores


def _ch_specs(S):
    nt = S // CH_TQ

    def kspec(b, base):
        return pl.BlockSpec((CH_TQ, 128), lambda p, i: (jnp.maximum(i - 2 + b, 0), base // 128 + p))

    specs = [pl.BlockSpec((CH_TQ, 128), lambda p, i: (i, p))]
    specs += [kspec(b, D_CH) for b in range(3)] + [kspec(b, 0) for b in range(3)]
    specs += [pl.BlockSpec((2, 3, CH_TQ, CH_TQ), lambda p, i: (p, 0, 0, 0))]
    return nt, specs


def _ch_fwd(pn, pv, bias):
    S = pn.shape[0]
    nt, specs = _ch_specs(S)

    def body(q_ref, k0, k1, k2, v0, v1, v2, b_ref, o_ref, lse_ref):
        i = pl.program_id(1)
        kn = [k0[...], k1[...], k2[...]]
        vs = [v0[...], v1[...], v2[...]]
        for r0 in range(0, CH_TQ, CH_SUB):
            rows = slice(r0, r0 + CH_SUB)
            lo, _, scores = _ch_logits(i, q_ref, kn, b_ref, rows)
            outs, lse = [], []
            for s in scores:
                m = jnp.max(jnp.maximum(jnp.maximum(s[0], s[1]), s[2]), axis=-1, keepdims=True)
                e = [jnp.exp(sb - m) for sb in s]
                l = jnp.sum((e[0] + e[1]) + e[2], axis=-1, keepdims=True)
                outs.append((_dot(e[0], vs[0]) + _dot(e[1], vs[1]) + _dot(e[2], vs[2])) / l)
                lse.append(m + jnp.log(l))
            o_ref[rows, :] = jnp.where(lo, outs[0], outs[1])
            lse_ref[0, rows, :] = jnp.where(lax.broadcasted_iota(jnp.int32, (CH_SUB, 2), 1) == 0, lse[0], lse[1])

    return pl.pallas_call(
        body, name="ch_fwd", grid=(4, nt), in_specs=specs,
        out_specs=[pl.BlockSpec((CH_TQ, 128), lambda p, i: (i, p)), pl.BlockSpec((1, CH_TQ, 2), lambda p, i: (p, i, 0))],
        out_shape=[jax.ShapeDtypeStruct((S, 512), F32), jax.ShapeDtypeStruct((4, S, 2), F32)],
        compiler_params=_cp(("parallel", "arbitrary"), 40),
    )(pn, pn, pn, pn, pv, pv, pv, bias)


def _ch_bwd(pn, pv, bias, lse, dycat):
    S = pn.shape[0]
    nt, specs = _ch_specs(S)
    specs = specs + [pl.BlockSpec((1, CH_TQ, 2), lambda p, i: (p, i, 0)), pl.BlockSpec((CH_TQ, 128), lambda p, i: (i, 256 // 128 + p))]

    def body(q_ref, k0, k1, k2, v0, v1, v2, b_ref, lse_ref, do_ref, dq_ref, db_ref, dk_hbm, dv_hbm, dk_acc, dv_acc, sem):
        p, i = pl.program_id(0), pl.program_id(1)

        @pl.when(i == 0)
        def _():
            dk_acc[...] = jnp.zeros_like(dk_acc)
            dv_acc[...] = jnp.zeros_like(dv_acc)
            db_ref[...] = jnp.zeros_like(db_ref)

        kn = [k0[...], k1[...], k2[...]]
        vs = [v0[...], v1[...], v2[...]]
        dk_b = [jnp.zeros((CH_TQ, 128), F32) for _ in range(3)]
        dv_b = [jnp.zeros((CH_TQ, 128), F32) for _ in range(3)]
        for r0 in range(0, CH_TQ, CH_SUB):
            rows = slice(r0, r0 + CH_SUB)
            lo, qh, scores = _ch_logits(i, q_ref, kn, b_ref, rows)
            do = do_ref[rows, :]
            dos = [_mx(jnp.where(lo, do, 0.0)), _mx(jnp.where(lo, 0.0, do))]
            dqn = []
            for h in range(2):
                pr = [jnp.exp(sb - lse_ref[0, rows, h:h + 1]) for sb in scores[h]]
                dp = [_dg(dos[h], vs[b], NT) for b in range(3)]
                delta = jnp.sum((pr[0] * dp[0] + pr[1] * dp[1]) + pr[2] * dp[2], axis=-1, keepdims=True)
                ds = [pr[b] * (dp[b] - delta) for b in range(3)]
                acc = jnp.zeros((CH_SUB, 128), F32)
                for b in range(3):
                    db_ref[h, b, rows, :] += ds[b]
                    dsm = _mx(ds[b])
                    acc = acc + _dot(dsm, kn[b])
                    dk_b[b] = dk_b[b] + _dg(dsm, qh[h], TN)
                    dv_b[b] = dv_b[b] + _dg(pr[b], dos[h], TN)
                dqn.append(acc)
            dq_ref[rows, :] = 0.125 * jnp.where(lo, dqn[0], dqn[1])

        for b in range(3):
            @pl.when(i - 2 + b >= 0)
            def _(b=b):
                rows = pl.ds(pl.multiple_of((i - 2 + b) * CH_TQ, CH_TQ), CH_TQ)
                dk_acc[rows, :] += dk_b[b]
                dv_acc[rows, :] += dv_b[b]

        @pl.when(i == nt - 1)
        def _():
            cols = pl.ds(pl.multiple_of(p * 128, 128), 128)
            ck = pltpu.make_async_copy(dk_acc, dk_hbm.at[:, cols], sem.at[0])
            cv = pltpu.make_async_copy(dv_acc, dv_hbm.at[:, cols], sem.at[1])
            ck.start()
            cv.start()
            ck.wait()
            cv.wait()

    return pl.pallas_call(
        body, name="ch_bwd", grid=(4, nt), in_specs=specs,
        out_specs=[pl.BlockSpec((CH_TQ, 128), lambda p, i: (i, p)),
                   pl.BlockSpec((2, 3, CH_TQ, CH_TQ), lambda p, i: (p, 0, 0, 0)), ANY, ANY],
        out_shape=[jax.ShapeDtypeStruct((S, 512), F32), jax.ShapeDtypeStruct((N_HEADS_CH, 3, CH_TQ, CH_TQ), F32),
                   jax.ShapeDtypeStruct((S, 512), F32), jax.ShapeDtypeStruct((S, 512), F32)],
        scratch_shapes=[pltpu.VMEM((S, 128), F32), pltpu.VMEM((S, 128), F32), pltpu.SemaphoreType.DMA((2,))],
        compiler_params=_cp(("arbitrary", "arbitrary"), 48),
    )(pn, pn, pn, pn, pv, pv, pv, bias, lse, dycat)


def _col_sum(d):
    _, _, R, C = d.shape

    def body(d_ref, o_ref):
        for b in range(3):
            o_ref[0, b] = jnp.sum(d_ref[0, b], axis=0, keepdims=True)

    return pl.pallas_call(
        body, name="col_sum", grid=(N_HEADS_CH,),
        in_specs=[pl.BlockSpec((1, 3, R, C), lambda h: (h, 0, 0, 0))],
        out_specs=pl.BlockSpec((1, 3, 1, C), lambda h: (h, 0, 0, 0)),
        out_shape=jax.ShapeDtypeStruct((N_HEADS_CH, 3, 1, C), F32),
        compiler_params=_cp(("parallel",), 32),
    )(d)


def _conv_fwd(gc, xc, hc_prev, cw_ref):
    hc = gc * xc
    ext = jnp.concatenate([hc_prev, hc], axis=0)
    r1 = pltpu.roll(ext, 1, 0)[8:]
    r2 = pltpu.roll(ext, 2, 0)[8:]
    return cw_ref[0:1, :] * r2 + cw_ref[1:2, :] * r1 + cw_ref[2:3, :] * hc, hc, r1, r2


def _merge_out(x, ysb, ych, pc, cw, onw, w_out):
    S = x.shape[0]
    tm = min(512, S)
    t8 = tm // 8

    def body(x_ref, ysb_ref, ych_ref, gb_ref, gc_ref, xc_ref, gch_ref, xch_ref, cw_ref, onw_ref, w_ref, xo_ref, yn_ref, ycv_ref):
        i = pl.program_id(0)
        lo = _lo_mask(tm)
        hc_prev = jnp.where(i > 0, gch_ref[...] * xch_ref[...], 0.0)
        conv, _, _, _ = _conv_fwd(gc_ref[...], xc_ref[...], hc_prev, cw_ref)
        ycv = gb_ref[...] * conv
        ycv_ref[...] = ycv
        chunks = [ysb_ref[:, 0:128], ysb_ref[:, 128:256]] + [ych_ref[:, 128 * k:128 * (k + 1)] for k in range(4)]
        chunks += [ycv[:, 0:128], ycv[:, 128:256]]
        for k, ch in enumerate(chunks):
            yn_ref[:, 128 * k:128 * (k + 1)] = _half_norm(ch, onw_ref[:, 128 * k:128 * (k + 1)], lo).astype(yn_ref.dtype)
        xo_ref[...] = x_ref[...] + jnp.dot(yn_ref[...], w_ref[...], preferred_element_type=F32)

    row = lambda i: (i, 0)
    halo = lambda cb: pl.BlockSpec((8, 256), lambda i: (jnp.maximum(i * t8 - 1, 0), cb))
    return pl.pallas_call(
        body, name="merge_out", grid=(S // tm,),
        in_specs=[pl.BlockSpec((tm, D), row), pl.BlockSpec((tm, 256), row), pl.BlockSpec((tm, 512), row),
                  pl.BlockSpec((tm, 256), lambda i: (i, 0)), pl.BlockSpec((tm, 256), lambda i: (i, 1)),
                  pl.BlockSpec((tm, 256), lambda i: (i, 2)), halo(1), halo(2),
                  pl.BlockSpec((3, 256), lambda i: (0, 0)), pl.BlockSpec((1, D), lambda i: (0, 0)), pl.BlockSpec((D, D), lambda i: (0, 0))],
        out_specs=[pl.BlockSpec((tm, D), row), pl.BlockSpec((tm, D), row), pl.BlockSpec((tm, 256), row)],
        out_shape=[jax.ShapeDtypeStruct((S, D), F32), jax.ShapeDtypeStruct((S, D), _MXU), jax.ShapeDtypeStruct((S, 256), F32)],
        compiler_params=_cp(("parallel",), 40),
    )(x, ysb, ych, pc, pc, pc, pc, pc, cw, onw, w_out)


def _ffn_fwd(x, nw, wg, wu, wd, nxt=(), split=()):
    S = x.shape[0]
    tm = min(1024, S)
    ni = S // tm
    n = len(nxt)

    def body(*refs):
        x_ref, nw_ref, wg_ref, wu_ref, wd_ref = refs[:5]
        xo_ref, hn_ref, u_ref, t_ref, a_ref = refs[5 + n:10 + n]
        acc = refs[10 + 2 * n]
        i, j = pl.program_id(0), pl.program_id(1)
        if n:
            gather = _Gather(refs[5:5 + n], refs[10 + n:10 + 2 * n], refs[11 + 2 * n:], split)

            @pl.when(jnp.logical_and(i == 0, j == 0))
            def _():
                gather.start()

            @pl.when(jnp.logical_and(i == max(ni - 2, 0), j == 0))
            def _():
                gather.forward()

        @pl.when(j == 0)
        def _():
            hn_ref[...] = _row_norm(x_ref[...], nw_ref[...]).astype(hn_ref.dtype)
            acc[...] = jnp.zeros_like(acc)

        for r in range(0, tm, min(FFN_SUB, tm)):
            rows = slice(r, r + min(FFN_SUB, tm))
            hn = hn_ref[rows, :]
            u = jnp.dot(hn, wg_ref[0], preferred_element_type=F32)
            t = jnp.dot(hn, wu_ref[0], preferred_element_type=F32)
            a = (u * _sigmoid(u) * t).astype(a_ref.dtype)
            u_ref[0, rows, :], t_ref[0, rows, :], a_ref[0, rows, :] = u.astype(u_ref.dtype), t.astype(t_ref.dtype), a
            acc[rows, :] += jnp.dot(a, wd_ref[0], preferred_element_type=F32)

        @pl.when(j == N_SH - 1)
        def _():
            xo_ref[...] = x_ref[...] + acc[...]

        if n:
            @pl.when(jnp.logical_and(i == ni - 1, j == N_SH - 1))
            def _():
                gather.finish()

    row = lambda i, j: (i, 0)
    sh = lambda i, j: (j, i, 0)
    return pl.pallas_call(
        body, name="ffn_fwd_gather" if n else "ffn_fwd", grid=(ni, N_SH),
        in_specs=[pl.BlockSpec((tm, D), row), pl.BlockSpec((1, D), lambda i, j: (0, 0)),
                  pl.BlockSpec((1, D, FF_SH), lambda i, j: (j, 0, 0)), pl.BlockSpec((1, D, FF_SH), lambda i, j: (j, 0, 0)),
                  pl.BlockSpec((1, FF_SH, D), lambda i, j: (j, 0, 0))] + [ANY] * n,
        out_specs=[pl.BlockSpec((tm, D), row), pl.BlockSpec((tm, D), row), pl.BlockSpec((1, tm, FF_SH), sh),
                   pl.BlockSpec((1, tm, FF_SH), sh), pl.BlockSpec((1, tm, FF_SH), sh)] + [ANY] * n,
        out_shape=[jax.ShapeDtypeStruct((S, D), F32), jax.ShapeDtypeStruct((S, D), _MXU),
                   jax.ShapeDtypeStruct((N_SH, S, FF_SH), _MXU), jax.ShapeDtypeStruct((N_SH, S, FF_SH), _MXU),
                   jax.ShapeDtypeStruct((N_SH, S, FF_SH), _MXU)] + _Gather.out_shape(nxt),
        scratch_shapes=[pltpu.VMEM((tm, D), F32)] + (_Gather.sems(n) if n else []),
        compiler_params=_cp(("arbitrary", "arbitrary"), 60),
    )(x, nw, wg, wu, wd, *nxt)


def _loss_head(x, target):
    S = x.shape[0]
    tm = min(512, S)

    def body(x_ref, t_ref, dx_ref, l_ref):
        @pl.when(pl.program_id(0) == 0)
        def _():
            l_ref[...] = jnp.zeros_like(l_ref)

        e = x_ref[...] - t_ref[...]
        dx_ref[...] = e * (1.0 / D)
        l_ref[...] += 0.5 * jnp.sum(jnp.sum(e * e, axis=-1, keepdims=True) * (1.0 / D))

    return pl.pallas_call(
        body, name="loss_head", grid=(S // tm,),
        in_specs=[pl.BlockSpec((tm, D), lambda i: (i, 0))] * 2,
        out_specs=[pl.BlockSpec((tm, D), lambda i: (i, 0)), pl.BlockSpec((8, 128), lambda i: (0, 0))],
        out_shape=[jax.ShapeDtypeStruct((S, D), F32), jax.ShapeDtypeStruct((8, 128), F32)],
        compiler_params=_cp(("arbitrary",), 32),
    )(x, target)


def _ffn_bwd(dx, xmid, nw, wg, wu, wd, u, t):
    S = dx.shape[0]
    tm = min(1024, S)

    def body(dx_ref, x_ref, nw_ref, wg_ref, wu_ref, wd_ref, u_ref, t_ref, du_ref, dt_ref, dxm_ref, dnw_ref, acc):
        i, j = pl.program_id(0), pl.program_id(1)

        @pl.when(j == 0)
        def _():
            acc[...] = jnp.zeros_like(acc)

        @pl.when(jnp.logical_and(i == 0, j == 0))
        def _():
            dnw_ref[...] = jnp.zeros_like(dnw_ref)

        for r in range(0, tm, min(FFN_SUB, tm)):
            rows = slice(r, r + min(FFN_SUB, tm))
            da = _dg(dx_ref[rows, :], wd_ref[0], NT)
            u, t = u_ref[0, rows, :].astype(F32), t_ref[0, rows, :].astype(F32)
            sg = _sigmoid(u)
            du = _mx(da * t * (sg * (1.0 + u * (1.0 - sg))))
            dt = _mx(da * (u * sg))
            du_ref[0, rows, :], dt_ref[0, rows, :] = du, dt
            acc[rows, :] += _dg(du, wg_ref[0], NT) + _dg(dt, wu_ref[0], NT)

        @pl.when(j == N_SH - 1)
        def _():
            dxn, dw = _row_norm_bwd(x_ref[...], nw_ref[...], acc[...])
            dxm_ref[...] = dx_ref[...] + dxn
            dnw_ref[0:1, :] += dw

    row = lambda i, j: (i, 0)
    sh = lambda i, j: (j, i, 0)
    wsp = lambda r, c: pl.BlockSpec((1, r, c), lambda i, j: (j, 0, 0))
    return pl.pallas_call(
        body, name="ffn_bwd", grid=(S // tm, N_SH),
        in_specs=[pl.BlockSpec((tm, D), row), pl.BlockSpec((tm, D), row), pl.BlockSpec((1, D), lambda i, j: (0, 0)),
                  wsp(D, FF_SH), wsp(D, FF_SH), wsp(FF_SH, D), pl.BlockSpec((1, tm, FF_SH), sh), pl.BlockSpec((1, tm, FF_SH), sh)],
        out_specs=[pl.BlockSpec((1, tm, FF_SH), sh), pl.BlockSpec((1, tm, FF_SH), sh), pl.BlockSpec((tm, D), row),
                   pl.BlockSpec((8, D), lambda i, j: (0, 0))],
        out_shape=[jax.ShapeDtypeStruct((N_SH, S, FF_SH), _MXU), jax.ShapeDtypeStruct((N_SH, S, FF_SH), _MXU),
                   jax.ShapeDtypeStruct((S, D), F32), jax.ShapeDtypeStruct((8, D), F32)],
        scratch_shapes=[pltpu.VMEM((tm, D), F32)],
        compiler_params=_cp(("arbitrary", "arbitrary"), 60),
    )(dx, xmid, nw, wg, wu, wd, u, t)


def _wgrad(a, b, a_spec, b_spec, out_block, name):
    S = a.shape[-2]
    tk = min(1024, S)

    def body(a_ref, b_ref, o_ref):
        @pl.when(pl.program_id(0) == 0)
        def _():
            o_ref[...] = jnp.zeros_like(o_ref)

        for r in range(N_SH):
            o_ref[r] += _dg(a_spec[1](a_ref, r), b_spec[1](b_ref, r), TN)

    return pl.pallas_call(
        body, name=name, grid=(S // tk,),
        in_specs=[a_spec[0](tk), b_spec[0](tk)],
        out_specs=pl.BlockSpec((N_SH,) + out_block, lambda k: (0, 0, 0)),
        out_shape=jax.ShapeDtypeStruct((N_SH,) + out_block, F32),
        compiler_params=_cp(("arbitrary",), 56),
    )(a, b)


def _tok(width):
    return (lambda tk: pl.BlockSpec((tk, width), lambda k: (k, 0))), (lambda ref, r: ref[...])


def _tok_cols(width):
    return (lambda tk: pl.BlockSpec((tk, N_SH * width), lambda k: (k, 0))), (lambda ref, r: ref[:, r * width:(r + 1) * width])


def _tok_sh(width):
    return (lambda tk: pl.BlockSpec((N_SH, tk, width), lambda k: (0, k, 0))), (lambda ref, r: ref[r])


def _out_bwd(dxm, w_out, ysb, ych, ycv, onw):
    S = dxm.shape[0]
    tm = min(512, S)

    def body(dx_ref, w_ref, ysb_ref, ych_ref, ycv_ref, onw_ref, dy_ref, dw_ref):
        @pl.when(pl.program_id(0) == 0)
        def _():
            dw_ref[...] = jnp.zeros_like(dw_ref)

        lo = _lo_mask(tm)
        dyn = _dg(dx_ref[...], w_ref[...], NT)
        chunks = [ysb_ref[:, 0:128], ysb_ref[:, 128:256]] + [ych_ref[:, 128 * k:128 * (k + 1)] for k in range(4)]
        chunks += [ycv_ref[:, 0:128], ycv_ref[:, 128:256]]
        for k, ch in enumerate(chunks):
            sl = slice(128 * k, 128 * (k + 1))
            d, dw = _half_norm_bwd(ch, onw_ref[:, sl], dyn[:, sl], lo)
            dy_ref[:, sl] = d
            dw_ref[0:1, sl] += dw

    row = lambda i: (i, 0)
    return pl.pallas_call(
        body, name="out_bwd", grid=(S // tm,),
        in_specs=[pl.BlockSpec((tm, D), row), pl.BlockSpec((D, D), lambda i: (0, 0)), pl.BlockSpec((tm, 256), row),
                  pl.BlockSpec((tm, 512), row), pl.BlockSpec((tm, 256), row), pl.BlockSpec((1, D), lambda i: (0, 0))],
        out_specs=[pl.BlockSpec((tm, D), row), pl.BlockSpec((8, D), lambda i: (0, 0))],
        out_shape=[jax.ShapeDtypeStruct((S, D), F32), jax.ShapeDtypeStruct((8, D), F32)],
        compiler_params=_cp(("arbitrary",), 40),
    )(dxm, w_out, ysb, ych, ycv, onw)


def _in_bwd(dxm, x, anw, w_sh, pqk, pc, dqa, dka, dva, dqh, dkh, dvb, dycat, cw, qw2, kw2):
    S = x.shape[0]
    tm = min(256, S)
    t8 = tm // 8
    last8 = S // 8 - 1

    def body(dxm_ref, x_ref, anw_ref, w_ref, q0_ref, q1_ref, k0_ref, k1_ref, gb_ref, gc_ref, xc_ref, gch_ref, xch_ref, gbn_ref,
             dqa_ref, dka_ref, dva_ref, dqh_ref, dkh_ref, dvb_ref, dyc_ref, dycn_ref, cw_ref, qw_ref, kw_ref,
             dp_ref, dx_ref, danw_ref, dqw_ref, dkw_ref, dcw_ref):
        i = pl.program_id(0)
        n = pl.num_programs(0)

        @pl.when(i == 0)
        def _():
            for r in (danw_ref, dqw_ref, dkw_ref, dcw_ref):
                r[...] = jnp.zeros_like(r)

        lo = _lo_mask(tm)
        dp_ref[:, QA:QA + 256] = dqa_ref[...].astype(dp_ref.dtype)
        dp_ref[:, KA:KA + 256] = dka_ref[...].astype(dp_ref.dtype)
        dp_ref[:, VA:VA + 256] = dva_ref[...].astype(dp_ref.dtype)
        dp_ref[:, VB:VB + 512] = dvb_ref[...].astype(dp_ref.dtype)
        for base, raws, d_ref, nw_ref, dw_ref in ((QB, (q0_ref, q1_ref), dqh_ref, qw_ref, dqw_ref), (KB, (k0_ref, k1_ref), dkh_ref, kw_ref, dkw_ref)):
            for k in range(4):
                raw = raws[k // 2][:, 128 * (k % 2):128 * (k % 2 + 1)]
                d, dw = _half_norm_bwd(raw, nw_ref[...], d_ref[:, 128 * k:128 * (k + 1)], lo)
                dp_ref[:, base + 128 * k:base + 128 * (k + 1)] = d.astype(dp_ref.dtype)
                dw_ref[0:1, :] += dw
        gb, gc, xc = gb_ref[...], gc_ref[...], xc_ref[...]
        hc_prev = jnp.where(i > 0, gch_ref[...] * xch_ref[...], 0.0)
        conv, hc, r1, r2 = _conv_fwd(gc, xc, hc_prev, cw_ref)
        dyc = dyc_ref[...]
        dconv = dyc * gb
        dconv_next = jnp.where(i < n - 1, dycn_ref[...] * gbn_ref[...], 0.0)
        ext = jnp.concatenate([dconv, dconv_next], axis=0)
        l1 = pltpu.roll(ext, tm + 7, 0)[:tm]
        l2 = pltpu.roll(ext, tm + 6, 0)[:tm]
        dhc = cw_ref[2:3, :] * dconv + cw_ref[1:2, :] * l1 + cw_ref[0:1, :] * l2
        dp_ref[:, GB:GB + 256] = (dyc * conv).astype(dp_ref.dtype)
        dp_ref[:, GC:GC + 256] = (dhc * xc).astype(dp_ref.dtype)
        dp_ref[:, XC:XC + 256] = (dhc * gc).astype(dp_ref.dtype)
        dcw_ref[0:1, :] += jnp.sum(dconv * r2, axis=0, keepdims=True)
        dcw_ref[1:2, :] += jnp.sum(dconv * r1, axis=0, keepdims=True)
        dcw_ref[2:3, :] += jnp.sum(dconv * hc, axis=0, keepdims=True)
        dh = jnp.zeros((tm, D), F32)
        for r in range(N_SH):
            dh = dh + lax.dot_general(dp_ref[:, IN_SH * r:IN_SH * (r + 1)], w_ref[r], NT, preferred_element_type=F32)
        dxn, dw = _row_norm_bwd(x_ref[...], anw_ref[...], dh)
        dx_ref[...] = dxm_ref[...] + dxn
        danw_ref[0:1, :] += dw

    row = lambda i: (i, 0)
    colb = lambda w, cb: pl.BlockSpec((tm, w), lambda i: (i, cb))
    prev = lambda cb: pl.BlockSpec((8, 256), lambda i: (jnp.maximum(i * t8 - 1, 0), cb))
    nxt = lambda cb: pl.BlockSpec((8, 256), lambda i: (jnp.minimum((i + 1) * t8, last8), cb))
    const = lambda shp: pl.BlockSpec(shp, lambda i: (0,) * len(shp))
    return pl.pallas_call(
        body, name="in_bwd", grid=(S // tm,),
        in_specs=[pl.BlockSpec((tm, D), row), pl.BlockSpec((tm, D), row), const((1, D)), const((N_SH, D, IN_SH)),
                  colb(256, 0), colb(256, 1), colb(256, 2), colb(256, 3),
                  colb(256, 0), colb(256, 1), colb(256, 2), prev(1), prev(2), nxt(0),
                  pl.BlockSpec((tm, 256), row), pl.BlockSpec((tm, 256), row), pl.BlockSpec((tm, 256), row),
                  pl.BlockSpec((tm, 512), row), pl.BlockSpec((tm, 512), row), pl.BlockSpec((tm, 512), row),
                  colb(256, 768 // 256), nxt(768 // 256), const((3, 256)), const((1, 128)), const((1, 128))],
        out_specs=[pl.BlockSpec((tm, D_IN), row), pl.BlockSpec((tm, D), row), const((8, D)), const((8, 128)), const((8, 128)), const((8, 256))],
        out_shape=[jax.ShapeDtypeStruct((S, D_IN), _MXU), jax.ShapeDtypeStruct((S, D), F32), jax.ShapeDtypeStruct((8, D), F32),
                   jax.ShapeDtypeStruct((8, 128), F32), jax.ShapeDtypeStruct((8, 128), F32), jax.ShapeDtypeStruct((8, 256), F32)],
        compiler_params=_cp(("arbitrary",), 56),
    )(dxm, x, anw, w_sh, pqk, pqk, pqk, pqk, pc, pc, pc, pc, pc, pc, dqa, dka, dva, dqh, dkh, dvb, dycat, dycat, cw, qw2, kw2)


def _rows_tile(h):
    return h if h <= 512 else 256


def _add_halves(g, r1, c_arr):
    L, _, R, C = g.shape
    h = R // 2
    tr = _rows_tile(h)
    nb = h // tr

    def body(c_ref, g_ref, r_ref, o_ref, ob_ref):
        s = g_ref[...] + r_ref[...]
        o_ref[...] = s
        ob_ref[...] = s.astype(ob_ref.dtype)

    blk = (1, 1, tr, C)
    return pl.pallas_call(
        body, name="add_halves",
        grid_spec=pltpu.PrefetchScalarGridSpec(
            num_scalar_prefetch=1, grid=(L, N_SH, nb),
            in_specs=[pl.BlockSpec(blk, lambda l, s, i, c: (l, s, c[0] * nb + i, 0)), pl.BlockSpec(blk, lambda l, s, i, c: (l, s, i, 0))],
            out_specs=[pl.BlockSpec(blk, lambda l, s, i, c: (l, s, i, 0))] * 2),
        out_shape=[jax.ShapeDtypeStruct((L, N_SH, h, C), F32), jax.ShapeDtypeStruct((L, N_SH, h, C), jnp.bfloat16)],
        compiler_params=_cp(("parallel", "parallel", "parallel"), 32),
    )(c_arr, g, r1)


def _add_shards(s1, r2, me_arr):
    L, _, h, C = s1.shape
    tr = _rows_tile(h)

    def body(me_ref, s_ref, r_ref, o_ref):
        o_ref[0] = ((s_ref[0, 0] + r_ref[0, 0].astype(F32)) + r_ref[1, 0].astype(F32)) + r_ref[2, 0].astype(F32)

    return pl.pallas_call(
        body, name="add_shards",
        grid_spec=pltpu.PrefetchScalarGridSpec(
            num_scalar_prefetch=1, grid=(L, h // tr),
            in_specs=[pl.BlockSpec((1, 1, tr, C), lambda l, i, me: (l, me[0], i, 0)), pl.BlockSpec((3, 1, tr, C), lambda l, i, me: (0, l, i, 0))],
            out_specs=pl.BlockSpec((1, tr, C), lambda l, i, me: (l, i, 0))),
        out_shape=jax.ShapeDtypeStruct((L, h, C), F32),
        compiler_params=_cp(("parallel", "parallel"), 32),
    )(me_arr, s1, r2)


def _adam_math(w, g, m, v):
    m = ADAM_B1 * m + (1.0 - ADAM_B1) * g
    v = ADAM_B2 * v + (1.0 - ADAM_B2) * (g * g)
    m_hat = m / (1.0 - ADAM_B1 ** ADAM_STEP)
    v_hat = v / (1.0 - ADAM_B2 ** ADAM_STEP)
    return -ADAM_LR * (m_hat / (jnp.sqrt(v_hat) + ADAM_EPS) + ADAM_WD * w), m, v


def _adam_big(w, m, v, mine, theirs, c_arr):
    L, R, C = w.shape
    h = R // 2
    tr = _rows_tile(h)
    nb = h // tr

    def body(c_ref, w_ref, m_ref, v_ref, a_ref, b_ref, g_ref, d_ref, mo_ref, vo_ref):
        g = jnp.where(pl.program_id(1) == c_ref[0], a_ref[...], b_ref[...])
        g_ref[...] = g
        d_ref[...], mo_ref[...], vo_ref[...] = _adam_math(w_ref[...], g, m_ref[...], v_ref[...])

    full = pl.BlockSpec((1, tr, C), lambda l, s, i, c: (l, s * nb + i, 0))
    half = pl.BlockSpec((1, tr, C), lambda l, s, i, c: (l, i, 0))
    return pl.pallas_call(
        body, name="adam_big",
        grid_spec=pltpu.PrefetchScalarGridSpec(num_scalar_prefetch=1, grid=(L, 2, nb), in_specs=[full, full, full, half, half],
                                               out_specs=[full] * 4),
        out_shape=[jax.ShapeDtypeStruct((L, R, C), F32)] * 4,
        compiler_params=_cp(("parallel", "parallel", "parallel"), 32),
    )(c_arr, w, m, v, mine, theirs)


def _adam_small(w, g, m, v):
    def body(w_ref, g_ref, m_ref, v_ref, d_ref, mo_ref, vo_ref):
        d_ref[...], mo_ref[...], vo_ref[...] = _adam_math(w_ref[...], g_ref[...], m_ref[...], v_ref[...])

    return pl.pallas_call(body, name="adam_small", out_shape=[jax.ShapeDtypeStruct(w.shape, F32)] * 3)(w, g, m, v)


def _allreduce_small(buf):
    R = buf.shape[0]

    def body(x_ref, o_ref, gat, ssem, rsem):
        x, y, c = lax.axis_index("x"), lax.axis_index("y"), lax.axis_index("c")
        me = 4 * x + 2 * y + c
        gat[me] = x_ref[...]
        cps = []
        for k in range(1, 8):
            peer = (x ^ (k >> 2), y ^ ((k >> 1) & 1), c ^ (k & 1))
            cps.append(pltpu.make_async_remote_copy(src_ref=x_ref, dst_ref=gat.at[me], send_sem=ssem.at[k - 1], recv_sem=rsem.at[k - 1],
                                                    device_id=peer, device_id_type=MESH))
        for cp in cps:
            cp.start()
        for cp in cps:
            cp.wait()
        acc = gat[0]
        for s in range(1, 8):
            acc = acc + gat[s]
        o_ref[...] = acc

    return pl.pallas_call(
        body, name="allreduce_small", out_shape=jax.ShapeDtypeStruct((R, 128), F32),
        in_specs=[pl.BlockSpec(memory_space=pltpu.VMEM)], out_specs=pl.BlockSpec(memory_space=pltpu.VMEM),
        scratch_shapes=[pltpu.VMEM((8, R, 128), F32), pltpu.SemaphoreType.DMA((7,)), pltpu.SemaphoreType.DMA((7,))],
    )(buf)


def _reduce_scatter(grads):
    x, y, c = lax.axis_index("x"), lax.axis_index("y"), lax.axis_index("c")
    c_arr = jnp.reshape(c, (1,)).astype(jnp.int32)
    me_arr = jnp.reshape(2 * x + y, (1,)).astype(jnp.int32)
    halves = [g.shape[2] // 2 for g in grads]

    n = len(grads)

    def body1(*refs):
        ins, outs = refs[:n], refs[n:2 * n]
        ssem, rsem = refs[2 * n:]
        x, y, c = lax.axis_index("x"), lax.axis_index("y"), lax.axis_index("c")
        cps = []
        for f in range(n):
            h = halves[f]
            src = ins[f].at[:, :, pl.ds(pl.multiple_of((1 - c) * h, 8), h), :]
            cps.append(pltpu.make_async_remote_copy(src_ref=src, dst_ref=outs[f], send_sem=ssem.at[f], recv_sem=rsem.at[f],
                                                    device_id=(x, y, 1 - c), device_id_type=MESH))
        for cp in cps:
            cp.start()
        for cp in cps:
            cp.wait()

    r1 = pl.pallas_call(
        body1, name="rs_sibling_halves",
        out_shape=[jax.ShapeDtypeStruct(g.shape[:2] + (h, g.shape[3]), F32) for g, h in zip(grads, halves)],
        in_specs=[ANY] * n, out_specs=[ANY] * n,
        scratch_shapes=[pltpu.SemaphoreType.DMA((n,)), pltpu.SemaphoreType.DMA((n,))],
    )(*grads)
    s1, s1_wire = zip(*[_add_halves(g, r, c_arr) for g, r in zip(grads, r1)])

    def body2(*refs):
        ins, outs = refs[:n], refs[n:2 * n]
        ssem, rsem = refs[2 * n:]
        x, y, c = lax.axis_index("x"), lax.axis_index("y"), lax.axis_index("c")
        peers = [(1 - x, y), (x, 1 - y), (1 - x, 1 - y)]
        cps = []
        for f in range(n):
            for k, (px, py) in enumerate(peers):
                cps.append(pltpu.make_async_remote_copy(src_ref=ins[f].at[:, 2 * px + py], dst_ref=outs[f].at[k], send_sem=ssem.at[3 * f + k],
                                                        recv_sem=rsem.at[3 * f + k], device_id=(px, py, c), device_id_type=MESH))
        for cp in cps:
            cp.start()
        for cp in cps:
            cp.wait()

    r2 = pl.pallas_call(
        body2, name="rs_chip_shards",
        out_shape=[jax.ShapeDtypeStruct((3, s.shape[0]) + s.shape[2:], s.dtype) for s in s1_wire],
        in_specs=[ANY] * n, out_specs=[ANY] * n,
        scratch_shapes=[pltpu.SemaphoreType.DMA((3 * n,)), pltpu.SemaphoreType.DMA((3 * n,))],
    )(*s1_wire)
    mine = [_add_shards(s, r, me_arr) for s, r in zip(s1, r2)]

    def body3(*refs):
        ins, outs = refs[:n], refs[n:2 * n]
        ssem, rsem = refs[2 * n:]
        x, y, c = lax.axis_index("x"), lax.axis_index("y"), lax.axis_index("c")
        cps = [pltpu.make_async_remote_copy(src_ref=ins[f], dst_ref=outs[f], send_sem=ssem.at[f], recv_sem=rsem.at[f],
                                            device_id=(x, y, 1 - c), device_id_type=MESH) for f in range(n)]
        for cp in cps:
            cp.start()
        for cp in cps:
            cp.wait()

    theirs = pl.pallas_call(
        body3, name="rs_sibling_reduced",
        out_shape=[jax.ShapeDtypeStruct(m.shape, F32) for m in mine],
        in_specs=[ANY] * n, out_specs=[ANY] * n,
        scratch_shapes=[pltpu.SemaphoreType.DMA((n,)), pltpu.SemaphoreType.DMA((n,))],
    )(*mine)
    return list(zip(mine, theirs)), c_arr


def _pad_rows(flat, rows):
    return jnp.pad(flat, (0, rows * 128 - flat.shape[0])).reshape(rows, 128)


class _ShardedWeights:
    split = [True] * 5 + [False]

    def __init__(self, w):
        self.parts = [[w[k][l].astype(_MXU) for k in _BIG] + [jnp.zeros((8, 128), F32).at[:3, :HD].set(w["conv_w"][l])]
                      for l in range(DEPTH)]

    def first(self):
        return self.unpack(_allgather_layer(self.parts[0], self.split))

    def following(self, l):
        return (self.parts[l + 1], self.split) if l + 1 < DEPTH else ((), ())

    @staticmethod
    def unpack(got):
        W = dict(zip(_BIG, got[:5]))
        W["conv_w"] = jnp.concatenate([got[5][r, :3, :HD] for r in range(N_SH)], axis=-1)
        return W


def _local_step(x, target, small, weights):
    S = x.shape[0]
    saved, big = [], []
    W = weights.first()
    for l in range(DEPTH):
        big.append(W)
        anw, onw, fnw = small["attn_norm_w"][l][None], small["out_norm_w"][l][None], small["ffn_norm_w"][l][None]
        qw2 = jnp.tile(small["q_norm_w"][l], 2)[None]
        kw2 = jnp.tile(small["k_norm_w"][l], 2)[None]
        cw = W["conv_w"]
        bias = _bias_tiles(small["rel_bias"][l])
        h, pa, pqk, pn, pv, pc = _attn_in(x, anw, W["w_in"], qw2, kw2)
        ysb = _sb_fwd(pa)
        ych, lse = _ch_fwd(pn, pv, bias)
        w_out = W["w_out"].reshape(D, D)
        xmid, yn, ycv = _merge_out(x, ysb, ych, pc, cw, onw, w_out)
        outs = _ffn_fwd(xmid, fnw, W["w_gate"], W["w_up"], W["w_down"], *weights.following(l))
        xo, hn, u, t, a = outs[:5]
        saved.append(dict(x=x, h=h, pa=pa, pqk=pqk, pn=pn, pv=pv, pc=pc, ysb=ysb, ych=ych, lse=lse, ycv=ycv, yn=yn, xmid=xmid, hn=hn, u=u, t=t, a=a,
                          anw=anw, onw=onw, fnw=fnw, qw2=qw2, kw2=kw2, cw=cw, bias=bias, w_out=w_out))
        x = xo
        if l + 1 < DEPTH:
            W = weights.unpack(outs[5:])
    dx, lpart = _loss_head(x, target)
    loss = lpart[0, 0]
    grads = [None] * DEPTH
    for l in reversed(range(DEPTH)):
        W, sv = big[l], saved[l]
        du, dt, dxm, dfnw = _ffn_bwd(dx, sv["xmid"], sv["fnw"], W["w_gate"], W["w_up"], W["w_down"], sv["u"], sv["t"])
        g_gate = _wgrad(sv["hn"], du, _tok(D), _tok_sh(FF_SH), (D, FF_SH), "wgrad_gate")
        g_up = _wgrad(sv["hn"], dt, _tok(D), _tok_sh(FF_SH), (D, FF_SH), "wgrad_up")
        g_down = _wgrad(sv["a"], dx, _tok_sh(FF_SH), _tok(D), (FF_SH, D), "wgrad_down")
        dycat, donw = _out_bwd(dxm, sv["w_out"], sv["ysb"], sv["ych"], sv["ycv"], sv["onw"])
        g_out = _wgrad(sv["yn"], dxm, _tok_cols(OUT_SH), _tok(D), (OUT_SH, D), "wgrad_out")
        dqa, dka, dva = _sb_bwd(sv["pa"], dycat)
        dqh, dbias, dkh, dvb = _ch_bwd(sv["pn"], sv["pv"], sv["bias"], sv["lse"], dycat)
        drelb = _bias_grad(dbias)
        dproj, dx, danw, dqw, dkw, dcw = _in_bwd(dxm, sv["x"], sv["anw"], W["w_in"], sv["pqk"], sv["pc"], dqa, dka, dva, dqh, dkh, dvb, dycat,
                                                 sv["cw"], sv["qw2"], sv["kw2"])
        g_in = _wgrad(sv["h"], dproj, _tok(D), _tok_cols(IN_SH), (D, IN_SH), "wgrad_in")
        grads[l] = dict(w_in=g_in, w_out=g_out, w_gate=g_gate, w_up=g_up, w_down=g_down,
                        attn_norm_w=danw[0], out_norm_w=donw[0], ffn_norm_w=dfnw[0],
                        q_norm_w=dqw[0, :HD] + dqw[0, HD:], k_norm_w=dkw[0, :HD] + dkw[0, HD:], rel_bias=drelb, conv_w=dcw[:3])
    return loss, dx, grads


_BIG = ("w_in", "w_out", "w_gate", "w_up", "w_down")
_SMALL = ("attn_norm_w", "q_norm_w", "k_norm_w", "rel_bias", "out_norm_w", "ffn_norm_w")
_ORDER = ("attn_norm_w", "w_in", "q_norm_w", "k_norm_w", "rel_bias", "conv_w", "out_norm_w", "w_out", "ffn_norm_w", "w_gate", "w_up", "w_down")


def kernel(x, attn_norm_w, w_in, q_norm_w, k_norm_w, rel_bias, conv_w, out_norm_w, w_out, ffn_norm_w, w_gate, w_up, w_down, loss_target, m_attn_norm_w, m_w_in, m_q_norm_w, m_k_norm_w, m_rel_bias, m_conv_w, m_out_norm_w, m_w_out, m_ffn_norm_w, m_w_gate, m_w_up, m_w_down, v_attn_norm_w, v_w_in, v_q_norm_w, v_k_norm_w, v_rel_bias, v_conv_w, v_out_norm_w, v_w_out, v_ffn_norm_w, v_w_gate, v_w_up, v_w_down):
    w = dict(attn_norm_w=attn_norm_w, w_in=w_in, q_norm_w=q_norm_w, k_norm_w=k_norm_w, rel_bias=rel_bias, conv_w=conv_w,
             out_norm_w=out_norm_w, w_out=w_out, ffn_norm_w=ffn_norm_w, w_gate=w_gate, w_up=w_up, w_down=w_down)
    m = dict(attn_norm_w=m_attn_norm_w, w_in=m_w_in, q_norm_w=m_q_norm_w, k_norm_w=m_k_norm_w, rel_bias=m_rel_bias, conv_w=m_conv_w,
             out_norm_w=m_out_norm_w, w_out=m_w_out, ffn_norm_w=m_ffn_norm_w, w_gate=m_w_gate, w_up=m_w_up, w_down=m_w_down)
    v = dict(attn_norm_w=v_attn_norm_w, w_in=v_w_in, q_norm_w=v_q_norm_w, k_norm_w=v_k_norm_w, rel_bias=v_rel_bias, conv_w=v_conv_w,
             out_norm_w=v_out_norm_w, w_out=v_w_out, ffn_norm_w=v_ffn_norm_w, w_gate=v_w_gate, w_up=v_w_up, w_down=v_w_down)
    xi, yi = lax.axis_index("x"), lax.axis_index("y")
    me = 2 * xi + yi

    small = {k: w[k] for k in _SMALL}
    loss, grad_x, grads = _local_step(x[0], loss_target[0], small, _ShardedWeights(w))
    loss = lax.psum(loss, ("x", "y", "c"))

    stacked = [jnp.stack([grads[l][k] for l in range(DEPTH)]) for k in _BIG]
    reduced, c_arr = _reduce_scatter(stacked)
    out = {}
    for k, (mine, theirs) in zip(_BIG, reduced):
        out[k] = _adam_big(w[k], m[k], v[k], mine, theirs, c_arr)

    names = _SMALL + ("conv_w",)
    gflat = jnp.concatenate([jnp.stack([grads[l][k] for l in range(DEPTH)]).reshape(-1) for k in names])
    rows = -(-gflat.shape[0] // 1024) * 8
    gsum = _allreduce_small(_pad_rows(gflat, rows)).reshape(-1)
    gs, off = {}, 0
    for k in names:
        shp = (DEPTH, 3, 4 * HD) if k == "conv_w" else w[k].shape
        size = int(np.prod(shp))
        gs[k] = gsum[off:off + size].reshape(shp)
        off += size
    gs["conv_w"] = lax.dynamic_slice_in_dim(gs["conv_w"], me * HD, HD, axis=2)
    flat = lambda d: jnp.concatenate([d[k].reshape(-1) for k in names])
    rows = -(-flat(w).shape[0] // 1024) * 8
    dsm, msm, vsm = _adam_small(_pad_rows(flat(w), rows), _pad_rows(flat(gs), rows), _pad_rows(flat(m), rows), _pad_rows(flat(v), rows))
    off = 0
    for k in names:
        size = int(np.prod(w[k].shape))
        out[k] = (gs[k],) + tuple(a.reshape(-1)[off:off + size].reshape(w[k].shape) for a in (dsm, msm, vsm))
        off += size

    return (loss, grad_x[None]) + tuple(out[k][j] for j in range(4) for k in _ORDER)
```

```python
import functools

import numpy as np
import jax
import jax.numpy as jnp
from jax import lax
from jax.experimental import pallas as pl
from jax.experimental.pallas import tpu as pltpu

F32 = jnp.float32
_MXU = jnp.bfloat16

D = 1024
DEPTH = 4
HD = 64
D_IN = 3072
N_SH = 4
IN_SH = D_IN // N_SH
FF_SH = 704
OUT_SH = 256
N_HEADS_CH = 8
D_CH = N_HEADS_CH * HD
REL = 257
QA, KA, VA, QB, KB, VB, GB, GC, XC = 0, 256, 512, 768, 1280, 1792, 2304, 2560, 2816
EPS = 1e-6
NEG = -1e30
FFN_SUB = 256
SB_BLK = 128
CH_TQ = 256
CH_SUB = 256
EXP_ZERO = -104.0

ADAM_LR, ADAM_B1, ADAM_B2, ADAM_EPS, ADAM_WD, ADAM_STEP = 0.001, 0.9, 0.999, 1e-08, 0.01, 10

NT = (((1,), (1,)), ((), ()))
TN = (((0,), (0,)), ((), ()))
MESH = pl.DeviceIdType.MESH
ANY = pl.BlockSpec(memory_space=pl.ANY)
MB = 1 << 20


def _cp(sem=None, vmem_mb=None):
    return pltpu.CompilerParams(dimension_semantics=sem, vmem_limit_bytes=None if vmem_mb is None else vmem_mb * MB)


def _mx(a):
    return a.astype(_MXU)


def _dot(a, b):
    return jnp.dot(_mx(a), _mx(b), preferred_element_type=F32)


def _dg(a, b, dims):
    return lax.dot_general(_mx(a), _mx(b), dims, preferred_element_type=F32)


def _sigmoid(u):
    return 1.0 / (1.0 + jnp.exp(-u))


def _row_norm(x, w):
    r = lax.rsqrt(jnp.mean(x * x, axis=-1, keepdims=True) + EPS)
    return x * r * w


def _row_norm_bwd(x, w, dy):
    r = lax.rsqrt(jnp.mean(x * x, axis=-1, keepdims=True) + EPS)
    xh = x * r
    dw = jnp.sum(dy * xh, axis=0, keepdims=True)
    dxh = dy * w
    return r * (dxh - xh * jnp.mean(dxh * xh, axis=-1, keepdims=True)), dw


def _half_rs(x, lo):
    sq = x * x
    s0 = jnp.sum(jnp.where(lo, sq, 0.0), axis=-1, keepdims=True)
    s1 = jnp.sum(jnp.where(lo, 0.0, sq), axis=-1, keepdims=True)
    return jnp.where(lo, lax.rsqrt(s0 * (1.0 / HD) + EPS), lax.rsqrt(s1 * (1.0 / HD) + EPS))


def _half_norm(x, w, lo):
    return x * _half_rs(x, lo) * w


def _half_norm_bwd(x, w, dy, lo):
    r = _half_rs(x, lo)
    xh = x * r
    dw = jnp.sum(dy * xh, axis=0, keepdims=True)
    dxh = dy * w
    pr = dxh * xh
    m0 = jnp.sum(jnp.where(lo, pr, 0.0), axis=-1, keepdims=True)
    m1 = jnp.sum(jnp.where(lo, 0.0, pr), axis=-1, keepdims=True)
    mean = jnp.where(lo, m0, m1) * (1.0 / HD)
    return r * (dxh - xh * mean), dw


def _lo_mask(t):
    return lax.broadcasted_iota(jnp.int32, (t, 128), 1) < HD


class _Gather:
    def __init__(self, ins, outs, sems, split):
        self.n = n = len(ins)
        lsem, ssem, rsem, fsem, gsem = sems
        x, y, c = lax.axis_index("x"), lax.axis_index("y"), lax.axis_index("c")
        me = 2 * x + y
        chips = [(1 - x, y), (x, 1 - y), (1 - x, 1 - y)]

        def half(f):
            h = ins[f].shape[0] // 2
            return pl.ds(pl.multiple_of(c * h, 16), h)

        self.local = [pltpu.make_async_copy(ins[f], outs[f].at[me], lsem.at[f]) for f in range(n)]
        self.sends, self.passed = [], []
        for f in range(n):
            for k, (px, py) in enumerate(chips):
                src, dst = (ins[f].at[half(f)], outs[f].at[me, half(f)]) if split[f] else (ins[f], outs[f].at[me])
                self.sends.append(pltpu.make_async_remote_copy(src_ref=src, dst_ref=dst, send_sem=ssem.at[3 * f + k], recv_sem=rsem.at[3 * f + k],
                                                               device_id=(px, py, c), device_id_type=MESH))
                if split[f]:
                    got = outs[f].at[2 * px + py, half(f)]
                    self.passed.append(pltpu.make_async_remote_copy(src_ref=got, dst_ref=got, send_sem=fsem.at[3 * f + k], recv_sem=gsem.at[3 * f + k],
                                                                    device_id=(x, y, 1 - c), device_id_type=MESH))
                else:
                    self.passed.append(None)

    def start(self):
        for cp in self.local + self.sends:
            cp.start()

    def forward(self):
        for cp, fw in zip(self.sends, self.passed):
            cp.wait_recv()
            if fw is not None:
                fw.start()

    def finish(self):
        passed = [fw for fw in self.passed if fw is not None]
        for fw in passed:
            fw.wait_recv()
        for cp in self.sends + passed:
            cp.wait_send()
        for cp in self.local:
            cp.wait()

    @staticmethod
    def sems(n):
        return [pltpu.SemaphoreType.DMA((n,))] + [pltpu.SemaphoreType.DMA((3 * n,))] * 4

    @staticmethod
    def out_shape(parts):
        return [jax.ShapeDtypeStruct((N_SH,) + p.shape, p.dtype) for p in parts]


def _allgather_layer(parts, split):
    n = len(parts)

    def body(*refs):
        g = _Gather(refs[:n], refs[n:2 * n], refs[2 * n:], split)
        g.start()
        g.forward()
        g.finish()

    return pl.pallas_call(
        body, name="allgather_layer", out_shape=_Gather.out_shape(parts),
        in_specs=[ANY] * n, out_specs=[ANY] * n, scratch_shapes=_Gather.sems(n),
    )(*parts)


def _attn_in(x, nw, w_sh, qw2, kw2):
    S = x.shape[0]
    tm = min(512, S)
    assert (QB, VB, GB) == (IN_SH, 2 * IN_SH + 256, 3 * IN_SH)

    def body(x_ref, nw_ref, w_ref, qw_ref, kw_ref, h_ref, pa_ref, pqk_ref, pn_ref, pv_ref, pc_ref):
        h = _row_norm(x_ref[...], nw_ref[...]).astype(h_ref.dtype)
        h_ref[...] = h
        pa_ref[...] = jnp.dot(h, w_ref[0], preferred_element_type=F32).astype(pa_ref.dtype)
        pqk_ref[:, :IN_SH] = jnp.dot(h, w_ref[1], preferred_element_type=F32)
        p2 = jnp.dot(h, w_ref[2], preferred_element_type=F32)
        pqk_ref[:, IN_SH:] = p2[:, :256]
        pv_ref[...] = p2[:, 256:].astype(pv_ref.dtype)
        pc_ref[...] = jnp.dot(h, w_ref[3], preferred_element_type=F32)
        lo = _lo_mask(tm)
        for k in range(2 * D_CH // 128):
            cols = slice(128 * k, 128 * (k + 1))
            is_q = k < D_CH // 128
            n = _half_norm(pqk_ref[:, cols], (qw_ref if is_q else kw_ref)[...], lo)
            pn_ref[:, cols] = (n * 0.125 if is_q else n).astype(pn_ref.dtype)

    row = lambda i: (i, 0)
    const = lambda shp: pl.BlockSpec(shp, lambda i: (0,) * len(shp))
    return pl.pallas_call(
        body, name="attn_in", grid=(S // tm,),
        in_specs=[pl.BlockSpec((tm, D), row), const((1, D)), const((N_SH, D, IN_SH)), const((1, 128)), const((1, 128))],
        out_specs=[pl.BlockSpec((tm, D), row), pl.BlockSpec((tm, IN_SH), row), pl.BlockSpec((tm, 2 * D_CH), row),
                   pl.BlockSpec((tm, 2 * D_CH), row), pl.BlockSpec((tm, D_CH), row), pl.BlockSpec((tm, IN_SH), row)],
        out_shape=[jax.ShapeDtypeStruct((S, D), _MXU), jax.ShapeDtypeStruct((S, IN_SH), _MXU), jax.ShapeDtypeStruct((S, 2 * D_CH), F32),
                   jax.ShapeDtypeStruct((S, 2 * D_CH), _MXU), jax.ShapeDtypeStruct((S, D_CH), _MXU), jax.ShapeDtypeStruct((S, IN_SH), F32)],
        compiler_params=_cp(("parallel",), 48),
    )(x, nw, w_sh, qw2, kw2)


SB_NB = 3
SB_TK = SB_NB * SB_BLK
SB_QB = 4


def _sb_any_open(states):
    return functools.reduce(jnp.logical_or, [jnp.logical_and(st[0] >= 0, st[1] > EXP_ZERO) for st in states])


def _sb_const_arrays():
    r, c = np.meshgrid(np.arange(SB_TK), np.arange(256), indexing="ij")
    kofs = (r - (r & 127)) + (c & 127)
    r2, c2 = np.meshgrid(np.arange(256), np.arange(256), indexing="ij")
    same = (r2 >= 128) == (c2 >= 128)
    nearer = same & ((r2 & 127) > (c2 & 127))
    at_or_nearer = same & ((r2 & 127) >= (c2 & 127))
    return (jnp.asarray(kofs, jnp.int32), jnp.asarray(kofs - (r & 127), jnp.int32), jnp.asarray(nearer, _MXU), jnp.asarray(at_or_nearer, _MXU))


_SB_CONST_SPECS = [pl.BlockSpec((SB_TK, 256), lambda p, i: (0, 0))] * 2 + [pl.BlockSpec((256, 256), lambda p, i: (0, 0))] * 2


class _SbConsts:
    def __init__(self, kofs_ref, rel_ref, nearer_ref, at_or_nearer_ref):
        self.kofs, self.rel, self.nearer, self.at_or_nearer = kofs_ref, rel_ref, nearer_ref, at_or_nearer_ref
        self.lo = lax.broadcasted_iota(jnp.int32, (SB_BLK, 128), 1) < HD
        self.head0 = lax.broadcasted_iota(jnp.int32, (SB_BLK, 256), 1) < 128


def _two_heads(x, lo):
    return jnp.concatenate([jnp.where(lo, x, 0), jnp.where(lo, 0, x)], axis=0)


def _per_block(vals, head0):
    return jnp.concatenate([jnp.where(head0, v0, v1) for v0, v1 in vals], axis=0)


def _head_sums(x):
    return jnp.sum(x[:, :128], axis=-1, keepdims=True), jnp.sum(x[:, 128:], axis=-1, keepdims=True)


def _sb_tile(q2, k_ref, i, jt, c0, c1, K):
    s0 = pl.multiple_of(jnp.maximum(jt - (SB_NB - 1), 0) * SB_BLK, SB_BLK)
    kt = k_ref[pl.ds(s0, SB_TK), :]
    k2 = [_two_heads(kt[b * SB_BLK:(b + 1) * SB_BLK], K.lo) for b in range(SB_NB)]
    keep = jnp.logical_and(K.rel[...] < i * SB_BLK - s0, K.kofs[...] < (jt + 1) * SB_BLK - s0)
    z = jnp.where(keep, jnp.concatenate([_dg(q2, k2[b], NT) for b in range(SB_NB)], axis=0), NEG)
    lr = -(jnp.maximum(z, 0.0) + jnp.log(1.0 + jnp.exp(-jnp.abs(z))))
    carries = [None] * SB_NB
    for b in reversed(range(SB_NB)):
        carries[b] = (c0, c1)
        s = _head_sums(lr[b * SB_BLK:(b + 1) * SB_BLK])
        c0, c1 = c0 + s[0], c1 + s[1]
    after = _dot(lr, K.nearer[...]) + _per_block(carries, K.head0)
    w = jnp.exp(z + lr + after)
    return s0, k2, z, lr, w, c0, c1


def _sb_fwd(pa):
    S = pa.shape[0]
    nq = S // SB_BLK
    assert nq % SB_QB == 0

    def body(q_ref, k_ref, v_ref, kofs_ref, rel_ref, nearer_ref, aon_ref, o_ref):
        K = _SbConsts(kofs_ref, rel_ref, nearer_ref, aon_ref)
        blocks = [pl.program_id(1) * SB_QB + s for s in range(SB_QB)]
        q2 = [q_ref[s * SB_BLK:(s + 1) * SB_BLK, :] * 0.125 for s in range(SB_QB)]

        def step(st):
            new = []
            for s, (jt, _, acc, c0, c1) in enumerate(st):
                s0, _, _, _, w, c0, c1 = _sb_tile(q2[s], k_ref, blocks[s], jt, c0, c1, K)
                vt = v_ref[pl.ds(s0, SB_TK), :]
                for b in range(SB_NB):
                    rows = slice(b * SB_BLK, (b + 1) * SB_BLK)
                    acc = acc + _dot(w[rows], _two_heads(vt[rows], K.lo))
                new.append((jt - SB_NB, jnp.max(jnp.maximum(c0, c1)), acc, c0, c1))
            return tuple(new)

        zc = jnp.zeros((SB_BLK, 1), F32)
        st = lax.while_loop(_sb_any_open, step, tuple((i, jnp.float32(0.0), jnp.zeros((SB_BLK, 128), F32), zc, zc) for i in blocks))
        for s in range(SB_QB):
            o_ref[s * SB_BLK:(s + 1) * SB_BLK, :] = st[s][2]

    tq = SB_QB * SB_BLK
    return pl.pallas_call(
        body, name="sb_fwd", grid=(2, nq // SB_QB),
        in_specs=[pl.BlockSpec((tq, 128), lambda p, i: (i, QA // 128 + p)),
                  pl.BlockSpec((S, 128), lambda p, i: (0, KA // 128 + p)),
                  pl.BlockSpec((S, 128), lambda p, i: (0, VA // 128 + p))] + _SB_CONST_SPECS,
        out_specs=pl.BlockSpec((tq, 128), lambda p, i: (i, p)),
        out_shape=jax.ShapeDtypeStruct((S, 256), F32),
        compiler_params=_cp(("parallel", "arbitrary"), 40),
    )(pa, pa, pa, *_sb_const_arrays())


def _sb_bwd(pa, dycat):
    S = pa.shape[0]
    nq = S // SB_BLK
    assert nq % SB_QB == 0
    n_steps = nq // SB_QB

    def body(q_ref, k_ref, v_ref, do_ref, kofs_ref, rel_ref, nearer_ref, aon_ref, dq_ref, dk_hbm, dv_hbm, dk_acc, dv_acc, sem):
        p, step_i = pl.program_id(0), pl.program_id(1)
        K = _SbConsts(kofs_ref, rel_ref, nearer_ref, aon_ref)

        @pl.when(step_i == 0)
        def _():
            dk_acc[...] = jnp.zeros_like(dk_acc)
            dv_acc[...] = jnp.zeros_like(dv_acc)

        blocks = [step_i * SB_QB + s for s in range(SB_QB)]
        q2 = [q_ref[s * SB_BLK:(s + 1) * SB_BLK, :] * 0.125 for s in range(SB_QB)]
        do = [_mx(do_ref[s * SB_BLK:(s + 1) * SB_BLK, :]) for s in range(SB_QB)]

        def tile(s, jt, c0, c1):
            s0, k2, z, lr, w, c0, c1 = _sb_tile(q2[s], k_ref, blocks[s], jt, c0, c1, K)
            vt = v_ref[pl.ds(s0, SB_TK), :]
            g = jnp.concatenate([_dg(do[s], _two_heads(vt[b * SB_BLK:(b + 1) * SB_BLK], K.lo), NT) for b in range(SB_NB)], axis=0) * w
            gs = [_head_sums(g[b * SB_BLK:(b + 1) * SB_BLK]) for b in range(SB_NB)]
            return dict(s0=s0, k2=k2, z=z, lr=lr, w=w, g=g, gs=gs, c0=c0, c1=c1, mx=jnp.max(jnp.maximum(c0, c1)),
                        g0=sum(x[0] for x in gs), g1=sum(x[1] for x in gs))

        def grads(s, T, tot, n0, n1):
            nearer = [None] * SB_NB
            for b in reversed(range(SB_NB)):
                nearer[b] = (n0, n1)
                n0, n1 = n0 + T["gs"][b][0], n1 + T["gs"][b][1]
            farther = _per_block([tot] * SB_NB, K.head0) - _per_block(nearer, K.head0) - _dot(T["g"], K.at_or_nearer[...])
            beta = jnp.exp(T["z"] + T["lr"])
            dz = _mx(T["g"] * (1.0 - beta) - farther * beta)
            wm = _mx(T["w"])
            dq = jnp.zeros((SB_BLK, 128), F32)
            dks, dvs = [], []
            for b in range(SB_NB):
                rows = slice(b * SB_BLK, (b + 1) * SB_BLK)
                dq = dq + _dot(dz[rows], T["k2"][b])
                dk2 = _dg(dz[rows], q2[s], TN)
                dv2 = _dg(wm[rows], do[s], TN)
                dks.append(jnp.where(K.lo, dk2[:SB_BLK], dk2[SB_BLK:]))
                dvs.append(jnp.where(K.lo, dv2[:SB_BLK], dv2[SB_BLK:]))
            dk_acc[pl.ds(T["s0"], SB_TK), :] += jnp.concatenate(dks, axis=0)
            dv_acc[pl.ds(T["s0"], SB_TK), :] += jnp.concatenate(dvs, axis=0)
            return 0.125 * dq

        zc = jnp.zeros((SB_BLK, 1), F32)
        first = [tile(s, blocks[s], zc, zc) for s in range(SB_QB)]

        def sum_step(st):
            new = []
            for s, (jt, _, c0, c1, t0, t1) in enumerate(st):
                T = tile(s, jt, c0, c1)
                new.append((jt - SB_NB, T["mx"], T["c0"], T["c1"], t0 + T["g0"], t1 + T["g1"]))
            return tuple(new)

        st = lax.while_loop(_sb_any_open, sum_step,
                            tuple((blocks[s] - SB_NB, T["mx"], T["c0"], T["c1"], T["g0"], T["g1"]) for s, T in enumerate(first)))
        tot = [x[4:] for x in st]

        def grad_step(st):
            new = []
            for s, (jt, _, dq, c0, c1, n0, n1) in enumerate(st):
                T = tile(s, jt, c0, c1)
                new.append((jt - SB_NB, T["mx"], dq + grads(s, T, tot[s], n0, n1), T["c0"], T["c1"], n0 + T["g0"], n1 + T["g1"]))
            return tuple(new)

        st = lax.while_loop(_sb_any_open, grad_step,
                            tuple((blocks[s] - SB_NB, T["mx"], grads(s, T, tot[s], zc, zc), T["c0"], T["c1"], T["g0"], T["g1"])
                                  for s, T in enumerate(first)))
        for s in range(SB_QB):
            dq_ref[s * SB_BLK:(s + 1) * SB_BLK, :] = st[s][2]

        @pl.when(step_i == n_steps - 1)
        def _():
            cols = pl.ds(pl.multiple_of(p * 128, 128), 128)
            ck = pltpu.make_async_copy(dk_acc, dk_hbm.at[:, cols], sem.at[0])
            cv = pltpu.make_async_copy(dv_acc, dv_hbm.at[:, cols], sem.at[1])
            ck.start()
            cv.start()
            ck.wait()
            cv.wait()

    tq = SB_QB * SB_BLK
    return pl.pallas_call(
        body, name="sb_bwd", grid=(2, n_steps),
        in_specs=[pl.BlockSpec((tq, 128), lambda p, i: (i, QA // 128 + p)),
                  pl.BlockSpec((S, 128), lambda p, i: (0, KA // 128 + p)),
                  pl.BlockSpec((S, 128), lambda p, i: (0, VA // 128 + p)),
                  pl.BlockSpec((tq, 128), lambda p, i: (i, p))] + _SB_CONST_SPECS,
        out_specs=[pl.BlockSpec((tq, 128), lambda p, i: (i, p)), ANY, ANY],
        out_shape=[jax.ShapeDtypeStruct((S, 256), F32)] * 3,
        scratch_shapes=[pltpu.VMEM((S, 128), F32), pltpu.VMEM((S, 128), F32), pltpu.SemaphoreType.DMA((2,))],
        compiler_params=_cp(("arbitrary", "arbitrary"), 48),
    )(pa, pa, pa, dycat, *_sb_const_arrays())


def _bias_tiles(rel_bias_l):
    r = np.arange(CH_TQ)[:, None]
    c = np.arange(CH_TQ)[None, :]
    valid = np.stack([(4 * b + c // 64 >= r // 64) & (4 * b + c // 64 <= r // 64 + 8) for b in range(3)])
    rev = rel_bias_l[:, _DIAG_IDX[:, ::-1]]
    rows = jnp.broadcast_to(jnp.pad(rev, ((0, 0), (0, 0), (0, 1)))[:, :, None, :], (N_HEADS_CH, 3, CH_TQ, 2 * CH_TQ))
    skew = rows.reshape(N_HEADS_CH, 3, -1)[:, :, :CH_TQ * (2 * CH_TQ - 1)].reshape(N_HEADS_CH, 3, CH_TQ, 2 * CH_TQ - 1)
    return jnp.where(valid[None], skew[:, :, :, CH_TQ - 1:], NEG)


_DIAG_IDX = np.stack([np.clip(512 - 256 * b + np.arange(2 * CH_TQ - 1) - (CH_TQ - 1), -128, 128) + 128 for b in range(3)])


def _bias_grad(dbias):
    d = jnp.pad(dbias, ((0, 0), (0, 0), (0, 0), (CH_TQ - 1, 0))).reshape(N_HEADS_CH, 3, -1)
    d = jnp.pad(d, ((0, 0), (0, 0), (0, CH_TQ))).reshape(N_HEADS_CH, 3, CH_TQ, 2 * CH_TQ)
    diag = _col_sum(d)[:, :, 0, :2 * CH_TQ - 1][:, :, ::-1]
    onehot = (_DIAG_IDX[:, :, None] == np.arange(REL)[None, None, :]).astype(np.float32)
    return jnp.einsum("hbk,bkr->hr", diag, jnp.asarray(onehot), precision=lax.Precision.HIGHEST)


def _ch_logits(i, q_ref, kn, b_ref, rows):
    lo = _lo_mask(rows.stop - rows.start)
    q = q_ref[rows, :]
    qh = [jnp.where(lo, q, 0), jnp.where(lo, 0, q)]
    scores = []
    for h in range(2):
        s = [_dg(qh[h], kn[b], NT) + b_ref[h, b, rows, :] for b in range(3)]
        s[0] = jnp.where(i >= 2, s[0], NEG)
        s[1] = jnp.where(i >= 1, s[1], NEG)
        scores.append(s)
    return lo, qh, scores


def _ch_specs(S):
    nt = S // CH_TQ

    def kspec(b, base):
        return pl.BlockSpec((CH_TQ, 128), lambda p, i: (jnp.maximum(i - 2 + b, 0), base // 128 + p))

    specs = [pl.BlockSpec((CH_TQ, 128), lambda p, i: (i, p))]
    specs += [kspec(b, D_CH) for b in range(3)] + [kspec(b, 0) for b in range(3)]
    specs += [pl.BlockSpec((2, 3, CH_TQ, CH_TQ), lambda p, i: (p, 0, 0, 0))]
    return nt, specs


def _ch_fwd(pn, pv, bias):
    S = pn.shape[0]
    nt, specs = _ch_specs(S)

    def body(q_ref, k0, k1, k2, v0, v1, v2, b_ref, o_ref, lse_ref):
        i = pl.program_id(1)
        kn = [k0[...], k1[...], k2[...]]
        vs = [v0[...], v1[...], v2[...]]
        for r0 in range(0, CH_TQ, CH_SUB):
            rows = slice(r0, r0 + CH_SUB)
            lo, _, scores = _ch_logits(i, q_ref, kn, b_ref, rows)
            outs, lse = [], []
            for s in scores:
                m = jnp.max(jnp.maximum(jnp.maximum(s[0], s[1]), s[2]), axis=-1, keepdims=True)
                e = [jnp.exp(sb - m) for sb in s]
                l = jnp.sum((e[0] + e[1]) + e[2], axis=-1, keepdims=True)
                outs.append((_dot(e[0], vs[0]) + _dot(e[1], vs[1]) + _dot(e[2], vs[2])) / l)
                lse.append(m + jnp.log(l))
            o_ref[rows, :] = jnp.where(lo, outs[0], outs[1])
            lse_ref[0, rows, :] = jnp.where(lax.broadcasted_iota(jnp.int32, (CH_SUB, 2), 1) == 0, lse[0], lse[1])

    return pl.pallas_call(
        body, name="ch_fwd", grid=(4, nt), in_specs=specs,
        out_specs=[pl.BlockSpec((CH_TQ, 128), lambda p, i: (i, p)), pl.BlockSpec((1, CH_TQ, 2), lambda p, i: (p, i, 0))],
        out_shape=[jax.ShapeDtypeStruct((S, 512), F32), jax.ShapeDtypeStruct((4, S, 2), F32)],
        compiler_params=_cp(("parallel", "arbitrary"), 40),
    )(pn, pn, pn, pn, pv, pv, pv, bias)


def _ch_bwd(pn, pv, bias, lse, dycat):
    S = pn.shape[0]
    nt, specs = _ch_specs(S)
    specs = specs + [pl.BlockSpec((1, CH_TQ, 2), lambda p, i: (p, i, 0)), pl.BlockSpec((CH_TQ, 128), lambda p, i: (i, 256 // 128 + p))]

    def body(q_ref, k0, k1, k2, v0, v1, v2, b_ref, lse_ref, do_ref, dq_ref, db_ref, dk_hbm, dv_hbm, dk_acc, dv_acc, sem):
        p, i = pl.program_id(0), pl.program_id(1)

        @pl.when(i == 0)
        def _():
            dk_acc[...] = jnp.zeros_like(dk_acc)
            dv_acc[...] = jnp.zeros_like(dv_acc)
            db_ref[...] = jnp.zeros_like(db_ref)

        kn = [k0[...], k1[...], k2[...]]
        vs = [v0[...], v1[...], v2[...]]
        dk_b = [jnp.zeros((CH_TQ, 128), F32) for _ in range(3)]
        dv_b = [jnp.zeros((CH_TQ, 128), F32) for _ in range(3)]
        for r0 in range(0, CH_TQ, CH_SUB):
            rows = slice(r0, r0 + CH_SUB)
            lo, qh, scores = _ch_logits(i, q_ref, kn, b_ref, rows)
            do = do_ref[rows, :]
            dos = [_mx(jnp.where(lo, do, 0.0)), _mx(jnp.where(lo, 0.0, do))]
            dqn = []
            for h in range(2):
                pr = [jnp.exp(sb - lse_ref[0, rows, h:h + 1]) for sb in scores[h]]
                dp = [_dg(dos[h], vs[b], NT) for b in range(3)]
                delta = jnp.sum((pr[0] * dp[0] + pr[1] * dp[1]) + pr[2] * dp[2], axis=-1, keepdims=True)
                ds = [pr[b] * (dp[b] - delta) for b in range(3)]
                acc = jnp.zeros((CH_SUB, 128), F32)
                for b in range(3):
                    db_ref[h, b, rows, :] += ds[b]
                    dsm = _mx(ds[b])
                    acc = acc + _dot(dsm, kn[b])
                    dk_b[b] = dk_b[b] + _dg(dsm, qh[h], TN)
                    dv_b[b] = dv_b[b] + _dg(pr[b], dos[h], TN)
                dqn.append(acc)
            dq_ref[rows, :] = 0.125 * jnp.where(lo, dqn[0], dqn[1])

        for b in range(3):
            @pl.when(i - 2 + b >= 0)
            def _(b=b):
                rows = pl.ds(pl.multiple_of((i - 2 + b) * CH_TQ, CH_TQ), CH_TQ)
                dk_acc[rows, :] += dk_b[b]
                dv_acc[rows, :] += dv_b[b]

        @pl.when(i == nt - 1)
        def _():
            cols = pl.ds(pl.multiple_of(p * 128, 128), 128)
            ck = pltpu.make_async_copy(dk_acc, dk_hbm.at[:, cols], sem.at[0])
            cv = pltpu.make_async_copy(dv_acc, dv_hbm.at[:, cols], sem.at[1])
            ck.start()
            cv.start()
            ck.wait()
            cv.wait()

    return pl.pallas_call(
        body, name="ch_bwd", grid=(4, nt), in_specs=specs,
        out_specs=[pl.BlockSpec((CH_TQ, 128), lambda p, i: (i, p)),
                   pl.BlockSpec((2, 3, CH_TQ, CH_TQ), lambda p, i: (p, 0, 0, 0)), ANY, ANY],
        out_shape=[jax.ShapeDtypeStruct((S, 512), F32), jax.ShapeDtypeStruct((N_HEADS_CH, 3, CH_TQ, CH_TQ), F32),
                   jax.ShapeDtypeStruct((S, 512), F32), jax.ShapeDtypeStruct((S, 512), F32)],
        scratch_shapes=[pltpu.VMEM((S, 128), F32), pltpu.VMEM((S, 128), F32), pltpu.SemaphoreType.DMA((2,))],
        compiler_params=_cp(("arbitrary", "arbitrary"), 48),
    )(pn, pn, pn, pn, pv, pv, pv, bias, lse, dycat)


def _col_sum(d):
    _, _, R, C = d.shape

    def body(d_ref, o_ref):
        for b in range(3):
            o_ref[0, b] = jnp.sum(d_ref[0, b], axis=0, keepdims=True)

    return pl.pallas_call(
        body, name="col_sum", grid=(N_HEADS_CH,),
        in_specs=[pl.BlockSpec((1, 3, R, C), lambda h: (h, 0, 0, 0))],
        out_specs=pl.BlockSpec((1, 3, 1, C), lambda h: (h, 0, 0, 0)),
        out_shape=jax.ShapeDtypeStruct((N_HEADS_CH, 3, 1, C), F32),
        compiler_params=_cp(("parallel",), 32),
    )(d)


def _conv_fwd(gc, xc, hc_prev, cw_ref):
    hc = gc * xc
    ext = jnp.concatenate([hc_prev, hc], axis=0)
    r1 = pltpu.roll(ext, 1, 0)[8:]
    r2 = pltpu.roll(ext, 2, 0)[8:]
    return cw_ref[0:1, :] * r2 + cw_ref[1:2, :] * r1 + cw_ref[2:3, :] * hc, hc, r1, r2


def _merge_out(x, ysb, ych, pc, cw, onw, w_out):
    S = x.shape[0]
    tm = min(512, S)
    t8 = tm // 8

    def body(x_ref, ysb_ref, ych_ref, gb_ref, gc_ref, xc_ref, gch_ref, xch_ref, cw_ref, onw_ref, w_ref, xo_ref, yn_ref, ycv_ref):
        i = pl.program_id(0)
        lo = _lo_mask(tm)
        hc_prev = jnp.where(i > 0, gch_ref[...] * xch_ref[...], 0.0)
        conv, _, _, _ = _conv_fwd(gc_ref[...], xc_ref[...], hc_prev, cw_ref)
        ycv = gb_ref[...] * conv
        ycv_ref[...] = ycv
        chunks = [ysb_ref[:, 0:128], ysb_ref[:, 128:256]] + [ych_ref[:, 128 * k:128 * (k + 1)] for k in range(4)]
        chunks += [ycv[:, 0:128], ycv[:, 128:256]]
        for k, ch in enumerate(chunks):
            yn_ref[:, 128 * k:128 * (k + 1)] = _half_norm(ch, onw_ref[:, 128 * k:128 * (k + 1)], lo).astype(yn_ref.dtype)
        xo_ref[...] = x_ref[...] + jnp.dot(yn_ref[...], w_ref[...], preferred_element_type=F32)

    row = lambda i: (i, 0)
    halo = lambda cb: pl.BlockSpec((8, 256), lambda i: (jnp.maximum(i * t8 - 1, 0), cb))
    return pl.pallas_call(
        body, name="merge_out", grid=(S // tm,),
        in_specs=[pl.BlockSpec((tm, D), row), pl.BlockSpec((tm, 256), row), pl.BlockSpec((tm, 512), row),
                  pl.BlockSpec((tm, 256), lambda i: (i, 0)), pl.BlockSpec((tm, 256), lambda i: (i, 1)),
                  pl.BlockSpec((tm, 256), lambda i: (i, 2)), halo(1), halo(2),
                  pl.BlockSpec((3, 256), lambda i: (0, 0)), pl.BlockSpec((1, D), lambda i: (0, 0)), pl.BlockSpec((D, D), lambda i: (0, 0))],
        out_specs=[pl.BlockSpec((tm, D), row), pl.BlockSpec((tm, D), row), pl.BlockSpec((tm, 256), row)],
        out_shape=[jax.ShapeDtypeStruct((S, D), F32), jax.ShapeDtypeStruct((S, D), _MXU), jax.ShapeDtypeStruct((S, 256), F32)],
        compiler_params=_cp(("parallel",), 40),
    )(x, ysb, ych, pc, pc, pc, pc, pc, cw, onw, w_out)


def _ffn_fwd(x, nw, wg, wu, wd, nxt=(), split=()):
    S = x.shape[0]
    tm = min(1024, S)
    ni = S // tm
    n = len(nxt)

    def body(*refs):
        x_ref, nw_ref, wg_ref, wu_ref, wd_ref = refs[:5]
        xo_ref, hn_ref, u_ref, t_ref, a_ref = refs[5 + n:10 + n]
        acc = refs[10 + 2 * n]
        i, j = pl.program_id(0), pl.program_id(1)
        if n:
            gather = _Gather(refs[5:5 + n], refs[10 + n:10 + 2 * n], refs[11 + 2 * n:], split)

            @pl.when(jnp.logical_and(i == 0, j == 0))
            def _():
                gather.start()

            @pl.when(jnp.logical_and(i == max(ni - 2, 0), j == 0))
            def _():
                gather.forward()

        @pl.when(j == 0)
        def _():
            hn_ref[...] = _row_norm(x_ref[...], nw_ref[...]).astype(hn_ref.dtype)
            acc[...] = jnp.zeros_like(acc)

        for r in range(0, tm, min(FFN_SUB, tm)):
            rows = slice(r, r + min(FFN_SUB, tm))
            hn = hn_ref[rows, :]
            u = jnp.dot(hn, wg_ref[0], preferred_element_type=F32)
            t = jnp.dot(hn, wu_ref[0], preferred_element_type=F32)
            a = (u * _sigmoid(u) * t).astype(a_ref.dtype)
            u_ref[0, rows, :], t_ref[0, rows, :], a_ref[0, rows, :] = u.astype(u_ref.dtype), t.astype(t_ref.dtype), a
            acc[rows, :] += jnp.dot(a, wd_ref[0], preferred_element_type=F32)

        @pl.when(j == N_SH - 1)
        def _():
            xo_ref[...] = x_ref[...] + acc[...]

        if n:
            @pl.when(jnp.logical_and(i == ni - 1, j == N_SH - 1))
            def _():
                gather.finish()

    row = lambda i, j: (i, 0)
    sh = lambda i, j: (j, i, 0)
    return pl.pallas_call(
        body, name="ffn_fwd_gather" if n else "ffn_fwd", grid=(ni, N_SH),
        in_specs=[pl.BlockSpec((tm, D), row), pl.BlockSpec((1, D), lambda i, j: (0, 0)),
                  pl.BlockSpec((1, D, FF_SH), lambda i, j: (j, 0, 0)), pl.BlockSpec((1, D, FF_SH), lambda i, j: (j, 0, 0)),
                  pl.BlockSpec((1, FF_SH, D), lambda i, j: (j, 0, 0))] + [ANY] * n,
        out_specs=[pl.BlockSpec((tm, D), row), pl.BlockSpec((tm, D), row), pl.BlockSpec((1, tm, FF_SH), sh),
                   pl.BlockSpec((1, tm, FF_SH), sh), pl.BlockSpec((1, tm, FF_SH), sh)] + [ANY] * n,
        out_shape=[jax.ShapeDtypeStruct((S, D), F32), jax.ShapeDtypeStruct((S, D), _MXU),
                   jax.ShapeDtypeStruct((N_SH, S, FF_SH), _MXU), jax.ShapeDtypeStruct((N_SH, S, FF_SH), _MXU),
                   jax.ShapeDtypeStruct((N_SH, S, FF_SH), _MXU)] + _Gather.out_shape(nxt),
        scratch_shapes=[pltpu.VMEM((tm, D), F32)] + (_Gather.sems(n) if n else []),
        compiler_params=_cp(("arbitrary", "arbitrary"), 60),
    )(x, nw, wg, wu, wd, *nxt)


def _loss_head(x, target):
    S = x.shape[0]
    tm = min(512, S)

    def body(x_ref, t_ref, dx_ref, l_ref):
        @pl.when(pl.program_id(0) == 0)
        def _():
            l_ref[...] = jnp.zeros_like(l_ref)

        e = x_ref[...] - t_ref[...]
        dx_ref[...] = e * (1.0 / D)
        l_ref[...] += 0.5 * jnp.sum(jnp.sum(e * e, axis=-1, keepdims=True) * (1.0 / D))

    return pl.pallas_call(
        body, name="loss_head", grid=(S // tm,),
        in_specs=[pl.BlockSpec((tm, D), lambda i: (i, 0))] * 2,
        out_specs=[pl.BlockSpec((tm, D), lambda i: (i, 0)), pl.BlockSpec((8, 128), lambda i: (0, 0))],
        out_shape=[jax.ShapeDtypeStruct((S, D), F32), jax.ShapeDtypeStruct((8, 128), F32)],
        compiler_params=_cp(("arbitrary",), 32),
    )(x, target)


def _ffn_bwd(dx, xmid, nw, wg, wu, wd, u, t, send=()):
    S = dx.shape[0]
    tm = min(1024, S)
    ni = S // tm
    n = len(send)

    def body(*refs):
        dx_ref, x_ref, nw_ref, wg_ref, wu_ref, wd_ref, u_ref, t_ref = refs[:8]
        du_ref, dt_ref, dxm_ref, dnw_ref = refs[8 + n:12 + n]
        acc = refs[12 + 2 * n]
        i, j = pl.program_id(0), pl.program_id(1)
        if n:
            exchange = _ShardExchange(refs[8:8 + n], refs[12 + n:12 + 2 * n], *refs[13 + 2 * n:])

            @pl.when(jnp.logical_and(i == 0, j == 0))
            def _():
                exchange.start()

        @pl.when(j == 0)
        def _():
            acc[...] = jnp.zeros_like(acc)

        @pl.when(jnp.logical_and(i == 0, j == 0))
        def _():
            dnw_ref[...] = jnp.zeros_like(dnw_ref)

        for r in range(0, tm, min(FFN_SUB, tm)):
            rows = slice(r, r + min(FFN_SUB, tm))
            da = _dg(dx_ref[rows, :], wd_ref[0], NT)
            u, t = u_ref[0, rows, :].astype(F32), t_ref[0, rows, :].astype(F32)
            sg = _sigmoid(u)
            du = _mx(da * t * (sg * (1.0 + u * (1.0 - sg))))
            dt = _mx(da * (u * sg))
            du_ref[0, rows, :], dt_ref[0, rows, :] = du, dt
            acc[rows, :] += _dg(du, wg_ref[0], NT) + _dg(dt, wu_ref[0], NT)

        @pl.when(j == N_SH - 1)
        def _():
            dxn, dw = _row_norm_bwd(x_ref[...], nw_ref[...], acc[...])
            dxm_ref[...] = dx_ref[...] + dxn
            dnw_ref[0:1, :] += dw

        if n:
            @pl.when(jnp.logical_and(i == ni - 1, j == N_SH - 1))
            def _():
                exchange.finish()

    row = lambda i, j: (i, 0)
    sh = lambda i, j: (j, i, 0)
    wsp = lambda r, c: pl.BlockSpec((1, r, c), lambda i, j: (j, 0, 0))
    return pl.pallas_call(
        body, name="ffn_bwd_exchange" if n else "ffn_bwd", grid=(ni, N_SH),
        in_specs=[pl.BlockSpec((tm, D), row), pl.BlockSpec((tm, D), row), pl.BlockSpec((1, D), lambda i, j: (0, 0)),
                  wsp(D, FF_SH), wsp(D, FF_SH), wsp(FF_SH, D), pl.BlockSpec((1, tm, FF_SH), sh), pl.BlockSpec((1, tm, FF_SH), sh)] + [ANY] * n,
        out_specs=[pl.BlockSpec((1, tm, FF_SH), sh), pl.BlockSpec((1, tm, FF_SH), sh), pl.BlockSpec((tm, D), row),
                   pl.BlockSpec((8, D), lambda i, j: (0, 0))] + [ANY] * n,
        out_shape=[jax.ShapeDtypeStruct((N_SH, S, FF_SH), _MXU), jax.ShapeDtypeStruct((N_SH, S, FF_SH), _MXU),
                   jax.ShapeDtypeStruct((S, D), F32), jax.ShapeDtypeStruct((8, D), F32)] + _ShardExchange.out_shape(send),
        scratch_shapes=[pltpu.VMEM((tm, D), F32)] + (_ShardExchange.sems(n) if n else []),
        compiler_params=_cp(("arbitrary", "arbitrary"), 60),
    )(dx, xmid, nw, wg, wu, wd, u, t, *send)


def _wgrad(a, b, a_spec, b_spec, out_block, name):
    S = a.shape[-2]
    tk = min(1024, S)

    def body(a_ref, b_ref, o_ref):
        @pl.when(pl.program_id(0) == 0)
        def _():
            o_ref[...] = jnp.zeros_like(o_ref)

        for r in range(N_SH):
            o_ref[r] += _dg(a_spec[1](a_ref, r), b_spec[1](b_ref, r), TN)

    return pl.pallas_call(
        body, name=name, grid=(S // tk,),
        in_specs=[a_spec[0](tk), b_spec[0](tk)],
        out_specs=pl.BlockSpec((N_SH,) + out_block, lambda k: (0, 0, 0)),
        out_shape=jax.ShapeDtypeStruct((N_SH,) + out_block, F32),
        compiler_params=_cp(("arbitrary",), 56),
    )(a, b)


def _tok(width):
    return (lambda tk: pl.BlockSpec((tk, width), lambda k: (k, 0))), (lambda ref, r: ref[...])


def _tok_cols(width):
    return (lambda tk: pl.BlockSpec((tk, N_SH * width), lambda k: (k, 0))), (lambda ref, r: ref[:, r * width:(r + 1) * width])


def _tok_sh(width):
    return (lambda tk: pl.BlockSpec((N_SH, tk, width), lambda k: (0, k, 0))), (lambda ref, r: ref[r])


def _out_bwd(dxm, w_out, ysb, ych, ycv, onw):
    S = dxm.shape[0]
    tm = min(512, S)

    def body(dx_ref, w_ref, ysb_ref, ych_ref, ycv_ref, onw_ref, dy_ref, dw_ref):
        @pl.when(pl.program_id(0) == 0)
        def _():
            dw_ref[...] = jnp.zeros_like(dw_ref)

        lo = _lo_mask(tm)
        dyn = _dg(dx_ref[...], w_ref[...], NT)
        chunks = [ysb_ref[:, 0:128], ysb_ref[:, 128:256]] + [ych_ref[:, 128 * k:128 * (k + 1)] for k in range(4)]
        chunks += [ycv_ref[:, 0:128], ycv_ref[:, 128:256]]
        for k, ch in enumerate(chunks):
            sl = slice(128 * k, 128 * (k + 1))
            d, dw = _half_norm_bwd(ch, onw_ref[:, sl], dyn[:, sl], lo)
            dy_ref[:, sl] = d
            dw_ref[0:1, sl] += dw

    row = lambda i: (i, 0)
    return pl.pallas_call(
        body, name="out_bwd", grid=(S // tm,),
        in_specs=[pl.BlockSpec((tm, D), row), pl.BlockSpec((D, D), lambda i: (0, 0)), pl.BlockSpec((tm, 256), row),
                  pl.BlockSpec((tm, 512), row), pl.BlockSpec((tm, 256), row), pl.BlockSpec((1, D), lambda i: (0, 0))],
        out_specs=[pl.BlockSpec((tm, D), row), pl.BlockSpec((8, D), lambda i: (0, 0))],
        out_shape=[jax.ShapeDtypeStruct((S, D), F32), jax.ShapeDtypeStruct((8, D), F32)],
        compiler_params=_cp(("arbitrary",), 40),
    )(dxm, w_out, ysb, ych, ycv, onw)


def _in_bwd(dxm, x, anw, w_sh, pqk, pc, dqa, dka, dva, dqh, dkh, dvb, dycat, cw, qw2, kw2):
    S = x.shape[0]
    tm = min(256, S)
    t8 = tm // 8
    last8 = S // 8 - 1

    def body(dxm_ref, x_ref, anw_ref, w_ref, q0_ref, q1_ref, k0_ref, k1_ref, gb_ref, gc_ref, xc_ref, gch_ref, xch_ref, gbn_ref,
             dqa_ref, dka_ref, dva_ref, dqh_ref, dkh_ref, dvb_ref, dyc_ref, dycn_ref, cw_ref, qw_ref, kw_ref,
             dp_ref, dx_ref, danw_ref, dqw_ref, dkw_ref, dcw_ref):
        i = pl.program_id(0)
        n = pl.num_programs(0)

        @pl.when(i == 0)
        def _():
            for r in (danw_ref, dqw_ref, dkw_ref, dcw_ref):
                r[...] = jnp.zeros_like(r)

        lo = _lo_mask(tm)
        dp_ref[:, QA:QA + 256] = dqa_ref[...].astype(dp_ref.dtype)
        dp_ref[:, KA:KA + 256] = dka_ref[...].astype(dp_ref.dtype)
        dp_ref[:, VA:VA + 256] = dva_ref[...].astype(dp_ref.dtype)
        dp_ref[:, VB:VB + 512] = dvb_ref[...].astype(dp_ref.dtype)
        for base, raws, d_ref, nw_ref, dw_ref in ((QB, (q0_ref, q1_ref), dqh_ref, qw_ref, dqw_ref), (KB, (k0_ref, k1_ref), dkh_ref, kw_ref, dkw_ref)):
            for k in range(4):
                raw = raws[k // 2][:, 128 * (k % 2):128 * (k % 2 + 1)]
                d, dw = _half_norm_bwd(raw, nw_ref[...], d_ref[:, 128 * k:128 * (k + 1)], lo)
                dp_ref[:, base + 128 * k:base + 128 * (k + 1)] = d.astype(dp_ref.dtype)
                dw_ref[0:1, :] += dw
        gb, gc, xc = gb_ref[...], gc_ref[...], xc_ref[...]
        hc_prev = jnp.where(i > 0, gch_ref[...] * xch_ref[...], 0.0)
        conv, hc, r1, r2 = _conv_fwd(gc, xc, hc_prev, cw_ref)
        dyc = dyc_ref[...]
        dconv = dyc * gb
        dconv_next = jnp.where(i < n - 1, dycn_ref[...] * gbn_ref[...], 0.0)
        ext = jnp.concatenate([dconv, dconv_next], axis=0)
        l1 = pltpu.roll(ext, tm + 7, 0)[:tm]
        l2 = pltpu.roll(ext, tm + 6, 0)[:tm]
        dhc = cw_ref[2:3, :] * dconv + cw_ref[1:2, :] * l1 + cw_ref[0:1, :] * l2
        dp_ref[:, GB:GB + 256] = (dyc * conv).astype(dp_ref.dtype)
        dp_ref[:, GC:GC + 256] = (dhc * xc).astype(dp_ref.dtype)
        dp_ref[:, XC:XC + 256] = (dhc * gc).astype(dp_ref.dtype)
        dcw_ref[0:1, :] += jnp.sum(dconv * r2, axis=0, keepdims=True)
        dcw_ref[1:2, :] += jnp.sum(dconv * r1, axis=0, keepdims=True)
        dcw_ref[2:3, :] += jnp.sum(dconv * hc, axis=0, keepdims=True)
        dh = jnp.zeros((tm, D), F32)
        for r in range(N_SH):
            dh = dh + lax.dot_general(dp_ref[:, IN_SH * r:IN_SH * (r + 1)], w_ref[r], NT, preferred_element_type=F32)
        dxn, dw = _row_norm_bwd(x_ref[...], anw_ref[...], dh)
        dx_ref[...] = dxm_ref[...] + dxn
        danw_ref[0:1, :] += dw

    row = lambda i: (i, 0)
    colb = lambda w, cb: pl.BlockSpec((tm, w), lambda i: (i, cb))
    prev = lambda cb: pl.BlockSpec((8, 256), lambda i: (jnp.maximum(i * t8 - 1, 0), cb))
    nxt = lambda cb: pl.BlockSpec((8, 256), lambda i: (jnp.minimum((i + 1) * t8, last8), cb))
    const = lambda shp: pl.BlockSpec(shp, lambda i: (0,) * len(shp))
    return pl.pallas_call(
        body, name="in_bwd", grid=(S // tm,),
        in_specs=[pl.BlockSpec((tm, D), row), pl.BlockSpec((tm, D), row), const((1, D)), const((N_SH, D, IN_SH)),
                  colb(256, 0), colb(256, 1), colb(256, 2), colb(256, 3),
                  colb(256, 0), colb(256, 1), colb(256, 2), prev(1), prev(2), nxt(0),
                  pl.BlockSpec((tm, 256), row), pl.BlockSpec((tm, 256), row), pl.BlockSpec((tm, 256), row),
                  pl.BlockSpec((tm, 512), row), pl.BlockSpec((tm, 512), row), pl.BlockSpec((tm, 512), row),
                  colb(256, 768 // 256), nxt(768 // 256), const((3, 256)), const((1, 128)), const((1, 128))],
        out_specs=[pl.BlockSpec((tm, D_IN), row), pl.BlockSpec((tm, D), row), const((8, D)), const((8, 128)), const((8, 128)), const((8, 256))],
        out_shape=[jax.ShapeDtypeStruct((S, D_IN), _MXU), jax.ShapeDtypeStruct((S, D), F32), jax.ShapeDtypeStruct((8, D), F32),
                   jax.ShapeDtypeStruct((8, 128), F32), jax.ShapeDtypeStruct((8, 128), F32), jax.ShapeDtypeStruct((8, 256), F32)],
        compiler_params=_cp(("arbitrary",), 56),
    )(dxm, x, anw, w_sh, pqk, pqk, pqk, pqk, pc, pc, pc, pc, pc, pc, dqa, dka, dva, dqh, dkh, dvb, dycat, dycat, cw, qw2, kw2)


def _rows_tile(h):
    return h if h <= 512 else 256


def _add_halves(g, r1, c_arr):
    _, R, C = g.shape
    h = R // 2
    tr = _rows_tile(h)
    nb = h // tr

    def body(c_ref, g_ref, r_ref, o_ref, ob_ref):
        s = g_ref[...] + r_ref[...]
        o_ref[...] = s
        ob_ref[...] = s.astype(ob_ref.dtype)

    blk = (1, tr, C)
    return pl.pallas_call(
        body, name="add_halves",
        grid_spec=pltpu.PrefetchScalarGridSpec(
            num_scalar_prefetch=1, grid=(N_SH, nb),
            in_specs=[pl.BlockSpec(blk, lambda s, i, c: (s, c[0] * nb + i, 0)), pl.BlockSpec(blk, lambda s, i, c: (s, i, 0))],
            out_specs=[pl.BlockSpec(blk, lambda s, i, c: (s, i, 0))] * 2),
        out_shape=[jax.ShapeDtypeStruct((N_SH, h, C), F32), jax.ShapeDtypeStruct((N_SH, h, C), jnp.bfloat16)],
        compiler_params=_cp(("parallel", "parallel"), 32),
    )(c_arr, g, r1)


def _add_shards(s1, r2, me_arr):
    _, h, C = s1.shape
    tr = _rows_tile(h)

    def body(me_ref, s_ref, r_ref, o_ref):
        o_ref[...] = ((s_ref[0] + r_ref[0].astype(F32)) + r_ref[1].astype(F32)) + r_ref[2].astype(F32)

    return pl.pallas_call(
        body, name="add_shards",
        grid_spec=pltpu.PrefetchScalarGridSpec(
            num_scalar_prefetch=1, grid=(h // tr,),
            in_specs=[pl.BlockSpec((1, tr, C), lambda i, me: (me[0], i, 0)), pl.BlockSpec((3, tr, C), lambda i, me: (0, i, 0))],
            out_specs=pl.BlockSpec((tr, C), lambda i, me: (i, 0))),
        out_shape=jax.ShapeDtypeStruct((h, C), F32),
        compiler_params=_cp(("parallel",), 32),
    )(me_arr, s1, r2)


def _adam_math(w, g, m, v):
    m = ADAM_B1 * m + (1.0 - ADAM_B1) * g
    v = ADAM_B2 * v + (1.0 - ADAM_B2) * (g * g)
    m_hat = m / (1.0 - ADAM_B1 ** ADAM_STEP)
    v_hat = v / (1.0 - ADAM_B2 ** ADAM_STEP)
    return -ADAM_LR * (m_hat / (jnp.sqrt(v_hat) + ADAM_EPS) + ADAM_WD * w), m, v


def _adam_big(w, m, v, mine, theirs, c_arr):
    L, R, C = w.shape
    h = R // 2
    tr = _rows_tile(h)
    nb = h // tr

    def body(c_ref, w_ref, m_ref, v_ref, a_ref, b_ref, g_ref, d_ref, mo_ref, vo_ref):
        g = jnp.where(pl.program_id(1) == c_ref[0], a_ref[...], b_ref[...])
        g_ref[...] = g
        d_ref[...], mo_ref[...], vo_ref[...] = _adam_math(w_ref[...], g, m_ref[...], v_ref[...])

    full = pl.BlockSpec((1, tr, C), lambda l, s, i, c: (l, s * nb + i, 0))
    half = pl.BlockSpec((1, tr, C), lambda l, s, i, c: (l, i, 0))
    return pl.pallas_call(
        body, name="adam_big",
        grid_spec=pltpu.PrefetchScalarGridSpec(num_scalar_prefetch=1, grid=(L, 2, nb), in_specs=[full, full, full, half, half],
                                               out_specs=[full] * 4),
        out_shape=[jax.ShapeDtypeStruct((L, R, C), F32)] * 4,
        compiler_params=_cp(("parallel", "parallel", "parallel"), 32),
    )(c_arr, w, m, v, mine, theirs)


def _adam_small(w, g, m, v):
    def body(w_ref, g_ref, m_ref, v_ref, d_ref, mo_ref, vo_ref):
        d_ref[...], mo_ref[...], vo_ref[...] = _adam_math(w_ref[...], g_ref[...], m_ref[...], v_ref[...])

    return pl.pallas_call(body, name="adam_small", out_shape=[jax.ShapeDtypeStruct(w.shape, F32)] * 3)(w, g, m, v)


def _allreduce_small(buf):
    R = buf.shape[0]

    def body(x_ref, o_ref, gat, ssem, rsem):
        x, y, c = lax.axis_index("x"), lax.axis_index("y"), lax.axis_index("c")
        me = 4 * x + 2 * y + c
        gat[me] = x_ref[...]
        cps = []
        for k in range(1, 8):
            peer = (x ^ (k >> 2), y ^ ((k >> 1) & 1), c ^ (k & 1))
            cps.append(pltpu.make_async_remote_copy(src_ref=x_ref, dst_ref=gat.at[me], send_sem=ssem.at[k - 1], recv_sem=rsem.at[k - 1],
                                                    device_id=peer, device_id_type=MESH))
        for cp in cps:
            cp.start()
        for cp in cps:
            cp.wait()
        acc = gat[0]
        for s in range(1, 8):
            acc = acc + gat[s]
        o_ref[...] = acc

    return pl.pallas_call(
        body, name="allreduce_small", out_shape=jax.ShapeDtypeStruct((R, 128), F32),
        in_specs=[pl.BlockSpec(memory_space=pltpu.VMEM)], out_specs=pl.BlockSpec(memory_space=pltpu.VMEM),
        scratch_shapes=[pltpu.VMEM((8, R, 128), F32), pltpu.SemaphoreType.DMA((7,)), pltpu.SemaphoreType.DMA((7,))],
    )(buf)


def _sibling_exchange(srcs, pick, shapes, name):
    n = len(srcs)

    def body(*refs):
        ins, outs = refs[:n], refs[n:2 * n]
        ssem, rsem = refs[2 * n:]
        x, y, c = lax.axis_index("x"), lax.axis_index("y"), lax.axis_index("c")
        cps = [pltpu.make_async_remote_copy(src_ref=pick(ins[f], f, c), dst_ref=outs[f], send_sem=ssem.at[f], recv_sem=rsem.at[f],
                                            device_id=(x, y, 1 - c), device_id_type=MESH) for f in range(n)]
        for cp in cps:
            cp.start()
        for cp in cps:
            cp.wait()

    return pl.pallas_call(
        body, name=name, out_shape=[jax.ShapeDtypeStruct(shp, F32) for shp in shapes],
        in_specs=[ANY] * n, out_specs=[ANY] * n,
        scratch_shapes=[pltpu.SemaphoreType.DMA((n,)), pltpu.SemaphoreType.DMA((n,))],
    )(*srcs)


class _ShardExchange:
    def __init__(self, ins, outs, ssem, rsem):
        x, y, c = lax.axis_index("x"), lax.axis_index("y"), lax.axis_index("c")
        self.copies = [pltpu.make_async_remote_copy(src_ref=ins[f].at[2 * px + py], dst_ref=outs[f].at[k], send_sem=ssem.at[3 * f + k],
                                                    recv_sem=rsem.at[3 * f + k], device_id=(px, py, c), device_id_type=MESH)
                       for f in range(len(ins)) for k, (px, py) in enumerate([(1 - x, y), (x, 1 - y), (1 - x, 1 - y)])]

    def start(self):
        for cp in self.copies:
            cp.start()

    def finish(self):
        for cp in self.copies:
            cp.wait()

    @staticmethod
    def sems(n):
        return [pltpu.SemaphoreType.DMA((3 * n,))] * 2

    @staticmethod
    def out_shape(srcs):
        return [jax.ShapeDtypeStruct((3,) + s.shape[1:], s.dtype) for s in srcs]


def _rs_chip_shards(srcs):
    n = len(srcs)

    def body(*refs):
        ex = _ShardExchange(refs[:n], refs[n:2 * n], *refs[2 * n:])
        ex.start()
        ex.finish()

    return pl.pallas_call(body, name="rs_chip_shards", out_shape=_ShardExchange.out_shape(srcs), in_specs=[ANY] * n, out_specs=[ANY] * n,
                          scratch_shapes=_ShardExchange.sems(n))(*srcs)


class _GradReducer:
    def __init__(self):
        x, y, c = lax.axis_index("x"), lax.axis_index("y"), lax.axis_index("c")
        self.c_arr = jnp.reshape(c, (1,)).astype(jnp.int32)
        self.me_arr = jnp.reshape(2 * x + y, (1,)).astype(jnp.int32)
        self.pending, self.done = None, {}

    def start(self, l, grads):
        halves = [g.shape[1] // 2 for g in grads]
        r1 = _sibling_exchange(grads, lambda ref, f, c: ref.at[:, pl.ds(pl.multiple_of((1 - c) * halves[f], 8), halves[f]), :],
                               [(N_SH, h, g.shape[2]) for g, h in zip(grads, halves)], "rs_sibling_halves")
        s1, wire = zip(*[_add_halves(g, r, self.c_arr) for g, r in zip(grads, r1)])
        self.pending = (l, s1, wire)

    def to_send(self):
        return self.pending[2] if self.pending else ()

    def received(self, r2):
        l, s1, _ = self.pending
        self.pending = None
        mine = [_add_shards(s, r, self.me_arr) for s, r in zip(s1, r2)]
        theirs = _sibling_exchange(mine, lambda ref, f, c: ref, [m.shape for m in mine], "rs_sibling_reduced")
        self.done[l] = (mine, theirs)

    def finish(self):
        self.received(_rs_chip_shards(self.pending[2]))


def _pad_rows(flat, rows):
    return jnp.pad(flat, (0, rows * 128 - flat.shape[0])).reshape(rows, 128)


class _ShardedWeights:
    split = [True] * 5 + [False]

    def __init__(self, w):
        self.parts = [[w[k][l].astype(_MXU) for k in _BIG] + [jnp.zeros((8, 128), F32).at[:3, :HD].set(w["conv_w"][l])]
                      for l in range(DEPTH)]

    def first(self):
        return self.unpack(_allgather_layer(self.parts[0], self.split))

    def following(self, l):
        return (self.parts[l + 1], self.split) if l + 1 < DEPTH else ((), ())

    @staticmethod
    def unpack(got):
        W = dict(zip(_BIG, got[:5]))
        W["conv_w"] = jnp.concatenate([got[5][r, :3, :HD] for r in range(N_SH)], axis=-1)
        return W


def _local_step(x, target, small, weights, reducer):
    S = x.shape[0]
    saved, big = [], []
    W = weights.first()
    for l in range(DEPTH):
        big.append(W)
        anw, onw, fnw = small["attn_norm_w"][l][None], small["out_norm_w"][l][None], small["ffn_norm_w"][l][None]
        qw2 = jnp.tile(small["q_norm_w"][l], 2)[None]
        kw2 = jnp.tile(small["k_norm_w"][l], 2)[None]
        cw = W["conv_w"]
        bias = _bias_tiles(small["rel_bias"][l])
        h, pa, pqk, pn, pv, pc = _attn_in(x, anw, W["w_in"], qw2, kw2)
        ysb = _sb_fwd(pa)
        ych, lse = _ch_fwd(pn, pv, bias)
        w_out = W["w_out"].reshape(D, D)
        xmid, yn, ycv = _merge_out(x, ysb, ych, pc, cw, onw, w_out)
        outs = _ffn_fwd(xmid, fnw, W["w_gate"], W["w_up"], W["w_down"], *weights.following(l))
        xo, hn, u, t, a = outs[:5]
        saved.append(dict(x=x, h=h, pa=pa, pqk=pqk, pn=pn, pv=pv, pc=pc, ysb=ysb, ych=ych, lse=lse, ycv=ycv, yn=yn, xmid=xmid, hn=hn, u=u, t=t, a=a,
                          anw=anw, onw=onw, fnw=fnw, qw2=qw2, kw2=kw2, cw=cw, bias=bias, w_out=w_out))
        x = xo
        if l + 1 < DEPTH:
            W = weights.unpack(outs[5:])
    dx, lpart = _loss_head(x, target)
    loss = lpart[0, 0]
    grads = [None] * DEPTH
    for l in reversed(range(DEPTH)):
        W, sv = big[l], saved[l]
        outs = _ffn_bwd(dx, sv["xmid"], sv["fnw"], W["w_gate"], W["w_up"], W["w_down"], sv["u"], sv["t"], reducer.to_send())
        du, dt, dxm, dfnw = outs[:4]
        if outs[4:]:
            reducer.received(outs[4:])
        g_gate = _wgrad(sv["hn"], du, _tok(D), _tok_sh(FF_SH), (D, FF_SH), "wgrad_gate")
        g_up = _wgrad(sv["hn"], dt, _tok(D), _tok_sh(FF_SH), (D, FF_SH), "wgrad_up")
        g_down = _wgrad(sv["a"], dx, _tok_sh(FF_SH), _tok(D), (FF_SH, D), "wgrad_down")
        dycat, donw = _out_bwd(dxm, sv["w_out"], sv["ysb"], sv["ych"], sv["ycv"], sv["onw"])
        g_out = _wgrad(sv["yn"], dxm, _tok_cols(OUT_SH), _tok(D), (OUT_SH, D), "wgrad_out")
        dqa, dka, dva = _sb_bwd(sv["pa"], dycat)
        dqh, dbias, dkh, dvb = _ch_bwd(sv["pn"], sv["pv"], sv["bias"], sv["lse"], dycat)
        drelb = _bias_grad(dbias)
        dproj, dx, danw, dqw, dkw, dcw = _in_bwd(dxm, sv["x"], sv["anw"], W["w_in"], sv["pqk"], sv["pc"], dqa, dka, dva, dqh, dkh, dvb, dycat,
                                                 sv["cw"], sv["qw2"], sv["kw2"])
        g_in = _wgrad(sv["h"], dproj, _tok(D), _tok_cols(IN_SH), (D, IN_SH), "wgrad_in")
        grads[l] = dict(w_in=g_in, w_out=g_out, w_gate=g_gate, w_up=g_up, w_down=g_down,
                        attn_norm_w=danw[0], out_norm_w=donw[0], ffn_norm_w=dfnw[0],
                        q_norm_w=dqw[0, :HD] + dqw[0, HD:], k_norm_w=dkw[0, :HD] + dkw[0, HD:], rel_bias=drelb, conv_w=dcw[:3])
        reducer.start(l, [grads[l][k] for k in _BIG])
    reducer.finish()
    return loss, dx, grads


_BIG = ("w_in", "w_out", "w_gate", "w_up", "w_down")
_SMALL = ("attn_norm_w", "q_norm_w", "k_norm_w", "rel_bias", "out_norm_w", "ffn_norm_w")
_ORDER = ("attn_norm_w", "w_in", "q_norm_w", "k_norm_w", "rel_bias", "conv_w", "out_norm_w", "w_out", "ffn_norm_w", "w_gate", "w_up", "w_down")


def kernel(x, attn_norm_w, w_in, q_norm_w, k_norm_w, rel_bias, conv_w, out_norm_w, w_out, ffn_norm_w, w_gate, w_up, w_down, loss_target, m_attn_norm_w, m_w_in, m_q_norm_w, m_k_norm_w, m_rel_bias, m_conv_w, m_out_norm_w, m_w_out, m_ffn_norm_w, m_w_gate, m_w_up, m_w_down, v_attn_norm_w, v_w_in, v_q_norm_w, v_k_norm_w, v_rel_bias, v_conv_w, v_out_norm_w, v_w_out, v_ffn_norm_w, v_w_gate, v_w_up, v_w_down):
    w = dict(attn_norm_w=attn_norm_w, w_in=w_in, q_norm_w=q_norm_w, k_norm_w=k_norm_w, rel_bias=rel_bias, conv_w=conv_w,
             out_norm_w=out_norm_w, w_out=w_out, ffn_norm_w=ffn_norm_w, w_gate=w_gate, w_up=w_up, w_down=w_down)
    m = dict(attn_norm_w=m_attn_norm_w, w_in=m_w_in, q_norm_w=m_q_norm_w, k_norm_w=m_k_norm_w, rel_bias=m_rel_bias, conv_w=m_conv_w,
             out_norm_w=m_out_norm_w, w_out=m_w_out, ffn_norm_w=m_ffn_norm_w, w_gate=m_w_gate, w_up=m_w_up, w_down=m_w_down)
    v = dict(attn_norm_w=v_attn_norm_w, w_in=v_w_in, q_norm_w=v_q_norm_w, k_norm_w=v_k_norm_w, rel_bias=v_rel_bias, conv_w=v_conv_w,
             out_norm_w=v_out_norm_w, w_out=v_w_out, ffn_norm_w=v_ffn_norm_w, w_gate=v_w_gate, w_up=v_w_up, w_down=v_w_down)
    xi, yi = lax.axis_index("x"), lax.axis_index("y")
    me = 2 * xi + yi

    small = {k: w[k] for k in _SMALL}
    reducer = _GradReducer()
    loss, grad_x, grads = _local_step(x[0], loss_target[0], small, _ShardedWeights(w), reducer)
    loss = lax.psum(loss, ("x", "y", "c"))

    out = {}
    for f, k in enumerate(_BIG):
        mine, theirs = (jnp.stack([reducer.done[l][side][f] for l in range(DEPTH)]) for side in (0, 1))
        out[k] = _adam_big(w[k], m[k], v[k], mine, theirs, reducer.c_arr)

    names = _SMALL + ("conv_w",)
    gflat = jnp.concatenate([jnp.stack([grads[l][k] for l in range(DEPTH)]).reshape(-1) for k in names])
    rows = -(-gflat.shape[0] // 1024) * 8
    gsum = _allreduce_small(_pad_rows(gflat, rows)).reshape(-1)
    gs, off = {}, 0
    for k in names:
        shp = (DEPTH, 3, 4 * HD) if k == "conv_w" else w[k].shape
        size = int(np.prod(shp))
        gs[k] = gsum[off:off + size].reshape(shp)
        off += size
    gs["conv_w"] = lax.dynamic_slice_in_dim(gs["conv_w"], me * HD, HD, axis=2)
    flat = lambda d: jnp.concatenate([d[k].reshape(-1) for k in names])
    rows = -(-flat(w).shape[0] // 1024) * 8
    dsm, msm, vsm = _adam_small(_pad_rows(flat(w), rows), _pad_rows(flat(gs), rows), _pad_rows(flat(m), rows), _pad_rows(flat(v), rows))
    off = 0
    for k in names:
        size = int(np.prod(w[k].shape))
        out[k] = (gs[k],) + tuple(a.reshape(-1)[off:off + size].reshape(w[k].shape) for a in (dsm, msm, vsm))
        off += size

    return (loss, grad_x[None]) + tuple(out[k][j] for j in range(4) for k in _ORDER)
```

```python
import functools

import numpy as np
import jax
import jax.numpy as jnp
from jax import lax
from jax.experimental import pallas as pl
from jax.experimental.pallas import tpu as pltpu

F32 = jnp.float32
_MXU = jnp.bfloat16

D = 1024
DEPTH = 4
HD = 64
D_IN = 3072
N_SH = 4
IN_SH = D_IN // N_SH
FF_SH = 704
OUT_SH = 256
N_HEADS_CH = 8
D_CH = N_HEADS_CH * HD
REL = 257
QA, KA, VA, QB, KB, VB, GB, GC, XC = 0, 256, 512, 768, 1280, 1792, 2304, 2560, 2816
EPS = 1e-6
NEG = -1e30
FFN_SUB = 256
SB_BLK = 128
CH_TQ = 256
CH_SUB = 256
EXP_ZERO = -104.0

ADAM_LR, ADAM_B1, ADAM_B2, ADAM_EPS, ADAM_WD, ADAM_STEP = 0.001, 0.9, 0.999, 1e-08, 0.01, 10

NT = (((1,), (1,)), ((), ()))
TN = (((0,), (0,)), ((), ()))
MESH = pl.DeviceIdType.MESH
ANY = pl.BlockSpec(memory_space=pl.ANY)
MB = 1 << 20


def _cp(sem=None, vmem_mb=None):
    return pltpu.CompilerParams(dimension_semantics=sem, vmem_limit_bytes=None if vmem_mb is None else vmem_mb * MB)


def _mx(a):
    return a.astype(_MXU)


def _dot(a, b):
    return jnp.dot(_mx(a), _mx(b), preferred_element_type=F32)


def _dg(a, b, dims):
    return lax.dot_general(_mx(a), _mx(b), dims, preferred_element_type=F32)


def _sigmoid(u):
    return 1.0 / (1.0 + jnp.exp(-u))


def _row_norm(x, w):
    r = lax.rsqrt(jnp.mean(x * x, axis=-1, keepdims=True) + EPS)
    return x * r * w


def _row_norm_bwd(x, w, dy):
    r = lax.rsqrt(jnp.mean(x * x, axis=-1, keepdims=True) + EPS)
    xh = x * r
    dw = jnp.sum(dy * xh, axis=0, keepdims=True)
    dxh = dy * w
    return r * (dxh - xh * jnp.mean(dxh * xh, axis=-1, keepdims=True)), dw


def _half_rs(x, lo):
    sq = x * x
    s0 = jnp.sum(jnp.where(lo, sq, 0.0), axis=-1, keepdims=True)
    s1 = jnp.sum(jnp.where(lo, 0.0, sq), axis=-1, keepdims=True)
    return jnp.where(lo, lax.rsqrt(s0 * (1.0 / HD) + EPS), lax.rsqrt(s1 * (1.0 / HD) + EPS))


def _half_norm(x, w, lo):
    return x * _half_rs(x, lo) * w


def _half_norm_bwd(x, w, dy, lo):
    r = _half_rs(x, lo)
    xh = x * r
    dw = jnp.sum(dy * xh, axis=0, keepdims=True)
    dxh = dy * w
    pr = dxh * xh
    m0 = jnp.sum(jnp.where(lo, pr, 0.0), axis=-1, keepdims=True)
    m1 = jnp.sum(jnp.where(lo, 0.0, pr), axis=-1, keepdims=True)
    mean = jnp.where(lo, m0, m1) * (1.0 / HD)
    return r * (dxh - xh * mean), dw


def _lo_mask(t):
    return lax.broadcasted_iota(jnp.int32, (t, 128), 1) < HD


class _Gather:
    def __init__(self, ins, outs, sems, split):
        self.n = n = len(ins)
        lsem, ssem, rsem, fsem, gsem = sems
        x, y, c = lax.axis_index("x"), lax.axis_index("y"), lax.axis_index("c")
        me = 2 * x + y
        chips = [(1 - x, y), (x, 1 - y), (1 - x, 1 - y)]

        def half(f):
            h = ins[f].shape[0] // 2
            return pl.ds(pl.multiple_of(c * h, 16), h)

        self.local = [pltpu.make_async_copy(ins[f], outs[f].at[me], lsem.at[f]) for f in range(n)]
        self.sends, self.passed = [], []
        for f in range(n):
            for k, (px, py) in enumerate(chips):
                src, dst = (ins[f].at[half(f)], outs[f].at[me, half(f)]) if split[f] else (ins[f], outs[f].at[me])
                self.sends.append(pltpu.make_async_remote_copy(src_ref=src, dst_ref=dst, send_sem=ssem.at[3 * f + k], recv_sem=rsem.at[3 * f + k],
                                                               device_id=(px, py, c), device_id_type=MESH))
                if split[f]:
                    got = outs[f].at[2 * px + py, half(f)]
                    self.passed.append(pltpu.make_async_remote_copy(src_ref=got, dst_ref=got, send_sem=fsem.at[3 * f + k], recv_sem=gsem.at[3 * f + k],
                                                                    device_id=(x, y, 1 - c), device_id_type=MESH))
                else:
                    self.passed.append(None)

    def start(self):
        for cp in self.local + self.sends:
            cp.start()

    def forward(self):
        for cp, fw in zip(self.sends, self.passed):
            cp.wait_recv()
            if fw is not None:
                fw.start()

    def finish(self):
        passed = [fw for fw in self.passed if fw is not None]
        for fw in passed:
            fw.wait_recv()
        for cp in self.sends + passed:
            cp.wait_send()
        for cp in self.local:
            cp.wait()

    @staticmethod
    def sems(n):
        return [pltpu.SemaphoreType.DMA((n,))] + [pltpu.SemaphoreType.DMA((3 * n,))] * 4

    @staticmethod
    def out_shape(parts):
        return [jax.ShapeDtypeStruct((N_SH,) + p.shape, p.dtype) for p in parts]


def _allgather_layer(parts, split):
    n = len(parts)

    def body(*refs):
        g = _Gather(refs[:n], refs[n:2 * n], refs[2 * n:], split)
        g.start()
        g.forward()
        g.finish()

    return pl.pallas_call(
        body, name="allgather_layer", out_shape=_Gather.out_shape(parts),
        in_specs=[ANY] * n, out_specs=[ANY] * n, scratch_shapes=_Gather.sems(n),
    )(*parts)


def _attn_in(x, nw, w_sh, qw2, kw2):
    S = x.shape[0]
    tm = min(512, S)
    assert (QB, VB, GB) == (IN_SH, 2 * IN_SH + 256, 3 * IN_SH)

    def body(x_ref, nw_ref, w_ref, qw_ref, kw_ref, h_ref, pa_ref, pqk_ref, pn_ref, pv_ref, pc_ref):
        h = _row_norm(x_ref[...], nw_ref[...]).astype(h_ref.dtype)
        h_ref[...] = h
        pa_ref[...] = jnp.dot(h, w_ref[0], preferred_element_type=F32).astype(pa_ref.dtype)
        pqk_ref[:, :IN_SH] = jnp.dot(h, w_ref[1], preferred_element_type=F32)
        p2 = jnp.dot(h, w_ref[2], preferred_element_type=F32)
        pqk_ref[:, IN_SH:] = p2[:, :256]
        pv_ref[...] = p2[:, 256:].astype(pv_ref.dtype)
        pc_ref[...] = jnp.dot(h, w_ref[3], preferred_element_type=F32)
        lo = _lo_mask(tm)
        for k in range(2 * D_CH // 128):
            cols = slice(128 * k, 128 * (k + 1))
            is_q = k < D_CH // 128
            n = _half_norm(pqk_ref[:, cols], (qw_ref if is_q else kw_ref)[...], lo)
            pn_ref[:, cols] = (n * 0.125 if is_q else n).astype(pn_ref.dtype)

    row = lambda i: (i, 0)
    const = lambda shp: pl.BlockSpec(shp, lambda i: (0,) * len(shp))
    return pl.pallas_call(
        body, name="attn_in", grid=(S // tm,),
        in_specs=[pl.BlockSpec((tm, D), row), const((1, D)), const((N_SH, D, IN_SH)), const((1, 128)), const((1, 128))],
        out_specs=[pl.BlockSpec((tm, D), row), pl.BlockSpec((tm, IN_SH), row), pl.BlockSpec((tm, 2 * D_CH), row),
                   pl.BlockSpec((tm, 2 * D_CH), row), pl.BlockSpec((tm, D_CH), row), pl.BlockSpec((tm, IN_SH), row)],
        out_shape=[jax.ShapeDtypeStruct((S, D), _MXU), jax.ShapeDtypeStruct((S, IN_SH), _MXU), jax.ShapeDtypeStruct((S, 2 * D_CH), F32),
                   jax.ShapeDtypeStruct((S, 2 * D_CH), _MXU), jax.ShapeDtypeStruct((S, D_CH), _MXU), jax.ShapeDtypeStruct((S, IN_SH), F32)],
        compiler_params=_cp(("parallel",), 48),
    )(x, nw, w_sh, qw2, kw2)


SB_NB = 3
SB_TK = SB_NB * SB_BLK
SB_QB = 4


def _sb_any_open(states):
    return functools.reduce(jnp.logical_or, [jnp.logical_and(st[0] >= 0, st[1] > EXP_ZERO) for st in states])


def _sb_const_arrays():
    r, c = np.meshgrid(np.arange(SB_TK), np.arange(256), indexing="ij")
    kofs = (r - (r & 127)) + (c & 127)
    r2, c2 = np.meshgrid(np.arange(256), np.arange(256), indexing="ij")
    same = (r2 >= 128) == (c2 >= 128)
    nearer = same & ((r2 & 127) > (c2 & 127))
    at_or_nearer = same & ((r2 & 127) >= (c2 & 127))
    return (jnp.asarray(kofs, jnp.int32), jnp.asarray(kofs - (r & 127), jnp.int32), jnp.asarray(nearer, _MXU), jnp.asarray(at_or_nearer, _MXU))


_SB_CONST_SPECS = [pl.BlockSpec((SB_TK, 256), lambda p, i: (0, 0))] * 2 + [pl.BlockSpec((256, 256), lambda p, i: (0, 0))] * 2


class _SbConsts:
    def __init__(self, kofs_ref, rel_ref, nearer_ref, at_or_nearer_ref):
        self.kofs, self.rel, self.nearer, self.at_or_nearer = kofs_ref, rel_ref, nearer_ref, at_or_nearer_ref
        self.lo = lax.broadcasted_iota(jnp.int32, (SB_BLK, 128), 1) < HD
        self.head0 = lax.broadcasted_iota(jnp.int32, (SB_BLK, 256), 1) < 128


def _two_heads(x, lo):
    return jnp.concatenate([jnp.where(lo, x, 0), jnp.where(lo, 0, x)], axis=0)


def _per_block(vals, head0):
    return jnp.concatenate([jnp.where(head0, v0, v1) for v0, v1 in vals], axis=0)


def _head_sums(x):
    return jnp.sum(x[:, :128], axis=-1, keepdims=True), jnp.sum(x[:, 128:], axis=-1, keepdims=True)


def _sb_tile(q2, k_ref, i, jt, c0, c1, K):
    s0 = pl.multiple_of(jnp.maximum(jt - (SB_NB - 1), 0) * SB_BLK, SB_BLK)
    kt = k_ref[pl.ds(s0, SB_TK), :]
    k2 = [_two_heads(kt[b * SB_BLK:(b + 1) * SB_BLK], K.lo) for b in range(SB_NB)]
    keep = jnp.logical_and(K.rel[...] < i * SB_BLK - s0, K.kofs[...] < (jt + 1) * SB_BLK - s0)
    z = jnp.where(keep, jnp.concatenate([_dg(q2, k2[b], NT) for b in range(SB_NB)], axis=0), NEG)
    lr = -(jnp.maximum(z, 0.0) + jnp.log(1.0 + jnp.exp(-jnp.abs(z))))
    carries = [None] * SB_NB
    for b in reversed(range(SB_NB)):
        carries[b] = (c0, c1)
        s = _head_sums(lr[b * SB_BLK:(b + 1) * SB_BLK])
        c0, c1 = c0 + s[0], c1 + s[1]
    after = _dot(lr, K.nearer[...]) + _per_block(carries, K.head0)
    w = jnp.exp(z + lr + after)
    return s0, k2, z, lr, w, c0, c1


def _sb_fwd(pa):
    S = pa.shape[0]
    nq = S // SB_BLK
    assert nq % SB_QB == 0

    def body(q_ref, k_ref, v_ref, kofs_ref, rel_ref, nearer_ref, aon_ref, o_ref):
        K = _SbConsts(kofs_ref, rel_ref, nearer_ref, aon_ref)
        blocks = [pl.program_id(1) * SB_QB + s for s in range(SB_QB)]
        q2 = [q_ref[s * SB_BLK:(s + 1) * SB_BLK, :] * 0.125 for s in range(SB_QB)]

        def step(st):
            new = []
            for s, (jt, _, acc, c0, c1) in enumerate(st):
                s0, _, _, _, w, c0, c1 = _sb_tile(q2[s], k_ref, blocks[s], jt, c0, c1, K)
                vt = v_ref[pl.ds(s0, SB_TK), :]
                for b in range(SB_NB):
                    rows = slice(b * SB_BLK, (b + 1) * SB_BLK)
                    acc = acc + _dot(w[rows], _two_heads(vt[rows], K.lo))
                new.append((jt - SB_NB, jnp.max(jnp.maximum(c0, c1)), acc, c0, c1))
            return tuple(new)

        zc = jnp.zeros((SB_BLK, 1), F32)
        st = lax.while_loop(_sb_any_open, step, tuple((i, jnp.float32(0.0), jnp.zeros((SB_BLK, 128), F32), zc, zc) for i in blocks))
        for s in range(SB_QB):
            o_ref[s * SB_BLK:(s + 1) * SB_BLK, :] = st[s][2]

    tq = SB_QB * SB_BLK
    return pl.pallas_call(
        body, name="sb_fwd", grid=(2, nq // SB_QB),
        in_specs=[pl.BlockSpec((tq, 128), lambda p, i: (i, QA // 128 + p)),
                  pl.BlockSpec((S, 128), lambda p, i: (0, KA // 128 + p)),
                  pl.BlockSpec((S, 128), lambda p, i: (0, VA // 128 + p))] + _SB_CONST_SPECS,
        out_specs=pl.BlockSpec((tq, 128), lambda p, i: (i, p)),
        out_shape=jax.ShapeDtypeStruct((S, 256), F32),
        compiler_params=_cp(("parallel", "arbitrary"), 40),
    )(pa, pa, pa, *_sb_const_arrays())


def _sb_bwd(pa, dycat):
    S = pa.shape[0]
    nq = S // SB_BLK
    assert nq % SB_QB == 0
    n_steps = nq // SB_QB

    def body(q_ref, k_ref, v_ref, do_ref, kofs_ref, rel_ref, nearer_ref, aon_ref, dq_ref, dk_hbm, dv_hbm, dk_acc, dv_acc, sem):
        p, step_i = pl.program_id(0), pl.program_id(1)
        K = _SbConsts(kofs_ref, rel_ref, nearer_ref, aon_ref)

        @pl.when(step_i == 0)
        def _():
            dk_acc[...] = jnp.zeros_like(dk_acc)
            dv_acc[...] = jnp.zeros_like(dv_acc)

        blocks = [step_i * SB_QB + s for s in range(SB_QB)]
        q2 = [q_ref[s * SB_BLK:(s + 1) * SB_BLK, :] * 0.125 for s in range(SB_QB)]
        do = [_mx(do_ref[s * SB_BLK:(s + 1) * SB_BLK, :]) for s in range(SB_QB)]

        def tile(s, jt, c0, c1):
            s0, k2, z, lr, w, c0, c1 = _sb_tile(q2[s], k_ref, blocks[s], jt, c0, c1, K)
            vt = v_ref[pl.ds(s0, SB_TK), :]
            g = jnp.concatenate([_dg(do[s], _two_heads(vt[b * SB_BLK:(b + 1) * SB_BLK], K.lo), NT) for b in range(SB_NB)], axis=0) * w
            gs = [_head_sums(g[b * SB_BLK:(b + 1) * SB_BLK]) for b in range(SB_NB)]
            return dict(s0=s0, k2=k2, z=z, lr=lr, w=w, g=g, gs=gs, c0=c0, c1=c1, mx=jnp.max(jnp.maximum(c0, c1)),
                        g0=sum(x[0] for x in gs), g1=sum(x[1] for x in gs))

        def grads(s, T, tot, n0, n1):
            nearer = [None] * SB_NB
            for b in reversed(range(SB_NB)):
                nearer[b] = (n0, n1)
                n0, n1 = n0 + T["gs"][b][0], n1 + T["gs"][b][1]
            farther = _per_block([tot] * SB_NB, K.head0) - _per_block(nearer, K.head0) - _dot(T["g"], K.at_or_nearer[...])
            beta = jnp.exp(T["z"] + T["lr"])
            dz = _mx(T["g"] * (1.0 - beta) - farther * beta)
            wm = _mx(T["w"])
            dq = jnp.zeros((SB_BLK, 128), F32)
            dks, dvs = [], []
            for b in range(SB_NB):
                rows = slice(b * SB_BLK, (b + 1) * SB_BLK)
                dq = dq + _dot(dz[rows], T["k2"][b])
                dk2 = _dg(dz[rows], q2[s], TN)
                dv2 = _dg(wm[rows], do[s], TN)
                dks.append(jnp.where(K.lo, dk2[:SB_BLK], dk2[SB_BLK:]))
                dvs.append(jnp.where(K.lo, dv2[:SB_BLK], dv2[SB_BLK:]))
            dk_acc[pl.ds(T["s0"], SB_TK), :] += jnp.concatenate(dks, axis=0)
            dv_acc[pl.ds(T["s0"], SB_TK), :] += jnp.concatenate(dvs, axis=0)
            return 0.125 * dq

        zc = jnp.zeros((SB_BLK, 1), F32)
        first = [tile(s, blocks[s], zc, zc) for s in range(SB_QB)]

        def sum_step(st):
            new = []
            for s, (jt, _, c0, c1, t0, t1) in enumerate(st):
                T = tile(s, jt, c0, c1)
                new.append((jt - SB_NB, T["mx"], T["c0"], T["c1"], t0 + T["g0"], t1 + T["g1"]))
            return tuple(new)

        st = lax.while_loop(_sb_any_open, sum_step,
                            tuple((blocks[s] - SB_NB, T["mx"], T["c0"], T["c1"], T["g0"], T["g1"]) for s, T in enumerate(first)))
        tot = [x[4:] for x in st]

        def grad_step(st):
            new = []
            for s, (jt, _, dq, c0, c1, n0, n1) in enumerate(st):
                T = tile(s, jt, c0, c1)
                new.append((jt - SB_NB, T["mx"], dq + grads(s, T, tot[s], n0, n1), T["c0"], T["c1"], n0 + T["g0"], n1 + T["g1"]))
            return tuple(new)

        st = lax.while_loop(_sb_any_open, grad_step,
                            tuple((blocks[s] - SB_NB, T["mx"], grads(s, T, tot[s], zc, zc), T["c0"], T["c1"], T["g0"], T["g1"])
                                  for s, T in enumerate(first)))
        for s in range(SB_QB):
            dq_ref[s * SB_BLK:(s + 1) * SB_BLK, :] = st[s][2]

        @pl.when(step_i == n_steps - 1)
        def _():
            cols = pl.ds(pl.multiple_of(p * 128, 128), 128)
            ck = pltpu.make_async_copy(dk_acc, dk_hbm.at[:, cols], sem.at[0])
            cv = pltpu.make_async_copy(dv_acc, dv_hbm.at[:, cols], sem.at[1])
            ck.start()
            cv.start()
            ck.wait()
            cv.wait()

    tq = SB_QB * SB_BLK
    return pl.pallas_call(
        body, name="sb_bwd", grid=(2, n_steps),
        in_specs=[pl.BlockSpec((tq, 128), lambda p, i: (i, QA // 128 + p)),
                  pl.BlockSpec((S, 128), lambda p, i: (0, KA // 128 + p)),
                  pl.BlockSpec((S, 128), lambda p, i: (0, VA // 128 + p)),
                  pl.BlockSpec((tq, 128), lambda p, i: (i, p))] + _SB_CONST_SPECS,
        out_specs=[pl.BlockSpec((tq, 128), lambda p, i: (i, p)), ANY, ANY],
        out_shape=[jax.ShapeDtypeStruct((S, 256), F32)] * 3,
        scratch_shapes=[pltpu.VMEM((S, 128), F32), pltpu.VMEM((S, 128), F32), pltpu.SemaphoreType.DMA((2,))],
        compiler_params=_cp(("arbitrary", "arbitrary"), 48),
    )(pa, pa, pa, dycat, *_sb_const_arrays())


def _bias_tiles(rel_bias_l):
    r = np.arange(CH_TQ)[:, None]
    c = np.arange(CH_TQ)[None, :]
    valid = np.stack([(4 * b + c // 64 >= r // 64) & (4 * b + c // 64 <= r // 64 + 8) for b in range(3)])
    rev = rel_bias_l[:, _DIAG_IDX[:, ::-1]]
    rows = jnp.broadcast_to(jnp.pad(rev, ((0, 0), (0, 0), (0, 1)))[:, :, None, :], (N_HEADS_CH, 3, CH_TQ, 2 * CH_TQ))
    skew = rows.reshape(N_HEADS_CH, 3, -1)[:, :, :CH_TQ * (2 * CH_TQ - 1)].reshape(N_HEADS_CH, 3, CH_TQ, 2 * CH_TQ - 1)
    return jnp.where(valid[None], skew[:, :, :, CH_TQ - 1:], NEG)


_DIAG_IDX = np.stack([np.clip(512 - 256 * b + np.arange(2 * CH_TQ - 1) - (CH_TQ - 1), -128, 128) + 128 for b in range(3)])


def _bias_grad(dbias):
    assert (_DIAG_IDX[0] == REL - 1).all()
    d = jnp.pad(dbias, ((0, 0), (0, 0), (0, 0), (CH_TQ - 1, 0))).reshape(N_HEADS_CH, 2, -1)
    d = jnp.pad(d, ((0, 0), (0, 0), (0, CH_TQ))).reshape(N_HEADS_CH, 2, CH_TQ, 2 * CH_TQ)
    diag = _col_sum(d)[:, :, 0, :2 * CH_TQ - 1][:, :, ::-1]
    onehot = (_DIAG_IDX[1:, :, None] == np.arange(REL)[None, None, :]).astype(np.float32)
    g = jnp.einsum("hbk,bkr->hr", diag, jnp.asarray(onehot), precision=lax.Precision.HIGHEST)
    return g.at[:, REL - 1].add(-jnp.sum(diag, axis=(1, 2)))


def _ch_logits(i, q_ref, kn, b_ref, rows):
    lo = _lo_mask(rows.stop - rows.start)
    q = q_ref[rows, :]
    qh = [jnp.where(lo, q, 0), jnp.where(lo, 0, q)]
    scores = []
    for h in range(2):
        s = [_dg(qh[h], kn[b], NT) + b_ref[h, b, rows, :] for b in range(3)]
        s[0] = jnp.where(i >= 2, s[0], NEG)
        s[1] = jnp.where(i >= 1, s[1], NEG)
        scores.append(s)
    return lo, qh, scores


def _ch_specs(S):
    nt = S // CH_TQ

    def kspec(b, base):
        return pl.BlockSpec((CH_TQ, 128), lambda p, i: (jnp.maximum(i - 2 + b, 0), base // 128 + p))

    specs = [pl.BlockSpec((CH_TQ, 128), lambda p, i: (i, p))]
    specs += [kspec(b, D_CH) for b in range(3)] + [kspec(b, 0) for b in range(3)]
    specs += [pl.BlockSpec((2, 3, CH_TQ, CH_TQ), lambda p, i: (p, 0, 0, 0))]
    return nt, specs


def _ch_fwd(pn, pv, bias):
    S = pn.shape[0]
    nt, specs = _ch_specs(S)

    def body(q_ref, k0, k1, k2, v0, v1, v2, b_ref, o_ref, lse_ref):
        i = pl.program_id(1)
        kn = [k0[...], k1[...], k2[...]]
        vs = [v0[...], v1[...], v2[...]]
        for r0 in range(0, CH_TQ, CH_SUB):
            rows = slice(r0, r0 + CH_SUB)
            lo, _, scores = _ch_logits(i, q_ref, kn, b_ref, rows)
            outs, lse = [], []
            for s in scores:
                m = jnp.max(jnp.maximum(jnp.maximum(s[0], s[1]), s[2]), axis=-1, keepdims=True)
                e = [jnp.exp(sb - m) for sb in s]
                l = jnp.sum((e[0] + e[1]) + e[2], axis=-1, keepdims=True)
                outs.append((_dot(e[0], vs[0]) + _dot(e[1], vs[1]) + _dot(e[2], vs[2])) / l)
                lse.append(m + jnp.log(l))
            o_ref[rows, :] = jnp.where(lo, outs[0], outs[1])
            lse_ref[0, rows, :] = jnp.where(lax.broadcasted_iota(jnp.int32, (CH_SUB, 2), 1) == 0, lse[0], lse[1])

    return pl.pallas_call(
        body, name="ch_fwd", grid=(4, nt), in_specs=specs,
        out_specs=[pl.BlockSpec((CH_TQ, 128), lambda p, i: (i, p)), pl.BlockSpec((1, CH_TQ, 2), lambda p, i: (p, i, 0))],
        out_shape=[jax.ShapeDtypeStruct((S, 512), F32), jax.ShapeDtypeStruct((4, S, 2), F32)],
        compiler_params=_cp(("parallel", "arbitrary"), 40),
    )(pn, pn, pn, pn, pv, pv, pv, bias)


def _ch_bwd(pn, pv, bias, lse, dycat, hosted=None):
    S = pn.shape[0]
    nt, specs = _ch_specs(S)
    specs = specs + [pl.BlockSpec((1, CH_TQ, 2), lambda p, i: (p, i, 0)), pl.BlockSpec((CH_TQ, 128), lambda p, i: (i, 256 // 128 + p))]
    n = len(hosted.srcs) if hosted else 0

    def body(*refs):
        q_ref, k0, k1, k2, v0, v1, v2, b_ref, lse_ref, do_ref = refs[:10]
        dq_ref, db_ref, dk_hbm, dv_hbm = refs[10 + n:14 + n]
        dk_acc, dv_acc, sem = refs[14 + 2 * n:17 + 2 * n]
        p, i = pl.program_id(0), pl.program_id(1)
        if n:
            copies = hosted.make(refs[10:10 + n], refs[14 + n:14 + 2 * n], *refs[17 + 2 * n:])

            @pl.when(jnp.logical_and(p == 0, i == 0))
            def _():
                for cp in copies:
                    cp.start()

        @pl.when(i == 0)
        def _():
            dk_acc[...] = jnp.zeros_like(dk_acc)
            dv_acc[...] = jnp.zeros_like(dv_acc)
            db_ref[...] = jnp.zeros_like(db_ref)

        kn = [k0[...], k1[...], k2[...]]
        vs = [v0[...], v1[...], v2[...]]
        dk_b = [jnp.zeros((CH_TQ, 128), F32) for _ in range(3)]
        dv_b = [jnp.zeros((CH_TQ, 128), F32) for _ in range(3)]
        for r0 in range(0, CH_TQ, CH_SUB):
            rows = slice(r0, r0 + CH_SUB)
            lo, qh, scores = _ch_logits(i, q_ref, kn, b_ref, rows)
            do = do_ref[rows, :]
            dos = [_mx(jnp.where(lo, do, 0.0)), _mx(jnp.where(lo, 0.0, do))]
            dqn = []
            for h in range(2):
                pr = [jnp.exp(sb - lse_ref[0, rows, h:h + 1]) for sb in scores[h]]
                dp = [_dg(dos[h], vs[b], NT) for b in range(3)]
                delta = jnp.sum((pr[0] * dp[0] + pr[1] * dp[1]) + pr[2] * dp[2], axis=-1, keepdims=True)
                ds = [pr[b] * (dp[b] - delta) for b in range(3)]
                acc = jnp.zeros((CH_SUB, 128), F32)
                for b in range(3):
                    if b > 0:
                        db_ref[h, b - 1, rows, :] += ds[b]
                    dsm = _mx(ds[b])
                    acc = acc + _dot(dsm, kn[b])
                    dk_b[b] = dk_b[b] + _dg(dsm, qh[h], TN)
                    dv_b[b] = dv_b[b] + _dg(pr[b], dos[h], TN)
                dqn.append(acc)
            dq_ref[rows, :] = 0.125 * jnp.where(lo, dqn[0], dqn[1])

        for b in range(3):
            @pl.when(i - 2 + b >= 0)
            def _(b=b):
                rows = pl.ds(pl.multiple_of((i - 2 + b) * CH_TQ, CH_TQ), CH_TQ)
                dk_acc[rows, :] += dk_b[b]
                dv_acc[rows, :] += dv_b[b]

        @pl.when(i == nt - 1)
        def _():
            cols = pl.ds(pl.multiple_of(p * 128, 128), 128)
            ck = pltpu.make_async_copy(dk_acc, dk_hbm.at[:, cols], sem.at[0])
            cv = pltpu.make_async_copy(dv_acc, dv_hbm.at[:, cols], sem.at[1])
            ck.start()
            cv.start()
            ck.wait()
            cv.wait()

        if n:
            @pl.when(jnp.logical_and(p == 3, i == nt - 1))
            def _():
                for cp in copies:
                    cp.wait()

    return pl.pallas_call(
        body, name="ch_bwd_exchange" if n else "ch_bwd", grid=(4, nt), in_specs=specs + [ANY] * n,
        out_specs=[pl.BlockSpec((CH_TQ, 128), lambda p, i: (i, p)),
                   pl.BlockSpec((2, 2, CH_TQ, CH_TQ), lambda p, i: (p, 0, 0, 0)), ANY, ANY] + [ANY] * n,
        out_shape=[jax.ShapeDtypeStruct((S, 512), F32), jax.ShapeDtypeStruct((N_HEADS_CH, 2, CH_TQ, CH_TQ), F32),
                   jax.ShapeDtypeStruct((S, 512), F32), jax.ShapeDtypeStruct((S, 512), F32)] + (hosted.out_shape if n else []),
        scratch_shapes=[pltpu.VMEM((S, 128), F32), pltpu.VMEM((S, 128), F32), pltpu.SemaphoreType.DMA((2,))] + (hosted.sems() if n else []),
        compiler_params=_cp(("arbitrary", "arbitrary"), 48),
    )(pn, pn, pn, pn, pv, pv, pv, bias, lse, dycat, *(hosted.srcs if n else []))


def _col_sum(d):
    _, B, R, C = d.shape

    def body(d_ref, o_ref):
        for b in range(B):
            o_ref[0, b] = jnp.sum(d_ref[0, b], axis=0, keepdims=True)

    return pl.pallas_call(
        body, name="col_sum", grid=(N_HEADS_CH,),
        in_specs=[pl.BlockSpec((1, B, R, C), lambda h: (h, 0, 0, 0))],
        out_specs=pl.BlockSpec((1, B, 1, C), lambda h: (h, 0, 0, 0)),
        out_shape=jax.ShapeDtypeStruct((N_HEADS_CH, B, 1, C), F32),
        compiler_params=_cp(("parallel",), 32),
    )(d)


def _conv_fwd(gc, xc, hc_prev, cw_ref):
    hc = gc * xc
    ext = jnp.concatenate([hc_prev, hc], axis=0)
    r1 = pltpu.roll(ext, 1, 0)[8:]
    r2 = pltpu.roll(ext, 2, 0)[8:]
    return cw_ref[0:1, :] * r2 + cw_ref[1:2, :] * r1 + cw_ref[2:3, :] * hc, hc, r1, r2


def _merge_out(x, ysb, ych, pc, cw, onw, w_out):
    S = x.shape[0]
    tm = min(512, S)
    t8 = tm // 8

    def body(x_ref, ysb_ref, ych_ref, gb_ref, gc_ref, xc_ref, gch_ref, xch_ref, cw_ref, onw_ref, w_ref, xo_ref, yn_ref, ycv_ref):
        i = pl.program_id(0)
        lo = _lo_mask(tm)
        hc_prev = jnp.where(i > 0, gch_ref[...] * xch_ref[...], 0.0)
        conv, _, _, _ = _conv_fwd(gc_ref[...], xc_ref[...], hc_prev, cw_ref)
        ycv = gb_ref[...] * conv
        ycv_ref[...] = ycv
        chunks = [ysb_ref[:, 0:128], ysb_ref[:, 128:256]] + [ych_ref[:, 128 * k:128 * (k + 1)] for k in range(4)]
        chunks += [ycv[:, 0:128], ycv[:, 128:256]]
        for k, ch in enumerate(chunks):
            yn_ref[:, 128 * k:128 * (k + 1)] = _half_norm(ch, onw_ref[:, 128 * k:128 * (k + 1)], lo).astype(yn_ref.dtype)
        xo_ref[...] = x_ref[...] + jnp.dot(yn_ref[...], w_ref[...], preferred_element_type=F32)

    row = lambda i: (i, 0)
    halo = lambda cb: pl.BlockSpec((8, 256), lambda i: (jnp.maximum(i * t8 - 1, 0), cb))
    return pl.pallas_call(
        body, name="merge_out", grid=(S // tm,),
        in_specs=[pl.BlockSpec((tm, D), row), pl.BlockSpec((tm, 256), row), pl.BlockSpec((tm, 512), row),
                  pl.BlockSpec((tm, 256), lambda i: (i, 0)), pl.BlockSpec((tm, 256), lambda i: (i, 1)),
                  pl.BlockSpec((tm, 256), lambda i: (i, 2)), halo(1), halo(2),
                  pl.BlockSpec((3, 256), lambda i: (0, 0)), pl.BlockSpec((1, D), lambda i: (0, 0)), pl.BlockSpec((D, D), lambda i: (0, 0))],
        out_specs=[pl.BlockSpec((tm, D), row), pl.BlockSpec((tm, D), row), pl.BlockSpec((tm, 256), row)],
        out_shape=[jax.ShapeDtypeStruct((S, D), F32), jax.ShapeDtypeStruct((S, D), _MXU), jax.ShapeDtypeStruct((S, 256), F32)],
        compiler_params=_cp(("parallel",), 40),
    )(x, ysb, ych, pc, pc, pc, pc, pc, cw, onw, w_out)


def _ffn_fwd(x, nw, wg, wu, wd, nxt=(), split=()):
    S = x.shape[0]
    tm = min(1024, S)
    ni = S // tm
    n = len(nxt)

    def body(*refs):
        x_ref, nw_ref, wg_ref, wu_ref, wd_ref = refs[:5]
        xo_ref, hn_ref, u_ref, t_ref, a_ref = refs[5 + n:10 + n]
        acc = refs[10 + 2 * n]
        i, j = pl.program_id(0), pl.program_id(1)
        if n:
            gather = _Gather(refs[5:5 + n], refs[10 + n:10 + 2 * n], refs[11 + 2 * n:], split)

            @pl.when(jnp.logical_and(i == 0, j == 0))
            def _():
                gather.start()

            @pl.when(jnp.logical_and(i == max(ni - 2, 0), j == 0))
            def _():
                gather.forward()

        @pl.when(j == 0)
        def _():
            hn_ref[...] = _row_norm(x_ref[...], nw_ref[...]).astype(hn_ref.dtype)
            acc[...] = jnp.zeros_like(acc)

        for r in range(0, tm, min(FFN_SUB, tm)):
            rows = slice(r, r + min(FFN_SUB, tm))
            hn = hn_ref[rows, :]
            u = jnp.dot(hn, wg_ref[0], preferred_element_type=F32)
            t = jnp.dot(hn, wu_ref[0], preferred_element_type=F32)
            a = (u * _sigmoid(u) * t).astype(a_ref.dtype)
            u_ref[0, rows, :], t_ref[0, rows, :], a_ref[0, rows, :] = u.astype(u_ref.dtype), t.astype(t_ref.dtype), a
            acc[rows, :] += jnp.dot(a, wd_ref[0], preferred_element_type=F32)

        @pl.when(j == N_SH - 1)
        def _():
            xo_ref[...] = x_ref[...] + acc[...]

        if n:
            @pl.when(jnp.logical_and(i == ni - 1, j == N_SH - 1))
            def _():
                gather.finish()

    row = lambda i, j: (i, 0)
    sh = lambda i, j: (j, i, 0)
    return pl.pallas_call(
        body, name="ffn_fwd_gather" if n else "ffn_fwd", grid=(ni, N_SH),
        in_specs=[pl.BlockSpec((tm, D), row), pl.BlockSpec((1, D), lambda i, j: (0, 0)),
                  pl.BlockSpec((1, D, FF_SH), lambda i, j: (j, 0, 0)), pl.BlockSpec((1, D, FF_SH), lambda i, j: (j, 0, 0)),
                  pl.BlockSpec((1, FF_SH, D), lambda i, j: (j, 0, 0))] + [ANY] * n,
        out_specs=[pl.BlockSpec((tm, D), row), pl.BlockSpec((tm, D), row), pl.BlockSpec((1, tm, FF_SH), sh),
                   pl.BlockSpec((1, tm, FF_SH), sh), pl.BlockSpec((1, tm, FF_SH), sh)] + [ANY] * n,
        out_shape=[jax.ShapeDtypeStruct((S, D), F32), jax.ShapeDtypeStruct((S, D), _MXU),
                   jax.ShapeDtypeStruct((N_SH, S, FF_SH), _MXU), jax.ShapeDtypeStruct((N_SH, S, FF_SH), _MXU),
                   jax.ShapeDtypeStruct((N_SH, S, FF_SH), _MXU)] + _Gather.out_shape(nxt),
        scratch_shapes=[pltpu.VMEM((tm, D), F32)] + (_Gather.sems(n) if n else []),
        compiler_params=_cp(("arbitrary", "arbitrary"), 60),
    )(x, nw, wg, wu, wd, *nxt)


def _loss_head(x, target):
    S = x.shape[0]
    tm = min(512, S)

    def body(x_ref, t_ref, dx_ref, l_ref):
        @pl.when(pl.program_id(0) == 0)
        def _():
            l_ref[...] = jnp.zeros_like(l_ref)

        e = x_ref[...] - t_ref[...]
        dx_ref[...] = e * (1.0 / D)
        l_ref[...] += 0.5 * jnp.sum(jnp.sum(e * e, axis=-1, keepdims=True) * (1.0 / D))

    return pl.pallas_call(
        body, name="loss_head", grid=(S // tm,),
        in_specs=[pl.BlockSpec((tm, D), lambda i: (i, 0))] * 2,
        out_specs=[pl.BlockSpec((tm, D), lambda i: (i, 0)), pl.BlockSpec((8, 128), lambda i: (0, 0))],
        out_shape=[jax.ShapeDtypeStruct((S, D), F32), jax.ShapeDtypeStruct((8, 128), F32)],
        compiler_params=_cp(("arbitrary",), 32),
    )(x, target)


def _ffn_bwd(dx, xmid, nw, wg, wu, wd, u, t, hosted=None):
    S = dx.shape[0]
    tm = min(1024, S)
    ni = S // tm
    n = len(hosted.srcs) if hosted else 0

    def body(*refs):
        dx_ref, x_ref, nw_ref, wg_ref, wu_ref, wd_ref, u_ref, t_ref = refs[:8]
        du_ref, dt_ref, dxm_ref, dnw_ref = refs[8 + n:12 + n]
        acc = refs[12 + 2 * n]
        i, j = pl.program_id(0), pl.program_id(1)
        if n:
            copies = hosted.make(refs[8:8 + n], refs[12 + n:12 + 2 * n], *refs[13 + 2 * n:])

            @pl.when(jnp.logical_and(i == 0, j == 0))
            def _():
                for cp in copies:
                    cp.start()

        @pl.when(j == 0)
        def _():
            acc[...] = jnp.zeros_like(acc)

        @pl.when(jnp.logical_and(i == 0, j == 0))
        def _():
            dnw_ref[...] = jnp.zeros_like(dnw_ref)

        for r in range(0, tm, min(FFN_SUB, tm)):
            rows = slice(r, r + min(FFN_SUB, tm))
            da = _dg(dx_ref[rows, :], wd_ref[0], NT)
            u, t = u_ref[0, rows, :].astype(F32), t_ref[0, rows, :].astype(F32)
            sg = _sigmoid(u)
            du = _mx(da * t * (sg * (1.0 + u * (1.0 - sg))))
            dt = _mx(da * (u * sg))
            du_ref[0, rows, :], dt_ref[0, rows, :] = du, dt
            acc[rows, :] += _dg(du, wg_ref[0], NT) + _dg(dt, wu_ref[0], NT)

        @pl.when(j == N_SH - 1)
        def _():
            dxn, dw = _row_norm_bwd(x_ref[...], nw_ref[...], acc[...])
            dxm_ref[...] = dx_ref[...] + dxn
            dnw_ref[0:1, :] += dw

        if n:
            @pl.when(jnp.logical_and(i == ni - 1, j == N_SH - 1))
            def _():
                for cp in copies:
                    cp.wait()

    row = lambda i, j: (i, 0)
    sh = lambda i, j: (j, i, 0)
    wsp = lambda r, c: pl.BlockSpec((1, r, c), lambda i, j: (j, 0, 0))
    return pl.pallas_call(
        body, name="ffn_bwd_exchange" if n else "ffn_bwd", grid=(ni, N_SH),
        in_specs=[pl.BlockSpec((tm, D), row), pl.BlockSpec((tm, D), row), pl.BlockSpec((1, D), lambda i, j: (0, 0)),
                  wsp(D, FF_SH), wsp(D, FF_SH), wsp(FF_SH, D), pl.BlockSpec((1, tm, FF_SH), sh), pl.BlockSpec((1, tm, FF_SH), sh)] + [ANY] * n,
        out_specs=[pl.BlockSpec((1, tm, FF_SH), sh), pl.BlockSpec((1, tm, FF_SH), sh), pl.BlockSpec((tm, D), row),
                   pl.BlockSpec((8, D), lambda i, j: (0, 0))] + [ANY] * n,
        out_shape=[jax.ShapeDtypeStruct((N_SH, S, FF_SH), _MXU), jax.ShapeDtypeStruct((N_SH, S, FF_SH), _MXU),
                   jax.ShapeDtypeStruct((S, D), F32), jax.ShapeDtypeStruct((8, D), F32)] + (hosted.out_shape if n else []),
        scratch_shapes=[pltpu.VMEM((tm, D), F32)] + (hosted.sems() if n else []),
        compiler_params=_cp(("arbitrary", "arbitrary"), 60),
    )(dx, xmid, nw, wg, wu, wd, u, t, *(hosted.srcs if n else []))


def _wgrad(a, b, a_spec, b_spec, out_block, name):
    S = a.shape[-2]
    tk = min(1024, S)

    def body(a_ref, b_ref, o_ref):
        @pl.when(pl.program_id(0) == 0)
        def _():
            o_ref[...] = jnp.zeros_like(o_ref)

        for r in range(N_SH):
            o_ref[r] += _dg(a_spec[1](a_ref, r), b_spec[1](b_ref, r), TN)

    return pl.pallas_call(
        body, name=name, grid=(S // tk,),
        in_specs=[a_spec[0](tk), b_spec[0](tk)],
        out_specs=pl.BlockSpec((N_SH,) + out_block, lambda k: (0, 0, 0)),
        out_shape=jax.ShapeDtypeStruct((N_SH,) + out_block, F32),
        compiler_params=_cp(("arbitrary",), 56),
    )(a, b)


def _tok(width):
    return (lambda tk: pl.BlockSpec((tk, width), lambda k: (k, 0))), (lambda ref, r: ref[...])


def _tok_cols(width):
    return (lambda tk: pl.BlockSpec((tk, N_SH * width), lambda k: (k, 0))), (lambda ref, r: ref[:, r * width:(r + 1) * width])


def _tok_sh(width):
    return (lambda tk: pl.BlockSpec((N_SH, tk, width), lambda k: (0, k, 0))), (lambda ref, r: ref[r])


def _out_bwd(dxm, w_out, ysb, ych, ycv, onw):
    S = dxm.shape[0]
    tm = min(512, S)

    def body(dx_ref, w_ref, ysb_ref, ych_ref, ycv_ref, onw_ref, dy_ref, dw_ref):
        @pl.when(pl.program_id(0) == 0)
        def _():
            dw_ref[...] = jnp.zeros_like(dw_ref)

        lo = _lo_mask(tm)
        dyn = _dg(dx_ref[...], w_ref[...], NT)
        chunks = [ysb_ref[:, 0:128], ysb_ref[:, 128:256]] + [ych_ref[:, 128 * k:128 * (k + 1)] for k in range(4)]
        chunks += [ycv_ref[:, 0:128], ycv_ref[:, 128:256]]
        for k, ch in enumerate(chunks):
            sl = slice(128 * k, 128 * (k + 1))
            d, dw = _half_norm_bwd(ch, onw_ref[:, sl], dyn[:, sl], lo)
            dy_ref[:, sl] = d
            dw_ref[0:1, sl] += dw

    row = lambda i: (i, 0)
    return pl.pallas_call(
        body, name="out_bwd", grid=(S // tm,),
        in_specs=[pl.BlockSpec((tm, D), row), pl.BlockSpec((D, D), lambda i: (0, 0)), pl.BlockSpec((tm, 256), row),
                  pl.BlockSpec((tm, 512), row), pl.BlockSpec((tm, 256), row), pl.BlockSpec((1, D), lambda i: (0, 0))],
        out_specs=[pl.BlockSpec((tm, D), row), pl.BlockSpec((8, D), lambda i: (0, 0))],
        out_shape=[jax.ShapeDtypeStruct((S, D), F32), jax.ShapeDtypeStruct((8, D), F32)],
        compiler_params=_cp(("arbitrary",), 40),
    )(dxm, w_out, ysb, ych, ycv, onw)


def _in_bwd(dxm, x, anw, w_sh, pqk, pc, dqa, dka, dva, dqh, dkh, dvb, dycat, cw, qw2, kw2):
    S = x.shape[0]
    tm = min(256, S)
    t8 = tm // 8
    last8 = S // 8 - 1

    def body(dxm_ref, x_ref, anw_ref, w_ref, q0_ref, q1_ref, k0_ref, k1_ref, gb_ref, gc_ref, xc_ref, gch_ref, xch_ref, gbn_ref,
             dqa_ref, dka_ref, dva_ref, dqh_ref, dkh_ref, dvb_ref, dyc_ref, dycn_ref, cw_ref, qw_ref, kw_ref,
             dp_ref, dx_ref, danw_ref, dqw_ref, dkw_ref, dcw_ref):
        i = pl.program_id(0)
        n = pl.num_programs(0)

        @pl.when(i == 0)
        def _():
            for r in (danw_ref, dqw_ref, dkw_ref, dcw_ref):
                r[...] = jnp.zeros_like(r)

        lo = _lo_mask(tm)
        dp_ref[:, QA:QA + 256] = dqa_ref[...].astype(dp_ref.dtype)
        dp_ref[:, KA:KA + 256] = dka_ref[...].astype(dp_ref.dtype)
        dp_ref[:, VA:VA + 256] = dva_ref[...].astype(dp_ref.dtype)
        dp_ref[:, VB:VB + 512] = dvb_ref[...].astype(dp_ref.dtype)
        for base, raws, d_ref, nw_ref, dw_ref in ((QB, (q0_ref, q1_ref), dqh_ref, qw_ref, dqw_ref), (KB, (k0_ref, k1_ref), dkh_ref, kw_ref, dkw_ref)):
            for k in range(4):
                raw = raws[k // 2][:, 128 * (k % 2):128 * (k % 2 + 1)]
                d, dw = _half_norm_bwd(raw, nw_ref[...], d_ref[:, 128 * k:128 * (k + 1)], lo)
                dp_ref[:, base + 128 * k:base + 128 * (k + 1)] = d.astype(dp_ref.dtype)
                dw_ref[0:1, :] += dw
        gb, gc, xc = gb_ref[...], gc_ref[...], xc_ref[...]
        hc_prev = jnp.where(i > 0, gch_ref[...] * xch_ref[...], 0.0)
        conv, hc, r1, r2 = _conv_fwd(gc, xc, hc_prev, cw_ref)
        dyc = dyc_ref[...]
        dconv = dyc * gb
        dconv_next = jnp.where(i < n - 1, dycn_ref[...] * gbn_ref[...], 0.0)
        ext = jnp.concatenate([dconv, dconv_next], axis=0)
        l1 = pltpu.roll(ext, tm + 7, 0)[:tm]
        l2 = pltpu.roll(ext, tm + 6, 0)[:tm]
        dhc = cw_ref[2:3, :] * dconv + cw_ref[1:2, :] * l1 + cw_ref[0:1, :] * l2
        dp_ref[:, GB:GB + 256] = (dyc * conv).astype(dp_ref.dtype)
        dp_ref[:, GC:GC + 256] = (dhc * xc).astype(dp_ref.dtype)
        dp_ref[:, XC:XC + 256] = (dhc * gc).astype(dp_ref.dtype)
        dcw_ref[0:1, :] += jnp.sum(dconv * r2, axis=0, keepdims=True)
        dcw_ref[1:2, :] += jnp.sum(dconv * r1, axis=0, keepdims=True)
        dcw_ref[2:3, :] += jnp.sum(dconv * hc, axis=0, keepdims=True)
        dh = jnp.zeros((tm, D), F32)
        for r in range(N_SH):
            dh = dh + lax.dot_general(dp_ref[:, IN_SH * r:IN_SH * (r + 1)], w_ref[r], NT, preferred_element_type=F32)
        dxn, dw = _row_norm_bwd(x_ref[...], anw_ref[...], dh)
        dx_ref[...] = dxm_ref[...] + dxn
        danw_ref[0:1, :] += dw

    row = lambda i: (i, 0)
    colb = lambda w, cb: pl.BlockSpec((tm, w), lambda i: (i, cb))
    prev = lambda cb: pl.BlockSpec((8, 256), lambda i: (jnp.maximum(i * t8 - 1, 0), cb))
    nxt = lambda cb: pl.BlockSpec((8, 256), lambda i: (jnp.minimum((i + 1) * t8, last8), cb))
    const = lambda shp: pl.BlockSpec(shp, lambda i: (0,) * len(shp))
    return pl.pallas_call(
        body, name="in_bwd", grid=(S // tm,),
        in_specs=[pl.BlockSpec((tm, D), row), pl.BlockSpec((tm, D), row), const((1, D)), const((N_SH, D, IN_SH)),
                  colb(256, 0), colb(256, 1), colb(256, 2), colb(256, 3),
                  colb(256, 0), colb(256, 1), colb(256, 2), prev(1), prev(2), nxt(0),
                  pl.BlockSpec((tm, 256), row), pl.BlockSpec((tm, 256), row), pl.BlockSpec((tm, 256), row),
                  pl.BlockSpec((tm, 512), row), pl.BlockSpec((tm, 512), row), pl.BlockSpec((tm, 512), row),
                  colb(256, 768 // 256), nxt(768 // 256), const((3, 256)), const((1, 128)), const((1, 128))],
        out_specs=[pl.BlockSpec((tm, D_IN), row), pl.BlockSpec((tm, D), row), const((8, D)), const((8, 128)), const((8, 128)), const((8, 256))],
        out_shape=[jax.ShapeDtypeStruct((S, D_IN), _MXU), jax.ShapeDtypeStruct((S, D), F32), jax.ShapeDtypeStruct((8, D), F32),
                   jax.ShapeDtypeStruct((8, 128), F32), jax.ShapeDtypeStruct((8, 128), F32), jax.ShapeDtypeStruct((8, 256), F32)],
        compiler_params=_cp(("arbitrary",), 56),
    )(dxm, x, anw, w_sh, pqk, pqk, pqk, pqk, pc, pc, pc, pc, pc, pc, dqa, dka, dva, dqh, dkh, dvb, dycat, dycat, cw, qw2, kw2)


def _rows_tile(h):
    return h if h <= 512 else 256


def _add_halves(g, r1, c_arr):
    _, R, C = g.shape
    h = R // 2
    tr = _rows_tile(h)
    nb = h // tr

    def body(c_ref, g_ref, r_ref, o_ref, ob_ref):
        s = g_ref[...] + r_ref[...]
        o_ref[...] = s
        ob_ref[...] = s.astype(ob_ref.dtype)

    blk = (1, tr, C)
    return pl.pallas_call(
        body, name="add_halves",
        grid_spec=pltpu.PrefetchScalarGridSpec(
            num_scalar_prefetch=1, grid=(N_SH, nb),
            in_specs=[pl.BlockSpec(blk, lambda s, i, c: (s, c[0] * nb + i, 0)), pl.BlockSpec(blk, lambda s, i, c: (s, i, 0))],
            out_specs=[pl.BlockSpec(blk, lambda s, i, c: (s, i, 0))] * 2),
        out_shape=[jax.ShapeDtypeStruct((N_SH, h, C), F32), jax.ShapeDtypeStruct((N_SH, h, C), jnp.bfloat16)],
        compiler_params=_cp(("parallel", "parallel"), 32),
    )(c_arr, g, r1)


def _add_shards(s1, r2, me_arr):
    _, h, C = s1.shape
    tr = _rows_tile(h)

    def body(me_ref, s_ref, r_ref, o_ref):
        o_ref[...] = ((s_ref[0] + r_ref[0].astype(F32)) + r_ref[1].astype(F32)) + r_ref[2].astype(F32)

    return pl.pallas_call(
        body, name="add_shards",
        grid_spec=pltpu.PrefetchScalarGridSpec(
            num_scalar_prefetch=1, grid=(h // tr,),
            in_specs=[pl.BlockSpec((1, tr, C), lambda i, me: (me[0], i, 0)), pl.BlockSpec((3, tr, C), lambda i, me: (0, i, 0))],
            out_specs=pl.BlockSpec((tr, C), lambda i, me: (i, 0))),
        out_shape=jax.ShapeDtypeStruct((h, C), F32),
        compiler_params=_cp(("parallel",), 32),
    )(me_arr, s1, r2)


def _adam_math(w, g, m, v):
    m = ADAM_B1 * m + (1.0 - ADAM_B1) * g
    v = ADAM_B2 * v + (1.0 - ADAM_B2) * (g * g)
    m_hat = m / (1.0 - ADAM_B1 ** ADAM_STEP)
    v_hat = v / (1.0 - ADAM_B2 ** ADAM_STEP)
    return -ADAM_LR * (m_hat / (jnp.sqrt(v_hat) + ADAM_EPS) + ADAM_WD * w), m, v


def _adam_big(w, m, v, mine, theirs, c_arr):
    L, R, C = w.shape
    h = R // 2
    tr = _rows_tile(h)
    nb = h // tr

    def body(c_ref, w_ref, m_ref, v_ref, a_ref, b_ref, g_ref, d_ref, mo_ref, vo_ref):
        g = jnp.where(pl.program_id(1) == c_ref[0], a_ref[...], b_ref[...])
        g_ref[...] = g
        d_ref[...], mo_ref[...], vo_ref[...] = _adam_math(w_ref[...], g, m_ref[...], v_ref[...])

    full = pl.BlockSpec((1, tr, C), lambda l, s, i, c: (l, s * nb + i, 0))
    half = pl.BlockSpec((1, tr, C), lambda l, s, i, c: (l, i, 0))
    return pl.pallas_call(
        body, name="adam_big",
        grid_spec=pltpu.PrefetchScalarGridSpec(num_scalar_prefetch=1, grid=(L, 2, nb), in_specs=[full, full, full, half, half],
                                               out_specs=[full] * 4),
        out_shape=[jax.ShapeDtypeStruct((L, R, C), F32)] * 4,
        compiler_params=_cp(("parallel", "parallel", "parallel"), 32),
    )(c_arr, w, m, v, mine, theirs)


def _adam_small(w, g, m, v):
    def body(w_ref, g_ref, m_ref, v_ref, d_ref, mo_ref, vo_ref):
        d_ref[...], mo_ref[...], vo_ref[...] = _adam_math(w_ref[...], g_ref[...], m_ref[...], v_ref[...])

    return pl.pallas_call(body, name="adam_small", out_shape=[jax.ShapeDtypeStruct(w.shape, F32)] * 3)(w, g, m, v)


def _allreduce_small(buf):
    R = buf.shape[0]

    def body(x_ref, o_ref, gat, ssem, rsem):
        x, y, c = lax.axis_index("x"), lax.axis_index("y"), lax.axis_index("c")
        me = 4 * x + 2 * y + c
        gat[me] = x_ref[...]
        cps = []
        for k in range(1, 8):
            peer = (x ^ (k >> 2), y ^ ((k >> 1) & 1), c ^ (k & 1))
            cps.append(pltpu.make_async_remote_copy(src_ref=x_ref, dst_ref=gat.at[me], send_sem=ssem.at[k - 1], recv_sem=rsem.at[k - 1],
                                                    device_id=peer, device_id_type=MESH))
        for cp in cps:
            cp.start()
        for cp in cps:
            cp.wait()
        acc = gat[0]
        for s in range(1, 8):
            acc = acc + gat[s]
        o_ref[...] = acc

    return pl.pallas_call(
        body, name="allreduce_small", out_shape=jax.ShapeDtypeStruct((R, 128), F32),
        in_specs=[pl.BlockSpec(memory_space=pltpu.VMEM)], out_specs=pl.BlockSpec(memory_space=pltpu.VMEM),
        scratch_shapes=[pltpu.VMEM((8, R, 128), F32), pltpu.SemaphoreType.DMA((7,)), pltpu.SemaphoreType.DMA((7,))],
    )(buf)


class _Exchange:
    def __init__(self, srcs, out_shape, n_copies, make):
        self.srcs, self.out_shape, self.n_copies, self.make = list(srcs), list(out_shape), n_copies, make

    def sems(self):
        return [pltpu.SemaphoreType.DMA((self.n_copies,))] * 2

    def run(self, name):
        n = len(self.srcs)

        def body(*refs):
            copies = self.make(refs[:n], refs[n:2 * n], *refs[2 * n:])
            for cp in copies:
                cp.start()
            for cp in copies:
                cp.wait()

        return pl.pallas_call(body, name=name, out_shape=self.out_shape, in_specs=[ANY] * n, out_specs=[ANY] * n,
                              scratch_shapes=self.sems())(*self.srcs)


def _to_sibling(srcs, pick, shapes):
    def make(ins, outs, ssem, rsem):
        x, y, c = lax.axis_index("x"), lax.axis_index("y"), lax.axis_index("c")
        return [pltpu.make_async_remote_copy(src_ref=pick(ins[f], f, c), dst_ref=outs[f], send_sem=ssem.at[f], recv_sem=rsem.at[f],
                                             device_id=(x, y, 1 - c), device_id_type=MESH) for f in range(len(ins))]

    return _Exchange(srcs, [jax.ShapeDtypeStruct(shp, F32) for shp in shapes], len(srcs), make)


def _to_chips(srcs):
    def make(ins, outs, ssem, rsem):
        x, y, c = lax.axis_index("x"), lax.axis_index("y"), lax.axis_index("c")
        return [pltpu.make_async_remote_copy(src_ref=ins[f].at[2 * px + py], dst_ref=outs[f].at[k], send_sem=ssem.at[3 * f + k],
                                             recv_sem=rsem.at[3 * f + k], device_id=(px, py, c), device_id_type=MESH)
                for f in range(len(ins)) for k, (px, py) in enumerate([(1 - x, y), (x, 1 - y), (1 - x, 1 - y)])]

    return _Exchange(srcs, [jax.ShapeDtypeStruct((3,) + s.shape[1:], s.dtype) for s in srcs], 3 * len(srcs), make)


class _GradReducer:
    def __init__(self):
        x, y, c = lax.axis_index("x"), lax.axis_index("y"), lax.axis_index("c")
        self.c_arr = jnp.reshape(c, (1,)).astype(jnp.int32)
        self.me_arr = jnp.reshape(2 * x + y, (1,)).astype(jnp.int32)
        self.fresh, self.pending, self.done = None, None, {}

    def start(self, l, grads):
        self.fresh = (l, grads)

    def halves(self):
        if self.fresh is None:
            return None
        grads = self.fresh[1]
        hs = [g.shape[1] // 2 for g in grads]
        return _to_sibling(grads, lambda ref, f, c: ref.at[:, pl.ds(pl.multiple_of((1 - c) * hs[f], 8), hs[f]), :],
                           [(N_SH, h, g.shape[2]) for g, h in zip(grads, hs)])

    def got_halves(self, r1):
        l, grads = self.fresh
        self.fresh = None
        s1, wire = zip(*[_add_halves(g, r, self.c_arr) for g, r in zip(grads, r1)])
        self.pending = (l, s1, wire)

    def shards(self):
        return None if self.pending is None else _to_chips(self.pending[2])

    def got_shards(self, r2):
        l, s1, _ = self.pending
        self.pending = None
        mine = [_add_shards(s, r, self.me_arr) for s, r in zip(s1, r2)]
        theirs = _to_sibling(mine, lambda ref, f, c: ref, [m.shape for m in mine]).run("rs_sibling_reduced")
        self.done[l] = (mine, theirs)

    def finish(self):
        if self.pending is not None:
            self.got_shards(self.shards().run("rs_chip_shards"))
        if self.fresh is not None:
            self.got_halves(self.halves().run("rs_sibling_halves"))
            self.got_shards(self.shards().run("rs_chip_shards"))


def _pad_rows(flat, rows):
    return jnp.pad(flat, (0, rows * 128 - flat.shape[0])).reshape(rows, 128)


class _ShardedWeights:
    split = [True] * 5 + [False]

    def __init__(self, w):
        self.parts = [[w[k][l].astype(_MXU) for k in _BIG] + [jnp.zeros((8, 128), F32).at[:3, :HD].set(w["conv_w"][l])]
                      for l in range(DEPTH)]

    def first(self):
        return self.unpack(_allgather_layer(self.parts[0], self.split))

    def following(self, l):
        return (self.parts[l + 1], self.split) if l + 1 < DEPTH else ((), ())

    @staticmethod
    def unpack(got):
        W = dict(zip(_BIG, got[:5]))
        W["conv_w"] = jnp.concatenate([got[5][r, :3, :HD] for r in range(N_SH)], axis=-1)
        return W


def _local_step(x, target, small, weights, reducer):
    S = x.shape[0]
    saved, big = [], []
    W = weights.first()
    for l in range(DEPTH):
        big.append(W)
        anw, onw, fnw = small["attn_norm_w"][l][None], small["out_norm_w"][l][None], small["ffn_norm_w"][l][None]
        qw2 = jnp.tile(small["q_norm_w"][l], 2)[None]
        kw2 = jnp.tile(small["k_norm_w"][l], 2)[None]
        cw = W["conv_w"]
        bias = _bias_tiles(small["rel_bias"][l])
        h, pa, pqk, pn, pv, pc = _attn_in(x, anw, W["w_in"], qw2, kw2)
        ysb = _sb_fwd(pa)
        ych, lse = _ch_fwd(pn, pv, bias)
        w_out = W["w_out"].reshape(D, D)
        xmid, yn, ycv = _merge_out(x, ysb, ych, pc, cw, onw, w_out)
        outs = _ffn_fwd(xmid, fnw, W["w_gate"], W["w_up"], W["w_down"], *weights.following(l))
        xo, hn, u, t, a = outs[:5]
        saved.append(dict(x=x, h=h, pa=pa, pqk=pqk, pn=pn, pv=pv, pc=pc, ysb=ysb, ych=ych, lse=lse, ycv=ycv, yn=yn, xmid=xmid, hn=hn, u=u, t=t, a=a,
                          anw=anw, onw=onw, fnw=fnw, qw2=qw2, kw2=kw2, cw=cw, bias=bias, w_out=w_out))
        x = xo
        if l + 1 < DEPTH:
            W = weights.unpack(outs[5:])
    dx, lpart = _loss_head(x, target)
    loss = lpart[0, 0]
    grads = [None] * DEPTH
    for l in reversed(range(DEPTH)):
        W, sv = big[l], saved[l]
        outs = _ffn_bwd(dx, sv["xmid"], sv["fnw"], W["w_gate"], W["w_up"], W["w_down"], sv["u"], sv["t"], reducer.halves())
        du, dt, dxm, dfnw = outs[:4]
        if outs[4:]:
            reducer.got_halves(outs[4:])
        g_gate = _wgrad(sv["hn"], du, _tok(D), _tok_sh(FF_SH), (D, FF_SH), "wgrad_gate")
        g_up = _wgrad(sv["hn"], dt, _tok(D), _tok_sh(FF_SH), (D, FF_SH), "wgrad_up")
        g_down = _wgrad(sv["a"], dx, _tok_sh(FF_SH), _tok(D), (FF_SH, D), "wgrad_down")
        dycat, donw = _out_bwd(dxm, sv["w_out"], sv["ysb"], sv["ych"], sv["ycv"], sv["onw"])
        g_out = _wgrad(sv["yn"], dxm, _tok_cols(OUT_SH), _tok(D), (OUT_SH, D), "wgrad_out")
        dqa, dka, dva = _sb_bwd(sv["pa"], dycat)
        outs = _ch_bwd(sv["pn"], sv["pv"], sv["bias"], sv["lse"], dycat, reducer.shards())
        dqh, dbias, dkh, dvb = outs[:4]
        if outs[4:]:
            reducer.got_shards(outs[4:])
        drelb = _bias_grad(dbias)
        dproj, dx, danw, dqw, dkw, dcw = _in_bwd(dxm, sv["x"], sv["anw"], W["w_in"], sv["pqk"], sv["pc"], dqa, dka, dva, dqh, dkh, dvb, dycat,
                                                 sv["cw"], sv["qw2"], sv["kw2"])
        g_in = _wgrad(sv["h"], dproj, _tok(D), _tok_cols(IN_SH), (D, IN_SH), "wgrad_in")
        grads[l] = dict(w_in=g_in, w_out=g_out, w_gate=g_gate, w_up=g_up, w_down=g_down,
                        attn_norm_w=danw[0], out_norm_w=donw[0], ffn_norm_w=dfnw[0],
                        q_norm_w=dqw[0, :HD] + dqw[0, HD:], k_norm_w=dkw[0, :HD] + dkw[0, HD:], rel_bias=drelb, conv_w=dcw[:3])
        reducer.start(l, [grads[l][k] for k in _BIG])
    reducer.finish()
    return loss, dx, grads


_BIG = ("w_in", "w_out", "w_gate", "w_up", "w_down")
_SMALL = ("attn_norm_w", "q_norm_w", "k_norm_w", "rel_bias", "out_norm_w", "ffn_norm_w")
_ORDER = ("attn_norm_w", "w_in", "q_norm_w", "k_norm_w", "rel_bias", "conv_w", "out_norm_w", "w_out", "ffn_norm_w", "w_gate", "w_up", "w_down")


def kernel(x, attn_norm_w, w_in, q_norm_w, k_norm_w, rel_bias, conv_w, out_norm_w, w_out, ffn_norm_w, w_gate, w_up, w_down, loss_target, m_attn_norm_w, m_w_in, m_q_norm_w, m_k_norm_w, m_rel_bias, m_conv_w, m_out_norm_w, m_w_out, m_ffn_norm_w, m_w_gate, m_w_up, m_w_down, v_attn_norm_w, v_w_in, v_q_norm_w, v_k_norm_w, v_rel_bias, v_conv_w, v_out_norm_w, v_w_out, v_ffn_norm_w, v_w_gate, v_w_up, v_w_down):
    w = dict(attn_norm_w=attn_norm_w, w_in=w_in, q_norm_w=q_norm_w, k_norm_w=k_norm_w, rel_bias=rel_bias, conv_w=conv_w,
             out_norm_w=out_norm_w, w_out=w_out, ffn_norm_w=ffn_norm_w, w_gate=w_gate, w_up=w_up, w_down=w_down)
    m = dict(attn_norm_w=m_attn_norm_w, w_in=m_w_in, q_norm_w=m_q_norm_w, k_norm_w=m_k_norm_w, rel_bias=m_rel_bias, conv_w=m_conv_w,
             out_norm_w=m_out_norm_w, w_out=m_w_out, ffn_norm_w=m_ffn_norm_w, w_gate=m_w_gate, w_up=m_w_up, w_down=m_w_down)
    v = dict(attn_norm_w=v_attn_norm_w, w_in=v_w_in, q_norm_w=v_q_norm_w, k_norm_w=v_k_norm_w, rel_bias=v_rel_bias, conv_w=v_conv_w,
             out_norm_w=v_out_norm_w, w_out=v_w_out, ffn_norm_w=v_ffn_norm_w, w_gate=v_w_gate, w_up=v_w_up, w_down=v_w_down)
    xi, yi = lax.axis_index("x"), lax.axis_index("y")
    me = 2 * xi + yi

    small = {k: w[k] for k in _SMALL}
    reducer = _GradReducer()
    loss, grad_x, grads = _local_step(x[0], loss_target[0], small, _ShardedWeights(w), reducer)
    loss = lax.psum(loss, ("x", "y", "c"))

    out = {}
    for f, k in enumerate(_BIG):
        mine, theirs = (jnp.stack([reducer.done[l][side][f] for l in range(DEPTH)]) for side in (0, 1))
        out[k] = _adam_big(w[k], m[k], v[k], mine, theirs, reducer.c_arr)

    names = _SMALL + ("conv_w",)
    gflat = jnp.concatenate([jnp.stack([grads[l][k] for l in range(DEPTH)]).reshape(-1) for k in names])
    rows = -(-gflat.shape[0] // 1024) * 8
    gsum = _allreduce_small(_pad_rows(gflat, rows)).reshape(-1)
    gs, off = {}, 0
    for k in names:
        shp = (DEPTH, 3, 4 * HD) if k == "conv_w" else w[k].shape
        size = int(np.prod(shp))
        gs[k] = gsum[off:off + size].reshape(shp)
        off += size
    gs["conv_w"] = lax.dynamic_slice_in_dim(gs["conv_w"], me * HD, HD, axis=2)
    flat = lambda d: jnp.concatenate([d[k].reshape(-1) for k in names])
    rows = -(-flat(w).shape[0] // 1024) * 8
    dsm, msm, vsm = _adam_small(_pad_rows(flat(w), rows), _pad_rows(flat(gs), rows), _pad_rows(flat(m), rows), _pad_rows(flat(v), rows))
    off = 0
    for k in names:
        size = int(np.prod(w[k].shape))
        out[k] = (gs[k],) + tuple(a.reshape(-1)[off:off + size].reshape(w[k].shape) for a in (dsm, msm, vsm))
        off += size

    return (loss, grad_x[None]) + tuple(out[k][j] for j in range(4) for k in _ORDER)
```

```python
import functools

import numpy as np
import jax
import jax.numpy as jnp
from jax import lax
from jax.experimental import pallas as pl
from jax.experimental.pallas import tpu as pltpu

F32 = jnp.float32
_MXU = jnp.bfloat16

D = 1024
DEPTH = 4
HD = 64
D_IN = 3072
N_SH = 4
IN_SH = D_IN // N_SH
FF_SH = 704
OUT_SH = 256
N_HEADS_CH = 8
D_CH = N_HEADS_CH * HD
REL = 257
QA, KA, VA, QB, KB, VB, GB, GC, XC = 0, 256, 512, 768, 1280, 1792, 2304, 2560, 2816
EPS = 1e-6
NEG = -1e30
FFN_SUB = 256
SB_BLK = 128
CH_TQ = 256
CH_SUB = 256
EXP_ZERO = -104.0

ADAM_LR, ADAM_B1, ADAM_B2, ADAM_EPS, ADAM_WD, ADAM_STEP = 0.001, 0.9, 0.999, 1e-08, 0.01, 10

NT = (((1,), (1,)), ((), ()))
TN = (((0,), (0,)), ((), ()))
MESH = pl.DeviceIdType.MESH
ANY = pl.BlockSpec(memory_space=pl.ANY)
MB = 1 << 20


def _cp(sem=None, vmem_mb=None):
    return pltpu.CompilerParams(dimension_semantics=sem, vmem_limit_bytes=None if vmem_mb is None else vmem_mb * MB)


def _mx(a):
    return a.astype(_MXU)


def _dot(a, b):
    return jnp.dot(_mx(a), _mx(b), preferred_element_type=F32)


def _dg(a, b, dims):
    return lax.dot_general(_mx(a), _mx(b), dims, preferred_element_type=F32)


def _sigmoid(u):
    return 1.0 / (1.0 + jnp.exp(-u))


def _row_norm(x, w):
    r = lax.rsqrt(jnp.mean(x * x, axis=-1, keepdims=True) + EPS)
    return x * r * w


def _row_norm_bwd(x, w, dy):
    r = lax.rsqrt(jnp.mean(x * x, axis=-1, keepdims=True) + EPS)
    xh = x * r
    dw = jnp.sum(dy * xh, axis=0, keepdims=True)
    dxh = dy * w
    return r * (dxh - xh * jnp.mean(dxh * xh, axis=-1, keepdims=True)), dw


def _half_rs(x, lo):
    sq = x * x
    s0 = jnp.sum(jnp.where(lo, sq, 0.0), axis=-1, keepdims=True)
    s1 = jnp.sum(jnp.where(lo, 0.0, sq), axis=-1, keepdims=True)
    return jnp.where(lo, lax.rsqrt(s0 * (1.0 / HD) + EPS), lax.rsqrt(s1 * (1.0 / HD) + EPS))


def _half_norm(x, w, lo):
    return x * _half_rs(x, lo) * w


def _half_norm_bwd(x, w, dy, lo):
    r = _half_rs(x, lo)
    xh = x * r
    dw = jnp.sum(dy * xh, axis=0, keepdims=True)
    dxh = dy * w
    pr = dxh * xh
    m0 = jnp.sum(jnp.where(lo, pr, 0.0), axis=-1, keepdims=True)
    m1 = jnp.sum(jnp.where(lo, 0.0, pr), axis=-1, keepdims=True)
    mean = jnp.where(lo, m0, m1) * (1.0 / HD)
    return r * (dxh - xh * mean), dw


def _lo_mask(t):
    return lax.broadcasted_iota(jnp.int32, (t, 128), 1) < HD


class _Gather:
    def __init__(self, ins, outs, sems, split):
        self.n = n = len(ins)
        lsem, ssem, rsem, fsem, gsem = sems
        x, y, c = lax.axis_index("x"), lax.axis_index("y"), lax.axis_index("c")
        me = 2 * x + y
        chips = [(1 - x, y), (x, 1 - y), (1 - x, 1 - y)]

        def half(f):
            h = ins[f].shape[0] // 2
            return pl.ds(pl.multiple_of(c * h, 16), h)

        self.local = [pltpu.make_async_copy(ins[f], outs[f].at[me], lsem.at[f]) for f in range(n)]
        self.sends, self.passed = [], []
        for f in range(n):
            for k, (px, py) in enumerate(chips):
                src, dst = (ins[f].at[half(f)], outs[f].at[me, half(f)]) if split[f] else (ins[f], outs[f].at[me])
                self.sends.append(pltpu.make_async_remote_copy(src_ref=src, dst_ref=dst, send_sem=ssem.at[3 * f + k], recv_sem=rsem.at[3 * f + k],
                                                               device_id=(px, py, c), device_id_type=MESH))
                if split[f]:
                    got = outs[f].at[2 * px + py, half(f)]
                    self.passed.append(pltpu.make_async_remote_copy(src_ref=got, dst_ref=got, send_sem=fsem.at[3 * f + k], recv_sem=gsem.at[3 * f + k],
                                                                    device_id=(x, y, 1 - c), device_id_type=MESH))
                else:
                    self.passed.append(None)

    def start(self):
        for cp in self.local + self.sends:
            cp.start()

    def forward(self):
        for cp, fw in zip(self.sends, self.passed):
            cp.wait_recv()
            if fw is not None:
                fw.start()

    def finish(self):
        passed = [fw for fw in self.passed if fw is not None]
        for fw in passed:
            fw.wait_recv()
        for cp in self.sends + passed:
            cp.wait_send()
        for cp in self.local:
            cp.wait()

    @staticmethod
    def sems(n):
        return [pltpu.SemaphoreType.DMA((n,))] + [pltpu.SemaphoreType.DMA((3 * n,))] * 4

    @staticmethod
    def out_shape(parts):
        return [jax.ShapeDtypeStruct((N_SH,) + p.shape, p.dtype) for p in parts]


def _allgather_layer(parts, split):
    n = len(parts)

    def body(*refs):
        g = _Gather(refs[:n], refs[n:2 * n], refs[2 * n:], split)
        g.start()
        g.forward()
        g.finish()

    return pl.pallas_call(
        body, name="allgather_layer", out_shape=_Gather.out_shape(parts),
        in_specs=[ANY] * n, out_specs=[ANY] * n, scratch_shapes=_Gather.sems(n),
    )(*parts)


def _attn_in(x, nw, w_sh, qw2, kw2):
    S = x.shape[0]
    tm = min(512, S)
    assert (QB, VB, GB) == (IN_SH, 2 * IN_SH + 256, 3 * IN_SH)

    def body(x_ref, nw_ref, w_ref, qw_ref, kw_ref, h_ref, pa_ref, pqk_ref, pn_ref, pv_ref, pc_ref):
        h = _row_norm(x_ref[...], nw_ref[...]).astype(h_ref.dtype)
        h_ref[...] = h
        pa_ref[...] = jnp.dot(h, w_ref[0], preferred_element_type=F32).astype(pa_ref.dtype)
        pqk_ref[:, :IN_SH] = jnp.dot(h, w_ref[1], preferred_element_type=F32)
        p2 = jnp.dot(h, w_ref[2], preferred_element_type=F32)
        pqk_ref[:, IN_SH:] = p2[:, :256]
        pv_ref[...] = p2[:, 256:].astype(pv_ref.dtype)
        pc_ref[...] = jnp.dot(h, w_ref[3], preferred_element_type=F32)
        lo = _lo_mask(tm)
        for k in range(2 * D_CH // 128):
            cols = slice(128 * k, 128 * (k + 1))
            is_q = k < D_CH // 128
            n = _half_norm(pqk_ref[:, cols], (qw_ref if is_q else kw_ref)[...], lo)
            pn_ref[:, cols] = (n * 0.125 if is_q else n).astype(pn_ref.dtype)

    row = lambda i: (i, 0)
    const = lambda shp: pl.BlockSpec(shp, lambda i: (0,) * len(shp))
    return pl.pallas_call(
        body, name="attn_in", grid=(S // tm,),
        in_specs=[pl.BlockSpec((tm, D), row), const((1, D)), const((N_SH, D, IN_SH)), const((1, 128)), const((1, 128))],
        out_specs=[pl.BlockSpec((tm, D), row), pl.BlockSpec((tm, IN_SH), row), pl.BlockSpec((tm, 2 * D_CH), row),
                   pl.BlockSpec((tm, 2 * D_CH), row), pl.BlockSpec((tm, D_CH), row), pl.BlockSpec((tm, IN_SH), row)],
        out_shape=[jax.ShapeDtypeStruct((S, D), _MXU), jax.ShapeDtypeStruct((S, IN_SH), _MXU), jax.ShapeDtypeStruct((S, 2 * D_CH), F32),
                   jax.ShapeDtypeStruct((S, 2 * D_CH), _MXU), jax.ShapeDtypeStruct((S, D_CH), _MXU), jax.ShapeDtypeStruct((S, IN_SH), F32)],
        compiler_params=_cp(("parallel",), 48),
    )(x, nw, w_sh, qw2, kw2)


SB_NB = 3
SB_TK = SB_NB * SB_BLK
SB_QB = 4


def _sb_any_open(states):
    return functools.reduce(jnp.logical_or, [jnp.logical_and(st[0] >= 0, st[1] > EXP_ZERO) for st in states])


def _sb_const_arrays():
    r, c = np.meshgrid(np.arange(SB_TK), np.arange(256), indexing="ij")
    kofs = (r - (r & 127)) + (c & 127)
    r2, c2 = np.meshgrid(np.arange(256), np.arange(256), indexing="ij")
    same = (r2 >= 128) == (c2 >= 128)
    nearer = same & ((r2 & 127) > (c2 & 127))
    at_or_nearer = same & ((r2 & 127) >= (c2 & 127))
    return (jnp.asarray(kofs, jnp.int32), jnp.asarray(kofs - (r & 127), jnp.int32), jnp.asarray(nearer, _MXU), jnp.asarray(at_or_nearer, _MXU))


_SB_CONST_SPECS = [pl.BlockSpec((SB_TK, 256), lambda p, i: (0, 0))] * 2 + [pl.BlockSpec((256, 256), lambda p, i: (0, 0))] * 2


class _SbConsts:
    def __init__(self, kofs_ref, rel_ref, nearer_ref, at_or_nearer_ref):
        self.kofs, self.rel, self.nearer, self.at_or_nearer = kofs_ref, rel_ref, nearer_ref, at_or_nearer_ref
        self.lo = lax.broadcasted_iota(jnp.int32, (SB_BLK, 128), 1) < HD
        self.head0 = lax.broadcasted_iota(jnp.int32, (SB_BLK, 256), 1) < 128


def _two_heads(x, lo):
    return jnp.concatenate([jnp.where(lo, x, 0), jnp.where(lo, 0, x)], axis=0)


def _per_block(vals, head0):
    return jnp.concatenate([jnp.where(head0, v0, v1) for v0, v1 in vals], axis=0)


def _head_sums(x):
    return jnp.sum(x[:, :128], axis=-1, keepdims=True), jnp.sum(x[:, 128:], axis=-1, keepdims=True)


def _sb_tile(q2, k_ref, i, jt, c0, c1, K):
    s0 = pl.multiple_of(jnp.maximum(jt - (SB_NB - 1), 0) * SB_BLK, SB_BLK)
    kt = k_ref[pl.ds(s0, SB_TK), :]
    k2 = [_two_heads(kt[b * SB_BLK:(b + 1) * SB_BLK], K.lo) for b in range(SB_NB)]
    keep = jnp.logical_and(K.rel[...] < i * SB_BLK - s0, K.kofs[...] < (jt + 1) * SB_BLK - s0)
    z = jnp.where(keep, jnp.concatenate([_dg(q2, k2[b], NT) for b in range(SB_NB)], axis=0), NEG)
    lr = -(jnp.maximum(z, 0.0) + jnp.log(1.0 + jnp.exp(-jnp.abs(z))))
    carries = [None] * SB_NB
    for b in reversed(range(SB_NB)):
        carries[b] = (c0, c1)
        s = _head_sums(lr[b * SB_BLK:(b + 1) * SB_BLK])
        c0, c1 = c0 + s[0], c1 + s[1]
    after = _dot(lr, K.nearer[...]) + _per_block(carries, K.head0)
    w = jnp.exp(z + lr + after)
    return s0, k2, z, lr, w, c0, c1


def _sb_fwd(pa):
    S = pa.shape[0]
    nq = S // SB_BLK
    assert nq % SB_QB == 0

    def body(q_ref, k_ref, v_ref, kofs_ref, rel_ref, nearer_ref, aon_ref, o_ref):
        K = _SbConsts(kofs_ref, rel_ref, nearer_ref, aon_ref)
        blocks = [pl.program_id(1) * SB_QB + s for s in range(SB_QB)]
        q2 = [q_ref[s * SB_BLK:(s + 1) * SB_BLK, :] * 0.125 for s in range(SB_QB)]

        def step(st):
            new = []
            for s, (jt, _, acc, c0, c1) in enumerate(st):
                s0, _, _, _, w, c0, c1 = _sb_tile(q2[s], k_ref, blocks[s], jt, c0, c1, K)
                vt = v_ref[pl.ds(s0, SB_TK), :]
                for b in range(SB_NB):
                    rows = slice(b * SB_BLK, (b + 1) * SB_BLK)
                    acc = acc + _dot(w[rows], _two_heads(vt[rows], K.lo))
                new.append((jt - SB_NB, jnp.max(jnp.maximum(c0, c1)), acc, c0, c1))
            return tuple(new)

        zc = jnp.zeros((SB_BLK, 1), F32)
        st = lax.while_loop(_sb_any_open, step, tuple((i, jnp.float32(0.0), jnp.zeros((SB_BLK, 128), F32), zc, zc) for i in blocks))
        for s in range(SB_QB):
            o_ref[s * SB_BLK:(s + 1) * SB_BLK, :] = st[s][2]

    tq = SB_QB * SB_BLK
    return pl.pallas_call(
        body, name="sb_fwd", grid=(2, nq // SB_QB),
        in_specs=[pl.BlockSpec((tq, 128), lambda p, i: (i, QA // 128 + p)),
                  pl.BlockSpec((S, 128), lambda p, i: (0, KA // 128 + p)),
                  pl.BlockSpec((S, 128), lambda p, i: (0, VA // 128 + p))] + _SB_CONST_SPECS,
        out_specs=pl.BlockSpec((tq, 128), lambda p, i: (i, p)),
        out_shape=jax.ShapeDtypeStruct((S, 256), F32),
        compiler_params=_cp(("parallel", "arbitrary"), 40),
    )(pa, pa, pa, *_sb_const_arrays())


def _sb_bwd(pa, dycat):
    S = pa.shape[0]
    nq = S // SB_BLK
    assert nq % SB_QB == 0
    n_steps = nq // SB_QB

    def body(q_ref, k_ref, v_ref, do_ref, kofs_ref, rel_ref, nearer_ref, aon_ref, dq_ref, dk_hbm, dv_hbm, dk_acc, dv_acc, sem):
        p, step_i = pl.program_id(0), pl.program_id(1)
        K = _SbConsts(kofs_ref, rel_ref, nearer_ref, aon_ref)

        @pl.when(step_i == 0)
        def _():
            dk_acc[...] = jnp.zeros_like(dk_acc)
            dv_acc[...] = jnp.zeros_like(dv_acc)

        blocks = [step_i * SB_QB + s for s in range(SB_QB)]
        q2 = [q_ref[s * SB_BLK:(s + 1) * SB_BLK, :] * 0.125 for s in range(SB_QB)]
        do = [_mx(do_ref[s * SB_BLK:(s + 1) * SB_BLK, :]) for s in range(SB_QB)]

        def tile(s, jt, c0, c1):
            s0, k2, z, lr, w, c0, c1 = _sb_tile(q2[s], k_ref, blocks[s], jt, c0, c1, K)
            vt = v_ref[pl.ds(s0, SB_TK), :]
            g = jnp.concatenate([_dg(do[s], _two_heads(vt[b * SB_BLK:(b + 1) * SB_BLK], K.lo), NT) for b in range(SB_NB)], axis=0) * w
            gs = [_head_sums(g[b * SB_BLK:(b + 1) * SB_BLK]) for b in range(SB_NB)]
            return dict(s0=s0, k2=k2, z=z, lr=lr, w=w, g=g, gs=gs, c0=c0, c1=c1, mx=jnp.max(jnp.maximum(c0, c1)),
                        g0=sum(x[0] for x in gs), g1=sum(x[1] for x in gs))

        def grads(s, T, tot, n0, n1):
            nearer = [None] * SB_NB
            for b in reversed(range(SB_NB)):
                nearer[b] = (n0, n1)
                n0, n1 = n0 + T["gs"][b][0], n1 + T["gs"][b][1]
            farther = _per_block([tot] * SB_NB, K.head0) - _per_block(nearer, K.head0) - _dot(T["g"], K.at_or_nearer[...])
            beta = jnp.exp(T["z"] + T["lr"])
            dz = _mx(T["g"] * (1.0 - beta) - farther * beta)
            wm = _mx(T["w"])
            dq = jnp.zeros((SB_BLK, 128), F32)
            dks, dvs = [], []
            for b in range(SB_NB):
                rows = slice(b * SB_BLK, (b + 1) * SB_BLK)
                dq = dq + _dot(dz[rows], T["k2"][b])
                dk2 = _dg(dz[rows], q2[s], TN)
                dv2 = _dg(wm[rows], do[s], TN)
                dks.append(jnp.where(K.lo, dk2[:SB_BLK], dk2[SB_BLK:]))
                dvs.append(jnp.where(K.lo, dv2[:SB_BLK], dv2[SB_BLK:]))
            dk_acc[pl.ds(T["s0"], SB_TK), :] += jnp.concatenate(dks, axis=0)
            dv_acc[pl.ds(T["s0"], SB_TK), :] += jnp.concatenate(dvs, axis=0)
            return 0.125 * dq

        zc = jnp.zeros((SB_BLK, 1), F32)
        first = [tile(s, blocks[s], zc, zc) for s in range(SB_QB)]

        def sum_step(st):
            new = []
            for s, (jt, _, c0, c1, t0, t1) in enumerate(st):
                T = tile(s, jt, c0, c1)
                new.append((jt - SB_NB, T["mx"], T["c0"], T["c1"], t0 + T["g0"], t1 + T["g1"]))
            return tuple(new)

        st = lax.while_loop(_sb_any_open, sum_step,
                            tuple((blocks[s] - SB_NB, T["mx"], T["c0"], T["c1"], T["g0"], T["g1"]) for s, T in enumerate(first)))
        tot = [x[4:] for x in st]

        def grad_step(st):
            new = []
            for s, (jt, _, dq, c0, c1, n0, n1) in enumerate(st):
                T = tile(s, jt, c0, c1)
                new.append((jt - SB_NB, T["mx"], dq + grads(s, T, tot[s], n0, n1), T["c0"], T["c1"], n0 + T["g0"], n1 + T["g1"]))
            return tuple(new)

        st = lax.while_loop(_sb_any_open, grad_step,
                            tuple((blocks[s] - SB_NB, T["mx"], grads(s, T, tot[s], zc, zc), T["c0"], T["c1"], T["g0"], T["g1"])
                                  for s, T in enumerate(first)))
        for s in range(SB_QB):
            dq_ref[s * SB_BLK:(s + 1) * SB_BLK, :] = st[s][2]

        @pl.when(step_i == n_steps - 1)
        def _():
            cols = pl.ds(pl.multiple_of(p * 128, 128), 128)
            ck = pltpu.make_async_copy(dk_acc, dk_hbm.at[:, cols], sem.at[0])
            cv = pltpu.make_async_copy(dv_acc, dv_hbm.at[:, cols], sem.at[1])
            ck.start()
            cv.start()
            ck.wait()
            cv.wait()

    tq = SB_QB * SB_BLK
    return pl.pallas_call(
        body, name="sb_bwd", grid=(2, n_steps),
        in_specs=[pl.BlockSpec((tq, 128), lambda p, i: (i, QA // 128 + p)),
                  pl.BlockSpec((S, 128), lambda p, i: (0, KA // 128 + p)),
                  pl.BlockSpec((S, 128), lambda p, i: (0, VA // 128 + p)),
                  pl.BlockSpec((tq, 128), lambda p, i: (i, p))] + _SB_CONST_SPECS,
        out_specs=[pl.BlockSpec((tq, 128), lambda p, i: (i, p)), ANY, ANY],
        out_shape=[jax.ShapeDtypeStruct((S, 256), F32)] * 3,
        scratch_shapes=[pltpu.VMEM((S, 128), F32), pltpu.VMEM((S, 128), F32), pltpu.SemaphoreType.DMA((2,))],
        compiler_params=_cp(("arbitrary", "arbitrary"), 48),
    )(pa, pa, pa, dycat, *_sb_const_arrays())


def _bias_tiles(rel_bias_l):
    r = np.arange(CH_TQ)[:, None]
    c = np.arange(CH_TQ)[None, :]
    valid = np.stack([(4 * b + c // 64 >= r // 64) & (4 * b + c // 64 <= r // 64 + 8) for b in range(3)])
    rev = rel_bias_l[:, _DIAG_IDX[:, ::-1]]
    rows = jnp.broadcast_to(jnp.pad(rev, ((0, 0), (0, 0), (0, 1)))[:, :, None, :], (N_HEADS_CH, 3, CH_TQ, 2 * CH_TQ))
    skew = rows.reshape(N_HEADS_CH, 3, -1)[:, :, :CH_TQ * (2 * CH_TQ - 1)].reshape(N_HEADS_CH, 3, CH_TQ, 2 * CH_TQ - 1)
    return jnp.where(valid[None], skew[:, :, :, CH_TQ - 1:], NEG)


_DIAG_IDX = np.stack([np.clip(512 - 256 * b + np.arange(2 * CH_TQ - 1) - (CH_TQ - 1), -128, 128) + 128 for b in range(3)])


def _bias_grad(dbias):
    assert (_DIAG_IDX[0] == REL - 1).all()
    d = jnp.pad(dbias, ((0, 0), (0, 0), (0, 0), (CH_TQ - 1, 0))).reshape(N_HEADS_CH, 2, -1)
    d = jnp.pad(d, ((0, 0), (0, 0), (0, CH_TQ))).reshape(N_HEADS_CH, 2, CH_TQ, 2 * CH_TQ)
    diag = _col_sum(d)[:, :, 0, :2 * CH_TQ - 1][:, :, ::-1]
    onehot = (_DIAG_IDX[1:, :, None] == np.arange(REL)[None, None, :]).astype(np.float32)
    g = jnp.einsum("hbk,bkr->hr", diag, jnp.asarray(onehot), precision=lax.Precision.HIGHEST)
    return g.at[:, REL - 1].add(-jnp.sum(diag, axis=(1, 2)))


def _ch_logits(i, q_ref, kn, b_ref, rows):
    lo = _lo_mask(rows.stop - rows.start)
    q = q_ref[rows, :]
    qh = [jnp.where(lo, q, 0), jnp.where(lo, 0, q)]
    scores = []
    for h in range(2):
        s = [_dg(qh[h], kn[b], NT) + b_ref[h, b, rows, :] for b in range(3)]
        s[0] = jnp.where(i >= 2, s[0], NEG)
        s[1] = jnp.where(i >= 1, s[1], NEG)
        scores.append(s)
    return lo, qh, scores


def _ch_specs(S):
    nt = S // CH_TQ

    def kspec(b, base):
        return pl.BlockSpec((CH_TQ, 128), lambda p, i: (jnp.maximum(i - 2 + b, 0), base // 128 + p))

    specs = [pl.BlockSpec((CH_TQ, 128), lambda p, i: (i, p))]
    specs += [kspec(b, D_CH) for b in range(3)] + [kspec(b, 0) for b in range(3)]
    specs += [pl.BlockSpec((2, 3, CH_TQ, CH_TQ), lambda p, i: (p, 0, 0, 0))]
    return nt, specs


def _ch_fwd(pn, pv, bias, nxt=(), split=()):
    S = pn.shape[0]
    nt, specs = _ch_specs(S)
    n = len(nxt)

    def body(*refs):
        q_ref, k0, k1, k2, v0, v1, v2, b_ref = refs[:8]
        o_ref, lse_ref = refs[8 + n:10 + n]
        p, i = pl.program_id(0), pl.program_id(1)
        if n:
            gather = _Gather(refs[8:8 + n], refs[10 + n:10 + 2 * n], refs[10 + 2 * n:], split)

            @pl.when(jnp.logical_and(p == 0, i == 0))
            def _():
                gather.start()

            @pl.when(jnp.logical_and(p == 2, i == 0))
            def _():
                gather.forward()

        kn = [k0[...], k1[...], k2[...]]
        vs = [v0[...], v1[...], v2[...]]
        for r0 in range(0, CH_TQ, CH_SUB):
            rows = slice(r0, r0 + CH_SUB)
            lo, _, scores = _ch_logits(i, q_ref, kn, b_ref, rows)
            outs, lse = [], []
            for s in scores:
                m = jnp.max(jnp.maximum(jnp.maximum(s[0], s[1]), s[2]), axis=-1, keepdims=True)
                e = [jnp.exp(sb - m) for sb in s]
                l = jnp.sum((e[0] + e[1]) + e[2], axis=-1, keepdims=True)
                outs.append((_dot(e[0], vs[0]) + _dot(e[1], vs[1]) + _dot(e[2], vs[2])) / l)
                lse.append(m + jnp.log(l))
            o_ref[rows, :] = jnp.where(lo, outs[0], outs[1])
            lse_ref[0, rows, :] = jnp.where(lax.broadcasted_iota(jnp.int32, (CH_SUB, 2), 1) == 0, lse[0], lse[1])

        if n:
            @pl.when(jnp.logical_and(p == 3, i == nt - 1))
            def _():
                gather.finish()

    return pl.pallas_call(
        body, name="ch_fwd_gather" if n else "ch_fwd", grid=(4, nt), in_specs=specs + [ANY] * n,
        out_specs=[pl.BlockSpec((CH_TQ, 128), lambda p, i: (i, p)), pl.BlockSpec((1, CH_TQ, 2), lambda p, i: (p, i, 0))] + [ANY] * n,
        out_shape=[jax.ShapeDtypeStruct((S, 512), F32), jax.ShapeDtypeStruct((4, S, 2), F32)] + _Gather.out_shape(nxt),
        scratch_shapes=_Gather.sems(n) if n else [],
        compiler_params=_cp(("arbitrary", "arbitrary") if n else ("parallel", "arbitrary"), 40),
    )(pn, pn, pn, pn, pv, pv, pv, bias, *nxt)


def _ch_bwd(pn, pv, bias, lse, dycat, hosted=None):
    S = pn.shape[0]
    nt, specs = _ch_specs(S)
    specs = specs + [pl.BlockSpec((1, CH_TQ, 2), lambda p, i: (p, i, 0)), pl.BlockSpec((CH_TQ, 128), lambda p, i: (i, 256 // 128 + p))]
    n = len(hosted.srcs) if hosted else 0

    def body(*refs):
        q_ref, k0, k1, k2, v0, v1, v2, b_ref, lse_ref, do_ref = refs[:10]
        dq_ref, db_ref, dk_hbm, dv_hbm = refs[10 + n:14 + n]
        dk_acc, dv_acc, sem = refs[14 + 2 * n:17 + 2 * n]
        p, i = pl.program_id(0), pl.program_id(1)
        if n:
            copies = hosted.make(refs[10:10 + n], refs[14 + n:14 + 2 * n], *refs[17 + 2 * n:])

            @pl.when(jnp.logical_and(p == 0, i == 0))
            def _():
                for cp in copies:
                    cp.start()

        @pl.when(i == 0)
        def _():
            dk_acc[...] = jnp.zeros_like(dk_acc)
            dv_acc[...] = jnp.zeros_like(dv_acc)
            db_ref[...] = jnp.zeros_like(db_ref)

        kn = [k0[...], k1[...], k2[...]]
        vs = [v0[...], v1[...], v2[...]]
        dk_b = [jnp.zeros((CH_TQ, 128), F32) for _ in range(3)]
        dv_b = [jnp.zeros((CH_TQ, 128), F32) for _ in range(3)]
        for r0 in range(0, CH_TQ, CH_SUB):
            rows = slice(r0, r0 + CH_SUB)
            lo, qh, scores = _ch_logits(i, q_ref, kn, b_ref, rows)
            do = do_ref[rows, :]
            dos = [_mx(jnp.where(lo, do, 0.0)), _mx(jnp.where(lo, 0.0, do))]
            dqn = []
            for h in range(2):
                pr = [jnp.exp(sb - lse_ref[0, rows, h:h + 1]) for sb in scores[h]]
                dp = [_dg(dos[h], vs[b], NT) for b in range(3)]
                delta = jnp.sum((pr[0] * dp[0] + pr[1] * dp[1]) + pr[2] * dp[2], axis=-1, keepdims=True)
                ds = [pr[b] * (dp[b] - delta) for b in range(3)]
                acc = jnp.zeros((CH_SUB, 128), F32)
                for b in range(3):
                    if b > 0:
                        db_ref[h, b - 1, rows, :] += ds[b]
                    dsm = _mx(ds[b])
                    acc = acc + _dot(dsm, kn[b])
                    dk_b[b] = dk_b[b] + _dg(dsm, qh[h], TN)
                    dv_b[b] = dv_b[b] + _dg(pr[b], dos[h], TN)
                dqn.append(acc)
            dq_ref[rows, :] = 0.125 * jnp.where(lo, dqn[0], dqn[1])

        for b in range(3):
            @pl.when(i - 2 + b >= 0)
            def _(b=b):
                rows = pl.ds(pl.multiple_of((i - 2 + b) * CH_TQ, CH_TQ), CH_TQ)
                dk_acc[rows, :] += dk_b[b]
                dv_acc[rows, :] += dv_b[b]

        @pl.when(i == nt - 1)
        def _():
            cols = pl.ds(pl.multiple_of(p * 128, 128), 128)
            ck = pltpu.make_async_copy(dk_acc, dk_hbm.at[:, cols], sem.at[0])
            cv = pltpu.make_async_copy(dv_acc, dv_hbm.at[:, cols], sem.at[1])
            ck.start()
            cv.start()
            ck.wait()
            cv.wait()

        if n:
            @pl.when(jnp.logical_and(p == 3, i == nt - 1))
            def _():
                for cp in copies:
                    cp.wait()

    return pl.pallas_call(
        body, name="ch_bwd_exchange" if n else "ch_bwd", grid=(4, nt), in_specs=specs + [ANY] * n,
        out_specs=[pl.BlockSpec((CH_TQ, 128), lambda p, i: (i, p)),
                   pl.BlockSpec((2, 2, CH_TQ, CH_TQ), lambda p, i: (p, 0, 0, 0)), ANY, ANY] + [ANY] * n,
        out_shape=[jax.ShapeDtypeStruct((S, 512), F32), jax.ShapeDtypeStruct((N_HEADS_CH, 2, CH_TQ, CH_TQ), F32),
                   jax.ShapeDtypeStruct((S, 512), F32), jax.ShapeDtypeStruct((S, 512), F32)] + (hosted.out_shape if n else []),
        scratch_shapes=[pltpu.VMEM((S, 128), F32), pltpu.VMEM((S, 128), F32), pltpu.SemaphoreType.DMA((2,))] + (hosted.sems() if n else []),
        compiler_params=_cp(("arbitrary", "arbitrary"), 48),
    )(pn, pn, pn, pn, pv, pv, pv, bias, lse, dycat, *(hosted.srcs if n else []))


def _col_sum(d):
    _, B, R, C = d.shape

    def body(d_ref, o_ref):
        for b in range(B):
            o_ref[0, b] = jnp.sum(d_ref[0, b], axis=0, keepdims=True)

    return pl.pallas_call(
        body, name="col_sum", grid=(N_HEADS_CH,),
        in_specs=[pl.BlockSpec((1, B, R, C), lambda h: (h, 0, 0, 0))],
        out_specs=pl.BlockSpec((1, B, 1, C), lambda h: (h, 0, 0, 0)),
        out_shape=jax.ShapeDtypeStruct((N_HEADS_CH, B, 1, C), F32),
        compiler_params=_cp(("parallel",), 32),
    )(d)


def _conv_fwd(gc, xc, hc_prev, cw_ref):
    hc = gc * xc
    ext = jnp.concatenate([hc_prev, hc], axis=0)
    r1 = pltpu.roll(ext, 1, 0)[8:]
    r2 = pltpu.roll(ext, 2, 0)[8:]
    return cw_ref[0:1, :] * r2 + cw_ref[1:2, :] * r1 + cw_ref[2:3, :] * hc, hc, r1, r2


def _merge_out(x, ysb, ych, pc, cw, onw, w_out):
    S = x.shape[0]
    tm = min(512, S)
    t8 = tm // 8

    def body(x_ref, ysb_ref, ych_ref, gb_ref, gc_ref, xc_ref, gch_ref, xch_ref, cw_ref, onw_ref, w_ref, xo_ref, yn_ref, ycv_ref):
        i = pl.program_id(0)
        lo = _lo_mask(tm)
        hc_prev = jnp.where(i > 0, gch_ref[...] * xch_ref[...], 0.0)
        conv, _, _, _ = _conv_fwd(gc_ref[...], xc_ref[...], hc_prev, cw_ref)
        ycv = gb_ref[...] * conv
        ycv_ref[...] = ycv
        chunks = [ysb_ref[:, 0:128], ysb_ref[:, 128:256]] + [ych_ref[:, 128 * k:128 * (k + 1)] for k in range(4)]
        chunks += [ycv[:, 0:128], ycv[:, 128:256]]
        for k, ch in enumerate(chunks):
            yn_ref[:, 128 * k:128 * (k + 1)] = _half_norm(ch, onw_ref[:, 128 * k:128 * (k + 1)], lo).astype(yn_ref.dtype)
        xo_ref[...] = x_ref[...] + jnp.dot(yn_ref[...], w_ref[...], preferred_element_type=F32)

    row = lambda i: (i, 0)
    halo = lambda cb: pl.BlockSpec((8, 256), lambda i: (jnp.maximum(i * t8 - 1, 0), cb))
    return pl.pallas_call(
        body, name="merge_out", grid=(S // tm,),
        in_specs=[pl.BlockSpec((tm, D), row), pl.BlockSpec((tm, 256), row), pl.BlockSpec((tm, 512), row),
                  pl.BlockSpec((tm, 256), lambda i: (i, 0)), pl.BlockSpec((tm, 256), lambda i: (i, 1)),
                  pl.BlockSpec((tm, 256), lambda i: (i, 2)), halo(1), halo(2),
                  pl.BlockSpec((3, 256), lambda i: (0, 0)), pl.BlockSpec((1, D), lambda i: (0, 0)), pl.BlockSpec((D, D), lambda i: (0, 0))],
        out_specs=[pl.BlockSpec((tm, D), row), pl.BlockSpec((tm, D), row), pl.BlockSpec((tm, 256), row)],
        out_shape=[jax.ShapeDtypeStruct((S, D), F32), jax.ShapeDtypeStruct((S, D), _MXU), jax.ShapeDtypeStruct((S, 256), F32)],
        compiler_params=_cp(("parallel",), 40),
    )(x, ysb, ych, pc, pc, pc, pc, pc, cw, onw, w_out)


def _ffn_fwd(x, nw, wg, wu, wd, nxt=(), split=()):
    S = x.shape[0]
    tm = min(1024, S)
    ni = S // tm
    n = len(nxt)

    def body(*refs):
        x_ref, nw_ref, wg_ref, wu_ref, wd_ref = refs[:5]
        xo_ref, hn_ref, u_ref, t_ref, a_ref = refs[5 + n:10 + n]
        acc = refs[10 + 2 * n]
        i, j = pl.program_id(0), pl.program_id(1)
        if n:
            gather = _Gather(refs[5:5 + n], refs[10 + n:10 + 2 * n], refs[11 + 2 * n:], split)

            @pl.when(jnp.logical_and(i == 0, j == 0))
            def _():
                gather.start()

            @pl.when(jnp.logical_and(i == max(ni - 2, 0), j == 0))
            def _():
                gather.forward()

        @pl.when(j == 0)
        def _():
            hn_ref[...] = _row_norm(x_ref[...], nw_ref[...]).astype(hn_ref.dtype)
            acc[...] = jnp.zeros_like(acc)

        for r in range(0, tm, min(FFN_SUB, tm)):
            rows = slice(r, r + min(FFN_SUB, tm))
            hn = hn_ref[rows, :]
            u = jnp.dot(hn, wg_ref[0], preferred_element_type=F32)
            t = jnp.dot(hn, wu_ref[0], preferred_element_type=F32)
            a = (u * _sigmoid(u) * t).astype(a_ref.dtype)
            u_ref[0, rows, :], t_ref[0, rows, :], a_ref[0, rows, :] = u.astype(u_ref.dtype), t.astype(t_ref.dtype), a
            acc[rows, :] += jnp.dot(a, wd_ref[0], preferred_element_type=F32)

        @pl.when(j == N_SH - 1)
        def _():
            xo_ref[...] = x_ref[...] + acc[...]

        if n:
            @pl.when(jnp.logical_and(i == ni - 1, j == N_SH - 1))
            def _():
                gather.finish()

    row = lambda i, j: (i, 0)
    sh = lambda i, j: (j, i, 0)
    return pl.pallas_call(
        body, name="ffn_fwd_gather" if n else "ffn_fwd", grid=(ni, N_SH),
        in_specs=[pl.BlockSpec((tm, D), row), pl.BlockSpec((1, D), lambda i, j: (0, 0)),
                  pl.BlockSpec((1, D, FF_SH), lambda i, j: (j, 0, 0)), pl.BlockSpec((1, D, FF_SH), lambda i, j: (j, 0, 0)),
                  pl.BlockSpec((1, FF_SH, D), lambda i, j: (j, 0, 0))] + [ANY] * n,
        out_specs=[pl.BlockSpec((tm, D), row), pl.BlockSpec((tm, D), row), pl.BlockSpec((1, tm, FF_SH), sh),
                   pl.BlockSpec((1, tm, FF_SH), sh), pl.BlockSpec((1, tm, FF_SH), sh)] + [ANY] * n,
        out_shape=[jax.ShapeDtypeStruct((S, D), F32), jax.ShapeDtypeStruct((S, D), _MXU),
                   jax.ShapeDtypeStruct((N_SH, S, FF_SH), _MXU), jax.ShapeDtypeStruct((N_SH, S, FF_SH), _MXU),
                   jax.ShapeDtypeStruct((N_SH, S, FF_SH), _MXU)] + _Gather.out_shape(nxt),
        scratch_shapes=[pltpu.VMEM((tm, D), F32)] + (_Gather.sems(n) if n else []),
        compiler_params=_cp(("arbitrary", "arbitrary"), 60),
    )(x, nw, wg, wu, wd, *nxt)


def _loss_head(x, target):
    S = x.shape[0]
    tm = min(512, S)

    def body(x_ref, t_ref, dx_ref, l_ref):
        @pl.when(pl.program_id(0) == 0)
        def _():
            l_ref[...] = jnp.zeros_like(l_ref)

        e = x_ref[...] - t_ref[...]
        dx_ref[...] = e * (1.0 / D)
        l_ref[...] += 0.5 * jnp.sum(jnp.sum(e * e, axis=-1, keepdims=True) * (1.0 / D))

    return pl.pallas_call(
        body, name="loss_head", grid=(S // tm,),
        in_specs=[pl.BlockSpec((tm, D), lambda i: (i, 0))] * 2,
        out_specs=[pl.BlockSpec((tm, D), lambda i: (i, 0)), pl.BlockSpec((8, 128), lambda i: (0, 0))],
        out_shape=[jax.ShapeDtypeStruct((S, D), F32), jax.ShapeDtypeStruct((8, 128), F32)],
        compiler_params=_cp(("arbitrary",), 32),
    )(x, target)


def _ffn_bwd(dx, xmid, nw, wg, wu, wd, u, t, hosted=None):
    S = dx.shape[0]
    tm = min(1024, S)
    ni = S // tm
    n = len(hosted.srcs) if hosted else 0

    def body(*refs):
        dx_ref, x_ref, nw_ref, wg_ref, wu_ref, wd_ref, u_ref, t_ref = refs[:8]
        du_ref, dt_ref, dxm_ref, dnw_ref = refs[8 + n:12 + n]
        acc = refs[12 + 2 * n]
        i, j = pl.program_id(0), pl.program_id(1)
        if n:
            copies = hosted.make(refs[8:8 + n], refs[12 + n:12 + 2 * n], *refs[13 + 2 * n:])

            @pl.when(jnp.logical_and(i == 0, j == 0))
            def _():
                for cp in copies:
                    cp.start()

        @pl.when(j == 0)
        def _():
            acc[...] = jnp.zeros_like(acc)

        @pl.when(jnp.logical_and(i == 0, j == 0))
        def _():
            dnw_ref[...] = jnp.zeros_like(dnw_ref)

        for r in range(0, tm, min(FFN_SUB, tm)):
            rows = slice(r, r + min(FFN_SUB, tm))
            da = _dg(dx_ref[rows, :], wd_ref[0], NT)
            u, t = u_ref[0, rows, :].astype(F32), t_ref[0, rows, :].astype(F32)
            sg = _sigmoid(u)
            du = _mx(da * t * (sg * (1.0 + u * (1.0 - sg))))
            dt = _mx(da * (u * sg))
            du_ref[0, rows, :], dt_ref[0, rows, :] = du, dt
            acc[rows, :] += _dg(du, wg_ref[0], NT) + _dg(dt, wu_ref[0], NT)

        @pl.when(j == N_SH - 1)
        def _():
            dxn, dw = _row_norm_bwd(x_ref[...], nw_ref[...], acc[...])
            dxm_ref[...] = dx_ref[...] + dxn
            dnw_ref[0:1, :] += dw

        if n:
            @pl.when(jnp.logical_and(i == ni - 1, j == N_SH - 1))
            def _():
                for cp in copies:
                    cp.wait()

    row = lambda i, j: (i, 0)
    sh = lambda i, j: (j, i, 0)
    wsp = lambda r, c: pl.BlockSpec((1, r, c), lambda i, j: (j, 0, 0))
    return pl.pallas_call(
        body, name="ffn_bwd_exchange" if n else "ffn_bwd", grid=(ni, N_SH),
        in_specs=[pl.BlockSpec((tm, D), row), pl.BlockSpec((tm, D), row), pl.BlockSpec((1, D), lambda i, j: (0, 0)),
                  wsp(D, FF_SH), wsp(D, FF_SH), wsp(FF_SH, D), pl.BlockSpec((1, tm, FF_SH), sh), pl.BlockSpec((1, tm, FF_SH), sh)] + [ANY] * n,
        out_specs=[pl.BlockSpec((1, tm, FF_SH), sh), pl.BlockSpec((1, tm, FF_SH), sh), pl.BlockSpec((tm, D), row),
                   pl.BlockSpec((8, D), lambda i, j: (0, 0))] + [ANY] * n,
        out_shape=[jax.ShapeDtypeStruct((N_SH, S, FF_SH), _MXU), jax.ShapeDtypeStruct((N_SH, S, FF_SH), _MXU),
                   jax.ShapeDtypeStruct((S, D), F32), jax.ShapeDtypeStruct((8, D), F32)] + (hosted.out_shape if n else []),
        scratch_shapes=[pltpu.VMEM((tm, D), F32)] + (hosted.sems() if n else []),
        compiler_params=_cp(("arbitrary", "arbitrary"), 60),
    )(dx, xmid, nw, wg, wu, wd, u, t, *(hosted.srcs if n else []))


def _wgrad(a, b, a_spec, b_spec, out_block, name):
    S = a.shape[-2]
    tk = min(1024, S)

    def body(a_ref, b_ref, o_ref):
        @pl.when(pl.program_id(0) == 0)
        def _():
            o_ref[...] = jnp.zeros_like(o_ref)

        for r in range(N_SH):
            o_ref[r] += _dg(a_spec[1](a_ref, r), b_spec[1](b_ref, r), TN)

    return pl.pallas_call(
        body, name=name, grid=(S // tk,),
        in_specs=[a_spec[0](tk), b_spec[0](tk)],
        out_specs=pl.BlockSpec((N_SH,) + out_block, lambda k: (0, 0, 0)),
        out_shape=jax.ShapeDtypeStruct((N_SH,) + out_block, F32),
        compiler_params=_cp(("arbitrary",), 56),
    )(a, b)


def _tok(width):
    return (lambda tk: pl.BlockSpec((tk, width), lambda k: (k, 0))), (lambda ref, r: ref[...])


def _tok_cols(width):
    return (lambda tk: pl.BlockSpec((tk, N_SH * width), lambda k: (k, 0))), (lambda ref, r: ref[:, r * width:(r + 1) * width])


def _tok_sh(width):
    return (lambda tk: pl.BlockSpec((N_SH, tk, width), lambda k: (0, k, 0))), (lambda ref, r: ref[r])


def _out_bwd(dxm, w_out, ysb, ych, ycv, onw):
    S = dxm.shape[0]
    tm = min(512, S)

    def body(dx_ref, w_ref, ysb_ref, ych_ref, ycv_ref, onw_ref, dy_ref, dw_ref):
        @pl.when(pl.program_id(0) == 0)
        def _():
            dw_ref[...] = jnp.zeros_like(dw_ref)

        lo = _lo_mask(tm)
        dyn = _dg(dx_ref[...], w_ref[...], NT)
        chunks = [ysb_ref[:, 0:128], ysb_ref[:, 128:256]] + [ych_ref[:, 128 * k:128 * (k + 1)] for k in range(4)]
        chunks += [ycv_ref[:, 0:128], ycv_ref[:, 128:256]]
        for k, ch in enumerate(chunks):
            sl = slice(128 * k, 128 * (k + 1))
            d, dw = _half_norm_bwd(ch, onw_ref[:, sl], dyn[:, sl], lo)
            dy_ref[:, sl] = d
            dw_ref[0:1, sl] += dw

    row = lambda i: (i, 0)
    return pl.pallas_call(
        body, name="out_bwd", grid=(S // tm,),
        in_specs=[pl.BlockSpec((tm, D), row), pl.BlockSpec((D, D), lambda i: (0, 0)), pl.BlockSpec((tm, 256), row),
                  pl.BlockSpec((tm, 512), row), pl.BlockSpec((tm, 256), row), pl.BlockSpec((1, D), lambda i: (0, 0))],
        out_specs=[pl.BlockSpec((tm, D), row), pl.BlockSpec((8, D), lambda i: (0, 0))],
        out_shape=[jax.ShapeDtypeStruct((S, D), F32), jax.ShapeDtypeStruct((8, D), F32)],
        compiler_params=_cp(("arbitrary",), 40),
    )(dxm, w_out, ysb, ych, ycv, onw)


def _in_bwd(dxm, x, anw, w_sh, pqk, pc, dqa, dka, dva, dqh, dkh, dvb, dycat, cw, qw2, kw2):
    S = x.shape[0]
    tm = min(256, S)
    t8 = tm // 8
    last8 = S // 8 - 1

    def body(dxm_ref, x_ref, anw_ref, w_ref, q0_ref, q1_ref, k0_ref, k1_ref, gb_ref, gc_ref, xc_ref, gch_ref, xch_ref, gbn_ref,
             dqa_ref, dka_ref, dva_ref, dqh_ref, dkh_ref, dvb_ref, dyc_ref, dycn_ref, cw_ref, qw_ref, kw_ref,
             dp_ref, dx_ref, danw_ref, dqw_ref, dkw_ref, dcw_ref):
        i = pl.program_id(0)
        n = pl.num_programs(0)

        @pl.when(i == 0)
        def _():
            for r in (danw_ref, dqw_ref, dkw_ref, dcw_ref):
                r[...] = jnp.zeros_like(r)

        lo = _lo_mask(tm)
        dp_ref[:, QA:QA + 256] = dqa_ref[...].astype(dp_ref.dtype)
        dp_ref[:, KA:KA + 256] = dka_ref[...].astype(dp_ref.dtype)
        dp_ref[:, VA:VA + 256] = dva_ref[...].astype(dp_ref.dtype)
        dp_ref[:, VB:VB + 512] = dvb_ref[...].astype(dp_ref.dtype)
        for base, raws, d_ref, nw_ref, dw_ref in ((QB, (q0_ref, q1_ref), dqh_ref, qw_ref, dqw_ref), (KB, (k0_ref, k1_ref), dkh_ref, kw_ref, dkw_ref)):
            for k in range(4):
                raw = raws[k // 2][:, 128 * (k % 2):128 * (k % 2 + 1)]
                d, dw = _half_norm_bwd(raw, nw_ref[...], d_ref[:, 128 * k:128 * (k + 1)], lo)
                dp_ref[:, base + 128 * k:base + 128 * (k + 1)] = d.astype(dp_ref.dtype)
                dw_ref[0:1, :] += dw
        gb, gc, xc = gb_ref[...], gc_ref[...], xc_ref[...]
        hc_prev = jnp.where(i > 0, gch_ref[...] * xch_ref[...], 0.0)
        conv, hc, r1, r2 = _conv_fwd(gc, xc, hc_prev, cw_ref)
        dyc = dyc_ref[...]
        dconv = dyc * gb
        dconv_next = jnp.where(i < n - 1, dycn_ref[...] * gbn_ref[...], 0.0)
        ext = jnp.concatenate([dconv, dconv_next], axis=0)
        l1 = pltpu.roll(ext, tm + 7, 0)[:tm]
        l2 = pltpu.roll(ext, tm + 6, 0)[:tm]
        dhc = cw_ref[2:3, :] * dconv + cw_ref[1:2, :] * l1 + cw_ref[0:1, :] * l2
        dp_ref[:, GB:GB + 256] = (dyc * conv).astype(dp_ref.dtype)
        dp_ref[:, GC:GC + 256] = (dhc * xc).astype(dp_ref.dtype)
        dp_ref[:, XC:XC + 256] = (dhc * gc).astype(dp_ref.dtype)
        dcw_ref[0:1, :] += jnp.sum(dconv * r2, axis=0, keepdims=True)
        dcw_ref[1:2, :] += jnp.sum(dconv * r1, axis=0, keepdims=True)
        dcw_ref[2:3, :] += jnp.sum(dconv * hc, axis=0, keepdims=True)
        dh = jnp.zeros((tm, D), F32)
        for r in range(N_SH):
            dh = dh + lax.dot_general(dp_ref[:, IN_SH * r:IN_SH * (r + 1)], w_ref[r], NT, preferred_element_type=F32)
        dxn, dw = _row_norm_bwd(x_ref[...], anw_ref[...], dh)
        dx_ref[...] = dxm_ref[...] + dxn
        danw_ref[0:1, :] += dw

    row = lambda i: (i, 0)
    colb = lambda w, cb: pl.BlockSpec((tm, w), lambda i: (i, cb))
    prev = lambda cb: pl.BlockSpec((8, 256), lambda i: (jnp.maximum(i * t8 - 1, 0), cb))
    nxt = lambda cb: pl.BlockSpec((8, 256), lambda i: (jnp.minimum((i + 1) * t8, last8), cb))
    const = lambda shp: pl.BlockSpec(shp, lambda i: (0,) * len(shp))
    return pl.pallas_call(
        body, name="in_bwd", grid=(S // tm,),
        in_specs=[pl.BlockSpec((tm, D), row), pl.BlockSpec((tm, D), row), const((1, D)), const((N_SH, D, IN_SH)),
                  colb(256, 0), colb(256, 1), colb(256, 2), colb(256, 3),
                  colb(256, 0), colb(256, 1), colb(256, 2), prev(1), prev(2), nxt(0),
                  pl.BlockSpec((tm, 256), row), pl.BlockSpec((tm, 256), row), pl.BlockSpec((tm, 256), row),
                  pl.BlockSpec((tm, 512), row), pl.BlockSpec((tm, 512), row), pl.BlockSpec((tm, 512), row),
                  colb(256, 768 // 256), nxt(768 // 256), const((3, 256)), const((1, 128)), const((1, 128))],
        out_specs=[pl.BlockSpec((tm, D_IN), row), pl.BlockSpec((tm, D), row), const((8, D)), const((8, 128)), const((8, 128)), const((8, 256))],
        out_shape=[jax.ShapeDtypeStruct((S, D_IN), _MXU), jax.ShapeDtypeStruct((S, D), F32), jax.ShapeDtypeStruct((8, D), F32),
                   jax.ShapeDtypeStruct((8, 128), F32), jax.ShapeDtypeStruct((8, 128), F32), jax.ShapeDtypeStruct((8, 256), F32)],
        compiler_params=_cp(("arbitrary",), 56),
    )(dxm, x, anw, w_sh, pqk, pqk, pqk, pqk, pc, pc, pc, pc, pc, pc, dqa, dka, dva, dqh, dkh, dvb, dycat, dycat, cw, qw2, kw2)


def _rows_tile(h):
    return h if h <= 512 else 256


def _add_halves(g, r1, c_arr):
    _, R, C = g.shape
    h = R // 2
    tr = _rows_tile(h)
    nb = h // tr

    def body(c_ref, g_ref, r_ref, o_ref, ob_ref):
        s = g_ref[...] + r_ref[...]
        o_ref[...] = s
        ob_ref[...] = s.astype(ob_ref.dtype)

    blk = (1, tr, C)
    return pl.pallas_call(
        body, name="add_halves",
        grid_spec=pltpu.PrefetchScalarGridSpec(
            num_scalar_prefetch=1, grid=(N_SH, nb),
            in_specs=[pl.BlockSpec(blk, lambda s, i, c: (s, c[0] * nb + i, 0)), pl.BlockSpec(blk, lambda s, i, c: (s, i, 0))],
            out_specs=[pl.BlockSpec(blk, lambda s, i, c: (s, i, 0))] * 2),
        out_shape=[jax.ShapeDtypeStruct((N_SH, h, C), F32), jax.ShapeDtypeStruct((N_SH, h, C), jnp.bfloat16)],
        compiler_params=_cp(("parallel", "parallel"), 32),
    )(c_arr, g, r1)


def _add_shards(s1, r2, me_arr):
    _, h, C = s1.shape
    tr = _rows_tile(h)

    def body(me_ref, s_ref, r_ref, o_ref):
        o_ref[...] = ((s_ref[0] + r_ref[0].astype(F32)) + r_ref[1].astype(F32)) + r_ref[2].astype(F32)

    return pl.pallas_call(
        body, name="add_shards",
        grid_spec=pltpu.PrefetchScalarGridSpec(
            num_scalar_prefetch=1, grid=(h // tr,),
            in_specs=[pl.BlockSpec((1, tr, C), lambda i, me: (me[0], i, 0)), pl.BlockSpec((3, tr, C), lambda i, me: (0, i, 0))],
            out_specs=pl.BlockSpec((tr, C), lambda i, me: (i, 0))),
        out_shape=jax.ShapeDtypeStruct((h, C), F32),
        compiler_params=_cp(("parallel",), 32),
    )(me_arr, s1, r2)


def _adam_math(w, g, m, v):
    m = ADAM_B1 * m + (1.0 - ADAM_B1) * g
    v = ADAM_B2 * v + (1.0 - ADAM_B2) * (g * g)
    m_hat = m / (1.0 - ADAM_B1 ** ADAM_STEP)
    v_hat = v / (1.0 - ADAM_B2 ** ADAM_STEP)
    return -ADAM_LR * (m_hat / (jnp.sqrt(v_hat) + ADAM_EPS) + ADAM_WD * w), m, v


def _adam_big(w, m, v, mine, theirs, c_arr):
    L, R, C = w.shape
    h = R // 2
    tr = _rows_tile(h)
    nb = h // tr

    def body(c_ref, w_ref, m_ref, v_ref, a_ref, b_ref, g_ref, d_ref, mo_ref, vo_ref):
        g = jnp.where(pl.program_id(1) == c_ref[0], a_ref[...], b_ref[...])
        g_ref[...] = g
        d_ref[...], mo_ref[...], vo_ref[...] = _adam_math(w_ref[...], g, m_ref[...], v_ref[...])

    full = pl.BlockSpec((1, tr, C), lambda l, s, i, c: (l, s * nb + i, 0))
    half = pl.BlockSpec((1, tr, C), lambda l, s, i, c: (l, i, 0))
    return pl.pallas_call(
        body, name="adam_big",
        grid_spec=pltpu.PrefetchScalarGridSpec(num_scalar_prefetch=1, grid=(L, 2, nb), in_specs=[full, full, full, half, half],
                                               out_specs=[full] * 4),
        out_shape=[jax.ShapeDtypeStruct((L, R, C), F32)] * 4,
        compiler_params=_cp(("parallel", "parallel", "parallel"), 32),
    )(c_arr, w, m, v, mine, theirs)


def _adam_small(w, g, m, v):
    def body(w_ref, g_ref, m_ref, v_ref, d_ref, mo_ref, vo_ref):
        d_ref[...], mo_ref[...], vo_ref[...] = _adam_math(w_ref[...], g_ref[...], m_ref[...], v_ref[...])

    return pl.pallas_call(body, name="adam_small", out_shape=[jax.ShapeDtypeStruct(w.shape, F32)] * 3)(w, g, m, v)


def _allreduce_small(buf):
    R = buf.shape[0]

    def body(x_ref, o_ref, gat, ssem, rsem):
        x, y, c = lax.axis_index("x"), lax.axis_index("y"), lax.axis_index("c")
        me = 4 * x + 2 * y + c
        gat[me] = x_ref[...]
        cps = []
        for k in range(1, 8):
            peer = (x ^ (k >> 2), y ^ ((k >> 1) & 1), c ^ (k & 1))
            cps.append(pltpu.make_async_remote_copy(src_ref=x_ref, dst_ref=gat.at[me], send_sem=ssem.at[k - 1], recv_sem=rsem.at[k - 1],
                                                    device_id=peer, device_id_type=MESH))
        for cp in cps:
            cp.start()
        for cp in cps:
            cp.wait()
        acc = gat[0]
        for s in range(1, 8):
            acc = acc + gat[s]
        o_ref[...] = acc

    return pl.pallas_call(
        body, name="allreduce_small", out_shape=jax.ShapeDtypeStruct((R, 128), F32),
        in_specs=[pl.BlockSpec(memory_space=pltpu.VMEM)], out_specs=pl.BlockSpec(memory_space=pltpu.VMEM),
        scratch_shapes=[pltpu.VMEM((8, R, 128), F32), pltpu.SemaphoreType.DMA((7,)), pltpu.SemaphoreType.DMA((7,))],
    )(buf)


class _Exchange:
    def __init__(self, srcs, out_shape, n_copies, make):
        self.srcs, self.out_shape, self.n_copies, self.make = list(srcs), list(out_shape), n_copies, make

    def sems(self):
        return [pltpu.SemaphoreType.DMA((self.n_copies,))] * 2

    def run(self, name):
        n = len(self.srcs)

        def body(*refs):
            copies = self.make(refs[:n], refs[n:2 * n], *refs[2 * n:])
            for cp in copies:
                cp.start()
            for cp in copies:
                cp.wait()

        return pl.pallas_call(body, name=name, out_shape=self.out_shape, in_specs=[ANY] * n, out_specs=[ANY] * n,
                              scratch_shapes=self.sems())(*self.srcs)


def _to_sibling(srcs, pick, shapes):
    def make(ins, outs, ssem, rsem):
        x, y, c = lax.axis_index("x"), lax.axis_index("y"), lax.axis_index("c")
        return [pltpu.make_async_remote_copy(src_ref=pick(ins[f], f, c), dst_ref=outs[f], send_sem=ssem.at[f], recv_sem=rsem.at[f],
                                             device_id=(x, y, 1 - c), device_id_type=MESH) for f in range(len(ins))]

    return _Exchange(srcs, [jax.ShapeDtypeStruct(shp, F32) for shp in shapes], len(srcs), make)


def _to_chips(srcs):
    def make(ins, outs, ssem, rsem):
        x, y, c = lax.axis_index("x"), lax.axis_index("y"), lax.axis_index("c")
        return [pltpu.make_async_remote_copy(src_ref=ins[f].at[2 * px + py], dst_ref=outs[f].at[k], send_sem=ssem.at[3 * f + k],
                                             recv_sem=rsem.at[3 * f + k], device_id=(px, py, c), device_id_type=MESH)
                for f in range(len(ins)) for k, (px, py) in enumerate([(1 - x, y), (x, 1 - y), (1 - x, 1 - y)])]

    return _Exchange(srcs, [jax.ShapeDtypeStruct((3,) + s.shape[1:], s.dtype) for s in srcs], 3 * len(srcs), make)


class _GradReducer:
    def __init__(self):
        x, y, c = lax.axis_index("x"), lax.axis_index("y"), lax.axis_index("c")
        self.c_arr = jnp.reshape(c, (1,)).astype(jnp.int32)
        self.me_arr = jnp.reshape(2 * x + y, (1,)).astype(jnp.int32)
        self.fresh, self.pending, self.done = None, None, {}

    def start(self, l, grads):
        self.fresh = (l, grads)

    def halves(self):
        if self.fresh is None:
            return None
        grads = self.fresh[1]
        hs = [g.shape[1] // 2 for g in grads]
        return _to_sibling(grads, lambda ref, f, c: ref.at[:, pl.ds(pl.multiple_of((1 - c) * hs[f], 8), hs[f]), :],
                           [(N_SH, h, g.shape[2]) for g, h in zip(grads, hs)])

    def got_halves(self, r1):
        l, grads = self.fresh
        self.fresh = None
        s1, wire = zip(*[_add_halves(g, r, self.c_arr) for g, r in zip(grads, r1)])
        self.pending = (l, s1, wire)

    def shards(self):
        return None if self.pending is None else _to_chips(self.pending[2])

    def got_shards(self, r2):
        l, s1, _ = self.pending
        self.pending = None
        mine = [_add_shards(s, r, self.me_arr) for s, r in zip(s1, r2)]
        theirs = _to_sibling(mine, lambda ref, f, c: ref, [m.shape for m in mine]).run("rs_sibling_reduced")
        self.done[l] = (mine, theirs)

    def finish(self):
        if self.pending is not None:
            self.got_shards(self.shards().run("rs_chip_shards"))
        if self.fresh is not None:
            self.got_halves(self.halves().run("rs_sibling_halves"))
            self.got_shards(self.shards().run("rs_chip_shards"))


def _pad_rows(flat, rows):
    return jnp.pad(flat, (0, rows * 128 - flat.shape[0])).reshape(rows, 128)


class _ShardedWeights:
    split = [True] * 5 + [False]

    def __init__(self, w):
        self.parts = [[w[k][l].astype(_MXU) for k in _BIG] + [jnp.zeros((8, 128), F32).at[:3, :HD].set(w["conv_w"][l])]
                      for l in range(DEPTH)]

    def first(self):
        return {"w_in": _allgather_layer(self.parts[0][:1], self.split[:1])[0]}

    def rest_of_first(self):
        return self.parts[0][1:], self.split[1:]

    def following(self, l):
        return (self.parts[l + 1], self.split) if l + 1 < DEPTH else ((), ())

    @staticmethod
    def unpack(got):
        W = dict(zip(_BIG[len(_BIG) + 1 - len(got):], got[:-1]))
        W["conv_w"] = jnp.concatenate([got[-1][r, :3, :HD] for r in range(N_SH)], axis=-1)
        return W


def _local_step(x, target, small, weights, reducer):
    S = x.shape[0]
    saved, big = [], []
    W = weights.first()
    for l in range(DEPTH):
        big.append(W)
        anw, onw, fnw = small["attn_norm_w"][l][None], small["out_norm_w"][l][None], small["ffn_norm_w"][l][None]
        qw2 = jnp.tile(small["q_norm_w"][l], 2)[None]
        kw2 = jnp.tile(small["k_norm_w"][l], 2)[None]
        bias = _bias_tiles(small["rel_bias"][l])
        h, pa, pqk, pn, pv, pc = _attn_in(x, anw, W["w_in"], qw2, kw2)
        ysb = _sb_fwd(pa)
        outs = _ch_fwd(pn, pv, bias, *(weights.rest_of_first() if l == 0 else ((), ())))
        ych, lse = outs[:2]
        if l == 0:
            W.update(weights.unpack(outs[2:]))
        cw = W["conv_w"]
        w_out = W["w_out"].reshape(D, D)
        xmid, yn, ycv = _merge_out(x, ysb, ych, pc, cw, onw, w_out)
        outs = _ffn_fwd(xmid, fnw, W["w_gate"], W["w_up"], W["w_down"], *weights.following(l))
        xo, hn, u, t, a = outs[:5]
        saved.append(dict(x=x, h=h, pa=pa, pqk=pqk, pn=pn, pv=pv, pc=pc, ysb=ysb, ych=ych, lse=lse, ycv=ycv, yn=yn, xmid=xmid, hn=hn, u=u, t=t, a=a,
                          anw=anw, onw=onw, fnw=fnw, qw2=qw2, kw2=kw2, cw=cw, bias=bias, w_out=w_out))
        x = xo
        if l + 1 < DEPTH:
            W = weights.unpack(outs[5:])
    dx, lpart = _loss_head(x, target)
    loss = lpart[0, 0]
    grads = [None] * DEPTH
    for l in reversed(range(DEPTH)):
        W, sv = big[l], saved[l]
        outs = _ffn_bwd(dx, sv["xmid"], sv["fnw"], W["w_gate"], W["w_up"], W["w_down"], sv["u"], sv["t"], reducer.halves())
        du, dt, dxm, dfnw = outs[:4]
        if outs[4:]:
            reducer.got_halves(outs[4:])
        g_gate = _wgrad(sv["hn"], du, _tok(D), _tok_sh(FF_SH), (D, FF_SH), "wgrad_gate")
        g_up = _wgrad(sv["hn"], dt, _tok(D), _tok_sh(FF_SH), (D, FF_SH), "wgrad_up")
        g_down = _wgrad(sv["a"], dx, _tok_sh(FF_SH), _tok(D), (FF_SH, D), "wgrad_down")
        dycat, donw = _out_bwd(dxm, sv["w_out"], sv["ysb"], sv["ych"], sv["ycv"], sv["onw"])
        g_out = _wgrad(sv["yn"], dxm, _tok_cols(OUT_SH), _tok(D), (OUT_SH, D), "wgrad_out")
        dqa, dka, dva = _sb_bwd(sv["pa"], dycat)
        outs = _ch_bwd(sv["pn"], sv["pv"], sv["bias"], sv["lse"], dycat, reducer.shards())
        dqh, dbias, dkh, dvb = outs[:4]
        if outs[4:]:
            reducer.got_shards(outs[4:])
        drelb = _bias_grad(dbias)
        dproj, dx, danw, dqw, dkw, dcw = _in_bwd(dxm, sv["x"], sv["anw"], W["w_in"], sv["pqk"], sv["pc"], dqa, dka, dva, dqh, dkh, dvb, dycat,
                                                 sv["cw"], sv["qw2"], sv["kw2"])
        g_in = _wgrad(sv["h"], dproj, _tok(D), _tok_cols(IN_SH), (D, IN_SH), "wgrad_in")
        grads[l] = dict(w_in=g_in, w_out=g_out, w_gate=g_gate, w_up=g_up, w_down=g_down,
                        attn_norm_w=danw[0], out_norm_w=donw[0], ffn_norm_w=dfnw[0],
                        q_norm_w=dqw[0, :HD] + dqw[0, HD:], k_norm_w=dkw[0, :HD] + dkw[0, HD:], rel_bias=drelb, conv_w=dcw[:3])
        reducer.start(l, [grads[l][k] for k in _BIG])
    reducer.finish()
    return loss, dx, grads


_BIG = ("w_in", "w_out", "w_gate", "w_up", "w_down")
_SMALL = ("attn_norm_w", "q_norm_w", "k_norm_w", "rel_bias", "out_norm_w", "ffn_norm_w")
_ORDER = ("attn_norm_w", "w_in", "q_norm_w", "k_norm_w", "rel_bias", "conv_w", "out_norm_w", "w_out", "ffn_norm_w", "w_gate", "w_up", "w_down")


def kernel(x, attn_norm_w, w_in, q_norm_w, k_norm_w, rel_bias, conv_w, out_norm_w, w_out, ffn_norm_w, w_gate, w_up, w_down, loss_target, m_attn_norm_w, m_w_in, m_q_norm_w, m_k_norm_w, m_rel_bias, m_conv_w, m_out_norm_w, m_w_out, m_ffn_norm_w, m_w_gate, m_w_up, m_w_down, v_attn_norm_w, v_w_in, v_q_norm_w, v_k_norm_w, v_rel_bias, v_conv_w, v_out_norm_w, v_w_out, v_ffn_norm_w, v_w_gate, v_w_up, v_w_down):
    w = dict(attn_norm_w=attn_norm_w, w_in=w_in, q_norm_w=q_norm_w, k_norm_w=k_norm_w, rel_bias=rel_bias, conv_w=conv_w,
             out_norm_w=out_norm_w, w_out=w_out, ffn_norm_w=ffn_norm_w, w_gate=w_gate, w_up=w_up, w_down=w_down)
    m = dict(attn_norm_w=m_attn_norm_w, w_in=m_w_in, q_norm_w=m_q_norm_w, k_norm_w=m_k_norm_w, rel_bias=m_rel_bias, conv_w=m_conv_w,
             out_norm_w=m_out_norm_w, w_out=m_w_out, ffn_norm_w=m_ffn_norm_w, w_gate=m_w_gate, w_up=m_w_up, w_down=m_w_down)
    v = dict(attn_norm_w=v_attn_norm_w, w_in=v_w_in, q_norm_w=v_q_norm_w, k_norm_w=v_k_norm_w, rel_bias=v_rel_bias, conv_w=v_conv_w,
             out_norm_w=v_out_norm_w, w_out=v_w_out, ffn_norm_w=v_ffn_norm_w, w_gate=v_w_gate, w_up=v_w_up, w_down=v_w_down)
    xi, yi = lax.axis_index("x"), lax.axis_index("y")
    me = 2 * xi + yi

    small = {k: w[k] for k in _SMALL}
    reducer = _GradReducer()
    loss, grad_x, grads = _local_step(x[0], loss_target[0], small, _ShardedWeights(w), reducer)
    loss = lax.psum(loss, ("x", "y", "c"))

    out = {}
    for f, k in enumerate(_BIG):
        mine, theirs = (jnp.stack([reducer.done[l][side][f] for l in range(DEPTH)]) for side in (0, 1))
        out[k] = _adam_big(w[k], m[k], v[k], mine, theirs, reducer.c_arr)

    names = _SMALL + ("conv_w",)
    gflat = jnp.concatenate([jnp.stack([grads[l][k] for l in range(DEPTH)]).reshape(-1) for k in names])
    rows = -(-gflat.shape[0] // 1024) * 8
    gsum = _allreduce_small(_pad_rows(gflat, rows)).reshape(-1)
    gs, off = {}, 0
    for k in names:
        shp = (DEPTH, 3, 4 * HD) if k == "conv_w" else w[k].shape
        size = int(np.prod(shp))
        gs[k] = gsum[off:off + size].reshape(shp)
        off += size
    gs["conv_w"] = lax.dynamic_slice_in_dim(gs["conv_w"], me * HD, HD, axis=2)
    flat = lambda d: jnp.concatenate([d[k].reshape(-1) for k in names])
    rows = -(-flat(w).shape[0] // 1024) * 8
    dsm, msm, vsm = _adam_small(_pad_rows(flat(w), rows), _pad_rows(flat(gs), rows), _pad_rows(flat(m), rows), _pad_rows(flat(v), rows))
    off = 0
    for k in names:
        size = int(np.prod(w[k].shape))
        out[k] = (gs[k],) + tuple(a.reshape(-1)[off:off + size].reshape(w[k].shape) for a in (dsm, msm, vsm))
        off += size

    return (loss, grad_x[None]) + tuple(out[k][j] for j in range(4) for k in _ORDER)
```

```python
import functools

import numpy as np
import jax
import jax.numpy as jnp
from jax import lax
from jax.experimental import pallas as pl
from jax.experimental.pallas import tpu as pltpu

F32 = jnp.float32
_MXU = jnp.bfloat16

D = 1024
DEPTH = 4
HD = 64
D_IN = 3072
N_SH = 4
IN_SH = D_IN // N_SH
FF_SH = 704
OUT_SH = 256
N_HEADS_CH = 8
D_CH = N_HEADS_CH * HD
REL = 257
QA, KA, VA, QB, KB, VB, GB, GC, XC = 0, 256, 512, 768, 1280, 1792, 2304, 2560, 2816
EPS = 1e-6
NEG = -1e30
FFN_SUB = 256
SB_BLK = 128
CH_TQ = 256
CH_SUB = 256
EXP_ZERO = -104.0

ADAM_LR, ADAM_B1, ADAM_B2, ADAM_EPS, ADAM_WD, ADAM_STEP = 0.001, 0.9, 0.999, 1e-08, 0.01, 10

NT = (((1,), (1,)), ((), ()))
TN = (((0,), (0,)), ((), ()))
MESH = pl.DeviceIdType.MESH
ANY = pl.BlockSpec(memory_space=pl.ANY)
MB = 1 << 20


def _cp(sem=None, vmem_mb=None):
    return pltpu.CompilerParams(dimension_semantics=sem, vmem_limit_bytes=None if vmem_mb is None else vmem_mb * MB)


def _mx(a):
    return a.astype(_MXU)


def _dot(a, b):
    return jnp.dot(_mx(a), _mx(b), preferred_element_type=F32)


def _dg(a, b, dims):
    return lax.dot_general(_mx(a), _mx(b), dims, preferred_element_type=F32)


def _sigmoid(u):
    return 1.0 / (1.0 + jnp.exp(-u))


def _row_norm(x, w):
    r = lax.rsqrt(jnp.mean(x * x, axis=-1, keepdims=True) + EPS)
    return x * r * w


def _row_norm_bwd(x, w, dy):
    r = lax.rsqrt(jnp.mean(x * x, axis=-1, keepdims=True) + EPS)
    xh = x * r
    dw = jnp.sum(dy * xh, axis=0, keepdims=True)
    dxh = dy * w
    return r * (dxh - xh * jnp.mean(dxh * xh, axis=-1, keepdims=True)), dw


def _half_rs(x, lo):
    sq = x * x
    s0 = jnp.sum(jnp.where(lo, sq, 0.0), axis=-1, keepdims=True)
    s1 = jnp.sum(jnp.where(lo, 0.0, sq), axis=-1, keepdims=True)
    return jnp.where(lo, lax.rsqrt(s0 * (1.0 / HD) + EPS), lax.rsqrt(s1 * (1.0 / HD) + EPS))


def _half_norm(x, w, lo):
    return x * _half_rs(x, lo) * w


def _half_norm_bwd(x, w, dy, lo):
    r = _half_rs(x, lo)
    xh = x * r
    dw = jnp.sum(dy * xh, axis=0, keepdims=True)
    dxh = dy * w
    pr = dxh * xh
    m0 = jnp.sum(jnp.where(lo, pr, 0.0), axis=-1, keepdims=True)
    m1 = jnp.sum(jnp.where(lo, 0.0, pr), axis=-1, keepdims=True)
    mean = jnp.where(lo, m0, m1) * (1.0 / HD)
    return r * (dxh - xh * mean), dw


def _lo_mask(t):
    return lax.broadcasted_iota(jnp.int32, (t, 128), 1) < HD


class _Gather:
    def __init__(self, ins, outs, sems, split):
        self.n = n = len(ins)
        lsem, ssem, rsem, fsem, gsem = sems
        x, y, c = lax.axis_index("x"), lax.axis_index("y"), lax.axis_index("c")
        me = 2 * x + y
        chips = [(1 - x, y), (x, 1 - y), (1 - x, 1 - y)]

        def half(f):
            h = ins[f].shape[0] // 2
            return pl.ds(pl.multiple_of(c * h, 16), h)

        self.local = [pltpu.make_async_copy(ins[f], outs[f].at[me], lsem.at[f]) for f in range(n)]
        self.sends, self.passed = [], []
        for f in range(n):
            for k, (px, py) in enumerate(chips):
                src, dst = (ins[f].at[half(f)], outs[f].at[me, half(f)]) if split[f] else (ins[f], outs[f].at[me])
                self.sends.append(pltpu.make_async_remote_copy(src_ref=src, dst_ref=dst, send_sem=ssem.at[3 * f + k], recv_sem=rsem.at[3 * f + k],
                                                               device_id=(px, py, c), device_id_type=MESH))
                if split[f]:
                    got = outs[f].at[2 * px + py, half(f)]
                    self.passed.append(pltpu.make_async_remote_copy(src_ref=got, dst_ref=got, send_sem=fsem.at[3 * f + k], recv_sem=gsem.at[3 * f + k],
                                                                    device_id=(x, y, 1 - c), device_id_type=MESH))
                else:
                    self.passed.append(None)

    def start(self):
        for cp in self.local + self.sends:
            cp.start()

    def forward(self):
        for cp, fw in zip(self.sends, self.passed):
            cp.wait_recv()
            if fw is not None:
                fw.start()

    def finish(self):
        passed = [fw for fw in self.passed if fw is not None]
        for fw in passed:
            fw.wait_recv()
        for cp in self.sends + passed:
            cp.wait_send()
        for cp in self.local:
            cp.wait()

    @staticmethod
    def sems(n):
        return [pltpu.SemaphoreType.DMA((n,))] + [pltpu.SemaphoreType.DMA((3 * n,))] * 4

    @staticmethod
    def out_shape(parts):
        return [jax.ShapeDtypeStruct((N_SH,) + p.shape, p.dtype) for p in parts]


def _allgather_layer(parts, split):
    n = len(parts)

    def body(*refs):
        g = _Gather(refs[:n], refs[n:2 * n], refs[2 * n:], split)
        g.start()
        g.forward()
        g.finish()

    return pl.pallas_call(
        body, name="allgather_layer", out_shape=_Gather.out_shape(parts),
        in_specs=[ANY] * n, out_specs=[ANY] * n, scratch_shapes=_Gather.sems(n),
    )(*parts)


def _attn_in(x, nw, w_sh, qw2, kw2):
    S = x.shape[0]
    tm = min(512, S)
    assert (QB, VB, GB) == (IN_SH, 2 * IN_SH + 256, 3 * IN_SH)

    def body(x_ref, nw_ref, w_ref, qw_ref, kw_ref, h_ref, pa_ref, pqk_ref, pn_ref, pv_ref, pc_ref):
        h = _row_norm(x_ref[...], nw_ref[...]).astype(h_ref.dtype)
        h_ref[...] = h
        pa_ref[...] = jnp.dot(h, w_ref[0], preferred_element_type=F32).astype(pa_ref.dtype)
        pqk_ref[:, :IN_SH] = jnp.dot(h, w_ref[1], preferred_element_type=F32)
        p2 = jnp.dot(h, w_ref[2], preferred_element_type=F32)
        pqk_ref[:, IN_SH:] = p2[:, :256]
        pv_ref[...] = p2[:, 256:].astype(pv_ref.dtype)
        pc_ref[...] = jnp.dot(h, w_ref[3], preferred_element_type=F32)
        lo = _lo_mask(tm)
        for k in range(2 * D_CH // 128):
            cols = slice(128 * k, 128 * (k + 1))
            is_q = k < D_CH // 128
            n = _half_norm(pqk_ref[:, cols], (qw_ref if is_q else kw_ref)[...], lo)
            pn_ref[:, cols] = (n * 0.125 if is_q else n).astype(pn_ref.dtype)

    row = lambda i: (i, 0)
    const = lambda shp: pl.BlockSpec(shp, lambda i: (0,) * len(shp))
    return pl.pallas_call(
        body, name="attn_in", grid=(S // tm,),
        in_specs=[pl.BlockSpec((tm, D), row), const((1, D)), const((N_SH, D, IN_SH)), const((1, 128)), const((1, 128))],
        out_specs=[pl.BlockSpec((tm, D), row), pl.BlockSpec((tm, IN_SH), row), pl.BlockSpec((tm, 2 * D_CH), row),
                   pl.BlockSpec((tm, 2 * D_CH), row), pl.BlockSpec((tm, D_CH), row), pl.BlockSpec((tm, IN_SH), row)],
        out_shape=[jax.ShapeDtypeStruct((S, D), _MXU), jax.ShapeDtypeStruct((S, IN_SH), _MXU), jax.ShapeDtypeStruct((S, 2 * D_CH), F32),
                   jax.ShapeDtypeStruct((S, 2 * D_CH), _MXU), jax.ShapeDtypeStruct((S, D_CH), _MXU), jax.ShapeDtypeStruct((S, IN_SH), F32)],
        compiler_params=_cp(("parallel",), 48),
    )(x, nw, w_sh, qw2, kw2)


SB_NB = 3
SB_TK = SB_NB * SB_BLK
SB_QB = 4


def _sb_any_open(states):
    return functools.reduce(jnp.logical_or, [jnp.logical_and(st[0] >= 0, st[1] > EXP_ZERO) for st in states])


def _sb_const_arrays():
    r, c = np.meshgrid(np.arange(SB_TK), np.arange(256), indexing="ij")
    kofs = (r - (r & 127)) + (c & 127)
    r2, c2 = np.meshgrid(np.arange(256), np.arange(256), indexing="ij")
    same = (r2 >= 128) == (c2 >= 128)
    nearer = same & ((r2 & 127) > (c2 & 127))
    at_or_nearer = same & ((r2 & 127) >= (c2 & 127))
    return (jnp.asarray(kofs, jnp.int32), jnp.asarray(kofs - (r & 127), jnp.int32), jnp.asarray(nearer, _MXU), jnp.asarray(at_or_nearer, _MXU))


_SB_CONST_SPECS = [pl.BlockSpec((SB_TK, 256), lambda p, i: (0, 0))] * 2 + [pl.BlockSpec((256, 256), lambda p, i: (0, 0))] * 2


class _SbConsts:
    def __init__(self, kofs_ref, rel_ref, nearer_ref, at_or_nearer_ref):
        self.kofs, self.rel, self.nearer, self.at_or_nearer = kofs_ref, rel_ref, nearer_ref, at_or_nearer_ref
        self.lo = lax.broadcasted_iota(jnp.int32, (SB_BLK, 128), 1) < HD
        self.head0 = lax.broadcasted_iota(jnp.int32, (SB_BLK, 256), 1) < 128


def _two_heads(x, lo):
    return jnp.concatenate([jnp.where(lo, x, 0), jnp.where(lo, 0, x)], axis=0)


def _per_block(vals, head0):
    return jnp.concatenate([jnp.where(head0, v0, v1) for v0, v1 in vals], axis=0)


def _head_sums(x):
    return jnp.sum(x[:, :128], axis=-1, keepdims=True), jnp.sum(x[:, 128:], axis=-1, keepdims=True)


def _sb_tile(q2, k_ref, i, jt, c0, c1, K):
    s0 = pl.multiple_of(jnp.maximum(jt - (SB_NB - 1), 0) * SB_BLK, SB_BLK)
    kt = k_ref[pl.ds(s0, SB_TK), :]
    k2 = [_two_heads(kt[b * SB_BLK:(b + 1) * SB_BLK], K.lo) for b in range(SB_NB)]
    keep = jnp.logical_and(K.rel[...] < i * SB_BLK - s0, K.kofs[...] < (jt + 1) * SB_BLK - s0)
    z = jnp.where(keep, jnp.concatenate([_dg(q2, k2[b], NT) for b in range(SB_NB)], axis=0), NEG)
    lr = -(jnp.maximum(z, 0.0) + jnp.log(1.0 + jnp.exp(-jnp.abs(z))))
    carries = [None] * SB_NB
    for b in reversed(range(SB_NB)):
        carries[b] = (c0, c1)
        s = _head_sums(lr[b * SB_BLK:(b + 1) * SB_BLK])
        c0, c1 = c0 + s[0], c1 + s[1]
    after = _dot(lr, K.nearer[...]) + _per_block(carries, K.head0)
    w = jnp.exp(z + lr + after)
    return s0, k2, z, lr, w, c0, c1


def _sb_fwd(pa):
    S = pa.shape[0]
    nq = S // SB_BLK
    assert nq % SB_QB == 0

    def body(q_ref, k_ref, v_ref, kofs_ref, rel_ref, nearer_ref, aon_ref, o_ref):
        K = _SbConsts(kofs_ref, rel_ref, nearer_ref, aon_ref)
        blocks = [pl.program_id(1) * SB_QB + s for s in range(SB_QB)]
        q2 = [q_ref[s * SB_BLK:(s + 1) * SB_BLK, :] * 0.125 for s in range(SB_QB)]

        def step(st):
            new = []
            for s, (jt, _, acc, c0, c1) in enumerate(st):
                s0, _, _, _, w, c0, c1 = _sb_tile(q2[s], k_ref, blocks[s], jt, c0, c1, K)
                vt = v_ref[pl.ds(s0, SB_TK), :]
                for b in range(SB_NB):
                    rows = slice(b * SB_BLK, (b + 1) * SB_BLK)
                    acc = acc + _dot(w[rows], _two_heads(vt[rows], K.lo))
                new.append((jt - SB_NB, jnp.max(jnp.maximum(c0, c1)), acc, c0, c1))
            return tuple(new)

        zc = jnp.zeros((SB_BLK, 1), F32)
        st = lax.while_loop(_sb_any_open, step, tuple((i, jnp.float32(0.0), jnp.zeros((SB_BLK, 128), F32), zc, zc) for i in blocks))
        for s in range(SB_QB):
            o_ref[s * SB_BLK:(s + 1) * SB_BLK, :] = st[s][2]

    tq = SB_QB * SB_BLK
    return pl.pallas_call(
        body, name="sb_fwd", grid=(2, nq // SB_QB),
        in_specs=[pl.BlockSpec((tq, 128), lambda p, i: (i, QA // 128 + p)),
                  pl.BlockSpec((S, 128), lambda p, i: (0, KA // 128 + p)),
                  pl.BlockSpec((S, 128), lambda p, i: (0, VA // 128 + p))] + _SB_CONST_SPECS,
        out_specs=pl.BlockSpec((tq, 128), lambda p, i: (i, p)),
        out_shape=jax.ShapeDtypeStruct((S, 256), F32),
        compiler_params=_cp(("parallel", "arbitrary"), 40),
    )(pa, pa, pa, *_sb_const_arrays())


def _sb_bwd(pa, dycat):
    S = pa.shape[0]
    nq = S // SB_BLK
    assert nq % SB_QB == 0
    n_steps = nq // SB_QB

    def body(q_ref, k_ref, v_ref, do_ref, kofs_ref, rel_ref, nearer_ref, aon_ref, dq_ref, dk_hbm, dv_hbm, dk_acc, dv_acc, sem):
        p, step_i = pl.program_id(0), pl.program_id(1)
        K = _SbConsts(kofs_ref, rel_ref, nearer_ref, aon_ref)

        @pl.when(step_i == 0)
        def _():
            dk_acc[...] = jnp.zeros_like(dk_acc)
            dv_acc[...] = jnp.zeros_like(dv_acc)

        blocks = [step_i * SB_QB + s for s in range(SB_QB)]
        q2 = [q_ref[s * SB_BLK:(s + 1) * SB_BLK, :] * 0.125 for s in range(SB_QB)]
        do = [_mx(do_ref[s * SB_BLK:(s + 1) * SB_BLK, :]) for s in range(SB_QB)]

        def tile(s, jt, c0, c1):
            s0, k2, z, lr, w, c0, c1 = _sb_tile(q2[s], k_ref, blocks[s], jt, c0, c1, K)
            vt = v_ref[pl.ds(s0, SB_TK), :]
            g = jnp.concatenate([_dg(do[s], _two_heads(vt[b * SB_BLK:(b + 1) * SB_BLK], K.lo), NT) for b in range(SB_NB)], axis=0) * w
            gs = [_head_sums(g[b * SB_BLK:(b + 1) * SB_BLK]) for b in range(SB_NB)]
            return dict(s0=s0, k2=k2, z=z, lr=lr, w=w, g=g, gs=gs, c0=c0, c1=c1, mx=jnp.max(jnp.maximum(c0, c1)),
                        g0=sum(x[0] for x in gs), g1=sum(x[1] for x in gs))

        def grads(s, T, tot, n0, n1):
            nearer = [None] * SB_NB
            for b in reversed(range(SB_NB)):
                nearer[b] = (n0, n1)
                n0, n1 = n0 + T["gs"][b][0], n1 + T["gs"][b][1]
            farther = _per_block([tot] * SB_NB, K.head0) - _per_block(nearer, K.head0) - _dot(T["g"], K.at_or_nearer[...])
            beta = jnp.exp(T["z"] + T["lr"])
            dz = _mx(T["g"] * (1.0 - beta) - farther * beta)
            wm = _mx(T["w"])
            dq = jnp.zeros((SB_BLK, 128), F32)
            dks, dvs = [], []
            for b in range(SB_NB):
                rows = slice(b * SB_BLK, (b + 1) * SB_BLK)
                dq = dq + _dot(dz[rows], T["k2"][b])
                dk2 = _dg(dz[rows], q2[s], TN)
                dv2 = _dg(wm[rows], do[s], TN)
                dks.append(jnp.where(K.lo, dk2[:SB_BLK], dk2[SB_BLK:]))
                dvs.append(jnp.where(K.lo, dv2[:SB_BLK], dv2[SB_BLK:]))
            dk_acc[pl.ds(T["s0"], SB_TK), :] += jnp.concatenate(dks, axis=0)
            dv_acc[pl.ds(T["s0"], SB_TK), :] += jnp.concatenate(dvs, axis=0)
            return 0.125 * dq

        zc = jnp.zeros((SB_BLK, 1), F32)
        first = [tile(s, blocks[s], zc, zc) for s in range(SB_QB)]

        def sum_step(st):
            new = []
            for s, (jt, _, c0, c1, t0, t1) in enumerate(st):
                T = tile(s, jt, c0, c1)
                new.append((jt - SB_NB, T["mx"], T["c0"], T["c1"], t0 + T["g0"], t1 + T["g1"]))
            return tuple(new)

        st = lax.while_loop(_sb_any_open, sum_step,
                            tuple((blocks[s] - SB_NB, T["mx"], T["c0"], T["c1"], T["g0"], T["g1"]) for s, T in enumerate(first)))
        tot = [x[4:] for x in st]

        def grad_step(st):
            new = []
            for s, (jt, _, dq, c0, c1, n0, n1) in enumerate(st):
                T = tile(s, jt, c0, c1)
                new.append((jt - SB_NB, T["mx"], dq + grads(s, T, tot[s], n0, n1), T["c0"], T["c1"], n0 + T["g0"], n1 + T["g1"]))
            return tuple(new)

        st = lax.while_loop(_sb_any_open, grad_step,
                            tuple((blocks[s] - SB_NB, T["mx"], grads(s, T, tot[s], zc, zc), T["c0"], T["c1"], T["g0"], T["g1"])
                                  for s, T in enumerate(first)))
        for s in range(SB_QB):
            dq_ref[s * SB_BLK:(s + 1) * SB_BLK, :] = st[s][2]

        @pl.when(step_i == n_steps - 1)
        def _():
            cols = pl.ds(pl.multiple_of(p * 128, 128), 128)
            ck = pltpu.make_async_copy(dk_acc, dk_hbm.at[:, cols], sem.at[0])
            cv = pltpu.make_async_copy(dv_acc, dv_hbm.at[:, cols], sem.at[1])
            ck.start()
            cv.start()
            ck.wait()
            cv.wait()

    tq = SB_QB * SB_BLK
    return pl.pallas_call(
        body, name="sb_bwd", grid=(2, n_steps),
        in_specs=[pl.BlockSpec((tq, 128), lambda p, i: (i, QA // 128 + p)),
                  pl.BlockSpec((S, 128), lambda p, i: (0, KA // 128 + p)),
                  pl.BlockSpec((S, 128), lambda p, i: (0, VA // 128 + p)),
                  pl.BlockSpec((tq, 128), lambda p, i: (i, p))] + _SB_CONST_SPECS,
        out_specs=[pl.BlockSpec((tq, 128), lambda p, i: (i, p)), ANY, ANY],
        out_shape=[jax.ShapeDtypeStruct((S, 256), F32)] * 3,
        scratch_shapes=[pltpu.VMEM((S, 128), F32), pltpu.VMEM((S, 128), F32), pltpu.SemaphoreType.DMA((2,))],
        compiler_params=_cp(("arbitrary", "arbitrary"), 48),
    )(pa, pa, pa, dycat, *_sb_const_arrays())


def _bias_tiles(rel_bias_l):
    r = np.arange(CH_TQ)[:, None]
    c = np.arange(CH_TQ)[None, :]
    valid = np.stack([(4 * b + c // 64 >= r // 64) & (4 * b + c // 64 <= r // 64 + 8) for b in range(3)])
    rev = rel_bias_l[:, _DIAG_IDX[:, ::-1]]
    rows = jnp.broadcast_to(jnp.pad(rev, ((0, 0), (0, 0), (0, 1)))[:, :, None, :], (N_HEADS_CH, 3, CH_TQ, 2 * CH_TQ))
    skew = rows.reshape(N_HEADS_CH, 3, -1)[:, :, :CH_TQ * (2 * CH_TQ - 1)].reshape(N_HEADS_CH, 3, CH_TQ, 2 * CH_TQ - 1)
    return jnp.where(valid[None], skew[:, :, :, CH_TQ - 1:], NEG)


_DIAG_IDX = np.stack([np.clip(512 - 256 * b + np.arange(2 * CH_TQ - 1) - (CH_TQ - 1), -128, 128) + 128 for b in range(3)])


def _bias_grad(dbias):
    assert (_DIAG_IDX[0] == REL - 1).all()
    d = jnp.pad(dbias, ((0, 0), (0, 0), (0, 0), (CH_TQ - 1, 0))).reshape(N_HEADS_CH, 2, -1)
    d = jnp.pad(d, ((0, 0), (0, 0), (0, CH_TQ))).reshape(N_HEADS_CH, 2, CH_TQ, 2 * CH_TQ)
    diag = _col_sum(d)[:, :, 0, :2 * CH_TQ - 1][:, :, ::-1]
    onehot = (_DIAG_IDX[1:, :, None] == np.arange(REL)[None, None, :]).astype(np.float32)
    g = jnp.einsum("hbk,bkr->hr", diag, jnp.asarray(onehot), precision=lax.Precision.HIGHEST)
    return g.at[:, REL - 1].add(-jnp.sum(diag, axis=(1, 2)))


def _ch_logits(i, q_ref, kn, b_ref, rows):
    lo = _lo_mask(rows.stop - rows.start)
    q = q_ref[rows, :]
    qh = [jnp.where(lo, q, 0), jnp.where(lo, 0, q)]
    scores = []
    for h in range(2):
        s = [_dg(qh[h], kn[b], NT) + b_ref[h, b, rows, :] for b in range(3)]
        s[0] = jnp.where(i >= 2, s[0], NEG)
        s[1] = jnp.where(i >= 1, s[1], NEG)
        scores.append(s)
    return lo, qh, scores


def _ch_specs(S):
    nt = S // CH_TQ

    def kspec(b, base):
        return pl.BlockSpec((CH_TQ, 128), lambda p, i: (jnp.maximum(i - 2 + b, 0), base // 128 + p))

    specs = [pl.BlockSpec((CH_TQ, 128), lambda p, i: (i, p))]
    specs += [kspec(b, D_CH) for b in range(3)] + [kspec(b, 0) for b in range(3)]
    specs += [pl.BlockSpec((2, 3, CH_TQ, CH_TQ), lambda p, i: (p, 0, 0, 0))]
    return nt, specs


def _ch_fwd(pn, pv, bias, nxt=(), split=()):
    S = pn.shape[0]
    nt, specs = _ch_specs(S)
    n = len(nxt)

    def body(*refs):
        q_ref, k0, k1, k2, v0, v1, v2, b_ref = refs[:8]
        o_ref, lse_ref = refs[8 + n:10 + n]
        p, i = pl.program_id(0), pl.program_id(1)
        if n:
            gather = _Gather(refs[8:8 + n], refs[10 + n:10 + 2 * n], refs[10 + 2 * n:], split)

            @pl.when(jnp.logical_and(p == 0, i == 0))
            def _():
                gather.start()

            @pl.when(jnp.logical_and(p == 2, i == 0))
            def _():
                gather.forward()

        kn = [k0[...], k1[...], k2[...]]
        vs = [v0[...], v1[...], v2[...]]
        for r0 in range(0, CH_TQ, CH_SUB):
            rows = slice(r0, r0 + CH_SUB)
            lo, _, scores = _ch_logits(i, q_ref, kn, b_ref, rows)
            outs, lse = [], []
            for s in scores:
                m = jnp.max(jnp.maximum(jnp.maximum(s[0], s[1]), s[2]), axis=-1, keepdims=True)
                e = [jnp.exp(sb - m) for sb in s]
                l = jnp.sum((e[0] + e[1]) + e[2], axis=-1, keepdims=True)
                outs.append((_dot(e[0], vs[0]) + _dot(e[1], vs[1]) + _dot(e[2], vs[2])) / l)
                lse.append(m + jnp.log(l))
            o_ref[rows, :] = jnp.where(lo, outs[0], outs[1])
            lse_ref[0, rows, :] = jnp.where(lax.broadcasted_iota(jnp.int32, (CH_SUB, 2), 1) == 0, lse[0], lse[1])

        if n:
            @pl.when(jnp.logical_and(p == 3, i == nt - 1))
            def _():
                gather.finish()

    return pl.pallas_call(
        body, name="ch_fwd_gather" if n else "ch_fwd", grid=(4, nt), in_specs=specs + [ANY] * n,
        out_specs=[pl.BlockSpec((CH_TQ, 128), lambda p, i: (i, p)), pl.BlockSpec((1, CH_TQ, 2), lambda p, i: (p, i, 0))] + [ANY] * n,
        out_shape=[jax.ShapeDtypeStruct((S, 512), F32), jax.ShapeDtypeStruct((4, S, 2), F32)] + _Gather.out_shape(nxt),
        scratch_shapes=_Gather.sems(n) if n else [],
        compiler_params=_cp(("arbitrary", "arbitrary") if n else ("parallel", "arbitrary"), 40),
    )(pn, pn, pn, pn, pv, pv, pv, bias, *nxt)


def _ch_bwd(pn, pv, bias, lse, dycat, hosted=None):
    S = pn.shape[0]
    nt, specs = _ch_specs(S)
    specs = specs + [pl.BlockSpec((1, CH_TQ, 2), lambda p, i: (p, i, 0)), pl.BlockSpec((CH_TQ, 128), lambda p, i: (i, 256 // 128 + p))]
    n = len(hosted.srcs) if hosted else 0

    def body(*refs):
        q_ref, k0, k1, k2, v0, v1, v2, b_ref, lse_ref, do_ref = refs[:10]
        dq_ref, db_ref, dk_hbm, dv_hbm = refs[10 + n:14 + n]
        dk_acc, dv_acc, sem = refs[14 + 2 * n:17 + 2 * n]
        p, i = pl.program_id(0), pl.program_id(1)
        if n:
            copies = hosted.make(refs[10:10 + n], refs[14 + n:14 + 2 * n], *refs[17 + 2 * n:])

            @pl.when(jnp.logical_and(p == 0, i == 0))
            def _():
                for cp in copies:
                    cp.start()

        @pl.when(i == 0)
        def _():
            dk_acc[...] = jnp.zeros_like(dk_acc)
            dv_acc[...] = jnp.zeros_like(dv_acc)
            db_ref[...] = jnp.zeros_like(db_ref)

        kn = [k0[...], k1[...], k2[...]]
        vs = [v0[...], v1[...], v2[...]]
        dk_b = [jnp.zeros((CH_TQ, 128), F32) for _ in range(3)]
        dv_b = [jnp.zeros((CH_TQ, 128), F32) for _ in range(3)]
        for r0 in range(0, CH_TQ, CH_SUB):
            rows = slice(r0, r0 + CH_SUB)
            lo, qh, scores = _ch_logits(i, q_ref, kn, b_ref, rows)
            do = do_ref[rows, :]
            dos = [_mx(jnp.where(lo, do, 0.0)), _mx(jnp.where(lo, 0.0, do))]
            dqn = []
            for h in range(2):
                pr = [jnp.exp(sb - lse_ref[0, rows, h:h + 1]) for sb in scores[h]]
                dp = [_dg(dos[h], vs[b], NT) for b in range(3)]
                delta = jnp.sum((pr[0] * dp[0] + pr[1] * dp[1]) + pr[2] * dp[2], axis=-1, keepdims=True)
                ds = [pr[b] * (dp[b] - delta) for b in range(3)]
                acc = jnp.zeros((CH_SUB, 128), F32)
                for b in range(3):
                    if b > 0:
                        db_ref[h, b - 1, rows, :] += ds[b]
                    dsm = _mx(ds[b])
                    acc = acc + _dot(dsm, kn[b])
                    dk_b[b] = dk_b[b] + _dg(dsm, qh[h], TN)
                    dv_b[b] = dv_b[b] + _dg(pr[b], dos[h], TN)
                dqn.append(acc)
            dq_ref[rows, :] = 0.125 * jnp.where(lo, dqn[0], dqn[1])

        for b in range(3):
            @pl.when(i - 2 + b >= 0)
            def _(b=b):
                rows = pl.ds(pl.multiple_of((i - 2 + b) * CH_TQ, CH_TQ), CH_TQ)
                dk_acc[rows, :] += dk_b[b]
                dv_acc[rows, :] += dv_b[b]

        @pl.when(i == nt - 1)
        def _():
            cols = pl.ds(pl.multiple_of(p * 128, 128), 128)
            ck = pltpu.make_async_copy(dk_acc, dk_hbm.at[:, cols], sem.at[0])
            cv = pltpu.make_async_copy(dv_acc, dv_hbm.at[:, cols], sem.at[1])
            ck.start()
            cv.start()
            ck.wait()
            cv.wait()

        if n:
            @pl.when(jnp.logical_and(p == 3, i == nt - 1))
            def _():
                for cp in copies:
                    cp.wait()

    return pl.pallas_call(
        body, name="ch_bwd_exchange" if n else "ch_bwd", grid=(4, nt), in_specs=specs + [ANY] * n,
        out_specs=[pl.BlockSpec((CH_TQ, 128), lambda p, i: (i, p)),
                   pl.BlockSpec((2, 2, CH_TQ, CH_TQ), lambda p, i: (p, 0, 0, 0)), ANY, ANY] + [ANY] * n,
        out_shape=[jax.ShapeDtypeStruct((S, 512), F32), jax.ShapeDtypeStruct((N_HEADS_CH, 2, CH_TQ, CH_TQ), F32),
                   jax.ShapeDtypeStruct((S, 512), F32), jax.ShapeDtypeStruct((S, 512), F32)] + (hosted.out_shape if n else []),
        scratch_shapes=[pltpu.VMEM((S, 128), F32), pltpu.VMEM((S, 128), F32), pltpu.SemaphoreType.DMA((2,))] + (hosted.sems() if n else []),
        compiler_params=_cp(("arbitrary", "arbitrary"), 48),
    )(pn, pn, pn, pn, pv, pv, pv, bias, lse, dycat, *(hosted.srcs if n else []))


def _col_sum(d):
    _, B, R, C = d.shape

    def body(d_ref, o_ref):
        for b in range(B):
            o_ref[0, b] = jnp.sum(d_ref[0, b], axis=0, keepdims=True)

    return pl.pallas_call(
        body, name="col_sum", grid=(N_HEADS_CH,),
        in_specs=[pl.BlockSpec((1, B, R, C), lambda h: (h, 0, 0, 0))],
        out_specs=pl.BlockSpec((1, B, 1, C), lambda h: (h, 0, 0, 0)),
        out_shape=jax.ShapeDtypeStruct((N_HEADS_CH, B, 1, C), F32),
        compiler_params=_cp(("parallel",), 32),
    )(d)


def _conv_fwd(gc, xc, hc_prev, cw_ref):
    hc = gc * xc
    ext = jnp.concatenate([hc_prev, hc], axis=0)
    r1 = pltpu.roll(ext, 1, 0)[8:]
    r2 = pltpu.roll(ext, 2, 0)[8:]
    return cw_ref[0:1, :] * r2 + cw_ref[1:2, :] * r1 + cw_ref[2:3, :] * hc, hc, r1, r2


def _merge_out(x, ysb, ych, pc, cw, onw, w_out):
    S = x.shape[0]
    tm = min(512, S)
    t8 = tm // 8

    def body(x_ref, ysb_ref, ych_ref, gb_ref, gc_ref, xc_ref, gch_ref, xch_ref, cw_ref, onw_ref, w_ref, xo_ref, yn_ref, ycv_ref):
        i = pl.program_id(0)
        lo = _lo_mask(tm)
        hc_prev = jnp.where(i > 0, gch_ref[...] * xch_ref[...], 0.0)
        conv, _, _, _ = _conv_fwd(gc_ref[...], xc_ref[...], hc_prev, cw_ref)
        ycv = gb_ref[...] * conv
        ycv_ref[...] = ycv
        chunks = [ysb_ref[:, 0:128], ysb_ref[:, 128:256]] + [ych_ref[:, 128 * k:128 * (k + 1)] for k in range(4)]
        chunks += [ycv[:, 0:128], ycv[:, 128:256]]
        for k, ch in enumerate(chunks):
            yn_ref[:, 128 * k:128 * (k + 1)] = _half_norm(ch, onw_ref[:, 128 * k:128 * (k + 1)], lo).astype(yn_ref.dtype)
        xo_ref[...] = x_ref[...] + jnp.dot(yn_ref[...], w_ref[...], preferred_element_type=F32)

    row = lambda i: (i, 0)
    halo = lambda cb: pl.BlockSpec((8, 256), lambda i: (jnp.maximum(i * t8 - 1, 0), cb))
    return pl.pallas_call(
        body, name="merge_out", grid=(S // tm,),
        in_specs=[pl.BlockSpec((tm, D), row), pl.BlockSpec((tm, 256), row), pl.BlockSpec((tm, 512), row),
                  pl.BlockSpec((tm, 256), lambda i: (i, 0)), pl.BlockSpec((tm, 256), lambda i: (i, 1)),
                  pl.BlockSpec((tm, 256), lambda i: (i, 2)), halo(1), halo(2),
                  pl.BlockSpec((3, 256), lambda i: (0, 0)), pl.BlockSpec((1, D), lambda i: (0, 0)), pl.BlockSpec((D, D), lambda i: (0, 0))],
        out_specs=[pl.BlockSpec((tm, D), row), pl.BlockSpec((tm, D), row), pl.BlockSpec((tm, 256), row)],
        out_shape=[jax.ShapeDtypeStruct((S, D), F32), jax.ShapeDtypeStruct((S, D), _MXU), jax.ShapeDtypeStruct((S, 256), F32)],
        compiler_params=_cp(("parallel",), 40),
    )(x, ysb, ych, pc, pc, pc, pc, pc, cw, onw, w_out)


def _ffn_fwd(x, nw, wg, wu, wd, nxt=(), split=(), target=None):
    S = x.shape[0]
    tm = min(1024, S)
    ni = S // tm
    n = len(nxt)
    m = 0 if target is None else 1
    o0 = 5 + m + n

    def body(*refs):
        x_ref, nw_ref, wg_ref, wu_ref, wd_ref = refs[:5]
        xo_ref, hn_ref, u_ref, t_ref, a_ref = refs[o0:o0 + 5]
        acc = refs[o0 + 5 + m + n]
        i, j = pl.program_id(0), pl.program_id(1)
        if n:
            gather = _Gather(refs[5 + m:5 + m + n], refs[o0 + 5 + m:o0 + 5 + m + n], refs[o0 + 6 + m + n:], split)

            @pl.when(jnp.logical_and(i == 0, j == 0))
            def _():
                gather.start()

            @pl.when(jnp.logical_and(i == max(ni - 2, 0), j == 0))
            def _():
                gather.forward()

        @pl.when(j == 0)
        def _():
            hn_ref[...] = _row_norm(x_ref[...], nw_ref[...]).astype(hn_ref.dtype)
            acc[...] = jnp.zeros_like(acc)

        for r in range(0, tm, min(FFN_SUB, tm)):
            rows = slice(r, r + min(FFN_SUB, tm))
            hn = hn_ref[rows, :]
            u = jnp.dot(hn, wg_ref[0], preferred_element_type=F32)
            t = jnp.dot(hn, wu_ref[0], preferred_element_type=F32)
            a = (u * _sigmoid(u) * t).astype(a_ref.dtype)
            u_ref[0, rows, :], t_ref[0, rows, :], a_ref[0, rows, :] = u.astype(u_ref.dtype), t.astype(t_ref.dtype), a
            acc[rows, :] += jnp.dot(a, wd_ref[0], preferred_element_type=F32)

        @pl.when(j == N_SH - 1)
        def _():
            if m:
                tgt_ref, l_ref = refs[5], refs[o0 + 5]

                @pl.when(i == 0)
                def _():
                    l_ref[...] = jnp.zeros_like(l_ref)

                e = (x_ref[...] + acc[...]) - tgt_ref[...]
                xo_ref[...] = e * (1.0 / D)
                l_ref[...] += 0.5 * jnp.sum(jnp.sum(e * e, axis=-1, keepdims=True) * (1.0 / D))
            else:
                xo_ref[...] = x_ref[...] + acc[...]

        if n:
            @pl.when(jnp.logical_and(i == ni - 1, j == N_SH - 1))
            def _():
                gather.finish()

    row = lambda i, j: (i, 0)
    sh = lambda i, j: (j, i, 0)
    return pl.pallas_call(
        body, name="ffn_fwd_gather" if n else ("ffn_fwd_loss" if m else "ffn_fwd"), grid=(ni, N_SH),
        in_specs=[pl.BlockSpec((tm, D), row), pl.BlockSpec((1, D), lambda i, j: (0, 0)),
                  pl.BlockSpec((1, D, FF_SH), lambda i, j: (j, 0, 0)), pl.BlockSpec((1, D, FF_SH), lambda i, j: (j, 0, 0)),
                  pl.BlockSpec((1, FF_SH, D), lambda i, j: (j, 0, 0))] + [pl.BlockSpec((tm, D), row)] * m + [ANY] * n,
        out_specs=[pl.BlockSpec((tm, D), row), pl.BlockSpec((tm, D), row), pl.BlockSpec((1, tm, FF_SH), sh),
                   pl.BlockSpec((1, tm, FF_SH), sh), pl.BlockSpec((1, tm, FF_SH), sh)] + [pl.BlockSpec((8, 128), lambda i, j: (0, 0))] * m + [ANY] * n,
        out_shape=[jax.ShapeDtypeStruct((S, D), F32), jax.ShapeDtypeStruct((S, D), _MXU),
                   jax.ShapeDtypeStruct((N_SH, S, FF_SH), _MXU), jax.ShapeDtypeStruct((N_SH, S, FF_SH), _MXU),
                   jax.ShapeDtypeStruct((N_SH, S, FF_SH), _MXU)] + [jax.ShapeDtypeStruct((8, 128), F32)] * m + _Gather.out_shape(nxt),
        scratch_shapes=[pltpu.VMEM((tm, D), F32)] + (_Gather.sems(n) if n else []),
        compiler_params=_cp(("arbitrary", "arbitrary"), 60),
    )(x, nw, wg, wu, wd, *([target] * m), *nxt)


def _loss_head(x, target):
    S = x.shape[0]
    tm = min(512, S)

    def body(x_ref, t_ref, dx_ref, l_ref):
        @pl.when(pl.program_id(0) == 0)
        def _():
            l_ref[...] = jnp.zeros_like(l_ref)

        e = x_ref[...] - t_ref[...]
        dx_ref[...] = e * (1.0 / D)
        l_ref[...] += 0.5 * jnp.sum(jnp.sum(e * e, axis=-1, keepdims=True) * (1.0 / D))

    return pl.pallas_call(
        body, name="loss_head", grid=(S // tm,),
        in_specs=[pl.BlockSpec((tm, D), lambda i: (i, 0))] * 2,
        out_specs=[pl.BlockSpec((tm, D), lambda i: (i, 0)), pl.BlockSpec((8, 128), lambda i: (0, 0))],
        out_shape=[jax.ShapeDtypeStruct((S, D), F32), jax.ShapeDtypeStruct((8, 128), F32)],
        compiler_params=_cp(("arbitrary",), 32),
    )(x, target)


def _ffn_bwd(dx, xmid, nw, wg, wu, wd, u, t, hosted=None):
    S = dx.shape[0]
    tm = min(1024, S)
    ni = S // tm
    n = len(hosted.srcs) if hosted else 0

    def body(*refs):
        dx_ref, x_ref, nw_ref, wg_ref, wu_ref, wd_ref, u_ref, t_ref = refs[:8]
        du_ref, dt_ref, dxm_ref, dnw_ref = refs[8 + n:12 + n]
        acc = refs[12 + 2 * n]
        i, j = pl.program_id(0), pl.program_id(1)
        if n:
            copies = hosted.make(refs[8:8 + n], refs[12 + n:12 + 2 * n], *refs[13 + 2 * n:])

            @pl.when(jnp.logical_and(i == 0, j == 0))
            def _():
                for cp in copies:
                    cp.start()

        @pl.when(j == 0)
        def _():
            acc[...] = jnp.zeros_like(acc)

        @pl.when(jnp.logical_and(i == 0, j == 0))
        def _():
            dnw_ref[...] = jnp.zeros_like(dnw_ref)

        for r in range(0, tm, min(FFN_SUB, tm)):
            rows = slice(r, r + min(FFN_SUB, tm))
            da = _dg(dx_ref[rows, :], wd_ref[0], NT)
            u, t = u_ref[0, rows, :].astype(F32), t_ref[0, rows, :].astype(F32)
            sg = _sigmoid(u)
            du = _mx(da * t * (sg * (1.0 + u * (1.0 - sg))))
            dt = _mx(da * (u * sg))
            du_ref[0, rows, :], dt_ref[0, rows, :] = du, dt
            acc[rows, :] += _dg(du, wg_ref[0], NT) + _dg(dt, wu_ref[0], NT)

        @pl.when(j == N_SH - 1)
        def _():
            dxn, dw = _row_norm_bwd(x_ref[...], nw_ref[...], acc[...])
            dxm_ref[...] = dx_ref[...] + dxn
            dnw_ref[0:1, :] += dw

        if n:
            @pl.when(jnp.logical_and(i == ni - 1, j == N_SH - 1))
            def _():
                for cp in copies:
                    cp.wait()

    row = lambda i, j: (i, 0)
    sh = lambda i, j: (j, i, 0)
    wsp = lambda r, c: pl.BlockSpec((1, r, c), lambda i, j: (j, 0, 0))
    return pl.pallas_call(
        body, name="ffn_bwd_exchange" if n else "ffn_bwd", grid=(ni, N_SH),
        in_specs=[pl.BlockSpec((tm, D), row), pl.BlockSpec((tm, D), row), pl.BlockSpec((1, D), lambda i, j: (0, 0)),
                  wsp(D, FF_SH), wsp(D, FF_SH), wsp(FF_SH, D), pl.BlockSpec((1, tm, FF_SH), sh), pl.BlockSpec((1, tm, FF_SH), sh)] + [ANY] * n,
        out_specs=[pl.BlockSpec((1, tm, FF_SH), sh), pl.BlockSpec((1, tm, FF_SH), sh), pl.BlockSpec((tm, D), row),
                   pl.BlockSpec((8, D), lambda i, j: (0, 0))] + [ANY] * n,
        out_shape=[jax.ShapeDtypeStruct((N_SH, S, FF_SH), _MXU), jax.ShapeDtypeStruct((N_SH, S, FF_SH), _MXU),
                   jax.ShapeDtypeStruct((S, D), F32), jax.ShapeDtypeStruct((8, D), F32)] + (hosted.out_shape if n else []),
        scratch_shapes=[pltpu.VMEM((tm, D), F32)] + (hosted.sems() if n else []),
        compiler_params=_cp(("arbitrary", "arbitrary"), 60),
    )(dx, xmid, nw, wg, wu, wd, u, t, *(hosted.srcs if n else []))


def _wgrad(a, b, a_spec, b_spec, out_block, name):
    S = a.shape[-2]
    tk = min(1024, S)

    def body(a_ref, b_ref, o_ref):
        @pl.when(pl.program_id(0) == 0)
        def _():
            o_ref[...] = jnp.zeros_like(o_ref)

        for r in range(N_SH):
            o_ref[r] += _dg(a_spec[1](a_ref, r), b_spec[1](b_ref, r), TN)

    return pl.pallas_call(
        body, name=name, grid=(S // tk,),
        in_specs=[a_spec[0](tk), b_spec[0](tk)],
        out_specs=pl.BlockSpec((N_SH,) + out_block, lambda k: (0, 0, 0)),
        out_shape=jax.ShapeDtypeStruct((N_SH,) + out_block, F32),
        compiler_params=_cp(("arbitrary",), 56),
    )(a, b)


def _tok(width):
    return (lambda tk: pl.BlockSpec((tk, width), lambda k: (k, 0))), (lambda ref, r: ref[...])


def _tok_cols(width):
    return (lambda tk: pl.BlockSpec((tk, N_SH * width), lambda k: (k, 0))), (lambda ref, r: ref[:, r * width:(r + 1) * width])


def _tok_sh(width):
    return (lambda tk: pl.BlockSpec((N_SH, tk, width), lambda k: (0, k, 0))), (lambda ref, r: ref[r])


def _out_bwd(dxm, w_out, ysb, ych, ycv, onw):
    S = dxm.shape[0]
    tm = min(512, S)

    def body(dx_ref, w_ref, ysb_ref, ych_ref, ycv_ref, onw_ref, dy_ref, dw_ref):
        @pl.when(pl.program_id(0) == 0)
        def _():
            dw_ref[...] = jnp.zeros_like(dw_ref)

        lo = _lo_mask(tm)
        dyn = _dg(dx_ref[...], w_ref[...], NT)
        chunks = [ysb_ref[:, 0:128], ysb_ref[:, 128:256]] + [ych_ref[:, 128 * k:128 * (k + 1)] for k in range(4)]
        chunks += [ycv_ref[:, 0:128], ycv_ref[:, 128:256]]
        for k, ch in enumerate(chunks):
            sl = slice(128 * k, 128 * (k + 1))
            d, dw = _half_norm_bwd(ch, onw_ref[:, sl], dyn[:, sl], lo)
            dy_ref[:, sl] = d
            dw_ref[0:1, sl] += dw

    row = lambda i: (i, 0)
    return pl.pallas_call(
        body, name="out_bwd", grid=(S // tm,),
        in_specs=[pl.BlockSpec((tm, D), row), pl.BlockSpec((D, D), lambda i: (0, 0)), pl.BlockSpec((tm, 256), row),
                  pl.BlockSpec((tm, 512), row), pl.BlockSpec((tm, 256), row), pl.BlockSpec((1, D), lambda i: (0, 0))],
        out_specs=[pl.BlockSpec((tm, D), row), pl.BlockSpec((8, D), lambda i: (0, 0))],
        out_shape=[jax.ShapeDtypeStruct((S, D), F32), jax.ShapeDtypeStruct((8, D), F32)],
        compiler_params=_cp(("arbitrary",), 40),
    )(dxm, w_out, ysb, ych, ycv, onw)


def _in_bwd(dxm, x, anw, w_sh, pqk, pc, dqa, dka, dva, dqh, dkh, dvb, dycat, cw, qw2, kw2):
    S = x.shape[0]
    tm = min(256, S)
    t8 = tm // 8
    last8 = S // 8 - 1

    def body(dxm_ref, x_ref, anw_ref, w_ref, q0_ref, q1_ref, k0_ref, k1_ref, gb_ref, gc_ref, xc_ref, gch_ref, xch_ref, gbn_ref,
             dqa_ref, dka_ref, dva_ref, dqh_ref, dkh_ref, dvb_ref, dyc_ref, dycn_ref, cw_ref, qw_ref, kw_ref,
             dp_ref, dx_ref, danw_ref, dqw_ref, dkw_ref, dcw_ref):
        i = pl.program_id(0)
        n = pl.num_programs(0)

        @pl.when(i == 0)
        def _():
            for r in (danw_ref, dqw_ref, dkw_ref, dcw_ref):
                r[...] = jnp.zeros_like(r)

        lo = _lo_mask(tm)
        dp_ref[:, QA:QA + 256] = dqa_ref[...].astype(dp_ref.dtype)
        dp_ref[:, KA:KA + 256] = dka_ref[...].astype(dp_ref.dtype)
        dp_ref[:, VA:VA + 256] = dva_ref[...].astype(dp_ref.dtype)
        dp_ref[:, VB:VB + 512] = dvb_ref[...].astype(dp_ref.dtype)
        for base, raws, d_ref, nw_ref, dw_ref in ((QB, (q0_ref, q1_ref), dqh_ref, qw_ref, dqw_ref), (KB, (k0_ref, k1_ref), dkh_ref, kw_ref, dkw_ref)):
            for k in range(4):
                raw = raws[k // 2][:, 128 * (k % 2):128 * (k % 2 + 1)]
                d, dw = _half_norm_bwd(raw, nw_ref[...], d_ref[:, 128 * k:128 * (k + 1)], lo)
                dp_ref[:, base + 128 * k:base + 128 * (k + 1)] = d.astype(dp_ref.dtype)
                dw_ref[0:1, :] += dw
        gb, gc, xc = gb_ref[...], gc_ref[...], xc_ref[...]
        hc_prev = jnp.where(i > 0, gch_ref[...] * xch_ref[...], 0.0)
        conv, hc, r1, r2 = _conv_fwd(gc, xc, hc_prev, cw_ref)
        dyc = dyc_ref[...]
        dconv = dyc * gb
        dconv_next = jnp.where(i < n - 1, dycn_ref[...] * gbn_ref[...], 0.0)
        ext = jnp.concatenate([dconv, dconv_next], axis=0)
        l1 = pltpu.roll(ext, tm + 7, 0)[:tm]
        l2 = pltpu.roll(ext, tm + 6, 0)[:tm]
        dhc = cw_ref[2:3, :] * dconv + cw_ref[1:2, :] * l1 + cw_ref[0:1, :] * l2
        dp_ref[:, GB:GB + 256] = (dyc * conv).astype(dp_ref.dtype)
        dp_ref[:, GC:GC + 256] = (dhc * xc).astype(dp_ref.dtype)
        dp_ref[:, XC:XC + 256] = (dhc * gc).astype(dp_ref.dtype)
        dcw_ref[0:1, :] += jnp.sum(dconv * r2, axis=0, keepdims=True)
        dcw_ref[1:2, :] += jnp.sum(dconv * r1, axis=0, keepdims=True)
        dcw_ref[2:3, :] += jnp.sum(dconv * hc, axis=0, keepdims=True)
        dh = jnp.zeros((tm, D), F32)
        for r in range(N_SH):
            dh = dh + lax.dot_general(dp_ref[:, IN_SH * r:IN_SH * (r + 1)], w_ref[r], NT, preferred_element_type=F32)
        dxn, dw = _row_norm_bwd(x_ref[...], anw_ref[...], dh)
        dx_ref[...] = dxm_ref[...] + dxn
        danw_ref[0:1, :] += dw

    row = lambda i: (i, 0)
    colb = lambda w, cb: pl.BlockSpec((tm, w), lambda i: (i, cb))
    prev = lambda cb: pl.BlockSpec((8, 256), lambda i: (jnp.maximum(i * t8 - 1, 0), cb))
    nxt = lambda cb: pl.BlockSpec((8, 256), lambda i: (jnp.minimum((i + 1) * t8, last8), cb))
    const = lambda shp: pl.BlockSpec(shp, lambda i: (0,) * len(shp))
    return pl.pallas_call(
        body, name="in_bwd", grid=(S // tm,),
        in_specs=[pl.BlockSpec((tm, D), row), pl.BlockSpec((tm, D), row), const((1, D)), const((N_SH, D, IN_SH)),
                  colb(256, 0), colb(256, 1), colb(256, 2), colb(256, 3),
                  colb(256, 0), colb(256, 1), colb(256, 2), prev(1), prev(2), nxt(0),
                  pl.BlockSpec((tm, 256), row), pl.BlockSpec((tm, 256), row), pl.BlockSpec((tm, 256), row),
                  pl.BlockSpec((tm, 512), row), pl.BlockSpec((tm, 512), row), pl.BlockSpec((tm, 512), row),
                  colb(256, 768 // 256), nxt(768 // 256), const((3, 256)), const((1, 128)), const((1, 128))],
        out_specs=[pl.BlockSpec((tm, D_IN), row), pl.BlockSpec((tm, D), row), const((8, D)), const((8, 128)), const((8, 128)), const((8, 256))],
        out_shape=[jax.ShapeDtypeStruct((S, D_IN), _MXU), jax.ShapeDtypeStruct((S, D), F32), jax.ShapeDtypeStruct((8, D), F32),
                   jax.ShapeDtypeStruct((8, 128), F32), jax.ShapeDtypeStruct((8, 128), F32), jax.ShapeDtypeStruct((8, 256), F32)],
        compiler_params=_cp(("arbitrary",), 56),
    )(dxm, x, anw, w_sh, pqk, pqk, pqk, pqk, pc, pc, pc, pc, pc, pc, dqa, dka, dva, dqh, dkh, dvb, dycat, dycat, cw, qw2, kw2)


def _rows_tile(h):
    return h if h <= 512 else 256


def _add_halves(g, r1, c_arr):
    _, R, C = g.shape
    h = R // 2
    tr = _rows_tile(h)
    nb = h // tr

    def body(c_ref, g_ref, r_ref, o_ref, ob_ref):
        s = g_ref[...] + r_ref[...]
        o_ref[...] = s
        ob_ref[...] = s.astype(ob_ref.dtype)

    blk = (1, tr, C)
    return pl.pallas_call(
        body, name="add_halves",
        grid_spec=pltpu.PrefetchScalarGridSpec(
            num_scalar_prefetch=1, grid=(N_SH, nb),
            in_specs=[pl.BlockSpec(blk, lambda s, i, c: (s, c[0] * nb + i, 0)), pl.BlockSpec(blk, lambda s, i, c: (s, i, 0))],
            out_specs=[pl.BlockSpec(blk, lambda s, i, c: (s, i, 0))] * 2),
        out_shape=[jax.ShapeDtypeStruct((N_SH, h, C), F32), jax.ShapeDtypeStruct((N_SH, h, C), jnp.bfloat16)],
        compiler_params=_cp(("parallel", "parallel"), 32),
    )(c_arr, g, r1)


def _add_shards(s1, r2, me_arr):
    _, h, C = s1.shape
    tr = _rows_tile(h)

    def body(me_ref, s_ref, r_ref, o_ref):
        o_ref[...] = ((s_ref[0] + r_ref[0].astype(F32)) + r_ref[1].astype(F32)) + r_ref[2].astype(F32)

    return pl.pallas_call(
        body, name="add_shards",
        grid_spec=pltpu.PrefetchScalarGridSpec(
            num_scalar_prefetch=1, grid=(h // tr,),
            in_specs=[pl.BlockSpec((1, tr, C), lambda i, me: (me[0], i, 0)), pl.BlockSpec((3, tr, C), lambda i, me: (0, i, 0))],
            out_specs=pl.BlockSpec((tr, C), lambda i, me: (i, 0))),
        out_shape=jax.ShapeDtypeStruct((h, C), F32),
        compiler_params=_cp(("parallel",), 32),
    )(me_arr, s1, r2)


def _adam_math(w, g, m, v):
    m = ADAM_B1 * m + (1.0 - ADAM_B1) * g
    v = ADAM_B2 * v + (1.0 - ADAM_B2) * (g * g)
    m_hat = m / (1.0 - ADAM_B1 ** ADAM_STEP)
    v_hat = v / (1.0 - ADAM_B2 ** ADAM_STEP)
    return -ADAM_LR * (m_hat / (jnp.sqrt(v_hat) + ADAM_EPS) + ADAM_WD * w), m, v


def _adam_big(w, m, v, mine, theirs, c_arr):
    L, R, C = w.shape
    h = R // 2
    tr = _rows_tile(h)
    nb = h // tr

    def body(c_ref, w_ref, m_ref, v_ref, a_ref, b_ref, g_ref, d_ref, mo_ref, vo_ref):
        g = jnp.where(pl.program_id(1) == c_ref[0], a_ref[...], b_ref[...])
        g_ref[...] = g
        d_ref[...], mo_ref[...], vo_ref[...] = _adam_math(w_ref[...], g, m_ref[...], v_ref[...])

    full = pl.BlockSpec((1, tr, C), lambda l, s, i, c: (l, s * nb + i, 0))
    half = pl.BlockSpec((1, tr, C), lambda l, s, i, c: (l, i, 0))
    return pl.pallas_call(
        body, name="adam_big",
        grid_spec=pltpu.PrefetchScalarGridSpec(num_scalar_prefetch=1, grid=(L, 2, nb), in_specs=[full, full, full, half, half],
                                               out_specs=[full] * 4),
        out_shape=[jax.ShapeDtypeStruct((L, R, C), F32)] * 4,
        compiler_params=_cp(("parallel", "parallel", "parallel"), 32),
    )(c_arr, w, m, v, mine, theirs)


def _adam_small(w, g, m, v):
    def body(w_ref, g_ref, m_ref, v_ref, d_ref, mo_ref, vo_ref):
        d_ref[...], mo_ref[...], vo_ref[...] = _adam_math(w_ref[...], g_ref[...], m_ref[...], v_ref[...])

    return pl.pallas_call(body, name="adam_small", out_shape=[jax.ShapeDtypeStruct(w.shape, F32)] * 3)(w, g, m, v)


def _allreduce_small(buf):
    R = buf.shape[0]

    def body(x_ref, o_ref, gat, ssem, rsem):
        x, y, c = lax.axis_index("x"), lax.axis_index("y"), lax.axis_index("c")
        me = 4 * x + 2 * y + c
        gat[me] = x_ref[...]
        cps = []
        for k in range(1, 8):
            peer = (x ^ (k >> 2), y ^ ((k >> 1) & 1), c ^ (k & 1))
            cps.append(pltpu.make_async_remote_copy(src_ref=x_ref, dst_ref=gat.at[me], send_sem=ssem.at[k - 1], recv_sem=rsem.at[k - 1],
                                                    device_id=peer, device_id_type=MESH))
        for cp in cps:
            cp.start()
        for cp in cps:
            cp.wait()
        acc = gat[0]
        for s in range(1, 8):
            acc = acc + gat[s]
        o_ref[...] = acc

    return pl.pallas_call(
        body, name="allreduce_small", out_shape=jax.ShapeDtypeStruct((R, 128), F32),
        in_specs=[pl.BlockSpec(memory_space=pltpu.VMEM)], out_specs=pl.BlockSpec(memory_space=pltpu.VMEM),
        scratch_shapes=[pltpu.VMEM((8, R, 128), F32), pltpu.SemaphoreType.DMA((7,)), pltpu.SemaphoreType.DMA((7,))],
    )(buf)


class _Exchange:
    def __init__(self, srcs, out_shape, n_copies, make):
        self.srcs, self.out_shape, self.n_copies, self.make = list(srcs), list(out_shape), n_copies, make

    def sems(self):
        return [pltpu.SemaphoreType.DMA((self.n_copies,))] * 2

    def run(self, name):
        n = len(self.srcs)

        def body(*refs):
            copies = self.make(refs[:n], refs[n:2 * n], *refs[2 * n:])
            for cp in copies:
                cp.start()
            for cp in copies:
                cp.wait()

        return pl.pallas_call(body, name=name, out_shape=self.out_shape, in_specs=[ANY] * n, out_specs=[ANY] * n,
                              scratch_shapes=self.sems())(*self.srcs)


def _to_sibling(srcs, pick, shapes):
    def make(ins, outs, ssem, rsem):
        x, y, c = lax.axis_index("x"), lax.axis_index("y"), lax.axis_index("c")
        return [pltpu.make_async_remote_copy(src_ref=pick(ins[f], f, c), dst_ref=outs[f], send_sem=ssem.at[f], recv_sem=rsem.at[f],
                                             device_id=(x, y, 1 - c), device_id_type=MESH) for f in range(len(ins))]

    return _Exchange(srcs, [jax.ShapeDtypeStruct(shp, F32) for shp in shapes], len(srcs), make)


def _to_chips(srcs):
    def make(ins, outs, ssem, rsem):
        x, y, c = lax.axis_index("x"), lax.axis_index("y"), lax.axis_index("c")
        return [pltpu.make_async_remote_copy(src_ref=ins[f].at[2 * px + py], dst_ref=outs[f].at[k], send_sem=ssem.at[3 * f + k],
                                             recv_sem=rsem.at[3 * f + k], device_id=(px, py, c), device_id_type=MESH)
                for f in range(len(ins)) for k, (px, py) in enumerate([(1 - x, y), (x, 1 - y), (1 - x, 1 - y)])]

    return _Exchange(srcs, [jax.ShapeDtypeStruct((3,) + s.shape[1:], s.dtype) for s in srcs], 3 * len(srcs), make)


class _GradReducer:
    def __init__(self):
        x, y, c = lax.axis_index("x"), lax.axis_index("y"), lax.axis_index("c")
        self.c_arr = jnp.reshape(c, (1,)).astype(jnp.int32)
        self.me_arr = jnp.reshape(2 * x + y, (1,)).astype(jnp.int32)
        self.fresh, self.pending, self.done = None, None, {}

    def start(self, l, grads):
        self.fresh = (l, grads)

    def halves(self):
        if self.fresh is None:
            return None
        grads = self.fresh[1]
        hs = [g.shape[1] // 2 for g in grads]
        return _to_sibling(grads, lambda ref, f, c: ref.at[:, pl.ds(pl.multiple_of((1 - c) * hs[f], 8), hs[f]), :],
                           [(N_SH, h, g.shape[2]) for g, h in zip(grads, hs)])

    def got_halves(self, r1):
        l, grads = self.fresh
        self.fresh = None
        s1, wire = zip(*[_add_halves(g, r, self.c_arr) for g, r in zip(grads, r1)])
        self.pending = (l, s1, wire)

    def shards(self):
        return None if self.pending is None else _to_chips(self.pending[2])

    def got_shards(self, r2):
        l, s1, _ = self.pending
        self.pending = None
        mine = [_add_shards(s, r, self.me_arr) for s, r in zip(s1, r2)]
        theirs = _to_sibling(mine, lambda ref, f, c: ref, [m.shape for m in mine]).run("rs_sibling_reduced")
        self.done[l] = (mine, theirs)

    def finish(self):
        if self.pending is not None:
            self.got_shards(self.shards().run("rs_chip_shards"))
        if self.fresh is not None:
            self.got_halves(self.halves().run("rs_sibling_halves"))
            self.got_shards(self.shards().run("rs_chip_shards"))


def _pad_rows(flat, rows):
    return jnp.pad(flat, (0, rows * 128 - flat.shape[0])).reshape(rows, 128)


class _ShardedWeights:
    split = [True] * 5 + [False]

    def __init__(self, w):
        self.parts = [[w[k][l].astype(_MXU) for k in _BIG] + [jnp.zeros((8, 128), F32).at[:3, :HD].set(w["conv_w"][l])]
                      for l in range(DEPTH)]

    def first(self):
        return {"w_in": _allgather_layer(self.parts[0][:1], self.split[:1])[0]}

    def rest_of_first(self):
        return self.parts[0][1:], self.split[1:]

    def following(self, l):
        return (self.parts[l + 1], self.split) if l + 1 < DEPTH else ((), ())

    @staticmethod
    def unpack(got):
        W = dict(zip(_BIG[len(_BIG) + 1 - len(got):], got[:-1]))
        W["conv_w"] = jnp.concatenate([got[-1][r, :3, :HD] for r in range(N_SH)], axis=-1)
        return W


def _local_step(x, target, small, weights, reducer):
    S = x.shape[0]
    saved, big = [], []
    W = weights.first()
    for l in range(DEPTH):
        big.append(W)
        anw, onw, fnw = small["attn_norm_w"][l][None], small["out_norm_w"][l][None], small["ffn_norm_w"][l][None]
        qw2 = jnp.tile(small["q_norm_w"][l], 2)[None]
        kw2 = jnp.tile(small["k_norm_w"][l], 2)[None]
        bias = _bias_tiles(small["rel_bias"][l])
        h, pa, pqk, pn, pv, pc = _attn_in(x, anw, W["w_in"], qw2, kw2)
        ysb = _sb_fwd(pa)
        outs = _ch_fwd(pn, pv, bias, *(weights.rest_of_first() if l == 0 else ((), ())))
        ych, lse = outs[:2]
        if l == 0:
            W.update(weights.unpack(outs[2:]))
        cw = W["conv_w"]
        w_out = W["w_out"].reshape(D, D)
        xmid, yn, ycv = _merge_out(x, ysb, ych, pc, cw, onw, w_out)
        outs = _ffn_fwd(xmid, fnw, W["w_gate"], W["w_up"], W["w_down"], *weights.following(l), target=target if l + 1 == DEPTH else None)
        xo, hn, u, t, a = outs[:5]
        saved.append(dict(x=x, h=h, pa=pa, pqk=pqk, pn=pn, pv=pv, pc=pc, ysb=ysb, ych=ych, lse=lse, ycv=ycv, yn=yn, xmid=xmid, hn=hn, u=u, t=t, a=a,
                          anw=anw, onw=onw, fnw=fnw, qw2=qw2, kw2=kw2, cw=cw, bias=bias, w_out=w_out))
        x = xo
        if l + 1 < DEPTH:
            W = weights.unpack(outs[5:])
    dx, loss = x, outs[5][0, 0]
    grads = [None] * DEPTH
    for l in reversed(range(DEPTH)):
        W, sv = big[l], saved[l]
        outs = _ffn_bwd(dx, sv["xmid"], sv["fnw"], W["w_gate"], W["w_up"], W["w_down"], sv["u"], sv["t"], reducer.halves())
        du, dt, dxm, dfnw = outs[:4]
        if outs[4:]:
            reducer.got_halves(outs[4:])
        g_gate = _wgrad(sv["hn"], du, _tok(D), _tok_sh(FF_SH), (D, FF_SH), "wgrad_gate")
        g_up = _wgrad(sv["hn"], dt, _tok(D), _tok_sh(FF_SH), (D, FF_SH), "wgrad_up")
        g_down = _wgrad(sv["a"], dx, _tok_sh(FF_SH), _tok(D), (FF_SH, D), "wgrad_down")
        dycat, donw = _out_bwd(dxm, sv["w_out"], sv["ysb"], sv["ych"], sv["ycv"], sv["onw"])
        g_out = _wgrad(sv["yn"], dxm, _tok_cols(OUT_SH), _tok(D), (OUT_SH, D), "wgrad_out")
        dqa, dka, dva = _sb_bwd(sv["pa"], dycat)
        outs = _ch_bwd(sv["pn"], sv["pv"], sv["bias"], sv["lse"], dycat, reducer.shards())
        dqh, dbias, dkh, dvb = outs[:4]
        if outs[4:]:
            reducer.got_shards(outs[4:])
        drelb = _bias_grad(dbias)
        dproj, dx, danw, dqw, dkw, dcw = _in_bwd(dxm, sv["x"], sv["anw"], W["w_in"], sv["pqk"], sv["pc"], dqa, dka, dva, dqh, dkh, dvb, dycat,
                                                 sv["cw"], sv["qw2"], sv["kw2"])
        g_in = _wgrad(sv["h"], dproj, _tok(D), _tok_cols(IN_SH), (D, IN_SH), "wgrad_in")
        grads[l] = dict(w_in=g_in, w_out=g_out, w_gate=g_gate, w_up=g_up, w_down=g_down,
                        attn_norm_w=danw[0], out_norm_w=donw[0], ffn_norm_w=dfnw[0],
                        q_norm_w=dqw[0, :HD] + dqw[0, HD:], k_norm_w=dkw[0, :HD] + dkw[0, HD:], rel_bias=drelb, conv_w=dcw[:3])
        reducer.start(l, [grads[l][k] for k in _BIG])
    reducer.finish()
    return loss, dx, grads


_BIG = ("w_in", "w_out", "w_gate", "w_up", "w_down")
_SMALL = ("attn_norm_w", "q_norm_w", "k_norm_w", "rel_bias", "out_norm_w", "ffn_norm_w")
_ORDER = ("attn_norm_w", "w_in", "q_norm_w", "k_norm_w", "rel_bias", "conv_w", "out_norm_w", "w_out", "ffn_norm_w", "w_gate", "w_up", "w_down")


def kernel(x, attn_norm_w, w_in, q_norm_w, k_norm_w, rel_bias, conv_w, out_norm_w, w_out, ffn_norm_w, w_gate, w_up, w_down, loss_target, m_attn_norm_w, m_w_in, m_q_norm_w, m_k_norm_w, m_rel_bias, m_conv_w, m_out_norm_w, m_w_out, m_ffn_norm_w, m_w_gate, m_w_up, m_w_down, v_attn_norm_w, v_w_in, v_q_norm_w, v_k_norm_w, v_rel_bias, v_conv_w, v_out_norm_w, v_w_out, v_ffn_norm_w, v_w_gate, v_w_up, v_w_down):
    w = dict(attn_norm_w=attn_norm_w, w_in=w_in, q_norm_w=q_norm_w, k_norm_w=k_norm_w, rel_bias=rel_bias, conv_w=conv_w,
             out_norm_w=out_norm_w, w_out=w_out, ffn_norm_w=ffn_norm_w, w_gate=w_gate, w_up=w_up, w_down=w_down)
    m = dict(attn_norm_w=m_attn_norm_w, w_in=m_w_in, q_norm_w=m_q_norm_w, k_norm_w=m_k_norm_w, rel_bias=m_rel_bias, conv_w=m_conv_w,
             out_norm_w=m_out_norm_w, w_out=m_w_out, ffn_norm_w=m_ffn_norm_w, w_gate=m_w_gate, w_up=m_w_up, w_down=m_w_down)
    v = dict(attn_norm_w=v_attn_norm_w, w_in=v_w_in, q_norm_w=v_q_norm_w, k_norm_w=v_k_norm_w, rel_bias=v_rel_bias, conv_w=v_conv_w,
             out_norm_w=v_out_norm_w, w_out=v_w_out, ffn_norm_w=v_ffn_norm_w, w_gate=v_w_gate, w_up=v_w_up, w_down=v_w_down)
    xi, yi = lax.axis_index("x"), lax.axis_index("y")
    me = 2 * xi + yi

    small = {k: w[k] for k in _SMALL}
    reducer = _GradReducer()
    loss, grad_x, grads = _local_step(x[0], loss_target[0], small, _ShardedWeights(w), reducer)
    loss = lax.psum(loss, ("x", "y", "c"))

    out = {}
    for f, k in enumerate(_BIG):
        mine, theirs = (jnp.stack([reducer.done[l][side][f] for l in range(DEPTH)]) for side in (0, 1))
        out[k] = _adam_big(w[k], m[k], v[k], mine, theirs, reducer.c_arr)

    names = _SMALL + ("conv_w",)
    gflat = jnp.concatenate([jnp.stack([grads[l][k] for l in range(DEPTH)]).reshape(-1) for k in names])
    rows = -(-gflat.shape[0] // 1024) * 8
    gsum = _allreduce_small(_pad_rows(gflat, rows)).reshape(-1)
    gs, off = {}, 0
    for k in names:
        shp = (DEPTH, 3, 4 * HD) if k == "conv_w" else w[k].shape
        size = int(np.prod(shp))
        gs[k] = gsum[off:off + size].reshape(shp)
        off += size
    gs["conv_w"] = lax.dynamic_slice_in_dim(gs["conv_w"], me * HD, HD, axis=2)
    flat = lambda d: jnp.concatenate([d[k].reshape(-1) for k in names])
    rows = -(-flat(w).shape[0] // 1024) * 8
    dsm, msm, vsm = _adam_small(_pad_rows(flat(w), rows), _pad_rows(flat(gs), rows), _pad_rows(flat(m), rows), _pad_rows(flat(v), rows))
    off = 0
    for k in names:
        size = int(np.prod(w[k].shape))
        out[k] = (gs[k],) + tuple(a.reshape(-1)[off:off + size].reshape(w[k].shape) for a in (dsm, msm, vsm))
        off += size

    return (loss, grad_x[None]) + tuple(out[k][j] for j in range(4) for k in _ORDER)
```
